```python
import math
import jax, jax.numpy as jnp
from jax import lax
import numpy as np

D_MODEL = 1024
BATCH = 1
SEQ = 16384
DEPTH = 2

GRID_W = 64
CTX_LEN = 256
MIX_W = D_MODEL
POOL_W = MIX_W // 2
SSM_W = MIX_W - POOL_W
POOL_WINDOWS = (2, 4, 8, 16)
POOL_GROUPS = len(POOL_WINDOWS)
POOL_GW = POOL_W // POOL_GROUPS
SSM_GROUP = 16
SSM_GROUPS = SSM_W // SSM_GROUP
SSM_STATE = 64
N_EXPERTS = 32
N_EXPERT_GROUPS = 4
EXPERTS_PER_GROUP = N_EXPERTS // N_EXPERT_GROUPS
TOP_K = 2
D_EXPERT = D_MODEL // 2
ROUTE_BLOCK = 128
EPS = 1e-6
DT_MIN = 0.001
DT_MAX = 0.1
N_MOD = 6

kernel_name = "hybrid_pool_s5_moe_prefix_dit"


def rmsnorm(x, g):
    xf = x.astype(jnp.float32)
    y = xf * lax.rsqrt(jnp.mean(xf * xf, axis=-1, keepdims=True) + EPS)
    return (y * g.astype(jnp.float32)).astype(x.dtype)


def modulate(h, shift, scale):
    return h * (1 + scale) + shift


def centred_mean(u, w):
    n = u.shape[-2]
    cs = jnp.cumsum(u.astype(jnp.float32), axis=-2)
    cs = jnp.concatenate([jnp.zeros_like(cs[..., :1, :]), cs], axis=-2)
    t = np.arange(n)
    lo = np.maximum(t - w // 2, 0)
    hi = np.minimum(t + w // 2, n)
    s = jnp.take(cs, hi, axis=-2) - jnp.take(cs, lo, axis=-2)
    cnt = (hi - lo).astype(np.float32)[:, None]
    return (s / cnt).astype(u.dtype)


def pool_mixer(u, w_pool, s_pool):
    parts = []
    for g, w in enumerate(POOL_WINDOWS):
        ug = u[..., g * POOL_GW:(g + 1) * POOL_GW]
        parts.append(centred_mean(ug, w) - ug)
    p = jnp.stack(parts, axis=-2)
    y = jnp.einsum('...ngc,gcd->...ngd', p, w_pool)
    return y.reshape(u.shape) * s_pool


def ssm_discretise(a_re, a_im, log_dt, b_re, b_im):
    a_re = jnp.minimum(a_re.astype(jnp.float32), -1e-4)
    a_im = a_im.astype(jnp.float32)
    dt = jnp.exp(log_dt.astype(jnp.float32))[:, None]
    mag = jnp.exp(a_re * dt)
    lb_re = mag * jnp.cos(a_im * dt)
    lb_im = mag * jnp.sin(a_im * dt)
    num_re = lb_re - 1.0
    num_im = lb_im
    den = a_re * a_re + a_im * a_im
    k_re = (num_re * a_re + num_im * a_im) / den
    k_im = (num_im * a_re - num_re * a_im) / den
    b_re = b_re.astype(jnp.float32)
    b_im = b_im.astype(jnp.float32)
    bb_re = k_re[..., None] * b_re - k_im[..., None] * b_im
    bb_im = k_re[..., None] * b_im + k_im[..., None] * b_re
    return lb_re, lb_im, bb_re, bb_im


def _combine(e1, e2):
    a1r, a1i, b1r, b1i = e1
    a2r, a2i, b2r, b2i = e2
    return (a2r * a1r - a2i * a1i,
            a2r * a1i + a2i * a1r,
            a2r * b1r - a2i * b1i + b2r,
            a2r * b1i + a2i * b1r + b2i)


def ssm_scan(u, disc, h0, reverse):
    lb_re, lb_im, bb_re, bb_im = disc
    bsz, n, _ = u.shape
    ug = u.astype(jnp.float32).reshape(bsz, n, SSM_GROUPS, SSM_GROUP)
    bu_re = jnp.einsum('bngi,gpi->bngp', ug, bb_re)
    bu_im = jnp.einsum('bngi,gpi->bngp', ug, bb_im)
    if h0 is not None:
        h0r, h0i = h0
        idx = n - 1 if reverse else 0
        bu_re = bu_re.at[:, idx].add(lb_re * h0r - lb_im * h0i)
        bu_im = bu_im.at[:, idx].add(lb_re * h0i + lb_im * h0r)
    a_re = jnp.broadcast_to(lb_re, bu_re.shape)
    a_im = jnp.broadcast_to(lb_im, bu_im.shape)
    _, _, h_re, h_im = lax.associative_scan(_combine, (a_re, a_im, bu_re, bu_im), reverse=reverse, axis=1)
    return h_re, h_im


def ssm_readout(h, c_re, c_im):
    h_re, h_im = h
    return (jnp.einsum('bngp,gip->bngi', h_re, c_re.astype(jnp.float32))
            - jnp.einsum('bngp,gip->bngi', h_im, c_im.astype(jnp.float32)))


def ssm_output(u, hf, hb, cf_re, cf_im, cb_re, cb_im, d, w_glu, b_glu):
    bsz, n, _ = u.shape
    y = ssm_readout(hf, cf_re, cf_im) + ssm_readout(hb, cb_re, cb_im)
    y = y.reshape(bsz, n, SSM_W) + d.astype(jnp.float32) * u.astype(jnp.float32)
    y = jax.nn.gelu(y).astype(u.dtype)
    return y * jax.nn.sigmoid(y @ w_glu + b_glu)


def route(h, w_router, b_router):
    t = h.shape[0]
    s = jax.nn.sigmoid(h.astype(jnp.float32) @ w_router.astype(jnp.float32))
    sel = (s + b_router.astype(jnp.float32)).reshape(t, N_EXPERT_GROUPS, EXPERTS_PER_GROUP)
    gscore = lax.top_k(sel, 2)[0].sum(-1)
    gidx = jnp.argmax(gscore, axis=-1)
    gmask = (gidx[:, None] == jnp.arange(N_EXPERT_GROUPS))[:, :, None]
    masked = jnp.where(gmask, sel, -jnp.inf).reshape(t, N_EXPERTS)
    _, eidx = lax.top_k(masked, TOP_K)
    w = jnp.take_along_axis(s, eidx, axis=-1)
    w = w / jnp.sum(w, axis=-1, keepdims=True)
    return eidx, w


def moe(h, eidx, gate, w_gate, w_up, w_down):
    t, d = h.shape
    a = t * TOP_K
    e_flat = eidx.reshape(a)
    tok = jnp.repeat(jnp.arange(t), TOP_K)
    g_flat = gate.reshape(a)
    order = jnp.argsort(e_flat)
    e_s = e_flat[order]
    tok_s = tok[order]
    g_s = g_flat[order]
    counts = jnp.bincount(e_flat, length=N_EXPERTS)
    offs = jnp.cumsum(counts) - counts
    padded = (counts + ROUTE_BLOCK - 1) // ROUTE_BLOCK * ROUTE_BLOCK
    pend = jnp.cumsum(padded)
    pstart = pend - padded
    dest = pstart[e_s] + (jnp.arange(a) - offs[e_s])
    rows = (a + ROUTE_BLOCK - 1) // ROUTE_BLOCK * ROUTE_BLOCK + N_EXPERTS * ROUTE_BLOCK
    nb = rows // ROUTE_BLOCK
    buf = jnp.zeros((rows, d), h.dtype).at[dest].set(h[tok_s])
    blk_e = jnp.minimum(jnp.searchsorted(pend, jnp.arange(nb) * ROUTE_BLOCK, side='right'), N_EXPERTS - 1)

    def expert_block(args):
        xb, e = args
        hid = jax.nn.silu(xb @ w_gate[e]) * (xb @ w_up[e])
        return hid @ w_down[e]

    out = lax.map(expert_block, (buf.reshape(nb, ROUTE_BLOCK, d), blk_e)).reshape(rows, d)
    y = out[dest] * g_s[:, None].astype(out.dtype)
    return jnp.zeros_like(h).at[tok_s].add(y)


def setup_inputs(seed: int = 0) -> dict:
    key = jax.random.key(seed)
    ks = jax.random.split(key, 32)
    nrm = lambda k, shape, s: jax.random.normal(k, shape, jnp.float32) * s
    D = D_MODEL
    G, P, I = SSM_GROUPS, SSM_STATE, SSM_GROUP
    a_im_init = jnp.broadcast_to(math.pi * jnp.arange(P, dtype=jnp.float32), (DEPTH, 2, G, P))
    return {
        "x": nrm(ks[0], (BATCH, SEQ, D), 1.0),
        "c": nrm(ks[1], (BATCH, D), 1.0),
        "ctx": nrm(ks[2], (BATCH, CTX_LEN, D), 1.0),
        "c_ctx": nrm(ks[3], (D,), 1.0),
        "w_mod": nrm(ks[4], (DEPTH, D, N_MOD * D), 0.5 * D ** -0.5),
        "b_mod": nrm(ks[5], (DEPTH, N_MOD * D), 0.02),
        "g_mix": 1.0 + nrm(ks[6], (DEPTH, D), 0.02),
        "g_ffn": 1.0 + nrm(ks[7], (DEPTH, D), 0.02),
        "w_in": nrm(ks[8], (DEPTH, D, MIX_W), D ** -0.5),
        "w_out": nrm(ks[9], (DEPTH, MIX_W, D), MIX_W ** -0.5),
        "w_pool": nrm(ks[10], (DEPTH, POOL_GROUPS, POOL_GW, POOL_GW), POOL_GW ** -0.5),
        "s_pool": 1.0 + nrm(ks[11], (DEPTH, POOL_W), 0.02),
        "ssm_a_re": -0.5 + nrm(ks[12], (DEPTH, 2, G, P), 0.01),
        "ssm_a_im": a_im_init + nrm(ks[13], (DEPTH, 2, G, P), 0.01),
        "ssm_log_dt": jax.random.uniform(ks[14], (DEPTH, 2, G), jnp.float32, math.log(DT_MIN), math.log(DT_MAX)),
        "ssm_b_re": nrm(ks[15], (DEPTH, 2, G, P, I), (2 * I) ** -0.5),
        "ssm_b_im": nrm(ks[16], (DEPTH, 2, G, P, I), (2 * I) ** -0.5),
        "ssm_c_re": nrm(ks[17], (DEPTH, 2, G, I, P), (2 * P) ** -0.5),
        "ssm_c_im": nrm(ks[18], (DEPTH, 2, G, I, P), (2 * P) ** -0.5),
        "ssm_d": nrm(ks[19], (DEPTH, SSM_W), 1.0),
        "w_glu": nrm(ks[20], (DEPTH, SSM_W, SSM_W), SSM_W ** -0.5),
        "b_glu": nrm(ks[21], (DEPTH, SSM_W), 0.02),
        "w_router": nrm(ks[22], (D, N_EXPERTS), D ** -0.5),
        "b_router": nrm(ks[23], (N_EXPERTS,), 0.01),
        "w_gate": nrm(ks[24], (DEPTH, N_EXPERTS, D, D_EXPERT), D ** -0.5),
        "w_up": nrm(ks[25], (DEPTH, N_EXPERTS, D, D_EXPERT), D ** -0.5),
        "w_down": nrm(ks[26], (DEPTH, N_EXPERTS, D_EXPERT, D), D_EXPERT ** -0.5),
        "g_final": 1.0 + nrm(ks[27], (D,), 0.02),
    }


def reference(x, c, ctx, c_ctx, w_mod, b_mod, g_mix, g_ffn, w_in, w_out, w_pool, s_pool,
              ssm_a_re, ssm_a_im, ssm_log_dt, ssm_b_re, ssm_b_im, ssm_c_re, ssm_c_im, ssm_d,
              w_glu, b_glu, w_router, b_router, w_gate, w_up, w_down, g_final):
    bsz, seq_len, d = x.shape
    ctx_len = ctx.shape[1]
    ROWS = seq_len // GRID_W
    cx = ctx
    for l in range(DEPTH):
        last = l == DEPTH - 1
        mod_x = (jax.nn.silu(c) @ w_mod[l] + b_mod[l])[:, None, :]
        mod_c = jax.nn.silu(c_ctx) @ w_mod[l] + b_mod[l]
        x_sh1, x_sc1, x_g1, x_sh2, x_sc2, x_g2 = jnp.split(mod_x, N_MOD, axis=-1)
        c_sh1, c_sc1, c_g1, c_sh2, c_sc2, c_g2 = jnp.split(mod_c, N_MOD, axis=-1)

        hx = modulate(rmsnorm(x, g_mix[l]), x_sh1, x_sc1)
        hc = modulate(rmsnorm(cx, g_mix[l]), c_sh1, c_sc1)
        ux = hx @ w_in[l]
        uc = hc @ (w_in[l][:, POOL_W:] if last else w_in[l])
        ux_ssm = ux[..., POOL_W:]
        uc_ssm = uc[..., -SSM_W:]
        disc_f = ssm_discretise(ssm_a_re[l, 0], ssm_a_im[l, 0], ssm_log_dt[l, 0], ssm_b_re[l, 0], ssm_b_im[l, 0])
        disc_b = ssm_discretise(ssm_a_re[l, 1], ssm_a_im[l, 1], ssm_log_dt[l, 1], ssm_b_re[l, 1], ssm_b_im[l, 1])
        hcf = ssm_scan(uc_ssm, disc_f, None, False)
        hcb = ssm_scan(uc_ssm, disc_b, None, True)
        hxf = ssm_scan(ux_ssm, disc_f, (hcf[0][:, -1], hcf[1][:, -1]), False)
        hxb = ssm_scan(ux_ssm, disc_b, (hcb[0][:, 0], hcb[1][:, 0]), True)
        ssm_x = ssm_output(ux_ssm, hxf, hxb, ssm_c_re[l, 0], ssm_c_im[l, 0], ssm_c_re[l, 1], ssm_c_im[l, 1],
                           ssm_d[l], w_glu[l], b_glu[l])
        pool_x = pool_mixer(ux[..., :POOL_W].reshape(bsz, ROWS, GRID_W, POOL_W), w_pool[l], s_pool[l])
        pool_x = pool_x.reshape(bsz, seq_len, POOL_W)
        x = x + x_g1 * (jnp.concatenate([pool_x, ssm_x], axis=-1) @ w_out[l])
        if not last:
            ssm_cc = ssm_output(uc_ssm, hcf, hcb, ssm_c_re[l, 0], ssm_c_im[l, 0], ssm_c_re[l, 1], ssm_c_im[l, 1],
                                ssm_d[l], w_glu[l], b_glu[l])
            pool_cc = pool_mixer(uc[..., :POOL_W], w_pool[l], s_pool[l])
            cx = cx + c_g1 * (jnp.concatenate([pool_cc, ssm_cc], axis=-1) @ w_out[l])

        hx2 = modulate(rmsnorm(x, g_ffn[l]), x_sh2, x_sc2).reshape(bsz * seq_len, d)
        if last:
            tokens = hx2
        else:
            hc2 = modulate(rmsnorm(cx, g_ffn[l]), c_sh2, c_sc2).reshape(bsz * ctx_len, d)
            tokens = jnp.concatenate([hx2, hc2], axis=0)
        eidx, gate = route(tokens, w_router, b_router)
        y = moe(tokens, eidx, gate, w_gate[l], w_up[l], w_down[l])
        x = x + x_g2 * y[:bsz * seq_len].reshape(bsz, seq_len, d)
        if not last:
            cx = cx + c_g2 * y[bsz * seq_len:].reshape(bsz, ctx_len, d)
    return rmsnorm(x, g_final)
```

```python
import functools

import numpy as np
import jax
import jax.numpy as jnp
from jax import lax
from jax.experimental import pallas as pl
from jax.experimental.pallas import tpu as pltpu

F32 = jnp.float32
BF16 = jnp.bfloat16

D_MODEL = 1024
POOL_W = 512
SSM_W = 512
POOL_WINDOWS = (2, 4, 8, 16)
POOL_GW = 128
SSM_GROUP = 16
SSM_GROUPS = 32
SSM_STATE = 64
N_EXPERTS = 32
N_EXPERT_GROUPS = 4
EXPERTS_PER_GROUP = 8
D_EXPERT = 512
GRID_W = 64
EPS = 1e-6
N_MOD = 6

LANES = 128
SUBLANES = 8
TM = 256
CH = 8
GB = 8
NGB = SSM_GROUPS // GB
CW = CH * LANES
SW = 2 * GB * SSM_STATE
SR = 256
MOE_BLK = 256
VMEM_LIMIT = 48 * 1024 * 1024


def _cparams(sem):
    return pltpu.CompilerParams(dimension_semantics=sem, vmem_limit_bytes=VMEM_LIMIT)


def _rmsnorm_mod(x, g, shift, scale):
    ms = jnp.mean(x * x, axis=-1, keepdims=True)
    y = x * lax.rsqrt(ms + EPS) * g
    return y * (1.0 + scale) + shift


def _mod_kernel(c_ref, w_ref, b_ref, o_ref):
    c = c_ref[...]
    a = c * jax.nn.sigmoid(c)
    o_ref[...] = jnp.dot(a, w_ref[...], preferred_element_type=F32,
                         precision=lax.Precision.HIGHEST) + b_ref[...]


def _modulation(cvec, w_mod, b_mod):
    depth, d, n = w_mod.shape
    tn = 1536
    return pl.pallas_call(
        _mod_kernel,
        grid=(depth, n // tn),
        in_specs=[
            pl.BlockSpec((SUBLANES, d), lambda l, j: (0, 0)),
            pl.BlockSpec((None, d, tn), lambda l, j: (l, 0, j)),
            pl.BlockSpec((None, 1, tn), lambda l, j: (l, 0, j)),
        ],
        out_specs=pl.BlockSpec((None, SUBLANES, tn), lambda l, j: (l, 0, j)),
        out_shape=jax.ShapeDtypeStruct((depth, SUBLANES, n), F32),
        compiler_params=_cparams(("arbitrary", "arbitrary")),
        name="modulation",
    )(cvec, w_mod, b_mod.reshape(depth, 1, n))


def _inproj_kernel(x_ref, mod_ref, g_ref, w_ref, ux_ref, uv_ref, scr):
    h = _rmsnorm_mod(x_ref[...], g_ref[...], mod_ref[0:1, :], mod_ref[1:2, :])
    ux = jnp.dot(h.astype(BF16), w_ref[...], preferred_element_type=F32)
    ux_ref[...] = ux
    rows = TM // CH
    for b in range(NGB):
        scr[b] = ux[:, POOL_W + LANES * b:POOL_W + LANES * (b + 1)]
        for t in range(CH):
            piece = scr[b, pl.ds(t, rows, stride=CH), :]
            uv_ref[:, pl.ds((b * CH + t) * LANES, LANES)] = piece.astype(BF16)


def _inproj(tok, mod_l, g, w_bf, n_x_tiles):
    t = tok.shape[0]
    nt = t // TM
    sel = lambda i: (jnp.where(i == n_x_tiles, 1, 0), 0, 0)
    return pl.pallas_call(
        _inproj_kernel,
        grid=(nt,),
        in_specs=[
            pl.BlockSpec((TM, D_MODEL), lambda i: (i, 0)),
            pl.BlockSpec((None, N_MOD, D_MODEL), sel),
            pl.BlockSpec((1, D_MODEL), lambda i: (0, 0)),
            pl.BlockSpec((D_MODEL, D_MODEL), lambda i: (0, 0)),
        ],
        out_specs=[
            pl.BlockSpec((TM, D_MODEL), lambda i: (i, 0)),
            pl.BlockSpec((TM // CH, NGB * CW), lambda i: (i, 0)),
        ],
        out_shape=[
            jax.ShapeDtypeStruct((t, D_MODEL), F32),
            jax.ShapeDtypeStruct((t // CH, NGB * CW), BF16),
        ],
        scratch_shapes=[pltpu.VMEM((NGB, TM, LANES), F32)],
        compiler_params=_cparams(("arbitrary",)),
        name="mixer_inproj",
    )(tok, mod_l, g, w_bf)


def _chunk_scan(s_ref, tab_ref, carry_ref, out_ref, nblk, reverse):
    half = SW // 2
    ncol = half // LANES
    row = lax.broadcasted_iota(jnp.int32, (SUBLANES, LANES), 0)
    edge = (row == SUBLANES - 1) if reverse else (row == 0)
    last = 0 if reverse else SUBLANES - 1

    def sub_block(r0, carry):
        outs_r, outs_i, new_carry = [], [], []
        for j in range(ncol):
            cr, ci = carry[2 * j], carry[2 * j + 1]
            lre = pl.ds(LANES * j, LANES)
            lim = pl.ds(half + LANES * j, LANES)
            zr = s_ref[pl.ds(r0, SUBLANES), lre]
            zi = s_ref[pl.ds(r0, SUBLANES), lim]
            for q, k in enumerate((1, 2, 4)):
                ar = tab_ref[pl.ds(16 * q, SUBLANES), lre]
                ai = tab_ref[pl.ds(16 * q + 8, SUBLANES), lre]
                sh = SUBLANES - k if reverse else k
                sr = pltpu.roll(zr, sh, axis=0)
                si = pltpu.roll(zi, sh, axis=0)
                zr, zi = zr + ar * sr - ai * si, zi + ar * si + ai * sr
            pr = tab_ref[pl.ds(48, SUBLANES), lre]
            pi = tab_ref[pl.ds(56, SUBLANES), lre]
            zr, zi = zr + pr * cr - pi * ci, zi + pr * ci + pi * cr
            sh1 = SUBLANES - 1 if reverse else 1
            outs_r.append(jnp.where(edge, cr, pltpu.roll(zr, sh1, axis=0)))
            outs_i.append(jnp.where(edge, ci, pltpu.roll(zi, sh1, axis=0)))
            new_carry.append(jnp.broadcast_to(zr[last:last + 1, :], (SUBLANES, LANES)))
            new_carry.append(jnp.broadcast_to(zi[last:last + 1, :], (SUBLANES, LANES)))
        return outs_r, outs_i, tuple(new_carry)

    def body(it, carry):
        bi = (nblk - 1 - it) if reverse else it
        r0 = pl.multiple_of(bi * 2 * SUBLANES, 2 * SUBLANES)
        if reverse:
            hi_r, hi_i, carry = sub_block(r0 + SUBLANES, carry)
            lo_r, lo_i, carry = sub_block(r0, carry)
        else:
            lo_r, lo_i, carry = sub_block(r0, carry)
            hi_r, hi_i, carry = sub_block(r0 + SUBLANES, carry)
        for j in range(ncol):
            out_ref[pl.ds(r0, 2 * SUBLANES), pl.ds(LANES * j, LANES)] = (
                jnp.concatenate([lo_r[j], hi_r[j]], axis=0).astype(BF16))
            out_ref[pl.ds(r0, 2 * SUBLANES), pl.ds(half + LANES * j, LANES)] = (
                jnp.concatenate([lo_i[j], hi_i[j]], axis=0).astype(BF16))
        return carry

    init = tuple(carry_ref[:, pl.ds(LANES * c, LANES)] for c in range(2 * ncol))
    final = lax.fori_loop(0, nblk, body, init)
    for c in range(2 * ncol):
        carry_ref[:, pl.ds(LANES * c, LANES)] = final[c]


def _ssm_state_kernel(uf_ref, ub_ref, msf_ref, msb_ref, tf_ref, tb_ref, hf_ref, gb_ref,
                      sf, sb, cf, cb, *, ctx_rows):
    step = pl.program_id(1)

    @pl.when(step == 0)
    def _():
        cf[...] = jnp.zeros_like(cf)
        cb[...] = jnp.zeros_like(cb)
        hf_ref[...] = jnp.zeros_like(hf_ref)
        gb_ref[...] = jnp.zeros_like(gb_ref)

    sf[...] = jnp.dot(uf_ref[...], msf_ref[...], preferred_element_type=F32)
    sb[...] = jnp.dot(ub_ref[...], msb_ref[...], preferred_element_type=F32)
    nblk = jnp.where(step == 0, ctx_rows // (2 * SUBLANES), SR // (2 * SUBLANES))
    _chunk_scan(sf, tf_ref, cf, hf_ref, nblk, reverse=False)
    _chunk_scan(sb, tb_ref, cb, gb_ref, nblk, reverse=True)


def _ssm_states(uv_ext, ms, tab, n_xt, ctx_rows):
    rows = uv_ext.shape[0]
    fwd = lambda b, s: (jnp.where(s == 0, n_xt, s - 1), b)
    bwd = lambda b, s: (jnp.where(s == 0, n_xt, n_xt - s), b)
    return pl.pallas_call(
        functools.partial(_ssm_state_kernel, ctx_rows=ctx_rows),
        grid=(NGB, n_xt + 1),
        in_specs=[
            pl.BlockSpec((SR, CW), fwd),
            pl.BlockSpec((SR, CW), bwd),
            pl.BlockSpec((None, None, CW, SW), lambda b, s: (0, b, 0, 0)),
            pl.BlockSpec((None, None, CW, SW), lambda b, s: (1, b, 0, 0)),
            pl.BlockSpec((None, None, 8 * SUBLANES, SW // 2), lambda b, s: (0, b, 0, 0)),
            pl.BlockSpec((None, None, 8 * SUBLANES, SW // 2), lambda b, s: (1, b, 0, 0)),
        ],
        out_specs=[pl.BlockSpec((SR, SW), fwd), pl.BlockSpec((SR, SW), bwd)],
        out_shape=[jax.ShapeDtypeStruct((rows, NGB * SW), BF16)] * 2,
        scratch_shapes=[pltpu.VMEM((SR, SW), F32), pltpu.VMEM((SR, SW), F32),
                        pltpu.VMEM((SUBLANES, SW), F32), pltpu.VMEM((SUBLANES, SW), F32)],
        compiler_params=_cparams(("arbitrary", "arbitrary")),
        name="ssm_states",
    )(uv_ext, uv_ext, ms, ms, tab, tab)


def _ssm_out_kernel(u_ref, hf_ref, gb_ref, m_ref, y_ref):
    res = jnp.dot(u_ref[...], m_ref[0:CW, :], preferred_element_type=F32)
    res += jnp.dot(hf_ref[...], m_ref[CW:CW + SW, :], preferred_element_type=F32)
    res += jnp.dot(gb_ref[...], m_ref[CW + SW:CW + 2 * SW, :], preferred_element_type=F32)
    for t in range(CH):
        y_ref[pl.ds(t, SR, stride=CH), :] = res[:, t * LANES:(t + 1) * LANES]


def _ssm_readout(uv_ext, hf, gb, mcat):
    rows = uv_ext.shape[0]
    nt = rows // SR
    blk = lambda b, i: (i, b)
    return pl.pallas_call(
        _ssm_out_kernel,
        grid=(NGB, nt),
        in_specs=[
            pl.BlockSpec((SR, CW), blk),
            pl.BlockSpec((SR, SW), blk),
            pl.BlockSpec((SR, SW), blk),
            pl.BlockSpec((None, CW + 2 * SW, CW), lambda b, i: (b, 0, 0)),
        ],
        out_specs=pl.BlockSpec((SR * CH, LANES), blk),
        out_shape=jax.ShapeDtypeStruct((rows * CH, SSM_W), F32),
        compiler_params=_cparams(("arbitrary", "arbitrary")),
        name="ssm_readout",
    )(uv_ext, hf, gb, mcat)


def _gelu_tanh(x):
    return 0.5 * x * (1.0 + jnp.tanh(0.7978845608028654 * (x + 0.044715 * x * x * x)))


def _route(s, b_router):
    sel = s + b_router
    lane_i = lax.broadcasted_iota(jnp.int32, s.shape, 1)
    grp = lane_i // EXPERTS_PER_GROUP
    lane = lane_i.astype(F32)
    neg = jnp.float32(-jnp.inf)
    big = jnp.float32(N_EXPERTS)

    def top2(vals):
        m1 = jnp.max(vals, axis=-1, keepdims=True)
        i1 = jnp.min(jnp.where(vals == m1, lane, big), axis=-1, keepdims=True)
        rest = jnp.where(lane == i1, neg, vals)
        m2 = jnp.max(rest, axis=-1, keepdims=True)
        i2 = jnp.min(jnp.where(rest == m2, lane, big), axis=-1, keepdims=True)
        return m1, i1, m2, i2

    best = None
    gidx = None
    for g in range(N_EXPERT_GROUPS):
        m1, _, m2, _ = top2(jnp.where(grp == g, sel, neg))
        score = m1 + m2
        if best is None:
            best, gidx = score, jnp.zeros_like(lane_i[:, 0:1])
        else:
            upd = score > best
            best = jnp.where(upd, score, best)
            gidx = jnp.where(upd, g, gidx)
    _, e1, _, e2 = top2(jnp.where(grp == gidx, sel, neg))
    w1 = jnp.sum(jnp.where(lane == e1, s, 0.0), axis=-1, keepdims=True)
    w2 = jnp.sum(jnp.where(lane == e2, s, 0.0), axis=-1, keepdims=True)
    tot = w1 + w2
    return e1.astype(jnp.int32), e2.astype(jnp.int32), w1 / tot, w2 / tot


def _mixout_kernel(x_ref, ux_ref, ys_ref, mod_ref, band_ref, icnt_ref, wp_ref, sp_ref, d_ref,
                   wglu_ref, bglu_ref, wout_ref, gffn_ref, wr_ref, br_ref,
                   xo_ref, h2_ref, eidx_ref, gate_ref):
    ux = ux_ref[...]
    parts = []
    for g in range(len(POOL_WINDOWS)):
        ug = ux[:, g * POOL_GW:(g + 1) * POOL_GW]
        hi = ug.astype(BF16)
        lo = (ug - hi.astype(F32)).astype(BF16)
        band = band_ref[g]
        wsum = (jnp.dot(band, hi, preferred_element_type=F32)
                + jnp.dot(band, lo, preferred_element_type=F32))
        p = wsum * icnt_ref[:, g * POOL_GW:(g + 1) * POOL_GW] - ug
        parts.append(jnp.dot(p.astype(BF16), wp_ref[g], preferred_element_type=F32))
    pool = jnp.concatenate(parts, axis=-1) * sp_ref[...]

    y = ys_ref[...] + d_ref[...] * ux[:, POOL_W:]
    y = _gelu_tanh(y)
    z = jnp.dot(y.astype(BF16), wglu_ref[...], preferred_element_type=F32) + bglu_ref[...]
    glu = y * jax.nn.sigmoid(z)

    cat = jnp.concatenate([pool, glu], axis=-1).astype(BF16)
    o = jnp.dot(cat, wout_ref[...], preferred_element_type=F32)
    xn = x_ref[...] + mod_ref[2:3, :] * o
    xo_ref[...] = xn

    h2 = _rmsnorm_mod(xn, gffn_ref[...], mod_ref[3:4, :], mod_ref[4:5, :])
    h2_ref[...] = h2
    logits = jnp.dot(h2, wr_ref[...], preferred_element_type=F32, precision=lax.Precision.HIGHEST)
    e1, e2, g1, g2 = _route(jax.nn.sigmoid(logits), br_ref[...])
    eidx_ref[...] = jnp.concatenate([e1, e2], axis=-1)
    gate_ref[...] = jnp.concatenate([g1, g2], axis=-1)


def _mixout(tok, ux, y_ssm, mod_l, band, icnt, wp_bf, sp, dvec, wglu_bf, bglu, wout_bf, gffn,
            w_router, b_router, nt, n_x_tiles):
    t = nt * TM
    sel = lambda i: (jnp.where(i == n_x_tiles, 1, 0), 0, 0)
    sel4 = lambda i: (jnp.where(i == n_x_tiles, 1, 0), 0, 0, 0)
    row = lambda i: (i, 0)
    fix2 = lambda i: (0, 0)
    return pl.pallas_call(
        _mixout_kernel,
        grid=(nt,),
        in_specs=[
            pl.BlockSpec((TM, D_MODEL), row),
            pl.BlockSpec((TM, D_MODEL), row),
            pl.BlockSpec((TM, SSM_W), row),
            pl.BlockSpec((None, N_MOD, D_MODEL), sel),
            pl.BlockSpec((None, len(POOL_WINDOWS), TM, TM), sel4),
            pl.BlockSpec((None, TM, POOL_W), sel),
            pl.BlockSpec((len(POOL_WINDOWS), POOL_GW, POOL_GW), lambda i: (0, 0, 0)),
            pl.BlockSpec((1, POOL_W), fix2),
            pl.BlockSpec((1, SSM_W), fix2),
            pl.BlockSpec((SSM_W, SSM_W), fix2),
            pl.BlockSpec((1, SSM_W), fix2),
            pl.BlockSpec((D_MODEL, D_MODEL), fix2),
            pl.BlockSpec((1, D_MODEL), fix2),
            pl.BlockSpec((D_MODEL, N_EXPERTS), fix2),
            pl.BlockSpec((1, N_EXPERTS), fix2),
        ],
        out_specs=[
            pl.BlockSpec((TM, D_MODEL), row),
            pl.BlockSpec((TM, D_MODEL), row),
            pl.BlockSpec((TM, 2), row),
            pl.BlockSpec((TM, 2), row),
        ],
        out_shape=[
            jax.ShapeDtypeStruct((t, D_MODEL), F32),
            jax.ShapeDtypeStruct((t, D_MODEL), F32),
            jax.ShapeDtypeStruct((t, 2), jnp.int32),
            jax.ShapeDtypeStruct((t, 2), F32),
        ],
        compiler_params=_cparams(("arbitrary",)),
        name="mixer_out_router",
    )(tok, ux, y_ssm, mod_l, band, icnt, wp_bf, sp, dvec, wglu_bf, bglu, wout_bf, gffn,
      w_router, b_router)


def _rank_kernel(eidx_ref, tri_ref, rank_ref, cnt_ref, run):
    i = pl.program_id(0)

    @pl.when(i == 0)
    def _():
        run[...] = jnp.zeros_like(run)

    e = eidx_ref[...]
    lane = lax.broadcasted_iota(jnp.int32, (TM, N_EXPERTS), 1)
    oh0 = lane == e[:, 0:1]
    oh1 = lane == e[:, 1:2]
    tri = tri_ref[...]
    before0 = jnp.dot(tri, jnp.where(oh0, 1.0, 0.0).astype(BF16), preferred_element_type=F32)
    before1 = jnp.dot(tri, jnp.where(oh1, 1.0, 0.0).astype(BF16), preferred_element_type=F32)
    tot0 = jnp.sum(jnp.where(oh0, 1.0, 0.0), axis=0, keepdims=True)
    tot1 = jnp.sum(jnp.where(oh1, 1.0, 0.0), axis=0, keepdims=True)
    base = run[...]
    r0 = jnp.sum(jnp.where(oh0, base + before0, 0.0), axis=-1, keepdims=True)
    r1 = jnp.sum(jnp.where(oh1, base + tot0 + before1, 0.0), axis=-1, keepdims=True)
    rank_ref[...] = jnp.concatenate([r0, r1], axis=-1).astype(jnp.int32)
    run[...] = base + tot0 + tot1
    cnt_ref[...] = (base + tot0 + tot1).astype(jnp.int32)


def _ranks(eidx, tri, nt):
    return pl.pallas_call(
        _rank_kernel,
        grid=(nt,),
        in_specs=[pl.BlockSpec((TM, 2), lambda i: (i, 0)),
                  pl.BlockSpec((TM, TM), lambda i: (0, 0))],
        out_specs=[pl.BlockSpec((TM, 2), lambda i: (i, 0)),
                   pl.BlockSpec((1, N_EXPERTS), lambda i: (0, 0))],
        out_shape=[jax.ShapeDtypeStruct((nt * TM, 2), jnp.int32),
                   jax.ShapeDtypeStruct((1, N_EXPERTS), jnp.int32)],
        scratch_shapes=[pltpu.VMEM((1, N_EXPERTS), F32)],
        compiler_params=_cparams(("arbitrary",)),
        name="moe_ranks",
    )(eidx, tri)


def _row_copy(src, s, dst, d, sem):
    return pltpu.make_async_copy(src.at[pl.ds(s, 1)], dst.at[pl.ds(d, 1)], sem)


def _dispatch_kernel(dest_ref, h_ref, zero_ref, buf_ref, sem):
    del zero_ref

    def issue(r, c):
        for k in range(2):
            _row_copy(h_ref, r, buf_ref, dest_ref[0, 2 * r + k], sem).start()
        return c

    lax.fori_loop(0, TM, issue, 0)

    def drain(r, c):
        for k in range(2):
            _row_copy(h_ref, r, buf_ref, dest_ref[0, 2 * r + k], sem).wait()
        return c

    lax.fori_loop(0, TM, drain, 0)


def _dispatch(dest3, h2, zeros_buf, nt):
    return pl.pallas_call(
        _dispatch_kernel,
        grid=(nt,),
        in_specs=[
            pl.BlockSpec((None, 1, 2 * TM), lambda i: (i, 0, 0), memory_space=pltpu.SMEM),
            pl.BlockSpec((TM, D_MODEL), lambda i: (i, 0)),
            pl.BlockSpec(memory_space=pl.ANY),
        ],
        out_specs=pl.BlockSpec(memory_space=pl.ANY),
        out_shape=jax.ShapeDtypeStruct(zeros_buf.shape, F32),
        scratch_shapes=[pltpu.SemaphoreType.DMA(())],
        input_output_aliases={2: 0},
        compiler_params=_cparams(("arbitrary",)),
        name="moe_dispatch",
    )(dest3, h2, zeros_buf)


def _expert_kernel(be_ref, nbu_ref, x_ref, wg_ref, wu_ref, wd_ref, o_ref, wg_s, wu_s, wd_s):
    i = pl.program_id(0)
    changed = jnp.logical_or(i == 0, be_ref[i] != be_ref[jnp.maximum(i - 1, 0)])

    @pl.when(changed)
    def _():
        wg_s[...] = wg_ref[...].astype(BF16)
        wu_s[...] = wu_ref[...].astype(BF16)
        wd_s[...] = wd_ref[...].astype(BF16)

    @pl.when(i < nbu_ref[0])
    def _():
        xb = x_ref[...].astype(BF16)
        g = jnp.dot(xb, wg_s[...], preferred_element_type=F32)
        u = jnp.dot(xb, wu_s[...], preferred_element_type=F32)
        hid = (g * jax.nn.sigmoid(g)) * u
        o_ref[...] = jnp.dot(hid.astype(BF16), wd_s[...], preferred_element_type=F32)

    @pl.when(i >= nbu_ref[0])
    def _():
        o_ref[...] = jnp.zeros_like(o_ref)


def _experts(blk_e, nb_used, buf, w_gate, w_up, w_down, layer):
    rows = buf.shape[0]
    nb = rows // MOE_BLK
    xmap = lambda i, be, nbu: (jnp.minimum(i, nbu[0] - 1), 0)
    wmap = lambda i, be, nbu: (layer, be[i], 0, 0)
    return pl.pallas_call(
        _expert_kernel,
        grid_spec=pltpu.PrefetchScalarGridSpec(
            num_scalar_prefetch=2,
            grid=(nb,),
            in_specs=[
                pl.BlockSpec((MOE_BLK, D_MODEL), xmap),
                pl.BlockSpec((None, None, D_MODEL, D_EXPERT), wmap),
                pl.BlockSpec((None, None, D_MODEL, D_EXPERT), wmap),
                pl.BlockSpec((None, None, D_EXPERT, D_MODEL), wmap),
            ],
            out_specs=pl.BlockSpec((MOE_BLK, D_MODEL), lambda i, be, nbu: (i, 0)),
            scratch_shapes=[pltpu.VMEM((D_MODEL, D_EXPERT), BF16),
                            pltpu.VMEM((D_MODEL, D_EXPERT), BF16),
                            pltpu.VMEM((D_EXPERT, D_MODEL), BF16)],
        ),
        out_shape=jax.ShapeDtypeStruct((rows, D_MODEL), F32),
        compiler_params=_cparams(("arbitrary",)),
        name="moe_experts",
    )(blk_e, nb_used, buf, w_gate, w_up, w_down)


def _combine_kernel(dest_ref, x_ref, gate_ref, mod_ref, gfin_ref, o_hbm, out_ref, gat, sem,
                    *, final):
    def issue(r, c):
        for k in range(2):
            _row_copy(o_hbm, dest_ref[0, 2 * r + k], gat.at[k], r, sem).start()
        return c

    lax.fori_loop(0, TM, issue, 0)

    def drain(r, c):
        for k in range(2):
            _row_copy(o_hbm, dest_ref[0, 2 * r + k], gat.at[k], r, sem).wait()
        return c

    lax.fori_loop(0, TM, drain, 0)

    gate = gate_ref[...]
    y = gate[:, 0:1] * gat[0] + gate[:, 1:2] * gat[1]
    xn = x_ref[...] + mod_ref[5:6, :] * y
    if final:
        ms = jnp.mean(xn * xn, axis=-1, keepdims=True)
        xn = xn * lax.rsqrt(ms + EPS) * gfin_ref[...]
    out_ref[...] = xn


def _combine(dest3, xn, gate, mod_l, g_final, expert_out, nt, n_x_tiles, final):
    sel = lambda i: (jnp.where(i == n_x_tiles, 1, 0), 0, 0)
    return pl.pallas_call(
        functools.partial(_combine_kernel, final=final),
        grid=(nt,),
        in_specs=[
            pl.BlockSpec((None, 1, 2 * TM), lambda i: (i, 0, 0), memory_space=pltpu.SMEM),
            pl.BlockSpec((TM, D_MODEL), lambda i: (i, 0)),
            pl.BlockSpec((TM, 2), lambda i: (i, 0)),
            pl.BlockSpec((None, N_MOD, D_MODEL), sel),
            pl.BlockSpec((1, D_MODEL), lambda i: (0, 0)),
            pl.BlockSpec(memory_space=pl.ANY),
        ],
        out_specs=pl.BlockSpec((TM, D_MODEL), lambda i: (i, 0)),
        out_shape=jax.ShapeDtypeStruct((nt * TM, D_MODEL), F32),
        scratch_shapes=[pltpu.VMEM((2, TM, D_MODEL), F32), pltpu.SemaphoreType.DMA(())],
        compiler_params=_cparams(("arbitrary",)),
        name="moe_combine",
    )(dest3, xn, gate, mod_l, g_final, expert_out)


def _moe(h2, eidx, gate, xn, mod_l, g_final, w_gate, w_up, w_down, layer, nt, n_x_tiles, final,
         tri):
    t = nt * TM
    rank, counts = _ranks(eidx, tri, nt)
    counts = counts[0]
    padded = (counts + MOE_BLK - 1) // MOE_BLK * MOE_BLK
    pend = jnp.cumsum(padded)
    pstart = pend - padded
    dest = pstart[eidx] + rank
    nb = (2 * t + N_EXPERTS * (MOE_BLK - 1)) // MOE_BLK + 1
    nb_used = (pend[-1] // MOE_BLK).astype(jnp.int32)
    blk_start = jnp.minimum(jnp.arange(nb, dtype=jnp.int32), nb_used - 1) * MOE_BLK
    blk_e = jnp.minimum(jnp.sum(pend[None, :] <= blk_start[:, None], axis=1),
                        N_EXPERTS - 1).astype(jnp.int32)
    dest3 = dest.reshape(nt, 1, 2 * TM)
    buf = _dispatch(dest3, h2, jnp.zeros((nb * MOE_BLK, D_MODEL), F32), nt)
    eo = _experts(blk_e, nb_used.reshape(1), buf, w_gate, w_up, w_down, layer)
    return _combine(dest3, xn, gate, mod_l, g_final, eo, nt, n_x_tiles, final)


def _cmul(ar, ai, br, bi):
    return ar * br - ai * bi, ar * bi + ai * br


def _ssm_matrices(a_re, a_im, log_dt, b_re, b_im, c_re, c_im):
    hp = lax.Precision.HIGHEST
    g, p, ch = SSM_GROUPS, SSM_STATE, SSM_GROUP
    a_re = jnp.minimum(a_re, -1e-4)
    dt = jnp.exp(log_dt)[..., None]
    mag = jnp.exp(a_re * dt)
    lr = mag * jnp.cos(a_im * dt)
    li = mag * jnp.sin(a_im * dt)
    den = a_re * a_re + a_im * a_im
    k_re = ((lr - 1.0) * a_re + li * a_im) / den
    k_im = (li * a_re - (lr - 1.0) * a_im) / den
    bb_re = k_re[..., None] * b_re - k_im[..., None] * b_im
    bb_im = k_re[..., None] * b_im + k_im[..., None] * b_re

    pw = [(jnp.ones_like(lr), jnp.zeros_like(lr))]
    for _ in range(CH):
        pw.append(_cmul(pw[-1][0], pw[-1][1], lr, li))
    pwr = jnp.stack([q[0] for q in pw])
    pwi = jnp.stack([q[1] for q in pw])
    ar8, ai8 = pw[CH]
    apw = [(ar8, ai8)]
    for _ in range(SUBLANES - 1):
        apw.append(_cmul(apw[-1][0], apw[-1][1], ar8, ai8))
    apr = jnp.stack([q[0] for q in apw])
    api = jnp.stack([q[1] for q in apw])

    er = c_re[None] * pwr[:, :, :, None, :] - c_im[None] * pwi[:, :, :, None, :]
    ei = c_re[None] * pwi[:, :, :, None, :] + c_im[None] * pwr[:, :, :, None, :]
    vr = pwr[..., None] * bb_re[None] - pwi[..., None] * bb_im[None]
    vi = pwr[..., None] * bb_im[None] + pwi[..., None] * bb_re[None]
    kk = (jnp.einsum('ndgip,dgpj->ndgij', er, bb_re, precision=hp)
          - jnp.einsum('ndgip,dgpj->ndgij', ei, bb_im, precision=hp))

    eye = jnp.eye(GB, dtype=F32)
    sig = np.arange(CH)[:, None]
    tau = np.arange(CH)[None, :]
    lag = np.abs(tau - sig).reshape(-1)
    kf = kk[lag, 0].reshape(CH, CH, g, ch, ch)
    kb = kk[lag, 1].reshape(CH, CH, g, ch, ch)
    m_f = jnp.asarray((tau >= sig).astype(np.float32))[:, :, None, None, None]
    m_b = jnp.asarray((tau <= sig).astype(np.float32))[:, :, None, None, None]
    kfull = (kf * m_f + kb * m_b).reshape(CH, CH, NGB, GB, ch, ch)
    m_intra = jnp.einsum('stbgij,gh->bsgjthi', kfull, eye).reshape(NGB, CW, CW)

    def state_mat(v_re, v_im):
        v_re = v_re.reshape(CH, NGB, GB, p, ch)
        v_im = v_im.reshape(CH, NGB, GB, p, ch)
        m_re = jnp.einsum('sbgpj,gh->bsgjhp', v_re, eye)
        m_im = jnp.einsum('sbgpj,gh->bsgjhp', v_im, eye)
        return jnp.stack([m_re, m_im], axis=4).reshape(NGB, CW, SW)

    ms_f = state_mat(vr[CH - 1 - np.arange(CH), 0], vi[CH - 1 - np.arange(CH), 0])
    ms_b = state_mat(vr[np.arange(CH), 1], vi[np.arange(CH), 1])

    def out_mat(e_re, e_im):
        e_re = e_re.reshape(CH, NGB, GB, ch, p)
        e_im = e_im.reshape(CH, NGB, GB, ch, p)
        m_re = jnp.einsum('tbgip,gh->bgpthi', e_re, eye)
        m_im = jnp.einsum('tbgip,gh->bgpthi', -e_im, eye)
        return jnp.stack([m_re, m_im], axis=1).reshape(NGB, SW, CW)

    mo_f = out_mat(er[1 + np.arange(CH), 0], ei[1 + np.arange(CH), 0])
    mo_b = out_mat(er[CH - np.arange(CH), 1], ei[CH - np.arange(CH), 1])

    ms = jnp.stack([ms_f, ms_b]).astype(BF16)
    mcat = jnp.concatenate([m_intra, mo_f, mo_b], axis=1).astype(BF16)

    rows = np.arange(SUBLANES)
    tabs = []
    for d in range(2):
        consts = []
        for k in (1, 2, 4):
            keep = (rows >= k) if d == 0 else (rows < SUBLANES - k)
            keep = jnp.asarray(keep.astype(np.float32))[:, None, None]
            consts.append(keep * apr[k - 1, d][None])
            consts.append(keep * api[k - 1, d][None])
        order = rows if d == 0 else SUBLANES - 1 - rows
        consts.append(apr[order, d])
        consts.append(api[order, d])
        tabs.append(jnp.stack(consts))
    tab = jnp.stack(tabs).reshape(2, 8 * SUBLANES, NGB, GB * p)
    tab = tab.transpose(0, 2, 1, 3)
    return ms, mcat, tab


def _pool_constants(seg):
    t = np.arange(TM)
    s0 = t // seg * seg
    band = np.zeros((len(POOL_WINDOWS), TM, TM), np.float32)
    icnt = np.zeros((TM, POOL_W), np.float32)
    for g, w in enumerate(POOL_WINDOWS):
        lo = np.maximum(t - w // 2, s0)
        hi = np.minimum(t + w // 2, s0 + seg)
        band[g] = (t[None, :] >= lo[:, None]) & (t[None, :] < hi[:, None])
        icnt[:, g * POOL_GW:(g + 1) * POOL_GW] = (1.0 / (hi - lo).astype(np.float32))[:, None]
    return band, icnt


def kernel(x, c, ctx, c_ctx, w_mod, b_mod, g_mix, g_ffn, w_in, w_out, w_pool, s_pool,
           ssm_a_re, ssm_a_im, ssm_log_dt, ssm_b_re, ssm_b_im, ssm_c_re, ssm_c_im, ssm_d,
           w_glu, b_glu, w_router, b_router, w_gate, w_up, w_down, g_final):
    bsz, seq, d = x.shape
    ctx_len = ctx.shape[1]
    depth = w_mod.shape[0]
    assert bsz == 1 and d == D_MODEL and ctx_len == TM
    assert seq % (SR * CH) == 0 and seq % GRID_W == 0
    n_xt = seq // TM
    n_st = seq // (SR * CH)
    ctx_rows = ctx_len // CH

    cvec = jnp.concatenate([c.reshape(1, d), c_ctx.reshape(1, d),
                            jnp.zeros((SUBLANES - 2, d), F32)], axis=0)
    mod = _modulation(cvec, w_mod, b_mod).reshape(depth, SUBLANES, N_MOD, d)

    band_x, icnt_x = _pool_constants(GRID_W)
    band_c, icnt_c = _pool_constants(ctx_len)
    band = jnp.asarray(np.stack([band_x, band_c]), BF16)
    icnt = jnp.asarray(np.stack([icnt_x, icnt_c]), F32)
    tri = jnp.asarray(np.tril(np.ones((TM, TM), np.float32), -1), BF16)

    tok = jnp.concatenate([x[0], ctx[0]], axis=0)
    for l in range(depth):
        last = l == depth - 1
        mod_l = mod[l, :2]
        ms, mcat, tab = _ssm_matrices(ssm_a_re[l], ssm_a_im[l], ssm_log_dt[l], ssm_b_re[l],
                                      ssm_b_im[l], ssm_c_re[l], ssm_c_im[l])
        ux, uv = _inproj(tok, mod_l, g_mix[l].reshape(1, d), w_in[l].astype(BF16), n_xt)
        uv_ext = jnp.pad(uv, ((0, SR - ctx_rows), (0, 0)))
        hf, gb = _ssm_states(uv_ext, ms, tab, n_st, ctx_rows)
        y_ssm = _ssm_readout(uv_ext, hf, gb, mcat)
        nt = n_xt if last else n_xt + 1
        xn, h2, eidx, gate = _mixout(
            tok, ux, y_ssm, mod_l, band, icnt, w_pool[l].astype(BF16), s_pool[l].reshape(1, -1),
            ssm_d[l].reshape(1, -1), w_glu[l].astype(BF16), b_glu[l].reshape(1, -1),
            w_out[l].astype(BF16), g_ffn[l].reshape(1, d), w_router, b_router.reshape(1, -1),
            nt, n_xt)
        tok = _moe(h2, eidx, gate, xn, mod_l, g_final.reshape(1, d), w_gate, w_up, w_down, l,
                   nt, n_xt, last, tri)
    return tok.reshape(bsz, seq, d)
```

```python
import functools

import numpy as np
import jax
import jax.numpy as jnp
from jax import lax
from jax.experimental import pallas as pl
from jax.experimental.pallas import tpu as pltpu

F32 = jnp.float32
BF16 = jnp.bfloat16

D_MODEL = 1024
POOL_W = 512
SSM_W = 512
POOL_WINDOWS = (2, 4, 8, 16)
POOL_GW = 128
SSM_GROUP = 16
SSM_GROUPS = 32
SSM_STATE = 64
N_EXPERTS = 32
N_EXPERT_GROUPS = 4
EXPERTS_PER_GROUP = 8
D_EXPERT = 512
GRID_W = 64
EPS = 1e-6
N_MOD = 6

LANES = 128
SUBLANES = 8
TM = 256
CH = 8
GB = 8
NGB = SSM_GROUPS // GB
CW = CH * LANES
SW = 2 * GB * SSM_STATE
SR = 256
MOE_BLK = 256
VMEM_LIMIT = 48 * 1024 * 1024


def _cparams(sem):
    return pltpu.CompilerParams(dimension_semantics=sem, vmem_limit_bytes=VMEM_LIMIT)


def _rmsnorm_mod(x, g, shift, scale):
    ms = jnp.mean(x * x, axis=-1, keepdims=True)
    y = x * lax.rsqrt(ms + EPS) * g
    return y * (1.0 + scale) + shift


def _mod_kernel(c_ref, w_ref, b_ref, o_ref):
    c = c_ref[...]
    a = c * jax.nn.sigmoid(c)
    o_ref[...] = jnp.dot(a, w_ref[...], preferred_element_type=F32,
                         precision=lax.Precision.HIGHEST) + b_ref[...]


def _modulation(cvec, w_mod, b_mod):
    depth, d, n = w_mod.shape
    tn = 1536
    return pl.pallas_call(
        _mod_kernel,
        grid=(depth, n // tn),
        in_specs=[
            pl.BlockSpec((SUBLANES, d), lambda l, j: (0, 0)),
            pl.BlockSpec((None, d, tn), lambda l, j: (l, 0, j)),
            pl.BlockSpec((None, 1, tn), lambda l, j: (l, 0, j)),
        ],
        out_specs=pl.BlockSpec((None, SUBLANES, tn), lambda l, j: (l, 0, j)),
        out_shape=jax.ShapeDtypeStruct((depth, SUBLANES, n), F32),
        compiler_params=_cparams(("arbitrary", "arbitrary")),
        name="modulation",
    )(cvec, w_mod, b_mod.reshape(depth, 1, n))


def _inproj_kernel(x_ref, mod_ref, g_ref, w_ref, ux_ref, uv_ref, scr):
    h = _rmsnorm_mod(x_ref[...], g_ref[...], mod_ref[0:1, :], mod_ref[1:2, :])
    ux = jnp.dot(h.astype(BF16), w_ref[...], preferred_element_type=F32)
    ux_ref[...] = ux
    rows = TM // CH
    for b in range(NGB):
        scr[b] = ux[:, POOL_W + LANES * b:POOL_W + LANES * (b + 1)]
        for t in range(CH):
            piece = scr[b, pl.ds(t, rows, stride=CH), :]
            uv_ref[:, pl.ds((b * CH + t) * LANES, LANES)] = piece.astype(BF16)


def _inproj(tok, mod_l, g, w_bf, n_x_tiles):
    t = tok.shape[0]
    nt = t // TM
    sel = lambda i: (jnp.where(i == n_x_tiles, 1, 0), 0, 0)
    return pl.pallas_call(
        _inproj_kernel,
        grid=(nt,),
        in_specs=[
            pl.BlockSpec((TM, D_MODEL), lambda i: (i, 0)),
            pl.BlockSpec((None, N_MOD, D_MODEL), sel),
            pl.BlockSpec((1, D_MODEL), lambda i: (0, 0)),
            pl.BlockSpec((D_MODEL, D_MODEL), lambda i: (0, 0)),
        ],
        out_specs=[
            pl.BlockSpec((TM, D_MODEL), lambda i: (i, 0)),
            pl.BlockSpec((TM // CH, NGB * CW), lambda i: (i, 0)),
        ],
        out_shape=[
            jax.ShapeDtypeStruct((t, D_MODEL), F32),
            jax.ShapeDtypeStruct((t // CH, NGB * CW), BF16),
        ],
        scratch_shapes=[pltpu.VMEM((NGB, TM, LANES), F32)],
        compiler_params=_cparams(("arbitrary",)),
        name="mixer_inproj",
    )(tok, mod_l, g, w_bf)


def _chunk_scan(s_ref, tab_ref, carry_ref, out_ref, nblk, reverse):
    half = SW // 2
    ncol = half // LANES
    row = lax.broadcasted_iota(jnp.int32, (SUBLANES, LANES), 0)
    edge = (row == SUBLANES - 1) if reverse else (row == 0)
    last = 0 if reverse else SUBLANES - 1

    def sub_block(r0, carry):
        outs_r, outs_i, new_carry = [], [], []
        for j in range(ncol):
            cr, ci = carry[2 * j], carry[2 * j + 1]
            lre = pl.ds(LANES * j, LANES)
            lim = pl.ds(half + LANES * j, LANES)
            zr = s_ref[pl.ds(r0, SUBLANES), lre]
            zi = s_ref[pl.ds(r0, SUBLANES), lim]
            for q, k in enumerate((1, 2, 4)):
                ar = tab_ref[pl.ds(16 * q, SUBLANES), lre]
                ai = tab_ref[pl.ds(16 * q + 8, SUBLANES), lre]
                sh = SUBLANES - k if reverse else k
                sr = pltpu.roll(zr, sh, axis=0)
                si = pltpu.roll(zi, sh, axis=0)
                zr, zi = zr + ar * sr - ai * si, zi + ar * si + ai * sr
            pr = tab_ref[pl.ds(48, SUBLANES), lre]
            pi = tab_ref[pl.ds(56, SUBLANES), lre]
            zr, zi = zr + pr * cr - pi * ci, zi + pr * ci + pi * cr
            sh1 = SUBLANES - 1 if reverse else 1
            outs_r.append(jnp.where(edge, cr, pltpu.roll(zr, sh1, axis=0)))
            outs_i.append(jnp.where(edge, ci, pltpu.roll(zi, sh1, axis=0)))
            new_carry.append(jnp.broadcast_to(zr[last:last + 1, :], (SUBLANES, LANES)))
            new_carry.append(jnp.broadcast_to(zi[last:last + 1, :], (SUBLANES, LANES)))
        return outs_r, outs_i, tuple(new_carry)

    def body(it, carry):
        bi = (nblk - 1 - it) if reverse else it
        r0 = pl.multiple_of(bi * 2 * SUBLANES, 2 * SUBLANES)
        if reverse:
            hi_r, hi_i, carry = sub_block(r0 + SUBLANES, carry)
            lo_r, lo_i, carry = sub_block(r0, carry)
        else:
            lo_r, lo_i, carry = sub_block(r0, carry)
            hi_r, hi_i, carry = sub_block(r0 + SUBLANES, carry)
        for j in range(ncol):
            out_ref[pl.ds(r0, 2 * SUBLANES), pl.ds(LANES * j, LANES)] = (
                jnp.concatenate([lo_r[j], hi_r[j]], axis=0).astype(BF16))
            out_ref[pl.ds(r0, 2 * SUBLANES), pl.ds(half + LANES * j, LANES)] = (
                jnp.concatenate([lo_i[j], hi_i[j]], axis=0).astype(BF16))
        return carry

    init = tuple(carry_ref[:, pl.ds(LANES * c, LANES)] for c in range(2 * ncol))
    final = lax.fori_loop(0, nblk, body, init)
    for c in range(2 * ncol):
        carry_ref[:, pl.ds(LANES * c, LANES)] = final[c]


def _ssm_state_kernel(uf_ref, ub_ref, msf_ref, msb_ref, tf_ref, tb_ref, hf_ref, gb_ref,
                      sf, sb, cf, cb, *, ctx_rows):
    step = pl.program_id(1)

    @pl.when(step == 0)
    def _():
        cf[...] = jnp.zeros_like(cf)
        cb[...] = jnp.zeros_like(cb)
        hf_ref[...] = jnp.zeros_like(hf_ref)
        gb_ref[...] = jnp.zeros_like(gb_ref)

    sf[...] = jnp.dot(uf_ref[...], msf_ref[...], preferred_element_type=F32)
    sb[...] = jnp.dot(ub_ref[...], msb_ref[...], preferred_element_type=F32)
    nblk = jnp.where(step == 0, ctx_rows // (2 * SUBLANES), SR // (2 * SUBLANES))
    _chunk_scan(sf, tf_ref, cf, hf_ref, nblk, reverse=False)
    _chunk_scan(sb, tb_ref, cb, gb_ref, nblk, reverse=True)


def _ssm_states(uv_ext, ms, tab, layer, n_xt, ctx_rows):
    rows = uv_ext.shape[0]
    fwd = lambda b, s: (jnp.where(s == 0, n_xt, s - 1), b)
    bwd = lambda b, s: (jnp.where(s == 0, n_xt, n_xt - s), b)
    return pl.pallas_call(
        functools.partial(_ssm_state_kernel, ctx_rows=ctx_rows),
        grid=(NGB, n_xt + 1),
        in_specs=[
            pl.BlockSpec((SR, CW), fwd),
            pl.BlockSpec((SR, CW), bwd),
            pl.BlockSpec((None, None, None, CW, SW), lambda b, s: (layer, b, 0, 0, 0)),
            pl.BlockSpec((None, None, None, CW, SW), lambda b, s: (layer, b, 1, 0, 0)),
            pl.BlockSpec((None, None, None, 8 * SUBLANES, SW // 2),
                         lambda b, s: (layer, 0, b, 0, 0)),
            pl.BlockSpec((None, None, None, 8 * SUBLANES, SW // 2),
                         lambda b, s: (layer, 1, b, 0, 0)),
        ],
        out_specs=[pl.BlockSpec((SR, SW), fwd), pl.BlockSpec((SR, SW), bwd)],
        out_shape=[jax.ShapeDtypeStruct((rows, NGB * SW), BF16)] * 2,
        scratch_shapes=[pltpu.VMEM((SR, SW), F32), pltpu.VMEM((SR, SW), F32),
                        pltpu.VMEM((SUBLANES, SW), F32), pltpu.VMEM((SUBLANES, SW), F32)],
        compiler_params=_cparams(("arbitrary", "arbitrary")),
        name="ssm_states",
    )(uv_ext, uv_ext, ms, ms, tab, tab)


def _ssm_out_kernel(u_ref, hf_ref, gb_ref, m_ref, y_ref):
    res = jnp.dot(u_ref[...], m_ref[0:CW, :], preferred_element_type=F32)
    res += jnp.dot(hf_ref[...], m_ref[CW:CW + SW, :], preferred_element_type=F32)
    res += jnp.dot(gb_ref[...], m_ref[CW + SW:CW + 2 * SW, :], preferred_element_type=F32)
    for t in range(CH):
        y_ref[pl.ds(t, SR, stride=CH), :] = res[:, t * LANES:(t + 1) * LANES]


def _ssm_readout(uv_ext, hf, gb, mcat, layer):
    rows = uv_ext.shape[0]
    nt = rows // SR
    blk = lambda b, i: (i, b)
    return pl.pallas_call(
        _ssm_out_kernel,
        grid=(NGB, nt),
        in_specs=[
            pl.BlockSpec((SR, CW), blk),
            pl.BlockSpec((SR, SW), blk),
            pl.BlockSpec((SR, SW), blk),
            pl.BlockSpec((None, None, CW + 2 * SW, CW), lambda b, i: (layer, b, 0, 0)),
        ],
        out_specs=pl.BlockSpec((SR * CH, LANES), blk),
        out_shape=jax.ShapeDtypeStruct((rows * CH, SSM_W), F32),
        compiler_params=_cparams(("arbitrary", "arbitrary")),
        name="ssm_readout",
    )(uv_ext, hf, gb, mcat)


def _gelu_tanh(x):
    return 0.5 * x * (1.0 + jnp.tanh(0.7978845608028654 * (x + 0.044715 * x * x * x)))


def _route(s, b_router):
    sel = s + b_router
    lane_i = lax.broadcasted_iota(jnp.int32, s.shape, 1)
    grp = lane_i // EXPERTS_PER_GROUP
    lane = lane_i.astype(F32)
    neg = jnp.float32(-jnp.inf)
    big = jnp.float32(N_EXPERTS)

    def top2(vals):
        m1 = jnp.max(vals, axis=-1, keepdims=True)
        i1 = jnp.min(jnp.where(vals == m1, lane, big), axis=-1, keepdims=True)
        rest = jnp.where(lane == i1, neg, vals)
        m2 = jnp.max(rest, axis=-1, keepdims=True)
        i2 = jnp.min(jnp.where(rest == m2, lane, big), axis=-1, keepdims=True)
        return m1, i1, m2, i2

    best = None
    gidx = None
    for g in range(N_EXPERT_GROUPS):
        m1, _, m2, _ = top2(jnp.where(grp == g, sel, neg))
        score = m1 + m2
        if best is None:
            best, gidx = score, jnp.zeros_like(lane_i[:, 0:1])
        else:
            upd = score > best
            best = jnp.where(upd, score, best)
            gidx = jnp.where(upd, g, gidx)
    _, e1, _, e2 = top2(jnp.where(grp == gidx, sel, neg))
    w1 = jnp.sum(jnp.where(lane == e1, s, 0.0), axis=-1, keepdims=True)
    w2 = jnp.sum(jnp.where(lane == e2, s, 0.0), axis=-1, keepdims=True)
    tot = w1 + w2
    return e1.astype(jnp.int32), e2.astype(jnp.int32), w1 / tot, w2 / tot


def _mixout_kernel(x_ref, ux_ref, ys_ref, mod_ref, band_ref, icnt_ref, wp_ref, sp_ref, d_ref,
                   wglu_ref, bglu_ref, wout_ref, gffn_ref, wr_ref, br_ref,
                   xo_ref, h2_ref, eidx_ref, gate_ref):
    ux = ux_ref[...]
    parts = []
    for g in range(len(POOL_WINDOWS)):
        ug = ux[:, g * POOL_GW:(g + 1) * POOL_GW]
        hi = ug.astype(BF16)
        lo = (ug - hi.astype(F32)).astype(BF16)
        band = band_ref[g]
        wsum = (jnp.dot(band, hi, preferred_element_type=F32)
                + jnp.dot(band, lo, preferred_element_type=F32))
        p = wsum * icnt_ref[:, g * POOL_GW:(g + 1) * POOL_GW] - ug
        parts.append(jnp.dot(p.astype(BF16), wp_ref[g], preferred_element_type=F32))
    pool = jnp.concatenate(parts, axis=-1) * sp_ref[...]

    y = ys_ref[...] + d_ref[...] * ux[:, POOL_W:]
    y = _gelu_tanh(y)
    z = jnp.dot(y.astype(BF16), wglu_ref[...], preferred_element_type=F32) + bglu_ref[...]
    glu = y * jax.nn.sigmoid(z)

    cat = jnp.concatenate([pool, glu], axis=-1).astype(BF16)
    o = jnp.dot(cat, wout_ref[...], preferred_element_type=F32)
    xn = x_ref[...] + mod_ref[2:3, :] * o
    xo_ref[...] = xn

    h2 = _rmsnorm_mod(xn, gffn_ref[...], mod_ref[3:4, :], mod_ref[4:5, :])
    h2_ref[...] = h2
    logits = jnp.dot(h2, wr_ref[...], preferred_element_type=F32, precision=lax.Precision.HIGHEST)
    e1, e2, g1, g2 = _route(jax.nn.sigmoid(logits), br_ref[...])
    eidx_ref[...] = jnp.concatenate([e1, e2], axis=-1)
    gate_ref[...] = jnp.concatenate([g1, g2], axis=-1)


def _mixout(tok, ux, y_ssm, mod_l, band, icnt, wp_bf, sp, dvec, wglu_bf, bglu, wout_bf, gffn,
            w_router, b_router, nt, n_x_tiles):
    t = nt * TM
    sel = lambda i: (jnp.where(i == n_x_tiles, 1, 0), 0, 0)
    sel4 = lambda i: (jnp.where(i == n_x_tiles, 1, 0), 0, 0, 0)
    row = lambda i: (i, 0)
    fix2 = lambda i: (0, 0)
    return pl.pallas_call(
        _mixout_kernel,
        grid=(nt,),
        in_specs=[
            pl.BlockSpec((TM, D_MODEL), row),
            pl.BlockSpec((TM, D_MODEL), row),
            pl.BlockSpec((TM, SSM_W), row),
            pl.BlockSpec((None, N_MOD, D_MODEL), sel),
            pl.BlockSpec((None, len(POOL_WINDOWS), TM, TM), sel4),
            pl.BlockSpec((None, TM, POOL_W), sel),
            pl.BlockSpec((len(POOL_WINDOWS), POOL_GW, POOL_GW), lambda i: (0, 0, 0)),
            pl.BlockSpec((1, POOL_W), fix2),
            pl.BlockSpec((1, SSM_W), fix2),
            pl.BlockSpec((SSM_W, SSM_W), fix2),
            pl.BlockSpec((1, SSM_W), fix2),
            pl.BlockSpec((D_MODEL, D_MODEL), fix2),
            pl.BlockSpec((1, D_MODEL), fix2),
            pl.BlockSpec((D_MODEL, N_EXPERTS), fix2),
            pl.BlockSpec((1, N_EXPERTS), fix2),
        ],
        out_specs=[
            pl.BlockSpec((TM, D_MODEL), row),
            pl.BlockSpec((TM, D_MODEL), row),
            pl.BlockSpec((TM, 2), row),
            pl.BlockSpec((TM, 2), row),
        ],
        out_shape=[
            jax.ShapeDtypeStruct((t, D_MODEL), F32),
            jax.ShapeDtypeStruct((t, D_MODEL), F32),
            jax.ShapeDtypeStruct((t, 2), jnp.int32),
            jax.ShapeDtypeStruct((t, 2), F32),
        ],
        compiler_params=_cparams(("arbitrary",)),
        name="mixer_out_router",
    )(tok, ux, y_ssm, mod_l, band, icnt, wp_bf, sp, dvec, wglu_bf, bglu, wout_bf, gffn,
      w_router, b_router)


def _rank_kernel(eidx_ref, tri_ref, rank_ref, cnt_ref, run):
    i = pl.program_id(0)

    @pl.when(i == 0)
    def _():
        run[...] = jnp.zeros_like(run)

    e = eidx_ref[...]
    lane = lax.broadcasted_iota(jnp.int32, (TM, N_EXPERTS), 1)
    oh0 = lane == e[:, 0:1]
    oh1 = lane == e[:, 1:2]
    tri = tri_ref[...]
    before0 = jnp.dot(tri, jnp.where(oh0, 1.0, 0.0).astype(BF16), preferred_element_type=F32)
    before1 = jnp.dot(tri, jnp.where(oh1, 1.0, 0.0).astype(BF16), preferred_element_type=F32)
    tot0 = jnp.sum(jnp.where(oh0, 1.0, 0.0), axis=0, keepdims=True)
    tot1 = jnp.sum(jnp.where(oh1, 1.0, 0.0), axis=0, keepdims=True)
    base = run[...]
    r0 = jnp.sum(jnp.where(oh0, base + before0, 0.0), axis=-1, keepdims=True)
    r1 = jnp.sum(jnp.where(oh1, base + tot0 + before1, 0.0), axis=-1, keepdims=True)
    rank_ref[...] = jnp.concatenate([r0, r1], axis=-1).astype(jnp.int32)
    run[...] = base + tot0 + tot1
    cnt_ref[...] = (base + tot0 + tot1).astype(jnp.int32)


def _ranks(eidx, tri, nt):
    return pl.pallas_call(
        _rank_kernel,
        grid=(nt,),
        in_specs=[pl.BlockSpec((TM, 2), lambda i: (i, 0)),
                  pl.BlockSpec((TM, TM), lambda i: (0, 0))],
        out_specs=[pl.BlockSpec((TM, 2), lambda i: (i, 0)),
                   pl.BlockSpec((1, N_EXPERTS), lambda i: (0, 0))],
        out_shape=[jax.ShapeDtypeStruct((nt * TM, 2), jnp.int32),
                   jax.ShapeDtypeStruct((1, N_EXPERTS), jnp.int32)],
        scratch_shapes=[pltpu.VMEM((1, N_EXPERTS), F32)],
        compiler_params=_cparams(("arbitrary",)),
        name="moe_ranks",
    )(eidx, tri)


def _row_copy(src, s, dst, d, sem):
    return pltpu.make_async_copy(src.at[pl.ds(s, 1)], dst.at[pl.ds(d, 1)], sem)


def _dispatch_kernel(dest_ref, h_ref, zero_ref, buf_ref, sem):
    del zero_ref

    def issue(r, c):
        for k in range(2):
            _row_copy(h_ref, r, buf_ref, dest_ref[0, 2 * r + k], sem).start()
        return c

    lax.fori_loop(0, TM, issue, 0)

    def drain(r, c):
        for k in range(2):
            _row_copy(h_ref, r, buf_ref, dest_ref[0, 2 * r + k], sem).wait()
        return c

    lax.fori_loop(0, TM, drain, 0)


def _dispatch(dest3, h2, zeros_buf, nt):
    return pl.pallas_call(
        _dispatch_kernel,
        grid=(nt,),
        in_specs=[
            pl.BlockSpec((None, 1, 2 * TM), lambda i: (i, 0, 0), memory_space=pltpu.SMEM),
            pl.BlockSpec((TM, D_MODEL), lambda i: (i, 0)),
            pl.BlockSpec(memory_space=pl.ANY),
        ],
        out_specs=pl.BlockSpec(memory_space=pl.ANY),
        out_shape=jax.ShapeDtypeStruct(zeros_buf.shape, F32),
        scratch_shapes=[pltpu.SemaphoreType.DMA(())],
        input_output_aliases={2: 0},
        compiler_params=_cparams(("arbitrary",)),
        name="moe_dispatch",
    )(dest3, h2, zeros_buf)


def _expert_kernel(be_ref, nbu_ref, x_ref, wg_ref, wu_ref, wd_ref, o_ref, wg_s, wu_s, wd_s):
    i = pl.program_id(0)
    changed = jnp.logical_or(i == 0, be_ref[i] != be_ref[jnp.maximum(i - 1, 0)])

    @pl.when(changed)
    def _():
        wg_s[...] = wg_ref[...].astype(BF16)
        wu_s[...] = wu_ref[...].astype(BF16)
        wd_s[...] = wd_ref[...].astype(BF16)

    @pl.when(i < nbu_ref[0])
    def _():
        xb = x_ref[...].astype(BF16)
        g = jnp.dot(xb, wg_s[...], preferred_element_type=F32)
        u = jnp.dot(xb, wu_s[...], preferred_element_type=F32)
        hid = (g * jax.nn.sigmoid(g)) * u
        o_ref[...] = jnp.dot(hid.astype(BF16), wd_s[...], preferred_element_type=F32)

    @pl.when(i >= nbu_ref[0])
    def _():
        o_ref[...] = jnp.zeros_like(o_ref)


def _experts(blk_e, nb_used, buf, w_gate, w_up, w_down, layer):
    rows = buf.shape[0]
    nb = rows // MOE_BLK
    xmap = lambda i, be, nbu: (jnp.minimum(i, nbu[0] - 1), 0)
    wmap = lambda i, be, nbu: (layer, be[i], 0, 0)
    return pl.pallas_call(
        _expert_kernel,
        grid_spec=pltpu.PrefetchScalarGridSpec(
            num_scalar_prefetch=2,
            grid=(nb,),
            in_specs=[
                pl.BlockSpec((MOE_BLK, D_MODEL), xmap),
                pl.BlockSpec((None, None, D_MODEL, D_EXPERT), wmap),
                pl.BlockSpec((None, None, D_MODEL, D_EXPERT), wmap),
                pl.BlockSpec((None, None, D_EXPERT, D_MODEL), wmap),
            ],
            out_specs=pl.BlockSpec((MOE_BLK, D_MODEL), lambda i, be, nbu: (i, 0)),
            scratch_shapes=[pltpu.VMEM((D_MODEL, D_EXPERT), BF16),
                            pltpu.VMEM((D_MODEL, D_EXPERT), BF16),
                            pltpu.VMEM((D_EXPERT, D_MODEL), BF16)],
        ),
        out_shape=jax.ShapeDtypeStruct((rows, D_MODEL), F32),
        compiler_params=_cparams(("arbitrary",)),
        name="moe_experts",
    )(blk_e, nb_used, buf, w_gate, w_up, w_down)


def _combine_kernel(dest_ref, x_ref, gate_ref, mod_ref, gfin_ref, o_hbm, out_ref, gat, sem,
                    *, final):
    def issue(r, c):
        for k in range(2):
            _row_copy(o_hbm, dest_ref[0, 2 * r + k], gat.at[k], r, sem).start()
        return c

    lax.fori_loop(0, TM, issue, 0)

    def drain(r, c):
        for k in range(2):
            _row_copy(o_hbm, dest_ref[0, 2 * r + k], gat.at[k], r, sem).wait()
        return c

    lax.fori_loop(0, TM, drain, 0)

    gate = gate_ref[...]
    y = gate[:, 0:1] * gat[0] + gate[:, 1:2] * gat[1]
    xn = x_ref[...] + mod_ref[5:6, :] * y
    if final:
        ms = jnp.mean(xn * xn, axis=-1, keepdims=True)
        xn = xn * lax.rsqrt(ms + EPS) * gfin_ref[...]
    out_ref[...] = xn


def _combine(dest3, xn, gate, mod_l, g_final, expert_out, nt, n_x_tiles, final):
    sel = lambda i: (jnp.where(i == n_x_tiles, 1, 0), 0, 0)
    return pl.pallas_call(
        functools.partial(_combine_kernel, final=final),
        grid=(nt,),
        in_specs=[
            pl.BlockSpec((None, 1, 2 * TM), lambda i: (i, 0, 0), memory_space=pltpu.SMEM),
            pl.BlockSpec((TM, D_MODEL), lambda i: (i, 0)),
            pl.BlockSpec((TM, 2), lambda i: (i, 0)),
            pl.BlockSpec((None, N_MOD, D_MODEL), sel),
            pl.BlockSpec((1, D_MODEL), lambda i: (0, 0)),
            pl.BlockSpec(memory_space=pl.ANY),
        ],
        out_specs=pl.BlockSpec((TM, D_MODEL), lambda i: (i, 0)),
        out_shape=jax.ShapeDtypeStruct((nt * TM, D_MODEL), F32),
        scratch_shapes=[pltpu.VMEM((2, TM, D_MODEL), F32), pltpu.SemaphoreType.DMA(())],
        compiler_params=_cparams(("arbitrary",)),
        name="moe_combine",
    )(dest3, xn, gate, mod_l, g_final, expert_out)


def _moe(h2, eidx, gate, xn, mod_l, g_final, w_gate, w_up, w_down, layer, nt, n_x_tiles, final,
         tri):
    t = nt * TM
    rank, counts = _ranks(eidx, tri, nt)
    counts = counts[0]
    padded = (counts + MOE_BLK - 1) // MOE_BLK * MOE_BLK
    pend = jnp.cumsum(padded)
    pstart = pend - padded
    dest = pstart[eidx] + rank
    nb = (2 * t + N_EXPERTS * (MOE_BLK - 1)) // MOE_BLK + 1
    nb_used = (pend[-1] // MOE_BLK).astype(jnp.int32)
    blk_start = jnp.minimum(jnp.arange(nb, dtype=jnp.int32), nb_used - 1) * MOE_BLK
    blk_e = jnp.minimum(jnp.sum(pend[None, :] <= blk_start[:, None], axis=1),
                        N_EXPERTS - 1).astype(jnp.int32)
    dest3 = dest.reshape(nt, 1, 2 * TM)
    buf = _dispatch(dest3, h2, jnp.zeros((nb * MOE_BLK, D_MODEL), F32), nt)
    eo = _experts(blk_e, nb_used.reshape(1), buf, w_gate, w_up, w_down, layer)
    return _combine(dest3, xn, gate, mod_l, g_final, eo, nt, n_x_tiles, final)


def _cmul(ar, ai, br, bi):
    return ar * br - ai * bi, ar * bi + ai * br


def _expand_rows(v, rep_ref, mask_ref):
    return jnp.dot(v.astype(BF16), rep_ref[...], preferred_element_type=F32) * mask_ref[...]


def _expand_cols(e, rept_ref, maskt_ref):
    out = lax.dot_general(rept_ref[...], e.astype(BF16), (((1,), (1,)), ((), ())),
                          preferred_element_type=F32)
    return out * maskt_ref[...]


def _ssm_prep_kernel(are_ref, aim_ref, ldt_ref, btr_ref, bti_ref, cr_ref, ci_ref,
                     rep_ref, mask_ref, rept_ref, maskt_ref, ms_ref, mcat_ref, ecat):
    lag_blocks = []
    for d in range(2):
        a_re = jnp.minimum(are_ref[d], -1e-4)
        a_im = aim_ref[d]
        dt = jnp.exp(ldt_ref[d])
        mag = jnp.exp(a_re * dt)
        lr = mag * jnp.cos(a_im * dt)
        li = mag * jnp.sin(a_im * dt)
        den = a_re * a_re + a_im * a_im
        k_re = ((lr - 1.0) * a_re + li * a_im) / den
        k_im = (li * a_re - (lr - 1.0) * a_im) / den
        bbr, bbi = _cmul(k_re, k_im, btr_ref[d], bti_ref[d])
        cr, ci = cr_ref[d], ci_ref[d]
        pr, pi = jnp.ones_like(lr), jnp.zeros_like(lr)
        v0 = None
        for n in range(CH + 1):
            er, ei = _cmul(pr, pi, cr, ci)
            et = jnp.concatenate([_expand_cols(er, rept_ref, maskt_ref),
                                  _expand_cols(-ei, rept_ref, maskt_ref)], axis=0).astype(BF16)
            if n < CH:
                vr, vi = _cmul(pr, pi, bbr, bbi)
                v = jnp.concatenate([_expand_rows(vr, rep_ref, mask_ref),
                                     _expand_rows(vi, rep_ref, mask_ref)], axis=1).astype(BF16)
                if n == 0:
                    v0 = v
                sigma = CH - 1 - n if d == 0 else n
                ms_ref[d, sigma * LANES:(sigma + 1) * LANES, :] = v
                ecat[:, n * LANES:(n + 1) * LANES] = et
            if n >= 1:
                tau = n - 1 if d == 0 else CH - n
                mcat_ref[CW + d * SW:CW + (d + 1) * SW, tau * LANES:(tau + 1) * LANES] = et
            pr, pi = _cmul(pr, pi, lr, li)
        lag_blocks.append(jnp.dot(v0, ecat[...], preferred_element_type=F32))
    kf, kb = lag_blocks
    for s in range(CH):
        for t in range(CH):
            if t > s:
                blk = kf[:, (t - s) * LANES:(t - s + 1) * LANES]
            elif t < s:
                blk = kb[:, (s - t) * LANES:(s - t + 1) * LANES]
            else:
                blk = kf[:, :LANES] + kb[:, :LANES]
            mcat_ref[s * LANES:(s + 1) * LANES, t * LANES:(t + 1) * LANES] = blk.astype(BF16)


def _ssm_operators(a_re, a_im, log_dt, b_re, b_im, c_re, c_im):
    depth = a_re.shape[0]
    rows = SSM_GROUPS * SSM_GROUP
    p = SSM_STATE
    rep_rows = lambda v: jnp.repeat(v, SSM_GROUP, axis=2)
    are = rep_rows(a_re)
    aim = rep_rows(a_im)
    ldt = jnp.broadcast_to(rep_rows(log_dt[..., None]), are.shape)
    btr = jnp.swapaxes(b_re, -1, -2).reshape(depth, 2, rows, p)
    bti = jnp.swapaxes(b_im, -1, -2).reshape(depth, 2, rows, p)
    cr = c_re.reshape(depth, 2, rows, p)
    ci = c_im.reshape(depth, 2, rows, p)
    rep = np.tile(np.eye(p, dtype=np.float32), (1, GB))
    mask = np.kron(np.eye(GB, dtype=np.float32), np.ones((SSM_GROUP, p), np.float32))
    par = pl.BlockSpec((None, 2, LANES, p), lambda l, b: (l, 0, b, 0))
    fix = lambda shape: pl.BlockSpec(shape, lambda l, b: (0, 0))
    return pl.pallas_call(
        _ssm_prep_kernel,
        grid=(depth, NGB),
        in_specs=[par] * 7 + [fix((p, GB * p)), fix((LANES, GB * p)),
                              fix((GB * p, p)), fix((GB * p, LANES))],
        out_specs=[
            pl.BlockSpec((None, None, 2, CW, SW), lambda l, b: (l, b, 0, 0, 0)),
            pl.BlockSpec((None, None, CW + 2 * SW, CW), lambda l, b: (l, b, 0, 0)),
        ],
        out_shape=[
            jax.ShapeDtypeStruct((depth, NGB, 2, CW, SW), BF16),
            jax.ShapeDtypeStruct((depth, NGB, CW + 2 * SW, CW), BF16),
        ],
        scratch_shapes=[pltpu.VMEM((SW, CW), BF16)],
        compiler_params=_cparams(("arbitrary", "arbitrary")),
        name="ssm_operators",
    )(are, aim, ldt, btr, bti, cr, ci, jnp.asarray(rep, BF16), jnp.asarray(mask),
      jnp.asarray(rep.T, BF16), jnp.asarray(mask.T))


def _scan_tables(a_re, a_im, log_dt):
    depth = a_re.shape[0]
    a_re = jnp.minimum(a_re, -1e-4)
    dt = jnp.exp(log_dt)[..., None]
    mag = jnp.exp(a_re * dt)
    lr = mag * jnp.cos(a_im * dt)
    li = mag * jnp.sin(a_im * dt)
    ar, ai = lr, li
    for _ in range(CH - 1):
        ar, ai = _cmul(ar, ai, lr, li)
    apw = [(ar, ai)]
    for _ in range(SUBLANES - 1):
        apw.append(_cmul(apw[-1][0], apw[-1][1], ar, ai))
    apr = jnp.stack([q[0] for q in apw], axis=1)
    api = jnp.stack([q[1] for q in apw], axis=1)
    rows = np.arange(SUBLANES)
    tabs = []
    for d in range(2):
        consts = []
        for k in (1, 2, 4):
            keep = (rows >= k) if d == 0 else (rows < SUBLANES - k)
            keep = jnp.asarray(keep.astype(np.float32))[None, :, None, None]
            consts.append(keep * apr[:, k - 1:k, d])
            consts.append(keep * api[:, k - 1:k, d])
        order = rows if d == 0 else SUBLANES - 1 - rows
        consts.append(apr[:, order, d])
        consts.append(api[:, order, d])
        tabs.append(jnp.stack(consts, axis=1))
    tab = jnp.stack(tabs, axis=1).reshape(depth, 2, 8 * SUBLANES, NGB, GB * SSM_STATE)
    return tab.transpose(0, 1, 3, 2, 4)


def _pool_constants(seg):
    t = np.arange(TM)
    s0 = t // seg * seg
    band = np.zeros((len(POOL_WINDOWS), TM, TM), np.float32)
    icnt = np.zeros((TM, POOL_W), np.float32)
    for g, w in enumerate(POOL_WINDOWS):
        lo = np.maximum(t - w // 2, s0)
        hi = np.minimum(t + w // 2, s0 + seg)
        band[g] = (t[None, :] >= lo[:, None]) & (t[None, :] < hi[:, None])
        icnt[:, g * POOL_GW:(g + 1) * POOL_GW] = (1.0 / (hi - lo).astype(np.float32))[:, None]
    return band, icnt


def kernel(x, c, ctx, c_ctx, w_mod, b_mod, g_mix, g_ffn, w_in, w_out, w_pool, s_pool,
           ssm_a_re, ssm_a_im, ssm_log_dt, ssm_b_re, ssm_b_im, ssm_c_re, ssm_c_im, ssm_d,
           w_glu, b_glu, w_router, b_router, w_gate, w_up, w_down, g_final):
    bsz, seq, d = x.shape
    ctx_len = ctx.shape[1]
    depth = w_mod.shape[0]
    assert bsz == 1 and d == D_MODEL and ctx_len == TM
    assert seq % (SR * CH) == 0 and seq % GRID_W == 0
    n_xt = seq // TM
    n_st = seq // (SR * CH)
    ctx_rows = ctx_len // CH

    cvec = jnp.concatenate([c.reshape(1, d), c_ctx.reshape(1, d),
                            jnp.zeros((SUBLANES - 2, d), F32)], axis=0)
    mod = _modulation(cvec, w_mod, b_mod).reshape(depth, SUBLANES, N_MOD, d)

    band_x, icnt_x = _pool_constants(GRID_W)
    band_c, icnt_c = _pool_constants(ctx_len)
    band = jnp.asarray(np.stack([band_x, band_c]), BF16)
    icnt = jnp.asarray(np.stack([icnt_x, icnt_c]), F32)
    tri = jnp.asarray(np.tril(np.ones((TM, TM), np.float32), -1), BF16)

    ms, mcat = _ssm_operators(ssm_a_re, ssm_a_im, ssm_log_dt, ssm_b_re, ssm_b_im,
                              ssm_c_re, ssm_c_im)
    tab = _scan_tables(ssm_a_re, ssm_a_im, ssm_log_dt)

    tok = jnp.concatenate([x[0], ctx[0]], axis=0)
    for l in range(depth):
        last = l == depth - 1
        mod_l = mod[l, :2]
        ux, uv = _inproj(tok, mod_l, g_mix[l].reshape(1, d), w_in[l].astype(BF16), n_xt)
        uv_ext = jnp.pad(uv, ((0, SR - ctx_rows), (0, 0)))
        hf, gb = _ssm_states(uv_ext, ms, tab, l, n_st, ctx_rows)
        y_ssm = _ssm_readout(uv_ext, hf, gb, mcat, l)
        nt = n_xt if last else n_xt + 1
        xn, h2, eidx, gate = _mixout(
            tok, ux, y_ssm, mod_l, band, icnt, w_pool[l].astype(BF16), s_pool[l].reshape(1, -1),
            ssm_d[l].reshape(1, -1), w_glu[l].astype(BF16), b_glu[l].reshape(1, -1),
            w_out[l].astype(BF16), g_ffn[l].reshape(1, d), w_router, b_router.reshape(1, -1),
            nt, n_xt)
        tok = _moe(h2, eidx, gate, xn, mod_l, g_final.reshape(1, d), w_gate, w_up, w_down, l,
                   nt, n_xt, last, tri)
    return tok.reshape(bsz, seq, d)
```

```python
import functools

import numpy as np
import jax
import jax.numpy as jnp
from jax import lax
from jax.experimental import pallas as pl
from jax.experimental.pallas import tpu as pltpu

F32 = jnp.float32
BF16 = jnp.bfloat16

D_MODEL = 1024
POOL_W = 512
SSM_W = 512
POOL_WINDOWS = (2, 4, 8, 16)
POOL_GW = 128
SSM_GROUP = 16
SSM_GROUPS = 32
SSM_STATE = 64
N_EXPERTS = 32
N_EXPERT_GROUPS = 4
EXPERTS_PER_GROUP = 8
D_EXPERT = 512
GRID_W = 64
EPS = 1e-6
N_MOD = 6

LANES = 128
SUBLANES = 8
TM = 256
CH = 8
GB = 8
NGB = SSM_GROUPS // GB
CW = CH * LANES
SW = 2 * GB * SSM_STATE
SR = 256
MOE_BLK = 256
VMEM_LIMIT = 48 * 1024 * 1024


def _cparams(sem):
    return pltpu.CompilerParams(dimension_semantics=sem, vmem_limit_bytes=VMEM_LIMIT)


def _rmsnorm_mod(x, g, shift, scale):
    ms = jnp.mean(x * x, axis=-1, keepdims=True)
    y = x * lax.rsqrt(ms + EPS) * g
    return y * (1.0 + scale) + shift


def _mod_kernel(c_ref, w_ref, b_ref, o_ref):
    c = c_ref[...]
    a = c * jax.nn.sigmoid(c)
    o_ref[...] = jnp.dot(a, w_ref[...], preferred_element_type=F32,
                         precision=lax.Precision.HIGHEST) + b_ref[...]


def _modulation(cvec, w_mod, b_mod):
    depth, d, n = w_mod.shape
    tn = 1536
    return pl.pallas_call(
        _mod_kernel,
        grid=(depth, n // tn),
        in_specs=[
            pl.BlockSpec((SUBLANES, d), lambda l, j: (0, 0)),
            pl.BlockSpec((None, d, tn), lambda l, j: (l, 0, j)),
            pl.BlockSpec((None, 1, tn), lambda l, j: (l, 0, j)),
        ],
        out_specs=pl.BlockSpec((None, SUBLANES, tn), lambda l, j: (l, 0, j)),
        out_shape=jax.ShapeDtypeStruct((depth, SUBLANES, n), F32),
        compiler_params=_cparams(("arbitrary", "arbitrary")),
        name="modulation",
    )(cvec, w_mod, b_mod.reshape(depth, 1, n))


def _inproj_kernel(x_ref, mod_ref, g_ref, w_ref, ux_ref, uv_ref, scr):
    h = _rmsnorm_mod(x_ref[...], g_ref[...], mod_ref[0:1, :], mod_ref[1:2, :])
    ux = jnp.dot(h.astype(BF16), w_ref[...], preferred_element_type=F32)
    ux_ref[...] = ux
    rows = TM // CH
    for b in range(NGB):
        scr[b] = ux[:, POOL_W + LANES * b:POOL_W + LANES * (b + 1)]
        for t in range(CH):
            piece = scr[b, pl.ds(t, rows, stride=CH), :]
            uv_ref[:, pl.ds((b * CH + t) * LANES, LANES)] = piece.astype(BF16)


def _inproj(tok, mod_l, g, w_bf, n_x_tiles):
    t = tok.shape[0]
    nt = t // TM
    sel = lambda i: (jnp.where(i == n_x_tiles, 1, 0), 0, 0)
    return pl.pallas_call(
        _inproj_kernel,
        grid=(nt,),
        in_specs=[
            pl.BlockSpec((TM, D_MODEL), lambda i: (i, 0)),
            pl.BlockSpec((None, N_MOD, D_MODEL), sel),
            pl.BlockSpec((1, D_MODEL), lambda i: (0, 0)),
            pl.BlockSpec((D_MODEL, D_MODEL), lambda i: (0, 0)),
        ],
        out_specs=[
            pl.BlockSpec((TM, D_MODEL), lambda i: (i, 0)),
            pl.BlockSpec((TM // CH, NGB * CW), lambda i: (i, 0)),
        ],
        out_shape=[
            jax.ShapeDtypeStruct((t, D_MODEL), F32),
            jax.ShapeDtypeStruct((t // CH, NGB * CW), BF16),
        ],
        scratch_shapes=[pltpu.VMEM((NGB, TM, LANES), F32)],
        compiler_params=_cparams(("arbitrary",)),
        name="mixer_inproj",
    )(tok, mod_l, g, w_bf)


def _chunk_scan(s_ref, tab_ref, carry_ref, out_ref, nblk, reverse):
    half = SW // 2
    ncol = half // LANES
    row = lax.broadcasted_iota(jnp.int32, (SUBLANES, LANES), 0)
    edge = (row == SUBLANES - 1) if reverse else (row == 0)
    last = 0 if reverse else SUBLANES - 1

    def sub_block(r0, carry):
        outs_r, outs_i, new_carry = [], [], []
        for j in range(ncol):
            cr, ci = carry[2 * j], carry[2 * j + 1]
            lre = pl.ds(LANES * j, LANES)
            lim = pl.ds(half + LANES * j, LANES)
            zr = s_ref[pl.ds(r0, SUBLANES), lre]
            zi = s_ref[pl.ds(r0, SUBLANES), lim]
            for q, k in enumerate((1, 2, 4)):
                ar = tab_ref[pl.ds(16 * q, SUBLANES), lre]
                ai = tab_ref[pl.ds(16 * q + 8, SUBLANES), lre]
                sh = SUBLANES - k if reverse else k
                sr = pltpu.roll(zr, sh, axis=0)
                si = pltpu.roll(zi, sh, axis=0)
                zr, zi = zr + ar * sr - ai * si, zi + ar * si + ai * sr
            pr = tab_ref[pl.ds(48, SUBLANES), lre]
            pi = tab_ref[pl.ds(56, SUBLANES), lre]
            zr, zi = zr + pr * cr - pi * ci, zi + pr * ci + pi * cr
            sh1 = SUBLANES - 1 if reverse else 1
            outs_r.append(jnp.where(edge, cr, pltpu.roll(zr, sh1, axis=0)))
            outs_i.append(jnp.where(edge, ci, pltpu.roll(zi, sh1, axis=0)))
            new_carry.append(jnp.broadcast_to(zr[last:last + 1, :], (SUBLANES, LANES)))
            new_carry.append(jnp.broadcast_to(zi[last:last + 1, :], (SUBLANES, LANES)))
        return outs_r, outs_i, tuple(new_carry)

    def body(it, carry):
        bi = (nblk - 1 - it) if reverse else it
        r0 = pl.multiple_of(bi * 2 * SUBLANES, 2 * SUBLANES)
        if reverse:
            hi_r, hi_i, carry = sub_block(r0 + SUBLANES, carry)
            lo_r, lo_i, carry = sub_block(r0, carry)
        else:
            lo_r, lo_i, carry = sub_block(r0, carry)
            hi_r, hi_i, carry = sub_block(r0 + SUBLANES, carry)
        for j in range(ncol):
            out_ref[pl.ds(r0, 2 * SUBLANES), pl.ds(LANES * j, LANES)] = (
                jnp.concatenate([lo_r[j], hi_r[j]], axis=0).astype(BF16))
            out_ref[pl.ds(r0, 2 * SUBLANES), pl.ds(half + LANES * j, LANES)] = (
                jnp.concatenate([lo_i[j], hi_i[j]], axis=0).astype(BF16))
        return carry

    init = tuple(carry_ref[:, pl.ds(LANES * c, LANES)] for c in range(2 * ncol))
    final = lax.fori_loop(0, nblk, body, init)
    for c in range(2 * ncol):
        carry_ref[:, pl.ds(LANES * c, LANES)] = final[c]


def _ssm_state_kernel(uf_ref, ub_ref, msf_ref, msb_ref, tf_ref, tb_ref, hf_ref, gb_ref,
                      sf, sb, cf, cb, *, ctx_rows):
    step = pl.program_id(1)

    @pl.when(step == 0)
    def _():
        cf[...] = jnp.zeros_like(cf)
        cb[...] = jnp.zeros_like(cb)
        hf_ref[...] = jnp.zeros_like(hf_ref)
        gb_ref[...] = jnp.zeros_like(gb_ref)

    sf[...] = jnp.dot(uf_ref[...], msf_ref[...], preferred_element_type=F32)
    sb[...] = jnp.dot(ub_ref[...], msb_ref[...], preferred_element_type=F32)
    nblk = jnp.where(step == 0, ctx_rows // (2 * SUBLANES), SR // (2 * SUBLANES))
    _chunk_scan(sf, tf_ref, cf, hf_ref, nblk, reverse=False)
    _chunk_scan(sb, tb_ref, cb, gb_ref, nblk, reverse=True)


def _ssm_states(uv_ext, ms, tab, layer, n_xt, ctx_rows):
    rows = uv_ext.shape[0]
    fwd = lambda b, s: (jnp.where(s == 0, n_xt, s - 1), b)
    bwd = lambda b, s: (jnp.where(s == 0, n_xt, n_xt - s), b)
    return pl.pallas_call(
        functools.partial(_ssm_state_kernel, ctx_rows=ctx_rows),
        grid=(NGB, n_xt + 1),
        in_specs=[
            pl.BlockSpec((SR, CW), fwd),
            pl.BlockSpec((SR, CW), bwd),
            pl.BlockSpec((None, None, None, CW, SW), lambda b, s: (layer, b, 0, 0, 0)),
            pl.BlockSpec((None, None, None, CW, SW), lambda b, s: (layer, b, 1, 0, 0)),
            pl.BlockSpec((None, None, None, 8 * SUBLANES, SW // 2),
                         lambda b, s: (layer, 0, b, 0, 0)),
            pl.BlockSpec((None, None, None, 8 * SUBLANES, SW // 2),
                         lambda b, s: (layer, 1, b, 0, 0)),
        ],
        out_specs=[pl.BlockSpec((SR, SW), fwd), pl.BlockSpec((SR, SW), bwd)],
        out_shape=[jax.ShapeDtypeStruct((rows, NGB * SW), BF16)] * 2,
        scratch_shapes=[pltpu.VMEM((SR, SW), F32), pltpu.VMEM((SR, SW), F32),
                        pltpu.VMEM((SUBLANES, SW), F32), pltpu.VMEM((SUBLANES, SW), F32)],
        compiler_params=_cparams(("arbitrary", "arbitrary")),
        name="ssm_states",
    )(uv_ext, uv_ext, ms, ms, tab, tab)


def _ssm_out_kernel(u_ref, hf_ref, gb_ref, m_ref, y_ref):
    res = jnp.dot(u_ref[...], m_ref[0:CW, :], preferred_element_type=F32)
    res += jnp.dot(hf_ref[...], m_ref[CW:CW + SW, :], preferred_element_type=F32)
    res += jnp.dot(gb_ref[...], m_ref[CW + SW:CW + 2 * SW, :], preferred_element_type=F32)
    for t in range(CH):
        y_ref[pl.ds(t, SR, stride=CH), :] = res[:, t * LANES:(t + 1) * LANES]


def _ssm_readout(uv_ext, hf, gb, mcat, layer):
    rows = uv_ext.shape[0]
    nt = rows // SR
    blk = lambda b, i: (i, b)
    return pl.pallas_call(
        _ssm_out_kernel,
        grid=(NGB, nt),
        in_specs=[
            pl.BlockSpec((SR, CW), blk),
            pl.BlockSpec((SR, SW), blk),
            pl.BlockSpec((SR, SW), blk),
            pl.BlockSpec((None, None, CW + 2 * SW, CW), lambda b, i: (layer, b, 0, 0)),
        ],
        out_specs=pl.BlockSpec((SR * CH, LANES), blk),
        out_shape=jax.ShapeDtypeStruct((rows * CH, SSM_W), F32),
        compiler_params=_cparams(("arbitrary", "arbitrary")),
        name="ssm_readout",
    )(uv_ext, hf, gb, mcat)


def _gelu_tanh(x):
    return 0.5 * x * (1.0 + jnp.tanh(0.7978845608028654 * (x + 0.044715 * x * x * x)))


def _route(s, b_router):
    sel = s + b_router
    lane_i = lax.broadcasted_iota(jnp.int32, s.shape, 1)
    grp = lane_i // EXPERTS_PER_GROUP
    lane = lane_i.astype(F32)
    neg = jnp.float32(-jnp.inf)
    big = jnp.float32(N_EXPERTS)

    def top2(vals):
        m1 = jnp.max(vals, axis=-1, keepdims=True)
        i1 = jnp.min(jnp.where(vals == m1, lane, big), axis=-1, keepdims=True)
        rest = jnp.where(lane == i1, neg, vals)
        m2 = jnp.max(rest, axis=-1, keepdims=True)
        i2 = jnp.min(jnp.where(rest == m2, lane, big), axis=-1, keepdims=True)
        return m1, i1, m2, i2

    best = None
    gidx = None
    for g in range(N_EXPERT_GROUPS):
        m1, _, m2, _ = top2(jnp.where(grp == g, sel, neg))
        score = m1 + m2
        if best is None:
            best, gidx = score, jnp.zeros_like(lane_i[:, 0:1])
        else:
            upd = score > best
            best = jnp.where(upd, score, best)
            gidx = jnp.where(upd, g, gidx)
    _, e1, _, e2 = top2(jnp.where(grp == gidx, sel, neg))
    w1 = jnp.sum(jnp.where(lane == e1, s, 0.0), axis=-1, keepdims=True)
    w2 = jnp.sum(jnp.where(lane == e2, s, 0.0), axis=-1, keepdims=True)
    tot = w1 + w2
    return e1.astype(jnp.int32), e2.astype(jnp.int32), w1 / tot, w2 / tot


def _mixout_kernel(x_ref, ux_ref, ys_ref, mod_ref, band_ref, icnt_ref, wp_ref, sp_ref, d_ref,
                   wglu_ref, bglu_ref, wout_ref, gffn_ref, wr_ref, br_ref,
                   xo_ref, h2_ref, eidx_ref, gate_ref):
    ux = ux_ref[...]
    parts = []
    for g in range(len(POOL_WINDOWS)):
        ug = ux[:, g * POOL_GW:(g + 1) * POOL_GW]
        hi = ug.astype(BF16)
        lo = (ug - hi.astype(F32)).astype(BF16)
        band = band_ref[g]
        wsum = (jnp.dot(band, hi, preferred_element_type=F32)
                + jnp.dot(band, lo, preferred_element_type=F32))
        p = wsum * icnt_ref[:, g * POOL_GW:(g + 1) * POOL_GW] - ug
        parts.append(jnp.dot(p.astype(BF16), wp_ref[g], preferred_element_type=F32))
    pool = jnp.concatenate(parts, axis=-1) * sp_ref[...]

    y = ys_ref[...] + d_ref[...] * ux[:, POOL_W:]
    y = _gelu_tanh(y)
    z = jnp.dot(y.astype(BF16), wglu_ref[...], preferred_element_type=F32) + bglu_ref[...]
    glu = y * jax.nn.sigmoid(z)

    cat = jnp.concatenate([pool, glu], axis=-1).astype(BF16)
    o = jnp.dot(cat, wout_ref[...], preferred_element_type=F32)
    xn = x_ref[...] + mod_ref[2:3, :] * o
    xo_ref[...] = xn

    h2 = _rmsnorm_mod(xn, gffn_ref[...], mod_ref[3:4, :], mod_ref[4:5, :])
    h2_ref[...] = h2
    logits = jnp.dot(h2, wr_ref[...], preferred_element_type=F32, precision=lax.Precision.HIGHEST)
    e1, e2, g1, g2 = _route(jax.nn.sigmoid(logits), br_ref[...])
    eidx_ref[...] = jnp.concatenate([e1, e2], axis=-1)
    gate_ref[...] = jnp.concatenate([g1, g2], axis=-1)


def _mixout(tok, ux, y_ssm, mod_l, band, icnt, wp_bf, sp, dvec, wglu_bf, bglu, wout_bf, gffn,
            w_router, b_router, nt, n_x_tiles):
    t = nt * TM
    sel = lambda i: (jnp.where(i == n_x_tiles, 1, 0), 0, 0)
    sel4 = lambda i: (jnp.where(i == n_x_tiles, 1, 0), 0, 0, 0)
    row = lambda i: (i, 0)
    fix2 = lambda i: (0, 0)
    return pl.pallas_call(
        _mixout_kernel,
        grid=(nt,),
        in_specs=[
            pl.BlockSpec((TM, D_MODEL), row),
            pl.BlockSpec((TM, D_MODEL), row),
            pl.BlockSpec((TM, SSM_W), row),
            pl.BlockSpec((None, N_MOD, D_MODEL), sel),
            pl.BlockSpec((None, len(POOL_WINDOWS), TM, TM), sel4),
            pl.BlockSpec((None, TM, POOL_W), sel),
            pl.BlockSpec((len(POOL_WINDOWS), POOL_GW, POOL_GW), lambda i: (0, 0, 0)),
            pl.BlockSpec((1, POOL_W), fix2),
            pl.BlockSpec((1, SSM_W), fix2),
            pl.BlockSpec((SSM_W, SSM_W), fix2),
            pl.BlockSpec((1, SSM_W), fix2),
            pl.BlockSpec((D_MODEL, D_MODEL), fix2),
            pl.BlockSpec((1, D_MODEL), fix2),
            pl.BlockSpec((D_MODEL, N_EXPERTS), fix2),
            pl.BlockSpec((1, N_EXPERTS), fix2),
        ],
        out_specs=[
            pl.BlockSpec((TM, D_MODEL), row),
            pl.BlockSpec((TM, D_MODEL), row),
            pl.BlockSpec((TM, 2), row),
            pl.BlockSpec((TM, 2), row),
        ],
        out_shape=[
            jax.ShapeDtypeStruct((t, D_MODEL), F32),
            jax.ShapeDtypeStruct((t, D_MODEL), F32),
            jax.ShapeDtypeStruct((t, 2), jnp.int32),
            jax.ShapeDtypeStruct((t, 2), F32),
        ],
        compiler_params=_cparams(("arbitrary",)),
        name="mixer_out_router",
    )(tok, ux, y_ssm, mod_l, band, icnt, wp_bf, sp, dvec, wglu_bf, bglu, wout_bf, gffn,
      w_router, b_router)


def _rank_kernel(eidx_ref, tri_ref, rank_ref, cnt_ref, run):
    i = pl.program_id(0)

    @pl.when(i == 0)
    def _():
        run[...] = jnp.zeros_like(run)

    e = eidx_ref[...]
    lane = lax.broadcasted_iota(jnp.int32, (TM, N_EXPERTS), 1)
    oh0 = lane == e[:, 0:1]
    oh1 = lane == e[:, 1:2]
    tri = tri_ref[...]
    before0 = jnp.dot(tri, jnp.where(oh0, 1.0, 0.0).astype(BF16), preferred_element_type=F32)
    before1 = jnp.dot(tri, jnp.where(oh1, 1.0, 0.0).astype(BF16), preferred_element_type=F32)
    tot0 = jnp.sum(jnp.where(oh0, 1.0, 0.0), axis=0, keepdims=True)
    tot1 = jnp.sum(jnp.where(oh1, 1.0, 0.0), axis=0, keepdims=True)
    base = run[...]
    r0 = jnp.sum(jnp.where(oh0, base + before0, 0.0), axis=-1, keepdims=True)
    r1 = jnp.sum(jnp.where(oh1, base + tot0 + before1, 0.0), axis=-1, keepdims=True)
    rank_ref[...] = jnp.concatenate([r0, r1], axis=-1).astype(jnp.int32)
    run[...] = base + tot0 + tot1
    cnt_ref[...] = (base + tot0 + tot1).astype(jnp.int32)


def _ranks(eidx, tri, nt):
    return pl.pallas_call(
        _rank_kernel,
        grid=(nt,),
        in_specs=[pl.BlockSpec((TM, 2), lambda i: (i, 0)),
                  pl.BlockSpec((TM, TM), lambda i: (0, 0))],
        out_specs=[pl.BlockSpec((TM, 2), lambda i: (i, 0)),
                   pl.BlockSpec((1, N_EXPERTS), lambda i: (0, 0))],
        out_shape=[jax.ShapeDtypeStruct((nt * TM, 2), jnp.int32),
                   jax.ShapeDtypeStruct((1, N_EXPERTS), jnp.int32)],
        scratch_shapes=[pltpu.VMEM((1, N_EXPERTS), F32)],
        compiler_params=_cparams(("arbitrary",)),
        name="moe_ranks",
    )(eidx, tri)


def _row_copy(src, s, dst, d, sem):
    return pltpu.make_async_copy(src.at[pl.ds(s, 1)], dst.at[pl.ds(d, 1)], sem)


ROW_UNROLL = 8


def _expert_kernel(be_ref, nbu_ref, nv_ref, src_ref, dst_ref, h_hbm, wg_ref, wu_ref, wd_ref,
                   y_hbm, xg0, xg1, og0, og1, wg_s, wu_s, wd_s, gsem, ssem):
    i = pl.program_id(0)
    nbu = nbu_ref[0]
    xg = (xg0, xg1)
    og = (og0, og1)

    def gather_start(blk, slot):
        base = blk * MOE_BLK

        def body(q, c):
            for j in range(ROW_UNROLL):
                r = q * ROW_UNROLL + j
                _row_copy(h_hbm, src_ref[base + r], xg[slot], r, gsem.at[slot]).start()
            return c

        lax.fori_loop(0, MOE_BLK // ROW_UNROLL, body, 0)

    def gather_wait(slot):
        pltpu.make_async_copy(h_hbm.at[pl.ds(0, MOE_BLK)], xg[slot], gsem.at[slot]).wait()

    def scatter_start(blk, slot):
        base = blk * MOE_BLK
        n = nv_ref[blk]

        def one(r):
            _row_copy(og[slot], r, y_hbm, dst_ref[base + r], ssem.at[slot]).start()

        def body(q, c):
            for j in range(ROW_UNROLL):
                one(q * ROW_UNROLL + j)
            return c

        full = n // ROW_UNROLL
        lax.fori_loop(0, full, body, 0)
        lax.fori_loop(full * ROW_UNROLL, n, lambda r, c: (one(r), c)[1], 0)

    def scatter_wait(blk, slot):
        n = nv_ref[blk]
        n8 = pl.multiple_of(n // SUBLANES * SUBLANES, SUBLANES)

        @pl.when(n8 > 0)
        def _():
            pltpu.make_async_copy(og[slot].at[pl.ds(0, n8)], y_hbm.at[pl.ds(0, n8)],
                                  ssem.at[slot]).wait()

        def one(r, c):
            _row_copy(og[slot], r, y_hbm, 0, ssem.at[slot]).wait()
            return c

        lax.fori_loop(n8, n, one, 0)

    changed = jnp.logical_or(i == 0, be_ref[i] != be_ref[jnp.maximum(i - 1, 0)])

    @pl.when(jnp.logical_and(changed, i < nbu))
    def _():
        wg_s[...] = wg_ref[...].astype(BF16)
        wu_s[...] = wu_ref[...].astype(BF16)
        wd_s[...] = wd_ref[...].astype(BF16)

    def step(slot):
        @pl.when(i == 0)
        def _():
            gather_start(0, slot)

        @pl.when(i + 1 < nbu)
        def _():
            gather_start(i + 1, 1 - slot)

        gather_wait(slot)

        @pl.when(i >= 2)
        def _():
            scatter_wait(i - 2, slot)

        xb = xg[slot][...].astype(BF16)
        g = jnp.dot(xb, wg_s[...], preferred_element_type=F32)
        u = jnp.dot(xb, wu_s[...], preferred_element_type=F32)
        hid = (g * jax.nn.sigmoid(g)) * u
        og[slot][...] = jnp.dot(hid.astype(BF16), wd_s[...], preferred_element_type=F32)
        scatter_start(i, slot)

        @pl.when(i == nbu - 1)
        def _():
            scatter_wait(i, slot)

            @pl.when(i >= 1)
            def _():
                scatter_wait(i - 1, 1 - slot)

    for slot in range(2):
        @pl.when(jnp.logical_and(i < nbu, i % 2 == slot))
        def _(slot=slot):
            step(slot)


def _experts(blk_e, nb_used, blk_valid, row_src, row_dst, h2, w_gate, w_up, w_down, layer,
             n_slot_rows):
    nb = row_src.shape[0] // MOE_BLK
    wmap = lambda i, be, nbu, nv, rs, rd: (layer, be[i], 0, 0)
    return pl.pallas_call(
        _expert_kernel,
        grid_spec=pltpu.PrefetchScalarGridSpec(
            num_scalar_prefetch=5,
            grid=(nb,),
            in_specs=[
                pl.BlockSpec(memory_space=pl.ANY),
                pl.BlockSpec((None, None, D_MODEL, D_EXPERT), wmap),
                pl.BlockSpec((None, None, D_MODEL, D_EXPERT), wmap),
                pl.BlockSpec((None, None, D_EXPERT, D_MODEL), wmap),
            ],
            out_specs=pl.BlockSpec(memory_space=pl.ANY),
            scratch_shapes=[pltpu.VMEM((MOE_BLK, D_MODEL), F32),
                            pltpu.VMEM((MOE_BLK, D_MODEL), F32),
                            pltpu.VMEM((MOE_BLK, D_MODEL), F32),
                            pltpu.VMEM((MOE_BLK, D_MODEL), F32),
                            pltpu.VMEM((D_MODEL, D_EXPERT), BF16),
                            pltpu.VMEM((D_MODEL, D_EXPERT), BF16),
                            pltpu.VMEM((D_EXPERT, D_MODEL), BF16),
                            pltpu.SemaphoreType.DMA((2,)),
                            pltpu.SemaphoreType.DMA((2,))],
        ),
        out_shape=jax.ShapeDtypeStruct((n_slot_rows, D_MODEL), F32),
        compiler_params=_cparams(("arbitrary",)),
        name="moe_experts",
    )(blk_e, nb_used, blk_valid, row_src, row_dst, h2, w_gate, w_up, w_down)


def _combine_kernel(x_ref, y0_ref, y1_ref, gate_ref, mod_ref, gfin_ref, out_ref, *, final):
    gate = gate_ref[...]
    y = gate[:, 0:1] * y0_ref[...] + gate[:, 1:2] * y1_ref[...]
    xn = x_ref[...] + mod_ref[5:6, :] * y
    if final:
        ms = jnp.mean(xn * xn, axis=-1, keepdims=True)
        xn = xn * lax.rsqrt(ms + EPS) * gfin_ref[...]
    out_ref[...] = xn


def _combine(xn, yslots, gate, mod_l, g_final, nt, n_x_tiles, final):
    sel = lambda i: (jnp.where(i == n_x_tiles, 1, 0), 0, 0)
    row = lambda i: (i, 0)
    return pl.pallas_call(
        functools.partial(_combine_kernel, final=final),
        grid=(nt,),
        in_specs=[
            pl.BlockSpec((TM, D_MODEL), row),
            pl.BlockSpec((TM, D_MODEL), row),
            pl.BlockSpec((TM, D_MODEL), lambda i: (i + nt, 0)),
            pl.BlockSpec((TM, 2), row),
            pl.BlockSpec((None, N_MOD, D_MODEL), sel),
            pl.BlockSpec((1, D_MODEL), lambda i: (0, 0)),
        ],
        out_specs=pl.BlockSpec((TM, D_MODEL), row),
        out_shape=jax.ShapeDtypeStruct((nt * TM, D_MODEL), F32),
        compiler_params=_cparams(("arbitrary",)),
        name="moe_combine",
    )(xn, yslots, yslots, gate, mod_l, g_final)


def _moe(h2, eidx, gate, xn, mod_l, g_final, w_gate, w_up, w_down, layer, nt, n_x_tiles, final,
         tri):
    t = nt * TM
    rank, counts = _ranks(eidx, tri, nt)
    counts = counts[0]
    padded = (counts + MOE_BLK - 1) // MOE_BLK * MOE_BLK
    pend = jnp.cumsum(padded)
    pstart = pend - padded
    dest = (pstart[eidx] + rank).reshape(-1)
    nb = (2 * t + N_EXPERTS * (MOE_BLK - 1)) // MOE_BLK + 1
    nb_used = (pend[-1] // MOE_BLK).astype(jnp.int32)
    blk_start = jnp.minimum(jnp.arange(nb, dtype=jnp.int32), nb_used - 1) * MOE_BLK
    blk_e = jnp.minimum(jnp.sum(pend[None, :] <= blk_start[:, None], axis=1),
                        N_EXPERTS - 1).astype(jnp.int32)
    blk_valid = jnp.clip((pstart + counts)[blk_e] - jnp.arange(nb, dtype=jnp.int32) * MOE_BLK,
                         0, MOE_BLK).astype(jnp.int32)
    a = jnp.arange(2 * t, dtype=jnp.int32)
    zeros = jnp.zeros((nb * MOE_BLK,), jnp.int32)
    row_src = zeros.at[dest].set(a // 2, unique_indices=True)
    row_dst = zeros.at[dest].set((a % 2) * t + a // 2, unique_indices=True)
    yslots = _experts(blk_e, nb_used.reshape(1), blk_valid, row_src, row_dst, h2, w_gate, w_up,
                      w_down, layer, 2 * t)
    return _combine(xn, yslots, gate, mod_l, g_final, nt, n_x_tiles, final)


def _cmul(ar, ai, br, bi):
    return ar * br - ai * bi, ar * bi + ai * br


def _expand_rows(v, rep_ref, mask_ref):
    return jnp.dot(v.astype(BF16), rep_ref[...], preferred_element_type=F32) * mask_ref[...]


def _expand_cols(e, rept_ref, maskt_ref):
    out = lax.dot_general(rept_ref[...], e.astype(BF16), (((1,), (1,)), ((), ())),
                          preferred_element_type=F32)
    return out * maskt_ref[...]


def _ssm_prep_kernel(are_ref, aim_ref, ldt_ref, btr_ref, bti_ref, cr_ref, ci_ref,
                     rep_ref, mask_ref, rept_ref, maskt_ref, ms_ref, mcat_ref, ecat):
    lag_blocks = []
    for d in range(2):
        a_re = jnp.minimum(are_ref[d], -1e-4)
        a_im = aim_ref[d]
        dt = jnp.exp(ldt_ref[d])
        mag = jnp.exp(a_re * dt)
        lr = mag * jnp.cos(a_im * dt)
        li = mag * jnp.sin(a_im * dt)
        den = a_re * a_re + a_im * a_im
        k_re = ((lr - 1.0) * a_re + li * a_im) / den
        k_im = (li * a_re - (lr - 1.0) * a_im) / den
        bbr, bbi = _cmul(k_re, k_im, btr_ref[d], bti_ref[d])
        cr, ci = cr_ref[d], ci_ref[d]
        pr, pi = jnp.ones_like(lr), jnp.zeros_like(lr)
        v0 = None
        for n in range(CH + 1):
            er, ei = _cmul(pr, pi, cr, ci)
            et = jnp.concatenate([_expand_cols(er, rept_ref, maskt_ref),
                                  _expand_cols(-ei, rept_ref, maskt_ref)], axis=0).astype(BF16)
            if n < CH:
                vr, vi = _cmul(pr, pi, bbr, bbi)
                v = jnp.concatenate([_expand_rows(vr, rep_ref, mask_ref),
                                     _expand_rows(vi, rep_ref, mask_ref)], axis=1).astype(BF16)
                if n == 0:
                    v0 = v
                sigma = CH - 1 - n if d == 0 else n
                ms_ref[d, sigma * LANES:(sigma + 1) * LANES, :] = v
                ecat[:, n * LANES:(n + 1) * LANES] = et
            if n >= 1:
                tau = n - 1 if d == 0 else CH - n
                mcat_ref[CW + d * SW:CW + (d + 1) * SW, tau * LANES:(tau + 1) * LANES] = et
            pr, pi = _cmul(pr, pi, lr, li)
        lag_blocks.append(jnp.dot(v0, ecat[...], preferred_element_type=F32))
    kf, kb = lag_blocks
    for s in range(CH):
        for t in range(CH):
            if t > s:
                blk = kf[:, (t - s) * LANES:(t - s + 1) * LANES]
            elif t < s:
                blk = kb[:, (s - t) * LANES:(s - t + 1) * LANES]
            else:
                blk = kf[:, :LANES] + kb[:, :LANES]
            mcat_ref[s * LANES:(s + 1) * LANES, t * LANES:(t + 1) * LANES] = blk.astype(BF16)


def _ssm_operators(a_re, a_im, log_dt, b_re, b_im, c_re, c_im):
    depth = a_re.shape[0]
    rows = SSM_GROUPS * SSM_GROUP
    p = SSM_STATE
    rep_rows = lambda v: jnp.repeat(v, SSM_GROUP, axis=2)
    are = rep_rows(a_re)
    aim = rep_rows(a_im)
    ldt = jnp.broadcast_to(rep_rows(log_dt[..., None]), are.shape)
    btr = jnp.swapaxes(b_re, -1, -2).reshape(depth, 2, rows, p)
    bti = jnp.swapaxes(b_im, -1, -2).reshape(depth, 2, rows, p)
    cr = c_re.reshape(depth, 2, rows, p)
    ci = c_im.reshape(depth, 2, rows, p)
    rep = np.tile(np.eye(p, dtype=np.float32), (1, GB))
    mask = np.kron(np.eye(GB, dtype=np.float32), np.ones((SSM_GROUP, p), np.float32))
    par = pl.BlockSpec((None, 2, LANES, p), lambda l, b: (l, 0, b, 0))
    fix = lambda shape: pl.BlockSpec(shape, lambda l, b: (0, 0))
    return pl.pallas_call(
        _ssm_prep_kernel,
        grid=(depth, NGB),
        in_specs=[par] * 7 + [fix((p, GB * p)), fix((LANES, GB * p)),
                              fix((GB * p, p)), fix((GB * p, LANES))],
        out_specs=[
            pl.BlockSpec((None, None, 2, CW, SW), lambda l, b: (l, b, 0, 0, 0)),
            pl.BlockSpec((None, None, CW + 2 * SW, CW), lambda l, b: (l, b, 0, 0)),
        ],
        out_shape=[
            jax.ShapeDtypeStruct((depth, NGB, 2, CW, SW), BF16),
            jax.ShapeDtypeStruct((depth, NGB, CW + 2 * SW, CW), BF16),
        ],
        scratch_shapes=[pltpu.VMEM((SW, CW), BF16)],
        compiler_params=_cparams(("arbitrary", "arbitrary")),
        name="ssm_operators",
    )(are, aim, ldt, btr, bti, cr, ci, jnp.asarray(rep, BF16), jnp.asarray(mask),
      jnp.asarray(rep.T, BF16), jnp.asarray(mask.T))


def _scan_tables(a_re, a_im, log_dt):
    depth = a_re.shape[0]
    a_re = jnp.minimum(a_re, -1e-4)
    dt = jnp.exp(log_dt)[..., None]
    mag = jnp.exp(a_re * dt)
    lr = mag * jnp.cos(a_im * dt)
    li = mag * jnp.sin(a_im * dt)
    ar, ai = lr, li
    for _ in range(CH - 1):
        ar, ai = _cmul(ar, ai, lr, li)
    apw = [(ar, ai)]
    for _ in range(SUBLANES - 1):
        apw.append(_cmul(apw[-1][0], apw[-1][1], ar, ai))
    apr = jnp.stack([q[0] for q in apw], axis=1)
    api = jnp.stack([q[1] for q in apw], axis=1)
    rows = np.arange(SUBLANES)
    tabs = []
    for d in range(2):
        consts = []
        for k in (1, 2, 4):
            keep = (rows >= k) if d == 0 else (rows < SUBLANES - k)
            keep = jnp.asarray(keep.astype(np.float32))[None, :, None, None]
            consts.append(keep * apr[:, k - 1:k, d])
            consts.append(keep * api[:, k - 1:k, d])
        order = rows if d == 0 else SUBLANES - 1 - rows
        consts.append(apr[:, order, d])
        consts.append(api[:, order, d])
        tabs.append(jnp.stack(consts, axis=1))
    tab = jnp.stack(tabs, axis=1).reshape(depth, 2, 8 * SUBLANES, NGB, GB * SSM_STATE)
    return tab.transpose(0, 1, 3, 2, 4)


def _pool_constants(seg):
    t = np.arange(TM)
    s0 = t // seg * seg
    band = np.zeros((len(POOL_WINDOWS), TM, TM), np.float32)
    icnt = np.zeros((TM, POOL_W), np.float32)
    for g, w in enumerate(POOL_WINDOWS):
        lo = np.maximum(t - w // 2, s0)
        hi = np.minimum(t + w // 2, s0 + seg)
        band[g] = (t[None, :] >= lo[:, None]) & (t[None, :] < hi[:, None])
        icnt[:, g * POOL_GW:(g + 1) * POOL_GW] = (1.0 / (hi - lo).astype(np.float32))[:, None]
    return band, icnt


def kernel(x, c, ctx, c_ctx, w_mod, b_mod, g_mix, g_ffn, w_in, w_out, w_pool, s_pool,
           ssm_a_re, ssm_a_im, ssm_log_dt, ssm_b_re, ssm_b_im, ssm_c_re, ssm_c_im, ssm_d,
           w_glu, b_glu, w_router, b_router, w_gate, w_up, w_down, g_final):
    bsz, seq, d = x.shape
    ctx_len = ctx.shape[1]
    depth = w_mod.shape[0]
    assert bsz == 1 and d == D_MODEL and ctx_len == TM
    assert seq % (SR * CH) == 0 and seq % GRID_W == 0
    n_xt = seq // TM
    n_st = seq // (SR * CH)
    ctx_rows = ctx_len // CH

    cvec = jnp.concatenate([c.reshape(1, d), c_ctx.reshape(1, d),
                            jnp.zeros((SUBLANES - 2, d), F32)], axis=0)
    mod = _modulation(cvec, w_mod, b_mod).reshape(depth, SUBLANES, N_MOD, d)

    band_x, icnt_x = _pool_constants(GRID_W)
    band_c, icnt_c = _pool_constants(ctx_len)
    band = jnp.asarray(np.stack([band_x, band_c]), BF16)
    icnt = jnp.asarray(np.stack([icnt_x, icnt_c]), F32)
    tri = jnp.asarray(np.tril(np.ones((TM, TM), np.float32), -1), BF16)

    ms, mcat = _ssm_operators(ssm_a_re, ssm_a_im, ssm_log_dt, ssm_b_re, ssm_b_im,
                              ssm_c_re, ssm_c_im)
    tab = _scan_tables(ssm_a_re, ssm_a_im, ssm_log_dt)

    tok = jnp.concatenate([x[0], ctx[0]], axis=0)
    for l in range(depth):
        last = l == depth - 1
        mod_l = mod[l, :2]
        ux, uv = _inproj(tok, mod_l, g_mix[l].reshape(1, d), w_in[l].astype(BF16), n_xt)
        uv_ext = jnp.pad(uv, ((0, SR - ctx_rows), (0, 0)))
        hf, gb = _ssm_states(uv_ext, ms, tab, l, n_st, ctx_rows)
        y_ssm = _ssm_readout(uv_ext, hf, gb, mcat, l)
        nt = n_xt if last else n_xt + 1
        xn, h2, eidx, gate = _mixout(
            tok, ux, y_ssm, mod_l, band, icnt, w_pool[l].astype(BF16), s_pool[l].reshape(1, -1),
            ssm_d[l].reshape(1, -1), w_glu[l].astype(BF16), b_glu[l].reshape(1, -1),
            w_out[l].astype(BF16), g_ffn[l].reshape(1, d), w_router, b_router.reshape(1, -1),
            nt, n_xt)
        tok = _moe(h2, eidx, gate, xn, mod_l, g_final.reshape(1, d), w_gate, w_up, w_down, l,
                   nt, n_xt, last, tri)
    return tok.reshape(bsz, seq, d)
```

```python
import functools

import numpy as np
import jax
import jax.numpy as jnp
from jax import lax
from jax.experimental import pallas as pl
from jax.experimental.pallas import tpu as pltpu

F32 = jnp.float32
BF16 = jnp.bfloat16

D_MODEL = 1024
POOL_W = 512
SSM_W = 512
POOL_WINDOWS = (2, 4, 8, 16)
POOL_GW = 128
SSM_GROUP = 16
SSM_GROUPS = 32
SSM_STATE = 64
N_EXPERTS = 32
N_EXPERT_GROUPS = 4
EXPERTS_PER_GROUP = 8
D_EXPERT = 512
GRID_W = 64
EPS = 1e-6
N_MOD = 6

LANES = 128
SUBLANES = 8
TM = 256
CH = 8
GB = 8
NGB = SSM_GROUPS // GB
CW = CH * LANES
SW = 2 * GB * SSM_STATE
SR = 256
MOE_BLK = 256
ROW_TILE = D_MODEL // LANES
ROW_UNROLL = 8
VMEM_LIMIT = 48 * 1024 * 1024


def _cparams(sem):
    return pltpu.CompilerParams(dimension_semantics=sem, vmem_limit_bytes=VMEM_LIMIT)


def _rmsnorm_mod(x, g, shift, scale):
    ms = jnp.mean(x * x, axis=-1, keepdims=True)
    y = x * lax.rsqrt(ms + EPS) * g
    return y * (1.0 + scale) + shift


def _mod_kernel(c_ref, w_ref, b_ref, o_ref):
    c = c_ref[...]
    a = c * jax.nn.sigmoid(c)
    o_ref[...] = jnp.dot(a, w_ref[...], preferred_element_type=F32,
                         precision=lax.Precision.HIGHEST) + b_ref[...]


def _modulation(cvec, w_mod, b_mod):
    depth, d, n = w_mod.shape
    tn = 1536
    return pl.pallas_call(
        _mod_kernel,
        grid=(depth, n // tn),
        in_specs=[
            pl.BlockSpec((SUBLANES, d), lambda l, j: (0, 0)),
            pl.BlockSpec((None, d, tn), lambda l, j: (l, 0, j)),
            pl.BlockSpec((None, 1, tn), lambda l, j: (l, 0, j)),
        ],
        out_specs=pl.BlockSpec((None, SUBLANES, tn), lambda l, j: (l, 0, j)),
        out_shape=jax.ShapeDtypeStruct((depth, SUBLANES, n), F32),
        compiler_params=_cparams(("arbitrary", "arbitrary")),
        name="modulation",
    )(cvec, w_mod, b_mod.reshape(depth, 1, n))


def _inproj_kernel(x_ref, mod_ref, g_ref, w_ref, ux_ref, uv_ref, scr):
    h = _rmsnorm_mod(x_ref[...], g_ref[...], mod_ref[0:1, :], mod_ref[1:2, :])
    ux = jnp.dot(h.astype(BF16), w_ref[...], preferred_element_type=F32)
    ux_ref[...] = ux
    rows = TM // CH
    for b in range(NGB):
        scr[b] = ux[:, POOL_W + LANES * b:POOL_W + LANES * (b + 1)]
        for t in range(CH):
            piece = scr[b, pl.ds(t, rows, stride=CH), :]
            uv_ref[:, pl.ds((b * CH + t) * LANES, LANES)] = piece.astype(BF16)


def _inproj(tok, mod_l, g, w_bf, n_x_tiles):
    t = tok.shape[0]
    nt = t // TM
    sel = lambda i: (jnp.where(i == n_x_tiles, 1, 0), 0, 0)
    return pl.pallas_call(
        _inproj_kernel,
        grid=(nt,),
        in_specs=[
            pl.BlockSpec((TM, D_MODEL), lambda i: (i, 0)),
            pl.BlockSpec((None, N_MOD, D_MODEL), sel),
            pl.BlockSpec((1, D_MODEL), lambda i: (0, 0)),
            pl.BlockSpec((D_MODEL, D_MODEL), lambda i: (0, 0)),
        ],
        out_specs=[
            pl.BlockSpec((TM, D_MODEL), lambda i: (i, 0)),
            pl.BlockSpec((TM // CH, NGB * CW), lambda i: (i, 0)),
        ],
        out_shape=[
            jax.ShapeDtypeStruct((t, D_MODEL), F32),
            jax.ShapeDtypeStruct((t // CH, NGB * CW), BF16),
        ],
        scratch_shapes=[pltpu.VMEM((NGB, TM, LANES), F32)],
        compiler_params=_cparams(("arbitrary",)),
        name="mixer_inproj",
    )(tok, mod_l, g, w_bf)


def _chunk_scan(s_ref, tab_ref, carry_ref, out_ref, nblk, reverse):
    half = SW // 2
    ncol = half // LANES
    row = lax.broadcasted_iota(jnp.int32, (SUBLANES, LANES), 0)
    edge = (row == SUBLANES - 1) if reverse else (row == 0)
    last = 0 if reverse else SUBLANES - 1

    def sub_block(r0, carry):
        outs_r, outs_i, new_carry = [], [], []
        for j in range(ncol):
            cr, ci = carry[2 * j], carry[2 * j + 1]
            lre = pl.ds(LANES * j, LANES)
            lim = pl.ds(half + LANES * j, LANES)
            zr = s_ref[pl.ds(r0, SUBLANES), lre]
            zi = s_ref[pl.ds(r0, SUBLANES), lim]
            for q, k in enumerate((1, 2, 4)):
                ar = tab_ref[pl.ds(16 * q, SUBLANES), lre]
                ai = tab_ref[pl.ds(16 * q + 8, SUBLANES), lre]
                sh = SUBLANES - k if reverse else k
                sr = pltpu.roll(zr, sh, axis=0)
                si = pltpu.roll(zi, sh, axis=0)
                zr, zi = zr + ar * sr - ai * si, zi + ar * si + ai * sr
            pr = tab_ref[pl.ds(48, SUBLANES), lre]
            pi = tab_ref[pl.ds(56, SUBLANES), lre]
            zr, zi = zr + pr * cr - pi * ci, zi + pr * ci + pi * cr
            sh1 = SUBLANES - 1 if reverse else 1
            outs_r.append(jnp.where(edge, cr, pltpu.roll(zr, sh1, axis=0)))
            outs_i.append(jnp.where(edge, ci, pltpu.roll(zi, sh1, axis=0)))
            new_carry.append(jnp.broadcast_to(zr[last:last + 1, :], (SUBLANES, LANES)))
            new_carry.append(jnp.broadcast_to(zi[last:last + 1, :], (SUBLANES, LANES)))
        return outs_r, outs_i, tuple(new_carry)

    def body(it, carry):
        bi = (nblk - 1 - it) if reverse else it
        r0 = pl.multiple_of(bi * 2 * SUBLANES, 2 * SUBLANES)
        if reverse:
            hi_r, hi_i, carry = sub_block(r0 + SUBLANES, carry)
            lo_r, lo_i, carry = sub_block(r0, carry)
        else:
            lo_r, lo_i, carry = sub_block(r0, carry)
            hi_r, hi_i, carry = sub_block(r0 + SUBLANES, carry)
        for j in range(ncol):
            out_ref[pl.ds(r0, 2 * SUBLANES), pl.ds(LANES * j, LANES)] = (
                jnp.concatenate([lo_r[j], hi_r[j]], axis=0).astype(BF16))
            out_ref[pl.ds(r0, 2 * SUBLANES), pl.ds(half + LANES * j, LANES)] = (
                jnp.concatenate([lo_i[j], hi_i[j]], axis=0).astype(BF16))
        return carry

    init = tuple(carry_ref[:, pl.ds(LANES * c, LANES)] for c in range(2 * ncol))
    final = lax.fori_loop(0, nblk, body, init)
    for c in range(2 * ncol):
        carry_ref[:, pl.ds(LANES * c, LANES)] = final[c]


def _ssm_state_kernel(uf_ref, ub_ref, msf_ref, msb_ref, tf_ref, tb_ref, hf_ref, gb_ref,
                      sf, sb, cf, cb, *, ctx_rows):
    step = pl.program_id(1)

    @pl.when(step == 0)
    def _():
        cf[...] = jnp.zeros_like(cf)
        cb[...] = jnp.zeros_like(cb)
        hf_ref[...] = jnp.zeros_like(hf_ref)
        gb_ref[...] = jnp.zeros_like(gb_ref)

    sf[...] = jnp.dot(uf_ref[...], msf_ref[...], preferred_element_type=F32)
    sb[...] = jnp.dot(ub_ref[...], msb_ref[...], preferred_element_type=F32)
    nblk = jnp.where(step == 0, ctx_rows // (2 * SUBLANES), SR // (2 * SUBLANES))
    _chunk_scan(sf, tf_ref, cf, hf_ref, nblk, reverse=False)
    _chunk_scan(sb, tb_ref, cb, gb_ref, nblk, reverse=True)


def _ssm_states(uv_ext, ms, tab, layer, n_xt, ctx_rows):
    rows = uv_ext.shape[0]
    fwd = lambda b, s: (jnp.where(s == 0, n_xt, s - 1), b)
    bwd = lambda b, s: (jnp.where(s == 0, n_xt, n_xt - s), b)
    return pl.pallas_call(
        functools.partial(_ssm_state_kernel, ctx_rows=ctx_rows),
        grid=(NGB, n_xt + 1),
        in_specs=[
            pl.BlockSpec((SR, CW), fwd),
            pl.BlockSpec((SR, CW), bwd),
            pl.BlockSpec((None, None, None, CW, SW), lambda b, s: (layer, b, 0, 0, 0)),
            pl.BlockSpec((None, None, None, CW, SW), lambda b, s: (layer, b, 1, 0, 0)),
            pl.BlockSpec((None, None, None, 8 * SUBLANES, SW // 2),
                         lambda b, s: (layer, 0, b, 0, 0)),
            pl.BlockSpec((None, None, None, 8 * SUBLANES, SW // 2),
                         lambda b, s: (layer, 1, b, 0, 0)),
        ],
        out_specs=[pl.BlockSpec((SR, SW), fwd), pl.BlockSpec((SR, SW), bwd)],
        out_shape=[jax.ShapeDtypeStruct((rows, NGB * SW), BF16)] * 2,
        scratch_shapes=[pltpu.VMEM((SR, SW), F32), pltpu.VMEM((SR, SW), F32),
                        pltpu.VMEM((SUBLANES, SW), F32), pltpu.VMEM((SUBLANES, SW), F32)],
        compiler_params=_cparams(("arbitrary", "arbitrary")),
        name="ssm_states",
    )(uv_ext, uv_ext, ms, ms, tab, tab)


def _ssm_out_kernel(u_ref, hf_ref, gb_ref, m_ref, y_ref):
    res = jnp.dot(u_ref[...], m_ref[0:CW, :], preferred_element_type=F32)
    res += jnp.dot(hf_ref[...], m_ref[CW:CW + SW, :], preferred_element_type=F32)
    res += jnp.dot(gb_ref[...], m_ref[CW + SW:CW + 2 * SW, :], preferred_element_type=F32)
    for t in range(CH):
        y_ref[pl.ds(t, SR, stride=CH), :] = res[:, t * LANES:(t + 1) * LANES]


def _ssm_readout(uv_ext, hf, gb, mcat, layer):
    rows = uv_ext.shape[0]
    nt = rows // SR
    blk = lambda b, i: (i, b)
    return pl.pallas_call(
        _ssm_out_kernel,
        grid=(NGB, nt),
        in_specs=[
            pl.BlockSpec((SR, CW), blk),
            pl.BlockSpec((SR, SW), blk),
            pl.BlockSpec((SR, SW), blk),
            pl.BlockSpec((None, None, CW + 2 * SW, CW), lambda b, i: (layer, b, 0, 0)),
        ],
        out_specs=pl.BlockSpec((SR * CH, LANES), blk),
        out_shape=jax.ShapeDtypeStruct((rows * CH, SSM_W), F32),
        compiler_params=_cparams(("arbitrary", "arbitrary")),
        name="ssm_readout",
    )(uv_ext, hf, gb, mcat)


def _gelu_tanh(x):
    return 0.5 * x * (1.0 + jnp.tanh(0.7978845608028654 * (x + 0.044715 * x * x * x)))


def _route(s, b_router):
    sel = s + b_router
    lane_i = lax.broadcasted_iota(jnp.int32, s.shape, 1)
    grp = lane_i // EXPERTS_PER_GROUP
    lane = lane_i.astype(F32)
    neg = jnp.float32(-jnp.inf)
    big = jnp.float32(N_EXPERTS)

    def top2(vals):
        m1 = jnp.max(vals, axis=-1, keepdims=True)
        i1 = jnp.min(jnp.where(vals == m1, lane, big), axis=-1, keepdims=True)
        rest = jnp.where(lane == i1, neg, vals)
        m2 = jnp.max(rest, axis=-1, keepdims=True)
        i2 = jnp.min(jnp.where(rest == m2, lane, big), axis=-1, keepdims=True)
        return m1, i1, m2, i2

    best = None
    gidx = None
    for g in range(N_EXPERT_GROUPS):
        m1, _, m2, _ = top2(jnp.where(grp == g, sel, neg))
        score = m1 + m2
        if best is None:
            best, gidx = score, jnp.zeros_like(lane_i[:, 0:1])
        else:
            upd = score > best
            best = jnp.where(upd, score, best)
            gidx = jnp.where(upd, g, gidx)
    _, e1, _, e2 = top2(jnp.where(grp == gidx, sel, neg))
    w1 = jnp.sum(jnp.where(lane == e1, s, 0.0), axis=-1, keepdims=True)
    w2 = jnp.sum(jnp.where(lane == e2, s, 0.0), axis=-1, keepdims=True)
    tot = w1 + w2
    return e1.astype(jnp.int32), e2.astype(jnp.int32), w1 / tot, w2 / tot


def _mixout_kernel(x_ref, ux_ref, ys_ref, mod_ref, band_ref, icnt_ref, wp_ref, sp_ref, d_ref,
                   wglu_ref, bglu_ref, wout_ref, gffn_ref, wr_ref, br_ref,
                   xo_ref, h2_ref, eidx_ref, gate_ref):
    ux = ux_ref[...]
    parts = []
    for g in range(len(POOL_WINDOWS)):
        ug = ux[:, g * POOL_GW:(g + 1) * POOL_GW]
        hi = ug.astype(BF16)
        lo = (ug - hi.astype(F32)).astype(BF16)
        band = band_ref[g]
        wsum = (jnp.dot(band, hi, preferred_element_type=F32)
                + jnp.dot(band, lo, preferred_element_type=F32))
        p = wsum * icnt_ref[:, g * POOL_GW:(g + 1) * POOL_GW] - ug
        parts.append(jnp.dot(p.astype(BF16), wp_ref[g], preferred_element_type=F32))
    pool = jnp.concatenate(parts, axis=-1) * sp_ref[...]

    y = ys_ref[...] + d_ref[...] * ux[:, POOL_W:]
    y = _gelu_tanh(y)
    z = jnp.dot(y.astype(BF16), wglu_ref[...], preferred_element_type=F32) + bglu_ref[...]
    glu = y * jax.nn.sigmoid(z)

    cat = jnp.concatenate([pool, glu], axis=-1).astype(BF16)
    o = jnp.dot(cat, wout_ref[...], preferred_element_type=F32)
    xn = x_ref[...] + mod_ref[2:3, :] * o
    xo_ref[...] = xn

    h2 = _rmsnorm_mod(xn, gffn_ref[...], mod_ref[3:4, :], mod_ref[4:5, :])
    for q in range(ROW_TILE):
        h2_ref[pl.ds(q, TM, stride=ROW_TILE), :] = h2[:, q * LANES:(q + 1) * LANES]
    logits = jnp.dot(h2, wr_ref[...], preferred_element_type=F32, precision=lax.Precision.HIGHEST)
    e1, e2, g1, g2 = _route(jax.nn.sigmoid(logits), br_ref[...])
    eidx_ref[...] = jnp.concatenate([e1, e2], axis=-1)
    gate_ref[...] = jnp.concatenate([g1, g2], axis=-1)


def _mixout(tok, ux, y_ssm, mod_l, band, icnt, wp_bf, sp, dvec, wglu_bf, bglu, wout_bf, gffn,
            w_router, b_router, nt, n_x_tiles):
    t = nt * TM
    sel = lambda i: (jnp.where(i == n_x_tiles, 1, 0), 0, 0)
    sel4 = lambda i: (jnp.where(i == n_x_tiles, 1, 0), 0, 0, 0)
    row = lambda i: (i, 0)
    fix2 = lambda i: (0, 0)
    return pl.pallas_call(
        _mixout_kernel,
        grid=(nt,),
        in_specs=[
            pl.BlockSpec((TM, D_MODEL), row),
            pl.BlockSpec((TM, D_MODEL), row),
            pl.BlockSpec((TM, SSM_W), row),
            pl.BlockSpec((None, N_MOD, D_MODEL), sel),
            pl.BlockSpec((None, len(POOL_WINDOWS), TM, TM), sel4),
            pl.BlockSpec((None, TM, POOL_W), sel),
            pl.BlockSpec((len(POOL_WINDOWS), POOL_GW, POOL_GW), lambda i: (0, 0, 0)),
            pl.BlockSpec((1, POOL_W), fix2),
            pl.BlockSpec((1, SSM_W), fix2),
            pl.BlockSpec((SSM_W, SSM_W), fix2),
            pl.BlockSpec((1, SSM_W), fix2),
            pl.BlockSpec((D_MODEL, D_MODEL), fix2),
            pl.BlockSpec((1, D_MODEL), fix2),
            pl.BlockSpec((D_MODEL, N_EXPERTS), fix2),
            pl.BlockSpec((1, N_EXPERTS), fix2),
        ],
        out_specs=[
            pl.BlockSpec((TM, D_MODEL), row),
            pl.BlockSpec((TM * ROW_TILE, LANES), row),
            pl.BlockSpec((TM, 2), row),
            pl.BlockSpec((TM, 2), row),
        ],
        out_shape=[
            jax.ShapeDtypeStruct((t, D_MODEL), F32),
            jax.ShapeDtypeStruct((t * ROW_TILE, LANES), F32),
            jax.ShapeDtypeStruct((t, 2), jnp.int32),
            jax.ShapeDtypeStruct((t, 2), F32),
        ],
        compiler_params=_cparams(("arbitrary",)),
        name="mixer_out_router",
    )(tok, ux, y_ssm, mod_l, band, icnt, wp_bf, sp, dvec, wglu_bf, bglu, wout_bf, gffn,
      w_router, b_router)


def _rank_kernel(eidx_ref, tri_ref, rank_ref, cnt_ref, run):
    i = pl.program_id(0)

    @pl.when(i == 0)
    def _():
        run[...] = jnp.zeros_like(run)

    e = eidx_ref[...]
    lane = lax.broadcasted_iota(jnp.int32, (TM, N_EXPERTS), 1)
    oh0 = lane == e[:, 0:1]
    oh1 = lane == e[:, 1:2]
    tri = tri_ref[...]
    before0 = jnp.dot(tri, jnp.where(oh0, 1.0, 0.0).astype(BF16), preferred_element_type=F32)
    before1 = jnp.dot(tri, jnp.where(oh1, 1.0, 0.0).astype(BF16), preferred_element_type=F32)
    tot0 = jnp.sum(jnp.where(oh0, 1.0, 0.0), axis=0, keepdims=True)
    tot1 = jnp.sum(jnp.where(oh1, 1.0, 0.0), axis=0, keepdims=True)
    base = run[...]
    r0 = jnp.sum(jnp.where(oh0, base + before0, 0.0), axis=-1, keepdims=True)
    r1 = jnp.sum(jnp.where(oh1, base + tot0 + before1, 0.0), axis=-1, keepdims=True)
    rank_ref[...] = jnp.concatenate([r0, r1], axis=-1).astype(jnp.int32)
    run[...] = base + tot0 + tot1
    cnt_ref[...] = (base + tot0 + tot1).astype(jnp.int32)


def _ranks(eidx, tri, nt):
    return pl.pallas_call(
        _rank_kernel,
        grid=(nt,),
        in_specs=[pl.BlockSpec((TM, 2), lambda i: (i, 0)),
                  pl.BlockSpec((TM, TM), lambda i: (0, 0))],
        out_specs=[pl.BlockSpec((TM, 2), lambda i: (i, 0)),
                   pl.BlockSpec((1, N_EXPERTS), lambda i: (0, 0))],
        out_shape=[jax.ShapeDtypeStruct((nt * TM, 2), jnp.int32),
                   jax.ShapeDtypeStruct((1, N_EXPERTS), jnp.int32)],
        scratch_shapes=[pltpu.VMEM((1, N_EXPERTS), F32)],
        compiler_params=_cparams(("arbitrary",)),
        name="moe_ranks",
    )(eidx, tri)


def _inverse_kernel(dest_ref, src_t, dst_t, src_o, dst_o, src_s, dst_s, sem, *, t):
    c1 = pltpu.make_async_copy(src_t, src_s, sem.at[0])
    c2 = pltpu.make_async_copy(dst_t, dst_s, sem.at[1])
    c1.start()
    c2.start()
    c1.wait()
    c2.wait()

    def body(q, c):
        for j in range(ROW_UNROLL):
            a = q * ROW_UNROLL + j
            d = dest_ref[a]
            src_s[d] = a >> 1
            dst_s[MOE_BLK + d] = (a & 1) * t + (a >> 1)
        return c

    lax.fori_loop(0, 2 * t // ROW_UNROLL, body, 0)
    o1 = pltpu.make_async_copy(src_s, src_o, sem.at[0])
    o2 = pltpu.make_async_copy(dst_s, dst_o, sem.at[1])
    o1.start()
    o2.start()
    o1.wait()
    o2.wait()


def _inverse_map(dest, nrows, t):
    r = np.arange(nrows + MOE_BLK, dtype=np.int32)
    src_t = jnp.zeros((nrows,), jnp.int32)
    dst_t = jnp.asarray(2 * t + r % MOE_BLK)
    return pl.pallas_call(
        functools.partial(_inverse_kernel, t=t),
        in_specs=[pl.BlockSpec(memory_space=pltpu.SMEM),
                  pl.BlockSpec(memory_space=pl.ANY),
                  pl.BlockSpec(memory_space=pl.ANY)],
        out_specs=[pl.BlockSpec(memory_space=pl.ANY), pl.BlockSpec(memory_space=pl.ANY)],
        out_shape=[jax.ShapeDtypeStruct((nrows,), jnp.int32),
                   jax.ShapeDtypeStruct((nrows + MOE_BLK,), jnp.int32)],
        scratch_shapes=[pltpu.SMEM((nrows,), jnp.int32),
                        pltpu.SMEM((nrows + MOE_BLK,), jnp.int32),
                        pltpu.SemaphoreType.DMA((2,))],
        name="moe_inverse_map",
    )(dest, src_t, dst_t)


def _expert_kernel(be_ref, nbu_ref, src_ref, dst_ref, h_hbm, h_flat, wg_ref, wu_ref, wd_ref,
                   y_hbm, xg0, xg1, og0, og1, wg_s, wu_s, wd_s, gsem, ssem, *, n_slots):
    s = pl.program_id(0)
    nbu = nbu_ref[0]
    xg = (xg0, xg1)
    og = (og0, og1)
    tile = lambda r: pl.ds(r * ROW_TILE, ROW_TILE)

    def gather_start(blk, slot, r):
        pltpu.make_async_copy(h_hbm.at[src_ref[blk * MOE_BLK + r]], xg[slot].at[tile(r)],
                              gsem.at[slot]).start()

    def scatter_start(blk, slot, r):
        pltpu.make_async_copy(og[slot].at[tile(r)], y_hbm.at[dst_ref[(blk + 1) * MOE_BLK + r]],
                              ssem.at[slot]).start()

    def gather_wait(slot):
        pltpu.make_async_copy(h_flat.at[pl.ds(0, MOE_BLK * ROW_TILE)], xg[slot],
                              gsem.at[slot]).wait()

    def scatter_wait(slot):
        pltpu.make_async_copy(h_flat.at[pl.ds(0, MOE_BLK * ROW_TILE)], og[slot],
                              ssem.at[slot]).wait()

    def rolled(fn, blk, slot):
        def body(q, c):
            for j in range(ROW_UNROLL):
                fn(blk, slot, q * ROW_UNROLL + j)
            return c
        lax.fori_loop(0, MOE_BLK // ROW_UNROLL, body, 0)

    changed = jnp.logical_or(s == 0, be_ref[s] != be_ref[jnp.maximum(s - 1, 0)])

    @pl.when(jnp.logical_and(changed, s < nbu))
    def _():
        wg_s[...] = wg_ref[...].astype(BF16)
        wu_s[...] = wu_ref[...].astype(BF16)
        wd_s[...] = wd_ref[...].astype(BF16)

    def step(slot):
        other = 1 - slot

        @pl.when(s == 0)
        def _():
            og[other][...] = jnp.zeros_like(og[other])
            rolled(gather_start, 0, slot)

        gather_wait(slot)

        @pl.when(s >= 1)
        def _():
            scatter_wait(slot)

        for r in range(MOE_BLK):
            gather_start(s + 1, other, r)
            scatter_start(s - 1, other, r)
        xb = jnp.concatenate([xg[slot][pl.ds(q, MOE_BLK, stride=ROW_TILE), :]
                              for q in range(ROW_TILE)], axis=-1).astype(BF16)
        g = jnp.dot(xb, wg_s[...], preferred_element_type=F32)
        u = jnp.dot(xb, wu_s[...], preferred_element_type=F32)
        hid = (g * jax.nn.sigmoid(g)) * u
        out = jnp.dot(hid.astype(BF16), wd_s[...], preferred_element_type=F32)
        for q in range(ROW_TILE):
            og[slot][pl.ds(q, MOE_BLK, stride=ROW_TILE), :] = out[:, q * LANES:(q + 1) * LANES]

        @pl.when(s == nbu - 1)
        def _():
            gather_wait(other)
            scatter_wait(other)
            rolled(scatter_start, s, slot)
            scatter_wait(slot)
            rolled(lambda blk, sl, r: pltpu.make_async_copy(
                og[sl].at[tile(r)], y_hbm.at[n_slots + r], ssem.at[sl]).start(), s, slot)
            scatter_wait(slot)

    for slot in range(2):
        @pl.when(jnp.logical_and(s < nbu, s % 2 == slot))
        def _(slot=slot):
            step(slot)


def _experts(blk_e, nb_used, row_src, row_dst, h2, w_gate, w_up, w_down, layer, n_slots):
    nb = row_src.shape[0] // MOE_BLK
    wmap = lambda i, be, nbu, rs, rd: (layer, be[i], 0, 0)
    buf = pltpu.VMEM((MOE_BLK * ROW_TILE, LANES), F32)
    return pl.pallas_call(
        functools.partial(_expert_kernel, n_slots=n_slots),
        grid_spec=pltpu.PrefetchScalarGridSpec(
            num_scalar_prefetch=4,
            grid=(nb,),
            in_specs=[
                pl.BlockSpec(memory_space=pl.ANY),
                pl.BlockSpec(memory_space=pl.ANY),
                pl.BlockSpec((None, None, D_MODEL, D_EXPERT), wmap),
                pl.BlockSpec((None, None, D_MODEL, D_EXPERT), wmap),
                pl.BlockSpec((None, None, D_EXPERT, D_MODEL), wmap),
            ],
            out_specs=pl.BlockSpec(memory_space=pl.ANY),
            scratch_shapes=[buf, buf, buf, buf,
                            pltpu.VMEM((D_MODEL, D_EXPERT), BF16),
                            pltpu.VMEM((D_MODEL, D_EXPERT), BF16),
                            pltpu.VMEM((D_EXPERT, D_MODEL), BF16),
                            pltpu.SemaphoreType.DMA((2,)),
                            pltpu.SemaphoreType.DMA((2,))],
        ),
        out_shape=jax.ShapeDtypeStruct((n_slots + MOE_BLK, ROW_TILE, LANES), F32),
        compiler_params=_cparams(("arbitrary",)),
        name="moe_experts",
    )(blk_e, nb_used, row_src, row_dst, h2.reshape(-1, ROW_TILE, LANES), h2, w_gate, w_up, w_down)


def _combine_kernel(x_ref, y0_ref, y1_ref, gate_ref, mod_ref, gfin_ref, out_ref, *, final):
    gate = gate_ref[...]
    rows = lambda ref: jnp.concatenate([ref[pl.ds(q, TM, stride=ROW_TILE), :]
                                        for q in range(ROW_TILE)], axis=-1)
    y = gate[:, 0:1] * rows(y0_ref) + gate[:, 1:2] * rows(y1_ref)
    xn = x_ref[...] + mod_ref[5:6, :] * y
    if final:
        ms = jnp.mean(xn * xn, axis=-1, keepdims=True)
        xn = xn * lax.rsqrt(ms + EPS) * gfin_ref[...]
    out_ref[...] = xn


def _combine(xn, yslots, gate, mod_l, g_final, nt, n_x_tiles, final):
    sel = lambda i: (jnp.where(i == n_x_tiles, 1, 0), 0, 0)
    row = lambda i: (i, 0)
    yflat = yslots.reshape(-1, LANES)
    return pl.pallas_call(
        functools.partial(_combine_kernel, final=final),
        grid=(nt,),
        in_specs=[
            pl.BlockSpec((TM, D_MODEL), row),
            pl.BlockSpec((TM * ROW_TILE, LANES), row),
            pl.BlockSpec((TM * ROW_TILE, LANES), lambda i: (i + nt, 0)),
            pl.BlockSpec((TM, 2), row),
            pl.BlockSpec((None, N_MOD, D_MODEL), sel),
            pl.BlockSpec((1, D_MODEL), lambda i: (0, 0)),
        ],
        out_specs=pl.BlockSpec((TM, D_MODEL), row),
        out_shape=jax.ShapeDtypeStruct((nt * TM, D_MODEL), F32),
        compiler_params=_cparams(("arbitrary",)),
        name="moe_combine",
    )(xn, yflat, yflat, gate, mod_l, g_final)


def _moe(h2, eidx, gate, xn, mod_l, g_final, w_gate, w_up, w_down, layer, nt, n_x_tiles, final,
         tri):
    t = nt * TM
    rank, counts = _ranks(eidx, tri, nt)
    counts = counts[0]
    padded = (counts + MOE_BLK - 1) // MOE_BLK * MOE_BLK
    pend = jnp.cumsum(padded)
    pstart = pend - padded
    dest = (pstart[eidx] + rank).reshape(-1)
    nb = (2 * t + N_EXPERTS * (MOE_BLK - 1)) // MOE_BLK + 1
    nb_used = (pend[-1] // MOE_BLK).astype(jnp.int32)
    blk_start = jnp.minimum(jnp.arange(nb, dtype=jnp.int32), nb_used - 1) * MOE_BLK
    blk_e = jnp.minimum(jnp.sum(pend[None, :] <= blk_start[:, None], axis=1),
                        N_EXPERTS - 1).astype(jnp.int32)
    row_src, row_dst = _inverse_map(dest, nb * MOE_BLK, t)
    yslots = _experts(blk_e, nb_used.reshape(1), row_src, row_dst, h2, w_gate, w_up, w_down,
                      layer, 2 * t)
    return _combine(xn, yslots, gate, mod_l, g_final, nt, n_x_tiles, final)


def _cmul(ar, ai, br, bi):
    return ar * br - ai * bi, ar * bi + ai * br


def _expand_rows(v, rep_ref, mask_ref):
    return jnp.dot(v.astype(BF16), rep_ref[...], preferred_element_type=F32) * mask_ref[...]


def _expand_cols(e, rept_ref, maskt_ref):
    out = lax.dot_general(rept_ref[...], e.astype(BF16), (((1,), (1,)), ((), ())),
                          preferred_element_type=F32)
    return out * maskt_ref[...]


def _ssm_prep_kernel(are_ref, aim_ref, ldt_ref, btr_ref, bti_ref, cr_ref, ci_ref,
                     rep_ref, mask_ref, rept_ref, maskt_ref, ms_ref, mcat_ref, ecat):
    lag_blocks = []
    for d in range(2):
        a_re = jnp.minimum(are_ref[d], -1e-4)
        a_im = aim_ref[d]
        dt = jnp.exp(ldt_ref[d])
        mag = jnp.exp(a_re * dt)
        lr = mag * jnp.cos(a_im * dt)
        li = mag * jnp.sin(a_im * dt)
        den = a_re * a_re + a_im * a_im
        k_re = ((lr - 1.0) * a_re + li * a_im) / den
        k_im = (li * a_re - (lr - 1.0) * a_im) / den
        bbr, bbi = _cmul(k_re, k_im, btr_ref[d], bti_ref[d])
        cr, ci = cr_ref[d], ci_ref[d]
        pr, pi = jnp.ones_like(lr), jnp.zeros_like(lr)
        v0 = None
        for n in range(CH + 1):
            er, ei = _cmul(pr, pi, cr, ci)
            et = jnp.concatenate([_expand_cols(er, rept_ref, maskt_ref),
                                  _expand_cols(-ei, rept_ref, maskt_ref)], axis=0).astype(BF16)
            if n < CH:
                vr, vi = _cmul(pr, pi, bbr, bbi)
                v = jnp.concatenate([_expand_rows(vr, rep_ref, mask_ref),
                                     _expand_rows(vi, rep_ref, mask_ref)], axis=1).astype(BF16)
                if n == 0:
                    v0 = v
                sigma = CH - 1 - n if d == 0 else n
                ms_ref[d, sigma * LANES:(sigma + 1) * LANES, :] = v
                ecat[:, n * LANES:(n + 1) * LANES] = et
            if n >= 1:
                tau = n - 1 if d == 0 else CH - n
                mcat_ref[CW + d * SW:CW + (d + 1) * SW, tau * LANES:(tau + 1) * LANES] = et
            pr, pi = _cmul(pr, pi, lr, li)
        lag_blocks.append(jnp.dot(v0, ecat[...], preferred_element_type=F32))
    kf, kb = lag_blocks
    for s in range(CH):
        for t in range(CH):
            if t > s:
                blk = kf[:, (t - s) * LANES:(t - s + 1) * LANES]
            elif t < s:
                blk = kb[:, (s - t) * LANES:(s - t + 1) * LANES]
            else:
                blk = kf[:, :LANES] + kb[:, :LANES]
            mcat_ref[s * LANES:(s + 1) * LANES, t * LANES:(t + 1) * LANES] = blk.astype(BF16)


def _ssm_operators(a_re, a_im, log_dt, b_re, b_im, c_re, c_im):
    depth = a_re.shape[0]
    rows = SSM_GROUPS * SSM_GROUP
    p = SSM_STATE
    rep_rows = lambda v: jnp.repeat(v, SSM_GROUP, axis=2)
    are = rep_rows(a_re)
    aim = rep_rows(a_im)
    ldt = jnp.broadcast_to(rep_rows(log_dt[..., None]), are.shape)
    btr = jnp.swapaxes(b_re, -1, -2).reshape(depth, 2, rows, p)
    bti = jnp.swapaxes(b_im, -1, -2).reshape(depth, 2, rows, p)
    cr = c_re.reshape(depth, 2, rows, p)
    ci = c_im.reshape(depth, 2, rows, p)
    rep = np.tile(np.eye(p, dtype=np.float32), (1, GB))
    mask = np.kron(np.eye(GB, dtype=np.float32), np.ones((SSM_GROUP, p), np.float32))
    par = pl.BlockSpec((None, 2, LANES, p), lambda l, b: (l, 0, b, 0))
    fix = lambda shape: pl.BlockSpec(shape, lambda l, b: (0, 0))
    return pl.pallas_call(
        _ssm_prep_kernel,
        grid=(depth, NGB),
        in_specs=[par] * 7 + [fix((p, GB * p)), fix((LANES, GB * p)),
                              fix((GB * p, p)), fix((GB * p, LANES))],
        out_specs=[
            pl.BlockSpec((None, None, 2, CW, SW), lambda l, b: (l, b, 0, 0, 0)),
            pl.BlockSpec((None, None, CW + 2 * SW, CW), lambda l, b: (l, b, 0, 0)),
        ],
        out_shape=[
            jax.ShapeDtypeStruct((depth, NGB, 2, CW, SW), BF16),
            jax.ShapeDtypeStruct((depth, NGB, CW + 2 * SW, CW), BF16),
        ],
        scratch_shapes=[pltpu.VMEM((SW, CW), BF16)],
        compiler_params=_cparams(("arbitrary", "arbitrary")),
        name="ssm_operators",
    )(are, aim, ldt, btr, bti, cr, ci, jnp.asarray(rep, BF16), jnp.asarray(mask),
      jnp.asarray(rep.T, BF16), jnp.asarray(mask.T))


def _scan_tables(a_re, a_im, log_dt):
    depth = a_re.shape[0]
    a_re = jnp.minimum(a_re, -1e-4)
    dt = jnp.exp(log_dt)[..., None]
    mag = jnp.exp(a_re * dt)
    lr = mag * jnp.cos(a_im * dt)
    li = mag * jnp.sin(a_im * dt)
    ar, ai = lr, li
    for _ in range(CH - 1):
        ar, ai = _cmul(ar, ai, lr, li)
    apw = [(ar, ai)]
    for _ in range(SUBLANES - 1):
        apw.append(_cmul(apw[-1][0], apw[-1][1], ar, ai))
    apr = jnp.stack([q[0] for q in apw], axis=1)
    api = jnp.stack([q[1] for q in apw], axis=1)
    rows = np.arange(SUBLANES)
    tabs = []
    for d in range(2):
        consts = []
        for k in (1, 2, 4):
            keep = (rows >= k) if d == 0 else (rows < SUBLANES - k)
            keep = jnp.asarray(keep.astype(np.float32))[None, :, None, None]
            consts.append(keep * apr[:, k - 1:k, d])
            consts.append(keep * api[:, k - 1:k, d])
        order = rows if d == 0 else SUBLANES - 1 - rows
        consts.append(apr[:, order, d])
        consts.append(api[:, order, d])
        tabs.append(jnp.stack(consts, axis=1))
    tab = jnp.stack(tabs, axis=1).reshape(depth, 2, 8 * SUBLANES, NGB, GB * SSM_STATE)
    return tab.transpose(0, 1, 3, 2, 4)


def _pool_constants(seg):
    t = np.arange(TM)
    s0 = t // seg * seg
    band = np.zeros((len(POOL_WINDOWS), TM, TM), np.float32)
    icnt = np.zeros((TM, POOL_W), np.float32)
    for g, w in enumerate(POOL_WINDOWS):
        lo = np.maximum(t - w // 2, s0)
        hi = np.minimum(t + w // 2, s0 + seg)
        band[g] = (t[None, :] >= lo[:, None]) & (t[None, :] < hi[:, None])
        icnt[:, g * POOL_GW:(g + 1) * POOL_GW] = (1.0 / (hi - lo).astype(np.float32))[:, None]
    return band, icnt


def kernel(x, c, ctx, c_ctx, w_mod, b_mod, g_mix, g_ffn, w_in, w_out, w_pool, s_pool,
           ssm_a_re, ssm_a_im, ssm_log_dt, ssm_b_re, ssm_b_im, ssm_c_re, ssm_c_im, ssm_d,
           w_glu, b_glu, w_router, b_router, w_gate, w_up, w_down, g_final):
    bsz, seq, d = x.shape
    ctx_len = ctx.shape[1]
    depth = w_mod.shape[0]
    assert bsz == 1 and d == D_MODEL and ctx_len == TM
    assert seq % (SR * CH) == 0 and seq % GRID_W == 0
    n_xt = seq // TM
    n_st = seq // (SR * CH)
    ctx_rows = ctx_len // CH

    cvec = jnp.concatenate([c.reshape(1, d), c_ctx.reshape(1, d),
                            jnp.zeros((SUBLANES - 2, d), F32)], axis=0)
    mod = _modulation(cvec, w_mod, b_mod).reshape(depth, SUBLANES, N_MOD, d)

    band_x, icnt_x = _pool_constants(GRID_W)
    band_c, icnt_c = _pool_constants(ctx_len)
    band = jnp.asarray(np.stack([band_x, band_c]), BF16)
    icnt = jnp.asarray(np.stack([icnt_x, icnt_c]), F32)
    tri = jnp.asarray(np.tril(np.ones((TM, TM), np.float32), -1), BF16)

    ms, mcat = _ssm_operators(ssm_a_re, ssm_a_im, ssm_log_dt, ssm_b_re, ssm_b_im,
                              ssm_c_re, ssm_c_im)
    tab = _scan_tables(ssm_a_re, ssm_a_im, ssm_log_dt)

    tok = jnp.concatenate([x[0], ctx[0]], axis=0)
    for l in range(depth):
        last = l == depth - 1
        mod_l = mod[l, :2]
        ux, uv = _inproj(tok, mod_l, g_mix[l].reshape(1, d), w_in[l].astype(BF16), n_xt)
        uv_ext = jnp.pad(uv, ((0, SR - ctx_rows), (0, 0)))
        hf, gb = _ssm_states(uv_ext, ms, tab, l, n_st, ctx_rows)
        y_ssm = _ssm_readout(uv_ext, hf, gb, mcat, l)
        nt = n_xt if last else n_xt + 1
        xn, h2, eidx, gate = _mixout(
            tok, ux, y_ssm, mod_l, band, icnt, w_pool[l].astype(BF16), s_pool[l].reshape(1, -1),
            ssm_d[l].reshape(1, -1), w_glu[l].astype(BF16), b_glu[l].reshape(1, -1),
            w_out[l].astype(BF16), g_ffn[l].reshape(1, d), w_router, b_router.reshape(1, -1),
            nt, n_xt)
        tok = _moe(h2, eidx, gate, xn, mod_l, g_final.reshape(1, d), w_gate, w_up, w_down, l,
                   nt, n_xt, last, tri)
    return tok.reshape(bsz, seq, d)
```

```python
import functools

import numpy as np
import jax
import jax.numpy as jnp
from jax import lax
from jax.experimental import pallas as pl
from jax.experimental.pallas import tpu as pltpu

F32 = jnp.float32
BF16 = jnp.bfloat16

D_MODEL = 1024
POOL_W = 512
SSM_W = 512
POOL_WINDOWS = (2, 4, 8, 16)
POOL_GW = 128
SSM_GROUP = 16
SSM_GROUPS = 32
SSM_STATE = 64
N_EXPERTS = 32
N_EXPERT_GROUPS = 4
EXPERTS_PER_GROUP = 8
D_EXPERT = 512
GRID_W = 64
EPS = 1e-6
N_MOD = 6

LANES = 128
SUBLANES = 8
TM = 256
CH = 8
GB = 8
NGB = SSM_GROUPS // GB
CW = CH * LANES
SW = 2 * GB * SSM_STATE
SR = 256
MOE_BLK = 256
ROW_TILE = D_MODEL // LANES
ROW_UNROLL = 8
RANK_BITS = 16
RANK_BASE = 1 << RANK_BITS
VMEM_LIMIT = 48 * 1024 * 1024


def _cparams(sem):
    return pltpu.CompilerParams(dimension_semantics=sem, vmem_limit_bytes=VMEM_LIMIT)


def _rmsnorm_mod(x, g, shift, scale):
    ms = jnp.mean(x * x, axis=-1, keepdims=True)
    y = x * lax.rsqrt(ms + EPS) * g
    return y * (1.0 + scale) + shift


def _mod_kernel(c_ref, w_ref, b_ref, o_ref):
    c = c_ref[...]
    a = c * jax.nn.sigmoid(c)
    o_ref[...] = jnp.dot(a, w_ref[...], preferred_element_type=F32,
                         precision=lax.Precision.HIGHEST) + b_ref[...]


def _modulation(cvec, w_mod, b_mod):
    depth, d, n = w_mod.shape
    tn = 1536
    return pl.pallas_call(
        _mod_kernel,
        grid=(depth, n // tn),
        in_specs=[
            pl.BlockSpec((SUBLANES, d), lambda l, j: (0, 0)),
            pl.BlockSpec((None, d, tn), lambda l, j: (l, 0, j)),
            pl.BlockSpec((None, 1, tn), lambda l, j: (l, 0, j)),
        ],
        out_specs=pl.BlockSpec((None, SUBLANES, tn), lambda l, j: (l, 0, j)),
        out_shape=jax.ShapeDtypeStruct((depth, SUBLANES, n), F32),
        compiler_params=_cparams(("arbitrary", "arbitrary")),
        name="modulation",
    )(cvec, w_mod, b_mod.reshape(depth, 1, n))


def _token_tile(x_ref, c_ref, n_x_tiles):
    return jnp.where(pl.program_id(0) == n_x_tiles, c_ref[...], x_ref[...])


def _inproj_kernel(x_ref, c_ref, mod_ref, g_ref, w_ref, ux_ref, uv_ref, scr, *, n_x_tiles):
    i = pl.program_id(0)

    @pl.when(i <= n_x_tiles)
    def _():
        x = _token_tile(x_ref, c_ref, n_x_tiles)
        h = _rmsnorm_mod(x, g_ref[...], mod_ref[0:1, :], mod_ref[1:2, :])
        ux = jnp.dot(h.astype(BF16), w_ref[...], preferred_element_type=F32)
        ux_ref[...] = ux
        rows = TM // CH
        for b in range(NGB):
            scr[b] = ux[:, POOL_W + LANES * b:POOL_W + LANES * (b + 1)]
            for t in range(CH):
                piece = scr[b, pl.ds(t, rows, stride=CH), :]
                uv_ref[:, pl.ds((b * CH + t) * LANES, LANES)] = piece.astype(BF16)

    @pl.when(i > n_x_tiles)
    def _():
        uv_ref[...] = jnp.zeros_like(uv_ref)


def _token_specs(n_x_tiles, ctx_block):
    return [pl.BlockSpec((TM, D_MODEL), lambda i: (jnp.minimum(i, n_x_tiles - 1), 0)),
            pl.BlockSpec((TM, D_MODEL), lambda i: (ctx_block, 0))]


def _inproj(xa, ca, ctx_block, mod_l, g, w_bf, n_x_tiles):
    nt = n_x_tiles + 1
    n_steps = n_x_tiles + SR // (TM // CH)
    sel = lambda i: (jnp.where(i >= n_x_tiles, 1, 0), 0, 0)
    last = lambda i: (jnp.minimum(i, n_x_tiles), 0)
    return pl.pallas_call(
        functools.partial(_inproj_kernel, n_x_tiles=n_x_tiles),
        grid=(n_steps,),
        in_specs=_token_specs(n_x_tiles, ctx_block) + [
            pl.BlockSpec((None, N_MOD, D_MODEL), sel),
            pl.BlockSpec((1, D_MODEL), lambda i: (0, 0)),
            pl.BlockSpec((D_MODEL, D_MODEL), lambda i: (0, 0)),
        ],
        out_specs=[
            pl.BlockSpec((TM, D_MODEL), last),
            pl.BlockSpec((TM // CH, NGB * CW), lambda i: (i, 0)),
        ],
        out_shape=[
            jax.ShapeDtypeStruct((nt * TM, D_MODEL), F32),
            jax.ShapeDtypeStruct((n_steps * TM // CH, NGB * CW), BF16),
        ],
        scratch_shapes=[pltpu.VMEM((NGB, TM, LANES), F32)],
        compiler_params=_cparams(("arbitrary",)),
        name="mixer_inproj",
    )(xa, ca, mod_l, g, w_bf)


def _chunk_scan(s_ref, tab_ref, carry_ref, out_ref, nblk, reverse):
    half = SW // 2
    ncol = half // LANES
    row = lax.broadcasted_iota(jnp.int32, (SUBLANES, LANES), 0)
    edge = (row == SUBLANES - 1) if reverse else (row == 0)
    last = 0 if reverse else SUBLANES - 1

    def sub_block(r0, carry):
        outs_r, outs_i, new_carry = [], [], []
        for j in range(ncol):
            cr, ci = carry[2 * j], carry[2 * j + 1]
            lre = pl.ds(LANES * j, LANES)
            lim = pl.ds(half + LANES * j, LANES)
            zr = s_ref[pl.ds(r0, SUBLANES), lre]
            zi = s_ref[pl.ds(r0, SUBLANES), lim]
            for q, k in enumerate((1, 2, 4)):
                ar = tab_ref[pl.ds(16 * q, SUBLANES), lre]
                ai = tab_ref[pl.ds(16 * q + 8, SUBLANES), lre]
                sh = SUBLANES - k if reverse else k
                sr = pltpu.roll(zr, sh, axis=0)
                si = pltpu.roll(zi, sh, axis=0)
                zr, zi = zr + ar * sr - ai * si, zi + ar * si + ai * sr
            pr = tab_ref[pl.ds(48, SUBLANES), lre]
            pi = tab_ref[pl.ds(56, SUBLANES), lre]
            zr, zi = zr + pr * cr - pi * ci, zi + pr * ci + pi * cr
            sh1 = SUBLANES - 1 if reverse else 1
            outs_r.append(jnp.where(edge, cr, pltpu.roll(zr, sh1, axis=0)))
            outs_i.append(jnp.where(edge, ci, pltpu.roll(zi, sh1, axis=0)))
            new_carry.append(jnp.broadcast_to(zr[last:last + 1, :], (SUBLANES, LANES)))
            new_carry.append(jnp.broadcast_to(zi[last:last + 1, :], (SUBLANES, LANES)))
        return outs_r, outs_i, tuple(new_carry)

    def body(it, carry):
        bi = (nblk - 1 - it) if reverse else it
        r0 = pl.multiple_of(bi * 2 * SUBLANES, 2 * SUBLANES)
        if reverse:
            hi_r, hi_i, carry = sub_block(r0 + SUBLANES, carry)
            lo_r, lo_i, carry = sub_block(r0, carry)
        else:
            lo_r, lo_i, carry = sub_block(r0, carry)
            hi_r, hi_i, carry = sub_block(r0 + SUBLANES, carry)
        for j in range(ncol):
            out_ref[pl.ds(r0, 2 * SUBLANES), pl.ds(LANES * j, LANES)] = (
                jnp.concatenate([lo_r[j], hi_r[j]], axis=0).astype(BF16))
            out_ref[pl.ds(r0, 2 * SUBLANES), pl.ds(half + LANES * j, LANES)] = (
                jnp.concatenate([lo_i[j], hi_i[j]], axis=0).astype(BF16))
        return carry

    init = tuple(carry_ref[:, pl.ds(LANES * c, LANES)] for c in range(2 * ncol))
    final = lax.fori_loop(0, nblk, body, init)
    for c in range(2 * ncol):
        carry_ref[:, pl.ds(LANES * c, LANES)] = final[c]


def _ssm_state_kernel(uf_ref, ub_ref, msf_ref, msb_ref, tf_ref, tb_ref, hf_ref, gb_ref,
                      sf, sb, cf, cb, *, ctx_rows):
    step = pl.program_id(1)

    @pl.when(step == 0)
    def _():
        cf[...] = jnp.zeros_like(cf)
        cb[...] = jnp.zeros_like(cb)
        hf_ref[...] = jnp.zeros_like(hf_ref)
        gb_ref[...] = jnp.zeros_like(gb_ref)

    sf[...] = jnp.dot(uf_ref[...], msf_ref[...], preferred_element_type=F32)
    sb[...] = jnp.dot(ub_ref[...], msb_ref[...], preferred_element_type=F32)
    nblk = jnp.where(step == 0, ctx_rows // (2 * SUBLANES), SR // (2 * SUBLANES))
    _chunk_scan(sf, tf_ref, cf, hf_ref, nblk, reverse=False)
    _chunk_scan(sb, tb_ref, cb, gb_ref, nblk, reverse=True)


def _ssm_states(uv_ext, ms, tab, layer, n_xt, ctx_rows):
    rows = uv_ext.shape[0]
    fwd = lambda b, s: (jnp.where(s == 0, n_xt, s - 1), b)
    bwd = lambda b, s: (jnp.where(s == 0, n_xt, n_xt - s), b)
    return pl.pallas_call(
        functools.partial(_ssm_state_kernel, ctx_rows=ctx_rows),
        grid=(NGB, n_xt + 1),
        in_specs=[
            pl.BlockSpec((SR, CW), fwd),
            pl.BlockSpec((SR, CW), bwd),
            pl.BlockSpec((None, None, None, CW, SW), lambda b, s: (layer, b, 0, 0, 0)),
            pl.BlockSpec((None, None, None, CW, SW), lambda b, s: (layer, b, 1, 0, 0)),
            pl.BlockSpec((None, None, None, 8 * SUBLANES, SW // 2),
                         lambda b, s: (layer, 0, b, 0, 0)),
            pl.BlockSpec((None, None, None, 8 * SUBLANES, SW // 2),
                         lambda b, s: (layer, 1, b, 0, 0)),
        ],
        out_specs=[pl.BlockSpec((SR, SW), fwd), pl.BlockSpec((SR, SW), bwd)],
        out_shape=[jax.ShapeDtypeStruct((rows, NGB * SW), BF16)] * 2,
        scratch_shapes=[pltpu.VMEM((SR, SW), F32), pltpu.VMEM((SR, SW), F32),
                        pltpu.VMEM((SUBLANES, SW), F32), pltpu.VMEM((SUBLANES, SW), F32)],
        compiler_params=_cparams(("arbitrary", "arbitrary")),
        name="ssm_states",
    )(uv_ext, uv_ext, ms, ms, tab, tab)


def _ssm_out_kernel(u_ref, hf_ref, gb_ref, m_ref, y_ref):
    res = jnp.dot(u_ref[...], m_ref[0:CW, :], preferred_element_type=F32)
    res += jnp.dot(hf_ref[...], m_ref[CW:CW + SW, :], preferred_element_type=F32)
    res += jnp.dot(gb_ref[...], m_ref[CW + SW:CW + 2 * SW, :], preferred_element_type=F32)
    for t in range(CH):
        y_ref[pl.ds(t, SR, stride=CH), :] = res[:, t * LANES:(t + 1) * LANES]


def _ssm_readout(uv_ext, hf, gb, mcat, layer):
    rows = uv_ext.shape[0]
    nt = rows // SR
    blk = lambda b, i: (i, b)
    return pl.pallas_call(
        _ssm_out_kernel,
        grid=(NGB, nt),
        in_specs=[
            pl.BlockSpec((SR, CW), blk),
            pl.BlockSpec((SR, SW), blk),
            pl.BlockSpec((SR, SW), blk),
            pl.BlockSpec((None, None, CW + 2 * SW, CW), lambda b, i: (layer, b, 0, 0)),
        ],
        out_specs=pl.BlockSpec((SR * CH, LANES), blk),
        out_shape=jax.ShapeDtypeStruct((rows * CH, SSM_W), F32),
        compiler_params=_cparams(("arbitrary", "arbitrary")),
        name="ssm_readout",
    )(uv_ext, hf, gb, mcat)


def _gelu_tanh(x):
    return 0.5 * x * (1.0 + jnp.tanh(0.7978845608028654 * (x + 0.044715 * x * x * x)))


def _route(s, b_router):
    sel = s + b_router
    lane_i = lax.broadcasted_iota(jnp.int32, s.shape, 1)
    grp = lane_i // EXPERTS_PER_GROUP
    lane = lane_i.astype(F32)
    neg = jnp.float32(-jnp.inf)
    big = jnp.float32(N_EXPERTS)

    def top2(vals):
        m1 = jnp.max(vals, axis=-1, keepdims=True)
        i1 = jnp.min(jnp.where(vals == m1, lane, big), axis=-1, keepdims=True)
        rest = jnp.where(lane == i1, neg, vals)
        m2 = jnp.max(rest, axis=-1, keepdims=True)
        i2 = jnp.min(jnp.where(rest == m2, lane, big), axis=-1, keepdims=True)
        return m1, i1, m2, i2

    best = None
    gidx = None
    for g in range(N_EXPERT_GROUPS):
        m1, _, m2, _ = top2(jnp.where(grp == g, sel, neg))
        score = m1 + m2
        if best is None:
            best, gidx = score, jnp.zeros_like(lane_i[:, 0:1])
        else:
            upd = score > best
            best = jnp.where(upd, score, best)
            gidx = jnp.where(upd, g, gidx)
    _, e1, _, e2 = top2(jnp.where(grp == gidx, sel, neg))
    w1 = jnp.sum(jnp.where(lane == e1, s, 0.0), axis=-1, keepdims=True)
    w2 = jnp.sum(jnp.where(lane == e2, s, 0.0), axis=-1, keepdims=True)
    tot = w1 + w2
    return e1.astype(jnp.int32), e2.astype(jnp.int32), w1 / tot, w2 / tot


def _mixout_kernel(x_ref, c_ref, ux_ref, ys_ref, mod_ref, band_ref, icnt_ref, wp_ref, sp_ref, d_ref,
                   wglu_ref, bglu_ref, wout_ref, gffn_ref, wr_ref, br_ref, tri_ref,
                   xo_ref, h2_ref, code_ref, gate_ref, cnt_ref, run, *, n_x_tiles):
    ux = ux_ref[...]
    parts = []
    for g in range(len(POOL_WINDOWS)):
        ug = ux[:, g * POOL_GW:(g + 1) * POOL_GW]
        hi = ug.astype(BF16)
        lo = (ug - hi.astype(F32)).astype(BF16)
        band = band_ref[g]
        wsum = (jnp.dot(band, hi, preferred_element_type=F32)
                + jnp.dot(band, lo, preferred_element_type=F32))
        p = wsum * icnt_ref[:, g * POOL_GW:(g + 1) * POOL_GW] - ug
        parts.append(jnp.dot(p.astype(BF16), wp_ref[g], preferred_element_type=F32))
    pool = jnp.concatenate(parts, axis=-1) * sp_ref[...]

    y = ys_ref[...] + d_ref[...] * ux[:, POOL_W:]
    y = _gelu_tanh(y)
    z = jnp.dot(y.astype(BF16), wglu_ref[...], preferred_element_type=F32) + bglu_ref[...]
    glu = y * jax.nn.sigmoid(z)

    cat = jnp.concatenate([pool, glu], axis=-1).astype(BF16)
    o = jnp.dot(cat, wout_ref[...], preferred_element_type=F32)
    xn = _token_tile(x_ref, c_ref, n_x_tiles) + mod_ref[2:3, :] * o
    xo_ref[...] = xn

    h2 = _rmsnorm_mod(xn, gffn_ref[...], mod_ref[3:4, :], mod_ref[4:5, :])
    for q in range(ROW_TILE):
        h2_ref[pl.ds(q, TM, stride=ROW_TILE), :] = h2[:, q * LANES:(q + 1) * LANES]
    h_hi = h2.astype(BF16)
    h_lo = (h2 - h_hi.astype(F32)).astype(BF16)
    wr = wr_ref[...]
    p_hi = jnp.dot(h_hi, wr, preferred_element_type=F32)
    p_lo = jnp.dot(h_lo, wr[:, :N_EXPERTS], preferred_element_type=F32)
    logits = p_hi[:, :N_EXPERTS] + (p_hi[:, N_EXPERTS:] + p_lo)
    e1, e2, g1, g2 = _route(jax.nn.sigmoid(logits), br_ref[...])
    gate_ref[...] = jnp.concatenate([g1, g2], axis=-1)

    @pl.when(pl.program_id(0) == 0)
    def _():
        run[...] = jnp.zeros_like(run)

    lane = lax.broadcasted_iota(jnp.int32, (TM, N_EXPERTS), 1)
    oh0 = jnp.where(lane == e1, 1.0, 0.0)
    oh1 = jnp.where(lane == e2, 1.0, 0.0)
    tri = tri_ref[...]
    before0 = jnp.dot(tri, oh0.astype(BF16), preferred_element_type=F32)
    before1 = jnp.dot(tri, oh1.astype(BF16), preferred_element_type=F32)
    tot0 = jnp.sum(oh0, axis=0, keepdims=True)
    tot1 = jnp.sum(oh1, axis=0, keepdims=True)
    base = run[...]
    r0 = jnp.sum(oh0 * (base + before0), axis=-1, keepdims=True).astype(jnp.int32)
    r1 = jnp.sum(oh1 * (base + tot0 + before1), axis=-1, keepdims=True).astype(jnp.int32)
    code_ref[...] = jnp.concatenate([e1 * RANK_BASE + r0, e2 * RANK_BASE + r1], axis=-1)
    run[...] = base + tot0 + tot1
    cnt_ref[...] = (base + tot0 + tot1).astype(jnp.int32)


def _mixout(xa, ca, ctx_block, ux, y_ssm, mod_l, band, icnt, wp_bf, sp, dvec, wglu_bf, bglu, wout_bf,
            gffn, w_router, b_router, tri, nt, n_x_tiles):
    t = nt * TM
    sel = lambda i: (jnp.where(i == n_x_tiles, 1, 0), 0, 0)
    sel4 = lambda i: (jnp.where(i == n_x_tiles, 1, 0), 0, 0, 0)
    row = lambda i: (i, 0)
    fix2 = lambda i: (0, 0)
    return pl.pallas_call(
        functools.partial(_mixout_kernel, n_x_tiles=n_x_tiles),
        grid=(nt,),
        in_specs=_token_specs(n_x_tiles, ctx_block) + [
            pl.BlockSpec((TM, D_MODEL), row),
            pl.BlockSpec((TM, SSM_W), row),
            pl.BlockSpec((None, N_MOD, D_MODEL), sel),
            pl.BlockSpec((None, len(POOL_WINDOWS), TM, TM), sel4),
            pl.BlockSpec((None, TM, POOL_W), sel),
            pl.BlockSpec((len(POOL_WINDOWS), POOL_GW, POOL_GW), lambda i: (0, 0, 0)),
            pl.BlockSpec((1, POOL_W), fix2),
            pl.BlockSpec((1, SSM_W), fix2),
            pl.BlockSpec((SSM_W, SSM_W), fix2),
            pl.BlockSpec((1, SSM_W), fix2),
            pl.BlockSpec((D_MODEL, D_MODEL), fix2),
            pl.BlockSpec((1, D_MODEL), fix2),
            pl.BlockSpec((D_MODEL, 2 * N_EXPERTS), fix2),
            pl.BlockSpec((1, N_EXPERTS), fix2),
            pl.BlockSpec((TM, TM), fix2),
        ],
        out_specs=[
            pl.BlockSpec((TM, D_MODEL), row),
            pl.BlockSpec((TM * ROW_TILE, LANES), row),
            pl.BlockSpec((TM, 2), row),
            pl.BlockSpec((TM, 2), row),
            pl.BlockSpec((1, N_EXPERTS), fix2),
        ],
        out_shape=[
            jax.ShapeDtypeStruct((t, D_MODEL), F32),
            jax.ShapeDtypeStruct((t * ROW_TILE, LANES), F32),
            jax.ShapeDtypeStruct((t, 2), jnp.int32),
            jax.ShapeDtypeStruct((t, 2), F32),
            jax.ShapeDtypeStruct((1, N_EXPERTS), jnp.int32),
        ],
        scratch_shapes=[pltpu.VMEM((1, N_EXPERTS), F32)],
        compiler_params=_cparams(("arbitrary",)),
        name="mixer_out_router",
    )(xa, ca, ux, y_ssm, mod_l, band, icnt, wp_bf, sp, dvec, wglu_bf, bglu, wout_bf, gffn,
      w_router, b_router, tri)


def _inverse_kernel(code_ref, pstart_ref, src_t, dst_t, src_o, dst_o, src_s, dst_s, sem, *, t):
    c1 = pltpu.make_async_copy(src_t, src_s, sem.at[0])
    c2 = pltpu.make_async_copy(dst_t, dst_s, sem.at[1])
    c1.start()
    c2.start()
    c1.wait()
    c2.wait()

    def body(q, c):
        for j in range(ROW_UNROLL):
            tok = q * ROW_UNROLL + j
            for k in range(2):
                code = code_ref[2 * tok + k]
                d = pstart_ref[code >> RANK_BITS] + (code & (RANK_BASE - 1))
                src_s[d] = tok
                dst_s[MOE_BLK + d] = k * t + tok
        return c

    lax.fori_loop(0, pstart_ref[N_EXPERTS], body, 0)
    o1 = pltpu.make_async_copy(src_s, src_o, sem.at[0])
    o2 = pltpu.make_async_copy(dst_s, dst_o, sem.at[1])
    o1.start()
    o2.start()
    o1.wait()
    o2.wait()


def _inverse_map(code, pstart, nrows, t):
    r = np.arange(nrows + MOE_BLK, dtype=np.int32)
    src_t = jnp.zeros((nrows,), jnp.int32)
    dst_t = jnp.asarray(2 * t + r % MOE_BLK)
    return pl.pallas_call(
        functools.partial(_inverse_kernel, t=t),
        in_specs=[pl.BlockSpec(memory_space=pltpu.SMEM),
                  pl.BlockSpec(memory_space=pltpu.SMEM),
                  pl.BlockSpec(memory_space=pl.ANY),
                  pl.BlockSpec(memory_space=pl.ANY)],
        out_specs=[pl.BlockSpec(memory_space=pl.ANY), pl.BlockSpec(memory_space=pl.ANY)],
        out_shape=[jax.ShapeDtypeStruct((nrows,), jnp.int32),
                   jax.ShapeDtypeStruct((nrows + MOE_BLK,), jnp.int32)],
        scratch_shapes=[pltpu.SMEM((nrows,), jnp.int32),
                        pltpu.SMEM((nrows + MOE_BLK,), jnp.int32),
                        pltpu.SemaphoreType.DMA((2,))],
        name="moe_inverse_map",
    )(code, pstart, src_t, dst_t)


def _expert_kernel(be_ref, nbu_ref, src_ref, dst_ref, h_hbm, h_flat, wg_ref, wu_ref, wd_ref,
                   y_hbm, xg0, xg1, og0, og1, wg_s, wu_s, wd_s, gsem, ssem, *, n_slots):
    s = pl.program_id(0)
    nbu = nbu_ref[0]
    xg = (xg0, xg1)
    og = (og0, og1)
    tile = lambda r: pl.ds(r * ROW_TILE, ROW_TILE)

    def gather_start(blk, slot, r):
        pltpu.make_async_copy(h_hbm.at[src_ref[blk * MOE_BLK + r]], xg[slot].at[tile(r)],
                              gsem.at[slot]).start()

    def scatter_start(blk, slot, r):
        pltpu.make_async_copy(og[slot].at[tile(r)], y_hbm.at[dst_ref[(blk + 1) * MOE_BLK + r]],
                              ssem.at[slot]).start()

    def gather_wait(slot):
        pltpu.make_async_copy(h_flat.at[pl.ds(0, MOE_BLK * ROW_TILE)], xg[slot],
                              gsem.at[slot]).wait()

    def scatter_wait(slot):
        pltpu.make_async_copy(h_flat.at[pl.ds(0, MOE_BLK * ROW_TILE)], og[slot],
                              ssem.at[slot]).wait()

    def rolled(fn, blk, slot):
        def body(q, c):
            for j in range(ROW_UNROLL):
                fn(blk, slot, q * ROW_UNROLL + j)
            return c
        lax.fori_loop(0, MOE_BLK // ROW_UNROLL, body, 0)

    changed = jnp.logical_or(s == 0, be_ref[s] != be_ref[jnp.maximum(s - 1, 0)])

    @pl.when(jnp.logical_and(changed, s < nbu))
    def _():
        wg_s[...] = wg_ref[...].astype(BF16)
        wu_s[...] = wu_ref[...].astype(BF16)
        wd_s[...] = wd_ref[...].astype(BF16)

    def step(slot):
        other = 1 - slot

        @pl.when(s == 0)
        def _():
            og[other][...] = jnp.zeros_like(og[other])
            rolled(gather_start, 0, slot)

        gather_wait(slot)

        @pl.when(s >= 1)
        def _():
            scatter_wait(slot)

        for r in range(MOE_BLK):
            gather_start(s + 1, other, r)
            scatter_start(s - 1, other, r)
        xb = jnp.concatenate([xg[slot][pl.ds(q, MOE_BLK, stride=ROW_TILE), :]
                              for q in range(ROW_TILE)], axis=-1).astype(BF16)
        g = jnp.dot(xb, wg_s[...], preferred_element_type=F32)
        u = jnp.dot(xb, wu_s[...], preferred_element_type=F32)
        hid = (g * jax.nn.sigmoid(g)) * u
        out = jnp.dot(hid.astype(BF16), wd_s[...], preferred_element_type=F32)
        for q in range(ROW_TILE):
            og[slot][pl.ds(q, MOE_BLK, stride=ROW_TILE), :] = out[:, q * LANES:(q + 1) * LANES]

        @pl.when(s == nbu - 1)
        def _():
            gather_wait(other)
            scatter_wait(other)
            rolled(scatter_start, s, slot)
            scatter_wait(slot)
            rolled(lambda blk, sl, r: pltpu.make_async_copy(
                og[sl].at[tile(r)], y_hbm.at[n_slots + r], ssem.at[sl]).start(), s, slot)
            scatter_wait(slot)

    for slot in range(2):
        @pl.when(jnp.logical_and(s < nbu, s % 2 == slot))
        def _(slot=slot):
            step(slot)


def _experts(blk_e, nb_used, row_src, row_dst, h2, w_gate, w_up, w_down, layer, n_slots):
    nb = row_src.shape[0] // MOE_BLK
    wmap = lambda i, be, nbu, rs, rd: (layer, be[i], 0, 0)
    buf = pltpu.VMEM((MOE_BLK * ROW_TILE, LANES), F32)
    return pl.pallas_call(
        functools.partial(_expert_kernel, n_slots=n_slots),
        grid_spec=pltpu.PrefetchScalarGridSpec(
            num_scalar_prefetch=4,
            grid=(nb,),
            in_specs=[
                pl.BlockSpec(memory_space=pl.ANY),
                pl.BlockSpec(memory_space=pl.ANY),
                pl.BlockSpec((None, None, D_MODEL, D_EXPERT), wmap),
                pl.BlockSpec((None, None, D_MODEL, D_EXPERT), wmap),
                pl.BlockSpec((None, None, D_EXPERT, D_MODEL), wmap),
            ],
            out_specs=pl.BlockSpec(memory_space=pl.ANY),
            scratch_shapes=[buf, buf, buf, buf,
                            pltpu.VMEM((D_MODEL, D_EXPERT), BF16),
                            pltpu.VMEM((D_MODEL, D_EXPERT), BF16),
                            pltpu.VMEM((D_EXPERT, D_MODEL), BF16),
                            pltpu.SemaphoreType.DMA((2,)),
                            pltpu.SemaphoreType.DMA((2,))],
        ),
        out_shape=jax.ShapeDtypeStruct((n_slots + MOE_BLK, ROW_TILE, LANES), F32),
        compiler_params=_cparams(("arbitrary",)),
        name="moe_experts",
    )(blk_e, nb_used, row_src, row_dst, h2.reshape(-1, ROW_TILE, LANES), h2, w_gate, w_up, w_down)


def _combine_kernel(x_ref, y0_ref, y1_ref, gate_ref, mod_ref, gfin_ref, out_ref, *, final):
    gate = gate_ref[...]
    rows = lambda ref: jnp.concatenate([ref[pl.ds(q, TM, stride=ROW_TILE), :]
                                        for q in range(ROW_TILE)], axis=-1)
    y = gate[:, 0:1] * rows(y0_ref) + gate[:, 1:2] * rows(y1_ref)
    xn = x_ref[...] + mod_ref[5:6, :] * y
    if final:
        ms = jnp.mean(xn * xn, axis=-1, keepdims=True)
        xn = xn * lax.rsqrt(ms + EPS) * gfin_ref[...]
    out_ref[...] = xn


def _combine(xn, yslots, gate, mod_l, g_final, nt, n_x_tiles, final):
    sel = lambda i: (jnp.where(i == n_x_tiles, 1, 0), 0, 0)
    row = lambda i: (i, 0)
    yflat = yslots.reshape(-1, LANES)
    return pl.pallas_call(
        functools.partial(_combine_kernel, final=final),
        grid=(nt,),
        in_specs=[
            pl.BlockSpec((TM, D_MODEL), row),
            pl.BlockSpec((TM * ROW_TILE, LANES), row),
            pl.BlockSpec((TM * ROW_TILE, LANES), lambda i: (i + nt, 0)),
            pl.BlockSpec((TM, 2), row),
            pl.BlockSpec((None, N_MOD, D_MODEL), sel),
            pl.BlockSpec((1, D_MODEL), lambda i: (0, 0)),
        ],
        out_specs=pl.BlockSpec((TM, D_MODEL), row),
        out_shape=jax.ShapeDtypeStruct((nt * TM, D_MODEL), F32),
        compiler_params=_cparams(("arbitrary",)),
        name="moe_combine",
    )(xn, yflat, yflat, gate, mod_l, g_final)


def _moe(h2, code, counts, gate, xn, mod_l, g_final, w_gate, w_up, w_down, layer, nt, n_x_tiles,
         final):
    t = nt * TM
    assert 2 * t < RANK_BASE
    counts = counts[0]
    padded = (counts + MOE_BLK - 1) // MOE_BLK * MOE_BLK
    pend = jnp.cumsum(padded)
    pstart = pend - padded
    nb = (2 * t + N_EXPERTS * (MOE_BLK - 1)) // MOE_BLK + 1
    nb_used = (pend[-1] // MOE_BLK).astype(jnp.int32)
    blk_start = jnp.minimum(jnp.arange(nb, dtype=jnp.int32), nb_used - 1) * MOE_BLK
    blk_e = jnp.minimum(jnp.sum(pend[None, :] <= blk_start[:, None], axis=1),
                        N_EXPERTS - 1).astype(jnp.int32)
    meta = jnp.concatenate([pstart, jnp.full((1,), t // ROW_UNROLL, jnp.int32)])
    row_src, row_dst = _inverse_map(code.reshape(-1), meta, nb * MOE_BLK, t)
    yslots = _experts(blk_e, nb_used.reshape(1), row_src, row_dst, h2, w_gate, w_up, w_down,
                      layer, 2 * t)
    return _combine(xn, yslots, gate, mod_l, g_final, nt, n_x_tiles, final)


def _cmul(ar, ai, br, bi):
    return ar * br - ai * bi, ar * bi + ai * br


def _expand_rows(v, rep_ref, mask_ref):
    return jnp.dot(v.astype(BF16), rep_ref[...], preferred_element_type=F32) * mask_ref[...]


def _expand_cols(e, rept_ref, maskt_ref):
    out = lax.dot_general(rept_ref[...], e.astype(BF16), (((1,), (1,)), ((), ())),
                          preferred_element_type=F32)
    return out * maskt_ref[...]


def _ssm_prep_kernel(are_ref, aim_ref, ldt_ref, btr_ref, bti_ref, cr_ref, ci_ref,
                     rep_ref, mask_ref, rept_ref, maskt_ref, ms_ref, mcat_ref, ecat):
    lag_blocks = []
    for d in range(2):
        a_re = jnp.minimum(are_ref[d], -1e-4)
        a_im = aim_ref[d]
        dt = jnp.exp(ldt_ref[d])
        mag = jnp.exp(a_re * dt)
        lr = mag * jnp.cos(a_im * dt)
        li = mag * jnp.sin(a_im * dt)
        den = a_re * a_re + a_im * a_im
        k_re = ((lr - 1.0) * a_re + li * a_im) / den
        k_im = (li * a_re - (lr - 1.0) * a_im) / den
        bbr, bbi = _cmul(k_re, k_im, btr_ref[d], bti_ref[d])
        cr, ci = cr_ref[d], ci_ref[d]
        pr, pi = jnp.ones_like(lr), jnp.zeros_like(lr)
        v0 = None
        for n in range(CH + 1):
            er, ei = _cmul(pr, pi, cr, ci)
            et = jnp.concatenate([_expand_cols(er, rept_ref, maskt_ref),
                                  _expand_cols(-ei, rept_ref, maskt_ref)], axis=0).astype(BF16)
            if n < CH:
                vr, vi = _cmul(pr, pi, bbr, bbi)
                v = jnp.concatenate([_expand_rows(vr, rep_ref, mask_ref),
                                     _expand_rows(vi, rep_ref, mask_ref)], axis=1).astype(BF16)
                if n == 0:
                    v0 = v
                sigma = CH - 1 - n if d == 0 else n
                ms_ref[d, sigma * LANES:(sigma + 1) * LANES, :] = v
                ecat[:, n * LANES:(n + 1) * LANES] = et
            if n >= 1:
                tau = n - 1 if d == 0 else CH - n
                mcat_ref[CW + d * SW:CW + (d + 1) * SW, tau * LANES:(tau + 1) * LANES] = et
            pr, pi = _cmul(pr, pi, lr, li)
        lag_blocks.append(jnp.dot(v0, ecat[...], preferred_element_type=F32))
    kf, kb = lag_blocks
    for s in range(CH):
        for t in range(CH):
            if t > s:
                blk = kf[:, (t - s) * LANES:(t - s + 1) * LANES]
            elif t < s:
                blk = kb[:, (s - t) * LANES:(s - t + 1) * LANES]
            else:
                blk = kf[:, :LANES] + kb[:, :LANES]
            mcat_ref[s * LANES:(s + 1) * LANES, t * LANES:(t + 1) * LANES] = blk.astype(BF16)


def _ssm_operators(a_re, a_im, log_dt, b_re, b_im, c_re, c_im):
    depth = a_re.shape[0]
    rows = SSM_GROUPS * SSM_GROUP
    p = SSM_STATE
    rep_rows = lambda v: jnp.repeat(v, SSM_GROUP, axis=2)
    are = rep_rows(a_re)
    aim = rep_rows(a_im)
    ldt = jnp.broadcast_to(rep_rows(log_dt[..., None]), are.shape)
    btr = jnp.swapaxes(b_re, -1, -2).reshape(depth, 2, rows, p)
    bti = jnp.swapaxes(b_im, -1, -2).reshape(depth, 2, rows, p)
    cr = c_re.reshape(depth, 2, rows, p)
    ci = c_im.reshape(depth, 2, rows, p)
    rep = np.tile(np.eye(p, dtype=np.float32), (1, GB))
    mask = np.kron(np.eye(GB, dtype=np.float32), np.ones((SSM_GROUP, p), np.float32))
    par = pl.BlockSpec((None, 2, LANES, p), lambda l, b: (l, 0, b, 0))
    fix = lambda shape: pl.BlockSpec(shape, lambda l, b: (0, 0))
    return pl.pallas_call(
        _ssm_prep_kernel,
        grid=(depth, NGB),
        in_specs=[par] * 7 + [fix((p, GB * p)), fix((LANES, GB * p)),
                              fix((GB * p, p)), fix((GB * p, LANES))],
        out_specs=[
            pl.BlockSpec((None, None, 2, CW, SW), lambda l, b: (l, b, 0, 0, 0)),
            pl.BlockSpec((None, None, CW + 2 * SW, CW), lambda l, b: (l, b, 0, 0)),
        ],
        out_shape=[
            jax.ShapeDtypeStruct((depth, NGB, 2, CW, SW), BF16),
            jax.ShapeDtypeStruct((depth, NGB, CW + 2 * SW, CW), BF16),
        ],
        scratch_shapes=[pltpu.VMEM((SW, CW), BF16)],
        compiler_params=_cparams(("arbitrary", "arbitrary")),
        name="ssm_operators",
    )(are, aim, ldt, btr, bti, cr, ci, jnp.asarray(rep, BF16), jnp.asarray(mask),
      jnp.asarray(rep.T, BF16), jnp.asarray(mask.T))


def _scan_tables(a_re, a_im, log_dt):
    depth = a_re.shape[0]
    a_re = jnp.minimum(a_re, -1e-4)
    dt = jnp.exp(log_dt)[..., None]
    mag = jnp.exp(a_re * dt)
    lr = mag * jnp.cos(a_im * dt)
    li = mag * jnp.sin(a_im * dt)
    ar, ai = lr, li
    for _ in range(CH - 1):
        ar, ai = _cmul(ar, ai, lr, li)
    apw = [(ar, ai)]
    for _ in range(SUBLANES - 1):
        apw.append(_cmul(apw[-1][0], apw[-1][1], ar, ai))
    apr = jnp.stack([q[0] for q in apw], axis=1)
    api = jnp.stack([q[1] for q in apw], axis=1)
    rows = np.arange(SUBLANES)
    tabs = []
    for d in range(2):
        consts = []
        for k in (1, 2, 4):
            keep = (rows >= k) if d == 0 else (rows < SUBLANES - k)
            keep = jnp.asarray(keep.astype(np.float32))[None, :, None, None]
            consts.append(keep * apr[:, k - 1:k, d])
            consts.append(keep * api[:, k - 1:k, d])
        order = rows if d == 0 else SUBLANES - 1 - rows
        consts.append(apr[:, order, d])
        consts.append(api[:, order, d])
        tabs.append(jnp.stack(consts, axis=1))
    tab = jnp.stack(tabs, axis=1).reshape(depth, 2, 8 * SUBLANES, NGB, GB * SSM_STATE)
    return tab.transpose(0, 1, 3, 2, 4)


def _pool_constants(seg):
    t = np.arange(TM)
    s0 = t // seg * seg
    band = np.zeros((len(POOL_WINDOWS), TM, TM), np.float32)
    icnt = np.zeros((TM, POOL_W), np.float32)
    for g, w in enumerate(POOL_WINDOWS):
        lo = np.maximum(t - w // 2, s0)
        hi = np.minimum(t + w // 2, s0 + seg)
        band[g] = (t[None, :] >= lo[:, None]) & (t[None, :] < hi[:, None])
        icnt[:, g * POOL_GW:(g + 1) * POOL_GW] = (1.0 / (hi - lo).astype(np.float32))[:, None]
    return band, icnt


def kernel(x, c, ctx, c_ctx, w_mod, b_mod, g_mix, g_ffn, w_in, w_out, w_pool, s_pool,
           ssm_a_re, ssm_a_im, ssm_log_dt, ssm_b_re, ssm_b_im, ssm_c_re, ssm_c_im, ssm_d,
           w_glu, b_glu, w_router, b_router, w_gate, w_up, w_down, g_final):
    bsz, seq, d = x.shape
    ctx_len = ctx.shape[1]
    depth = w_mod.shape[0]
    assert bsz == 1 and d == D_MODEL and ctx_len == TM
    assert seq % (SR * CH) == 0 and seq % GRID_W == 0
    n_xt = seq // TM
    n_st = seq // (SR * CH)
    ctx_rows = ctx_len // CH

    cvec = jnp.concatenate([c.reshape(1, d), c_ctx.reshape(1, d),
                            jnp.zeros((SUBLANES - 2, d), F32)], axis=0)
    mod = _modulation(cvec, w_mod, b_mod).reshape(depth, SUBLANES, N_MOD, d)

    band_x, icnt_x = _pool_constants(GRID_W)
    band_c, icnt_c = _pool_constants(ctx_len)
    band = jnp.asarray(np.stack([band_x, band_c]), BF16)
    icnt = jnp.asarray(np.stack([icnt_x, icnt_c]), F32)
    tri = jnp.asarray(np.tril(np.ones((TM, TM), np.float32), -1), BF16)
    wr_hi = w_router.astype(BF16)
    wr_split = jnp.concatenate([wr_hi, (w_router - wr_hi.astype(F32)).astype(BF16)], axis=1)

    ms, mcat = _ssm_operators(ssm_a_re, ssm_a_im, ssm_log_dt, ssm_b_re, ssm_b_im,
                              ssm_c_re, ssm_c_im)
    tab = _scan_tables(ssm_a_re, ssm_a_im, ssm_log_dt)

    xa, ca, ctx_block = x[0], ctx[0], 0
    for l in range(depth):
        last = l == depth - 1
        mod_l = mod[l, :2]
        ux, uv_ext = _inproj(xa, ca, ctx_block, mod_l, g_mix[l].reshape(1, d),
                             w_in[l].astype(BF16), n_xt)
        hf, gb = _ssm_states(uv_ext, ms, tab, l, n_st, ctx_rows)
        y_ssm = _ssm_readout(uv_ext, hf, gb, mcat, l)
        nt = n_xt if last else n_xt + 1
        xn, h2, code, gate, counts = _mixout(
            xa, ca, ctx_block, ux, y_ssm, mod_l, band, icnt, w_pool[l].astype(BF16),
            s_pool[l].reshape(1, -1), ssm_d[l].reshape(1, -1), w_glu[l].astype(BF16),
            b_glu[l].reshape(1, -1), w_out[l].astype(BF16), g_ffn[l].reshape(1, d), wr_split,
            b_router.reshape(1, -1), tri, nt, n_xt)
        tok = _moe(h2, code, counts, gate, xn, mod_l, g_final.reshape(1, d), w_gate, w_up, w_down,
                   l, nt, n_xt, last)
        xa, ca, ctx_block = tok, tok, n_xt
    return tok.reshape(bsz, seq, d)
```

```python
import functools

import numpy as np
import jax
import jax.numpy as jnp
from jax import lax
from jax.experimental import pallas as pl
from jax.experimental.pallas import tpu as pltpu

F32 = jnp.float32
BF16 = jnp.bfloat16

D_MODEL = 1024
POOL_W = 512
SSM_W = 512
POOL_WINDOWS = (2, 4, 8, 16)
POOL_GW = 128
SSM_GROUP = 16
SSM_GROUPS = 32
SSM_STATE = 64
N_EXPERTS = 32
N_EXPERT_GROUPS = 4
EXPERTS_PER_GROUP = 8
D_EXPERT = 512
GRID_W = 64
EPS = 1e-6
N_MOD = 6

LANES = 128
SUBLANES = 8
TM = 256
CH = 8
GB = 8
NGB = SSM_GROUPS // GB
CW = CH * LANES
SW = 2 * GB * SSM_STATE
SR = 512
MOE_BLK = 256
ROW_TILE = D_MODEL // LANES
ROW_UNROLL = 8
RANK_BITS = 16
RANK_BASE = 1 << RANK_BITS
VMEM_LIMIT = 48 * 1024 * 1024


def _cparams(sem):
    return pltpu.CompilerParams(dimension_semantics=sem, vmem_limit_bytes=VMEM_LIMIT)


def _rmsnorm_mod(x, g, shift, scale):
    ms = jnp.mean(x * x, axis=-1, keepdims=True)
    y = x * lax.rsqrt(ms + EPS) * g
    return y * (1.0 + scale) + shift


def _mod_kernel(c_ref, w_ref, b_ref, o_ref):
    c = c_ref[...]
    a = c * jax.nn.sigmoid(c)
    o_ref[...] = jnp.dot(a, w_ref[...], preferred_element_type=F32,
                         precision=lax.Precision.HIGHEST) + b_ref[...]


def _modulation(cvec, w_mod, b_mod):
    depth, d, n = w_mod.shape
    tn = 1536
    return pl.pallas_call(
        _mod_kernel,
        grid=(depth, n // tn),
        in_specs=[
            pl.BlockSpec((SUBLANES, d), lambda l, j: (0, 0)),
            pl.BlockSpec((None, d, tn), lambda l, j: (l, 0, j)),
            pl.BlockSpec((None, 1, tn), lambda l, j: (l, 0, j)),
        ],
        out_specs=pl.BlockSpec((None, SUBLANES, tn), lambda l, j: (l, 0, j)),
        out_shape=jax.ShapeDtypeStruct((depth, SUBLANES, n), F32),
        compiler_params=_cparams(("arbitrary", "arbitrary")),
        name="modulation",
    )(cvec, w_mod, b_mod.reshape(depth, 1, n))


def _token_tile(x_ref, c_ref, n_x_tiles):
    return jnp.where(pl.program_id(0) == n_x_tiles, c_ref[...], x_ref[...])


def _inproj_kernel(x_ref, c_ref, mod_ref, g_ref, w_ref, ux_ref, uv_ref, scr, *, n_x_tiles):
    i = pl.program_id(0)

    @pl.when(i <= n_x_tiles)
    def _():
        x = _token_tile(x_ref, c_ref, n_x_tiles)
        h = _rmsnorm_mod(x, g_ref[...], mod_ref[0:1, :], mod_ref[1:2, :])
        ux = jnp.dot(h.astype(BF16), w_ref[...], preferred_element_type=F32)
        ux_ref[...] = ux
        rows = TM // CH
        for b in range(NGB):
            scr[b] = ux[:, POOL_W + LANES * b:POOL_W + LANES * (b + 1)]
            for t in range(CH):
                piece = scr[b, pl.ds(t, rows, stride=CH), :]
                uv_ref[:, pl.ds((b * CH + t) * LANES, LANES)] = piece.astype(BF16)

    @pl.when(i > n_x_tiles)
    def _():
        uv_ref[...] = jnp.zeros_like(uv_ref)


def _token_specs(n_x_tiles, ctx_block):
    return [pl.BlockSpec((TM, D_MODEL), lambda i: (jnp.minimum(i, n_x_tiles - 1), 0)),
            pl.BlockSpec((TM, D_MODEL), lambda i: (ctx_block, 0))]


def _inproj(xa, ca, ctx_block, mod_l, g, w_bf, n_x_tiles):
    nt = n_x_tiles + 1
    n_steps = n_x_tiles + SR // (TM // CH)
    sel = lambda i: (jnp.where(i >= n_x_tiles, 1, 0), 0, 0)
    last = lambda i: (jnp.minimum(i, n_x_tiles), 0)
    return pl.pallas_call(
        functools.partial(_inproj_kernel, n_x_tiles=n_x_tiles),
        grid=(n_steps,),
        in_specs=_token_specs(n_x_tiles, ctx_block) + [
            pl.BlockSpec((None, N_MOD, D_MODEL), sel),
            pl.BlockSpec((1, D_MODEL), lambda i: (0, 0)),
            pl.BlockSpec((D_MODEL, D_MODEL), lambda i: (0, 0)),
        ],
        out_specs=[
            pl.BlockSpec((TM, D_MODEL), last),
            pl.BlockSpec((TM // CH, NGB * CW), lambda i: (i, 0)),
        ],
        out_shape=[
            jax.ShapeDtypeStruct((nt * TM, D_MODEL), F32),
            jax.ShapeDtypeStruct((n_steps * TM // CH, NGB * CW), BF16),
        ],
        scratch_shapes=[pltpu.VMEM((NGB, TM, LANES), F32)],
        compiler_params=_cparams(("arbitrary",)),
        name="mixer_inproj",
    )(xa, ca, mod_l, g, w_bf)


def _chunk_scan(s_ref, tab_ref, carry_ref, out_ref, nblk, reverse):
    half = SW // 2
    ncol = half // LANES
    row = lax.broadcasted_iota(jnp.int32, (SUBLANES, LANES), 0)
    edge = (row == SUBLANES - 1) if reverse else (row == 0)
    last = 0 if reverse else SUBLANES - 1

    def sub_block(r0, carry):
        outs_r, outs_i, new_carry = [], [], []
        for j in range(ncol):
            cr, ci = carry[2 * j], carry[2 * j + 1]
            lre = pl.ds(LANES * j, LANES)
            lim = pl.ds(half + LANES * j, LANES)
            zr = s_ref[pl.ds(r0, SUBLANES), lre]
            zi = s_ref[pl.ds(r0, SUBLANES), lim]
            for q, k in enumerate((1, 2, 4)):
                ar = tab_ref[pl.ds(16 * q, SUBLANES), lre]
                ai = tab_ref[pl.ds(16 * q + 8, SUBLANES), lre]
                sh = SUBLANES - k if reverse else k
                sr = pltpu.roll(zr, sh, axis=0)
                si = pltpu.roll(zi, sh, axis=0)
                zr, zi = zr + ar * sr - ai * si, zi + ar * si + ai * sr
            pr = tab_ref[pl.ds(48, SUBLANES), lre]
            pi = tab_ref[pl.ds(56, SUBLANES), lre]
            zr, zi = zr + pr * cr - pi * ci, zi + pr * ci + pi * cr
            sh1 = SUBLANES - 1 if reverse else 1
            outs_r.append(jnp.where(edge, cr, pltpu.roll(zr, sh1, axis=0)))
            outs_i.append(jnp.where(edge, ci, pltpu.roll(zi, sh1, axis=0)))
            new_carry.append(jnp.broadcast_to(zr[last:last + 1, :], (SUBLANES, LANES)))
            new_carry.append(jnp.broadcast_to(zi[last:last + 1, :], (SUBLANES, LANES)))
        return outs_r, outs_i, tuple(new_carry)

    def body(it, carry):
        bi = (nblk - 1 - it) if reverse else it
        r0 = pl.multiple_of(bi * 2 * SUBLANES, 2 * SUBLANES)
        if reverse:
            hi_r, hi_i, carry = sub_block(r0 + SUBLANES, carry)
            lo_r, lo_i, carry = sub_block(r0, carry)
        else:
            lo_r, lo_i, carry = sub_block(r0, carry)
            hi_r, hi_i, carry = sub_block(r0 + SUBLANES, carry)
        for j in range(ncol):
            out_ref[pl.ds(r0, 2 * SUBLANES), pl.ds(LANES * j, LANES)] = (
                jnp.concatenate([lo_r[j], hi_r[j]], axis=0).astype(BF16))
            out_ref[pl.ds(r0, 2 * SUBLANES), pl.ds(half + LANES * j, LANES)] = (
                jnp.concatenate([lo_i[j], hi_i[j]], axis=0).astype(BF16))
        return carry

    init = tuple(carry_ref[:, pl.ds(LANES * c, LANES)] for c in range(2 * ncol))
    final = lax.fori_loop(0, nblk, body, init)
    for c in range(2 * ncol):
        carry_ref[:, pl.ds(LANES * c, LANES)] = final[c]


def _ssm_state_kernel(uf_ref, ub_ref, msf_ref, msb_ref, tf_ref, tb_ref, hf_ref, gb_ref,
                      sf, sb, cf, cb, *, ctx_rows):
    step = pl.program_id(1)

    @pl.when(step == 0)
    def _():
        cf[...] = jnp.zeros_like(cf)
        cb[...] = jnp.zeros_like(cb)
        hf_ref[...] = jnp.zeros_like(hf_ref)
        gb_ref[...] = jnp.zeros_like(gb_ref)

    sf[...] = jnp.dot(uf_ref[...], msf_ref[...], preferred_element_type=F32)
    sb[...] = jnp.dot(ub_ref[...], msb_ref[...], preferred_element_type=F32)
    nblk = jnp.where(step == 0, ctx_rows // (2 * SUBLANES), SR // (2 * SUBLANES))
    _chunk_scan(sf, tf_ref, cf, hf_ref, nblk, reverse=False)
    _chunk_scan(sb, tb_ref, cb, gb_ref, nblk, reverse=True)


def _ssm_states(uv_ext, ms, tab, layer, n_xt, ctx_rows):
    rows = uv_ext.shape[0]
    fwd = lambda b, s: (jnp.where(s == 0, n_xt, s - 1), b)
    bwd = lambda b, s: (jnp.where(s == 0, n_xt, n_xt - s), b)
    return pl.pallas_call(
        functools.partial(_ssm_state_kernel, ctx_rows=ctx_rows),
        grid=(NGB, n_xt + 1),
        in_specs=[
            pl.BlockSpec((SR, CW), fwd),
            pl.BlockSpec((SR, CW), bwd),
            pl.BlockSpec((None, None, None, CW, SW), lambda b, s: (layer, b, 0, 0, 0)),
            pl.BlockSpec((None, None, None, CW, SW), lambda b, s: (layer, b, 1, 0, 0)),
            pl.BlockSpec((None, None, None, 8 * SUBLANES, SW // 2),
                         lambda b, s: (layer, 0, b, 0, 0)),
            pl.BlockSpec((None, None, None, 8 * SUBLANES, SW // 2),
                         lambda b, s: (layer, 1, b, 0, 0)),
        ],
        out_specs=[pl.BlockSpec((SR, SW), fwd), pl.BlockSpec((SR, SW), bwd)],
        out_shape=[jax.ShapeDtypeStruct((rows, NGB * SW), BF16)] * 2,
        scratch_shapes=[pltpu.VMEM((SR, SW), F32), pltpu.VMEM((SR, SW), F32),
                        pltpu.VMEM((SUBLANES, SW), F32), pltpu.VMEM((SUBLANES, SW), F32)],
        compiler_params=_cparams(("arbitrary", "arbitrary")),
        name="ssm_states",
    )(uv_ext, uv_ext, ms, ms, tab, tab)


def _ssm_out_kernel(u_ref, hf_ref, gb_ref, m_ref, y_ref):
    res = jnp.dot(u_ref[...], m_ref[0:CW, :], preferred_element_type=F32)
    res += jnp.dot(hf_ref[...], m_ref[CW:CW + SW, :], preferred_element_type=F32)
    res += jnp.dot(gb_ref[...], m_ref[CW + SW:CW + 2 * SW, :], preferred_element_type=F32)
    for t in range(CH):
        y_ref[pl.ds(t, SR, stride=CH), :] = res[:, t * LANES:(t + 1) * LANES]


def _ssm_readout(uv_ext, hf, gb, mcat, layer):
    rows = uv_ext.shape[0]
    nt = rows // SR
    blk = lambda b, i: (i, b)
    return pl.pallas_call(
        _ssm_out_kernel,
        grid=(NGB, nt),
        in_specs=[
            pl.BlockSpec((SR, CW), blk),
            pl.BlockSpec((SR, SW), blk),
            pl.BlockSpec((SR, SW), blk),
            pl.BlockSpec((None, None, CW + 2 * SW, CW), lambda b, i: (layer, b, 0, 0)),
        ],
        out_specs=pl.BlockSpec((SR * CH, LANES), blk),
        out_shape=jax.ShapeDtypeStruct((rows * CH, SSM_W), F32),
        compiler_params=_cparams(("arbitrary", "arbitrary")),
        name="ssm_readout",
    )(uv_ext, hf, gb, mcat)


def _gelu_tanh(x):
    return 0.5 * x * (1.0 + jnp.tanh(0.7978845608028654 * (x + 0.044715 * x * x * x)))


def _route(s, b_router):
    sel = s + b_router
    lane_i = lax.broadcasted_iota(jnp.int32, s.shape, 1)
    grp = lane_i // EXPERTS_PER_GROUP
    lane = lane_i.astype(F32)
    neg = jnp.float32(-jnp.inf)
    big = jnp.float32(N_EXPERTS)

    def top2(vals):
        m1 = jnp.max(vals, axis=-1, keepdims=True)
        i1 = jnp.min(jnp.where(vals == m1, lane, big), axis=-1, keepdims=True)
        rest = jnp.where(lane == i1, neg, vals)
        m2 = jnp.max(rest, axis=-1, keepdims=True)
        i2 = jnp.min(jnp.where(rest == m2, lane, big), axis=-1, keepdims=True)
        return m1, i1, m2, i2

    best = None
    gidx = None
    for g in range(N_EXPERT_GROUPS):
        m1, _, m2, _ = top2(jnp.where(grp == g, sel, neg))
        score = m1 + m2
        if best is None:
            best, gidx = score, jnp.zeros_like(lane_i[:, 0:1])
        else:
            upd = score > best
            best = jnp.where(upd, score, best)
            gidx = jnp.where(upd, g, gidx)
    _, e1, _, e2 = top2(jnp.where(grp == gidx, sel, neg))
    w1 = jnp.sum(jnp.where(lane == e1, s, 0.0), axis=-1, keepdims=True)
    w2 = jnp.sum(jnp.where(lane == e2, s, 0.0), axis=-1, keepdims=True)
    tot = w1 + w2
    return e1.astype(jnp.int32), e2.astype(jnp.int32), w1 / tot, w2 / tot


def _mixout_kernel(x_ref, c_ref, ux_ref, ys_ref, mod_ref, band_ref, icnt_ref, wp_ref, sp_ref, d_ref,
                   wglu_ref, bglu_ref, wout_ref, gffn_ref, wr_ref, br_ref, tri_ref,
                   xo_ref, h2_ref, code_ref, gate_ref, cnt_ref, run, *, n_x_tiles):
    ux = ux_ref[...]
    parts = []
    for g in range(len(POOL_WINDOWS)):
        ug = ux[:, g * POOL_GW:(g + 1) * POOL_GW]
        hi = ug.astype(BF16)
        lo = (ug - hi.astype(F32)).astype(BF16)
        band = band_ref[g]
        wsum = (jnp.dot(band, hi, preferred_element_type=F32)
                + jnp.dot(band, lo, preferred_element_type=F32))
        p = wsum * icnt_ref[:, g * POOL_GW:(g + 1) * POOL_GW] - ug
        parts.append(jnp.dot(p.astype(BF16), wp_ref[g], preferred_element_type=F32))
    pool = jnp.concatenate(parts, axis=-1) * sp_ref[...]

    y = ys_ref[...] + d_ref[...] * ux[:, POOL_W:]
    y = _gelu_tanh(y)
    z = jnp.dot(y.astype(BF16), wglu_ref[...], preferred_element_type=F32) + bglu_ref[...]
    glu = y * jax.nn.sigmoid(z)

    cat = jnp.concatenate([pool, glu], axis=-1).astype(BF16)
    o = jnp.dot(cat, wout_ref[...], preferred_element_type=F32)
    xn = _token_tile(x_ref, c_ref, n_x_tiles) + mod_ref[2:3, :] * o
    xo_ref[...] = xn

    h2 = _rmsnorm_mod(xn, gffn_ref[...], mod_ref[3:4, :], mod_ref[4:5, :])
    for q in range(ROW_TILE):
        h2_ref[pl.ds(q, TM, stride=ROW_TILE), :] = h2[:, q * LANES:(q + 1) * LANES]
    h_hi = h2.astype(BF16)
    h_lo = (h2 - h_hi.astype(F32)).astype(BF16)
    wr = wr_ref[...]
    p_hi = jnp.dot(h_hi, wr, preferred_element_type=F32)
    p_lo = jnp.dot(h_lo, wr[:, :N_EXPERTS], preferred_element_type=F32)
    logits = p_hi[:, :N_EXPERTS] + (p_hi[:, N_EXPERTS:] + p_lo)
    e1, e2, g1, g2 = _route(jax.nn.sigmoid(logits), br_ref[...])
    gate_ref[...] = jnp.concatenate([g1, g2], axis=-1)

    @pl.when(pl.program_id(0) == 0)
    def _():
        run[...] = jnp.zeros_like(run)

    lane = lax.broadcasted_iota(jnp.int32, (TM, N_EXPERTS), 1)
    oh0 = jnp.where(lane == e1, 1.0, 0.0)
    oh1 = jnp.where(lane == e2, 1.0, 0.0)
    tri = tri_ref[...]
    before0 = jnp.dot(tri, oh0.astype(BF16), preferred_element_type=F32)
    before1 = jnp.dot(tri, oh1.astype(BF16), preferred_element_type=F32)
    tot0 = jnp.sum(oh0, axis=0, keepdims=True)
    tot1 = jnp.sum(oh1, axis=0, keepdims=True)
    base = run[...]
    r0 = jnp.sum(oh0 * (base + before0), axis=-1, keepdims=True).astype(jnp.int32)
    r1 = jnp.sum(oh1 * (base + tot0 + before1), axis=-1, keepdims=True).astype(jnp.int32)
    code_ref[...] = jnp.concatenate([e1 * RANK_BASE + r0, e2 * RANK_BASE + r1], axis=-1)
    run[...] = base + tot0 + tot1
    cnt_ref[...] = (base + tot0 + tot1).astype(jnp.int32)


def _mixout(xa, ca, ctx_block, ux, y_ssm, mod_l, band, icnt, wp_bf, sp, dvec, wglu_bf, bglu, wout_bf,
            gffn, w_router, b_router, tri, nt, n_x_tiles):
    t = nt * TM
    sel = lambda i: (jnp.where(i == n_x_tiles, 1, 0), 0, 0)
    sel4 = lambda i: (jnp.where(i == n_x_tiles, 1, 0), 0, 0, 0)
    row = lambda i: (i, 0)
    fix2 = lambda i: (0, 0)
    return pl.pallas_call(
        functools.partial(_mixout_kernel, n_x_tiles=n_x_tiles),
        grid=(nt,),
        in_specs=_token_specs(n_x_tiles, ctx_block) + [
            pl.BlockSpec((TM, D_MODEL), row),
            pl.BlockSpec((TM, SSM_W), row),
            pl.BlockSpec((None, N_MOD, D_MODEL), sel),
            pl.BlockSpec((None, len(POOL_WINDOWS), TM, TM), sel4),
            pl.BlockSpec((None, TM, POOL_W), sel),
            pl.BlockSpec((len(POOL_WINDOWS), POOL_GW, POOL_GW), lambda i: (0, 0, 0)),
            pl.BlockSpec((1, POOL_W), fix2),
            pl.BlockSpec((1, SSM_W), fix2),
            pl.BlockSpec((SSM_W, SSM_W), fix2),
            pl.BlockSpec((1, SSM_W), fix2),
            pl.BlockSpec((D_MODEL, D_MODEL), fix2),
            pl.BlockSpec((1, D_MODEL), fix2),
            pl.BlockSpec((D_MODEL, 2 * N_EXPERTS), fix2),
            pl.BlockSpec((1, N_EXPERTS), fix2),
            pl.BlockSpec((TM, TM), fix2),
        ],
        out_specs=[
            pl.BlockSpec((TM, D_MODEL), row),
            pl.BlockSpec((TM * ROW_TILE, LANES), row),
            pl.BlockSpec((TM, 2), row),
            pl.BlockSpec((TM, 2), row),
            pl.BlockSpec((1, N_EXPERTS), fix2),
        ],
        out_shape=[
            jax.ShapeDtypeStruct((t, D_MODEL), F32),
            jax.ShapeDtypeStruct((t * ROW_TILE, LANES), F32),
            jax.ShapeDtypeStruct((t, 2), jnp.int32),
            jax.ShapeDtypeStruct((t, 2), F32),
            jax.ShapeDtypeStruct((1, N_EXPERTS), jnp.int32),
        ],
        scratch_shapes=[pltpu.VMEM((1, N_EXPERTS), F32)],
        compiler_params=_cparams(("arbitrary",)),
        name="mixer_out_router",
    )(xa, ca, ux, y_ssm, mod_l, band, icnt, wp_bf, sp, dvec, wglu_bf, bglu, wout_bf, gffn,
      w_router, b_router, tri)


def _inverse_kernel(code_ref, pstart_ref, src_t, dst_t, src_o, dst_o, src_s, dst_s, sem, *, t):
    c1 = pltpu.make_async_copy(src_t, src_s, sem.at[0])
    c2 = pltpu.make_async_copy(dst_t, dst_s, sem.at[1])
    c1.start()
    c2.start()
    c1.wait()
    c2.wait()

    def body(q, c):
        for j in range(ROW_UNROLL):
            tok = q * ROW_UNROLL + j
            for k in range(2):
                code = code_ref[2 * tok + k]
                d = pstart_ref[code >> RANK_BITS] + (code & (RANK_BASE - 1))
                src_s[d] = tok
                dst_s[MOE_BLK + d] = k * t + tok
        return c

    lax.fori_loop(0, pstart_ref[N_EXPERTS], body, 0)
    o1 = pltpu.make_async_copy(src_s, src_o, sem.at[0])
    o2 = pltpu.make_async_copy(dst_s, dst_o, sem.at[1])
    o1.start()
    o2.start()
    o1.wait()
    o2.wait()


def _inverse_map(code, pstart, nrows, t):
    r = np.arange(nrows + MOE_BLK, dtype=np.int32)
    src_t = jnp.zeros((nrows,), jnp.int32)
    dst_t = jnp.asarray(2 * t + r % MOE_BLK)
    return pl.pallas_call(
        functools.partial(_inverse_kernel, t=t),
        in_specs=[pl.BlockSpec(memory_space=pltpu.SMEM),
                  pl.BlockSpec(memory_space=pltpu.SMEM),
                  pl.BlockSpec(memory_space=pl.ANY),
                  pl.BlockSpec(memory_space=pl.ANY)],
        out_specs=[pl.BlockSpec(memory_space=pl.ANY), pl.BlockSpec(memory_space=pl.ANY)],
        out_shape=[jax.ShapeDtypeStruct((nrows,), jnp.int32),
                   jax.ShapeDtypeStruct((nrows + MOE_BLK,), jnp.int32)],
        scratch_shapes=[pltpu.SMEM((nrows,), jnp.int32),
                        pltpu.SMEM((nrows + MOE_BLK,), jnp.int32),
                        pltpu.SemaphoreType.DMA((2,))],
        name="moe_inverse_map",
    )(code, pstart, src_t, dst_t)


def _expert_kernel(be_ref, nbu_ref, src_ref, dst_ref, h_hbm, h_flat, wg_ref, wu_ref, wd_ref,
                   y_hbm, xg0, xg1, og0, og1, wg_s, wu_s, wd_s, gsem, ssem, *, n_slots):
    s = pl.program_id(0)
    nbu = nbu_ref[0]
    xg = (xg0, xg1)
    og = (og0, og1)
    tile = lambda r: pl.ds(r * ROW_TILE, ROW_TILE)

    def gather_start(blk, slot, r):
        pltpu.make_async_copy(h_hbm.at[src_ref[blk * MOE_BLK + r]], xg[slot].at[tile(r)],
                              gsem.at[slot]).start()

    def scatter_start(blk, slot, r):
        pltpu.make_async_copy(og[slot].at[tile(r)], y_hbm.at[dst_ref[(blk + 1) * MOE_BLK + r]],
                              ssem.at[slot]).start()

    def gather_wait(slot):
        pltpu.make_async_copy(h_flat.at[pl.ds(0, MOE_BLK * ROW_TILE)], xg[slot],
                              gsem.at[slot]).wait()

    def scatter_wait(slot):
        pltpu.make_async_copy(h_flat.at[pl.ds(0, MOE_BLK * ROW_TILE)], og[slot],
                              ssem.at[slot]).wait()

    def rolled(fn, blk, slot):
        def body(q, c):
            for j in range(ROW_UNROLL):
                fn(blk, slot, q * ROW_UNROLL + j)
            return c
        lax.fori_loop(0, MOE_BLK // ROW_UNROLL, body, 0)

    changed = jnp.logical_or(s == 0, be_ref[s] != be_ref[jnp.maximum(s - 1, 0)])

    @pl.when(jnp.logical_and(changed, s < nbu))
    def _():
        wg_s[...] = wg_ref[...].astype(BF16)
        wu_s[...] = wu_ref[...].astype(BF16)
        wd_s[...] = wd_ref[...].astype(BF16)

    def step(slot):
        other = 1 - slot

        @pl.when(s == 0)
        def _():
            og[other][...] = jnp.zeros_like(og[other])
            rolled(gather_start, 0, slot)

        gather_wait(slot)

        @pl.when(s >= 1)
        def _():
            scatter_wait(slot)

        for r in range(MOE_BLK):
            gather_start(s + 1, other, r)
            scatter_start(s - 1, other, r)
        xb = jnp.concatenate([xg[slot][pl.ds(q, MOE_BLK, stride=ROW_TILE), :]
                              for q in range(ROW_TILE)], axis=-1).astype(BF16)
        g = jnp.dot(xb, wg_s[...], preferred_element_type=F32)
        u = jnp.dot(xb, wu_s[...], preferred_element_type=F32)
        hid = (g * jax.nn.sigmoid(g)) * u
        out = jnp.dot(hid.astype(BF16), wd_s[...], preferred_element_type=F32)
        for q in range(ROW_TILE):
            og[slot][pl.ds(q, MOE_BLK, stride=ROW_TILE), :] = out[:, q * LANES:(q + 1) * LANES]

        @pl.when(s == nbu - 1)
        def _():
            gather_wait(other)
            scatter_wait(other)
            rolled(scatter_start, s, slot)
            scatter_wait(slot)
            rolled(lambda blk, sl, r: pltpu.make_async_copy(
                og[sl].at[tile(r)], y_hbm.at[n_slots + r], ssem.at[sl]).start(), s, slot)
            scatter_wait(slot)

    for slot in range(2):
        @pl.when(jnp.logical_and(s < nbu, s % 2 == slot))
        def _(slot=slot):
            step(slot)


def _experts(blk_e, nb_used, row_src, row_dst, h2, w_gate, w_up, w_down, layer, n_slots):
    nb = row_src.shape[0] // MOE_BLK
    wmap = lambda i, be, nbu, rs, rd: (layer, be[i], 0, 0)
    buf = pltpu.VMEM((MOE_BLK * ROW_TILE, LANES), F32)
    return pl.pallas_call(
        functools.partial(_expert_kernel, n_slots=n_slots),
        grid_spec=pltpu.PrefetchScalarGridSpec(
            num_scalar_prefetch=4,
            grid=(nb,),
            in_specs=[
                pl.BlockSpec(memory_space=pl.ANY),
                pl.BlockSpec(memory_space=pl.ANY),
                pl.BlockSpec((None, None, D_MODEL, D_EXPERT), wmap),
                pl.BlockSpec((None, None, D_MODEL, D_EXPERT), wmap),
                pl.BlockSpec((None, None, D_EXPERT, D_MODEL), wmap),
            ],
            out_specs=pl.BlockSpec(memory_space=pl.ANY),
            scratch_shapes=[buf, buf, buf, buf,
                            pltpu.VMEM((D_MODEL, D_EXPERT), BF16),
                            pltpu.VMEM((D_MODEL, D_EXPERT), BF16),
                            pltpu.VMEM((D_EXPERT, D_MODEL), BF16),
                            pltpu.SemaphoreType.DMA((2,)),
                            pltpu.SemaphoreType.DMA((2,))],
        ),
        out_shape=jax.ShapeDtypeStruct((n_slots + MOE_BLK, ROW_TILE, LANES), F32),
        compiler_params=_cparams(("arbitrary",)),
        name="moe_experts",
    )(blk_e, nb_used, row_src, row_dst, h2.reshape(-1, ROW_TILE, LANES), h2, w_gate, w_up, w_down)


def _combine_kernel(x_ref, y0_ref, y1_ref, gate_ref, mod_ref, gfin_ref, out_ref, *, final):
    gate = gate_ref[...]
    rows = lambda ref: jnp.concatenate([ref[pl.ds(q, TM, stride=ROW_TILE), :]
                                        for q in range(ROW_TILE)], axis=-1)
    y = gate[:, 0:1] * rows(y0_ref) + gate[:, 1:2] * rows(y1_ref)
    xn = x_ref[...] + mod_ref[5:6, :] * y
    if final:
        ms = jnp.mean(xn * xn, axis=-1, keepdims=True)
        xn = xn * lax.rsqrt(ms + EPS) * gfin_ref[...]
    out_ref[...] = xn


def _combine(xn, yslots, gate, mod_l, g_final, nt, n_x_tiles, final):
    sel = lambda i: (jnp.where(i == n_x_tiles, 1, 0), 0, 0)
    row = lambda i: (i, 0)
    yflat = yslots.reshape(-1, LANES)
    return pl.pallas_call(
        functools.partial(_combine_kernel, final=final),
        grid=(nt,),
        in_specs=[
            pl.BlockSpec((TM, D_MODEL), row),
            pl.BlockSpec((TM * ROW_TILE, LANES), row),
            pl.BlockSpec((TM * ROW_TILE, LANES), lambda i: (i + nt, 0)),
            pl.BlockSpec((TM, 2), row),
            pl.BlockSpec((None, N_MOD, D_MODEL), sel),
            pl.BlockSpec((1, D_MODEL), lambda i: (0, 0)),
        ],
        out_specs=pl.BlockSpec((TM, D_MODEL), row),
        out_shape=jax.ShapeDtypeStruct((nt * TM, D_MODEL), F32),
        compiler_params=_cparams(("arbitrary",)),
        name="moe_combine",
    )(xn, yflat, yflat, gate, mod_l, g_final)


def _moe(h2, code, counts, gate, xn, mod_l, g_final, w_gate, w_up, w_down, layer, nt, n_x_tiles,
         final):
    t = nt * TM
    assert 2 * t < RANK_BASE
    counts = counts[0]
    padded = (counts + MOE_BLK - 1) // MOE_BLK * MOE_BLK
    pend = jnp.cumsum(padded)
    pstart = pend - padded
    nb = (2 * t + N_EXPERTS * (MOE_BLK - 1)) // MOE_BLK + 1
    nb_used = (pend[-1] // MOE_BLK).astype(jnp.int32)
    blk_start = jnp.minimum(jnp.arange(nb, dtype=jnp.int32), nb_used - 1) * MOE_BLK
    blk_e = jnp.minimum(jnp.sum(pend[None, :] <= blk_start[:, None], axis=1),
                        N_EXPERTS - 1).astype(jnp.int32)
    meta = jnp.concatenate([pstart, jnp.full((1,), t // ROW_UNROLL, jnp.int32)])
    row_src, row_dst = _inverse_map(code.reshape(-1), meta, nb * MOE_BLK, t)
    yslots = _experts(blk_e, nb_used.reshape(1), row_src, row_dst, h2, w_gate, w_up, w_down,
                      layer, 2 * t)
    return _combine(xn, yslots, gate, mod_l, g_final, nt, n_x_tiles, final)


def _cmul(ar, ai, br, bi):
    return ar * br - ai * bi, ar * bi + ai * br


def _expand_rows(v, rep_ref, mask_ref):
    return jnp.dot(v.astype(BF16), rep_ref[...], preferred_element_type=F32) * mask_ref[...]


def _expand_cols(e, rept_ref, maskt_ref):
    out = lax.dot_general(rept_ref[...], e.astype(BF16), (((1,), (1,)), ((), ())),
                          preferred_element_type=F32)
    return out * maskt_ref[...]


def _ssm_prep_kernel(are_ref, aim_ref, ldt_ref, btr_ref, bti_ref, cr_ref, ci_ref,
                     rep_ref, mask_ref, rept_ref, maskt_ref, ms_ref, mcat_ref, ecat):
    lag_blocks = []
    for d in range(2):
        a_re = jnp.minimum(are_ref[d], -1e-4)
        a_im = aim_ref[d]
        dt = jnp.exp(ldt_ref[d])
        mag = jnp.exp(a_re * dt)
        lr = mag * jnp.cos(a_im * dt)
        li = mag * jnp.sin(a_im * dt)
        den = a_re * a_re + a_im * a_im
        k_re = ((lr - 1.0) * a_re + li * a_im) / den
        k_im = (li * a_re - (lr - 1.0) * a_im) / den
        bbr, bbi = _cmul(k_re, k_im, btr_ref[d], bti_ref[d])
        cr, ci = cr_ref[d], ci_ref[d]
        pr, pi = jnp.ones_like(lr), jnp.zeros_like(lr)
        v0 = None
        for n in range(CH + 1):
            er, ei = _cmul(pr, pi, cr, ci)
            et = jnp.concatenate([_expand_cols(er, rept_ref, maskt_ref),
                                  _expand_cols(-ei, rept_ref, maskt_ref)], axis=0).astype(BF16)
            if n < CH:
                vr, vi = _cmul(pr, pi, bbr, bbi)
                v = jnp.concatenate([_expand_rows(vr, rep_ref, mask_ref),
                                     _expand_rows(vi, rep_ref, mask_ref)], axis=1).astype(BF16)
                if n == 0:
                    v0 = v
                sigma = CH - 1 - n if d == 0 else n
                ms_ref[d, sigma * LANES:(sigma + 1) * LANES, :] = v
                ecat[:, n * LANES:(n + 1) * LANES] = et
            if n >= 1:
                tau = n - 1 if d == 0 else CH - n
                mcat_ref[CW + d * SW:CW + (d + 1) * SW, tau * LANES:(tau + 1) * LANES] = et
            pr, pi = _cmul(pr, pi, lr, li)
        lag_blocks.append(jnp.dot(v0, ecat[...], preferred_element_type=F32))
    kf, kb = lag_blocks
    for s in range(CH):
        for t in range(CH):
            if t > s:
                blk = kf[:, (t - s) * LANES:(t - s + 1) * LANES]
            elif t < s:
                blk = kb[:, (s - t) * LANES:(s - t + 1) * LANES]
            else:
                blk = kf[:, :LANES] + kb[:, :LANES]
            mcat_ref[s * LANES:(s + 1) * LANES, t * LANES:(t + 1) * LANES] = blk.astype(BF16)


def _ssm_operators(a_re, a_im, log_dt, b_re, b_im, c_re, c_im):
    depth = a_re.shape[0]
    rows = SSM_GROUPS * SSM_GROUP
    p = SSM_STATE
    rep_rows = lambda v: jnp.repeat(v, SSM_GROUP, axis=2)
    are = rep_rows(a_re)
    aim = rep_rows(a_im)
    ldt = jnp.broadcast_to(rep_rows(log_dt[..., None]), are.shape)
    btr = jnp.swapaxes(b_re, -1, -2).reshape(depth, 2, rows, p)
    bti = jnp.swapaxes(b_im, -1, -2).reshape(depth, 2, rows, p)
    cr = c_re.reshape(depth, 2, rows, p)
    ci = c_im.reshape(depth, 2, rows, p)
    rep = np.tile(np.eye(p, dtype=np.float32), (1, GB))
    mask = np.kron(np.eye(GB, dtype=np.float32), np.ones((SSM_GROUP, p), np.float32))
    par = pl.BlockSpec((None, 2, LANES, p), lambda l, b: (l, 0, b, 0))
    fix = lambda shape: pl.BlockSpec(shape, lambda l, b: (0, 0))
    return pl.pallas_call(
        _ssm_prep_kernel,
        grid=(depth, NGB),
        in_specs=[par] * 7 + [fix((p, GB * p)), fix((LANES, GB * p)),
                              fix((GB * p, p)), fix((GB * p, LANES))],
        out_specs=[
            pl.BlockSpec((None, None, 2, CW, SW), lambda l, b: (l, b, 0, 0, 0)),
            pl.BlockSpec((None, None, CW + 2 * SW, CW), lambda l, b: (l, b, 0, 0)),
        ],
        out_shape=[
            jax.ShapeDtypeStruct((depth, NGB, 2, CW, SW), BF16),
            jax.ShapeDtypeStruct((depth, NGB, CW + 2 * SW, CW), BF16),
        ],
        scratch_shapes=[pltpu.VMEM((SW, CW), BF16)],
        compiler_params=_cparams(("arbitrary", "arbitrary")),
        name="ssm_operators",
    )(are, aim, ldt, btr, bti, cr, ci, jnp.asarray(rep, BF16), jnp.asarray(mask),
      jnp.asarray(rep.T, BF16), jnp.asarray(mask.T))


def _scan_tables(a_re, a_im, log_dt):
    depth = a_re.shape[0]
    a_re = jnp.minimum(a_re, -1e-4)
    dt = jnp.exp(log_dt)[..., None]
    mag = jnp.exp(a_re * dt)
    lr = mag * jnp.cos(a_im * dt)
    li = mag * jnp.sin(a_im * dt)
    ar, ai = lr, li
    for _ in range(CH - 1):
        ar, ai = _cmul(ar, ai, lr, li)
    apw = [(ar, ai)]
    for _ in range(SUBLANES - 1):
        apw.append(_cmul(apw[-1][0], apw[-1][1], ar, ai))
    apr = jnp.stack([q[0] for q in apw], axis=1)
    api = jnp.stack([q[1] for q in apw], axis=1)
    rows = np.arange(SUBLANES)
    tabs = []
    for d in range(2):
        consts = []
        for k in (1, 2, 4):
            keep = (rows >= k) if d == 0 else (rows < SUBLANES - k)
            keep = jnp.asarray(keep.astype(np.float32))[None, :, None, None]
            consts.append(keep * apr[:, k - 1:k, d])
            consts.append(keep * api[:, k - 1:k, d])
        order = rows if d == 0 else SUBLANES - 1 - rows
        consts.append(apr[:, order, d])
        consts.append(api[:, order, d])
        tabs.append(jnp.stack(consts, axis=1))
    tab = jnp.stack(tabs, axis=1).reshape(depth, 2, 8 * SUBLANES, NGB, GB * SSM_STATE)
    return tab.transpose(0, 1, 3, 2, 4)


def _pool_constants(seg):
    t = np.arange(TM)
    s0 = t // seg * seg
    band = np.zeros((len(POOL_WINDOWS), TM, TM), np.float32)
    icnt = np.zeros((TM, POOL_W), np.float32)
    for g, w in enumerate(POOL_WINDOWS):
        lo = np.maximum(t - w // 2, s0)
        hi = np.minimum(t + w // 2, s0 + seg)
        band[g] = (t[None, :] >= lo[:, None]) & (t[None, :] < hi[:, None])
        icnt[:, g * POOL_GW:(g + 1) * POOL_GW] = (1.0 / (hi - lo).astype(np.float32))[:, None]
    return band, icnt


def kernel(x, c, ctx, c_ctx, w_mod, b_mod, g_mix, g_ffn, w_in, w_out, w_pool, s_pool,
           ssm_a_re, ssm_a_im, ssm_log_dt, ssm_b_re, ssm_b_im, ssm_c_re, ssm_c_im, ssm_d,
           w_glu, b_glu, w_router, b_router, w_gate, w_up, w_down, g_final):
    bsz, seq, d = x.shape
    ctx_len = ctx.shape[1]
    depth = w_mod.shape[0]
    assert bsz == 1 and d == D_MODEL and ctx_len == TM
    assert seq % (SR * CH) == 0 and seq % GRID_W == 0
    n_xt = seq // TM
    n_st = seq // (SR * CH)
    ctx_rows = ctx_len // CH

    cvec = jnp.concatenate([c.reshape(1, d), c_ctx.reshape(1, d),
                            jnp.zeros((SUBLANES - 2, d), F32)], axis=0)
    mod = _modulation(cvec, w_mod, b_mod).reshape(depth, SUBLANES, N_MOD, d)

    band_x, icnt_x = _pool_constants(GRID_W)
    band_c, icnt_c = _pool_constants(ctx_len)
    band = jnp.asarray(np.stack([band_x, band_c]), BF16)
    icnt = jnp.asarray(np.stack([icnt_x, icnt_c]), F32)
    tri = jnp.asarray(np.tril(np.ones((TM, TM), np.float32), -1), BF16)
    wr_hi = w_router.astype(BF16)
    wr_split = jnp.concatenate([wr_hi, (w_router - wr_hi.astype(F32)).astype(BF16)], axis=1)

    ms, mcat = _ssm_operators(ssm_a_re, ssm_a_im, ssm_log_dt, ssm_b_re, ssm_b_im,
                              ssm_c_re, ssm_c_im)
    tab = _scan_tables(ssm_a_re, ssm_a_im, ssm_log_dt)

    xa, ca, ctx_block = x[0], ctx[0], 0
    for l in range(depth):
        last = l == depth - 1
        mod_l = mod[l, :2]
        ux, uv_ext = _inproj(xa, ca, ctx_block, mod_l, g_mix[l].reshape(1, d),
                             w_in[l].astype(BF16), n_xt)
        hf, gb = _ssm_states(uv_ext, ms, tab, l, n_st, ctx_rows)
        y_ssm = _ssm_readout(uv_ext, hf, gb, mcat, l)
        nt = n_xt if last else n_xt + 1
        xn, h2, code, gate, counts = _mixout(
            xa, ca, ctx_block, ux, y_ssm, mod_l, band, icnt, w_pool[l].astype(BF16),
            s_pool[l].reshape(1, -1), ssm_d[l].reshape(1, -1), w_glu[l].astype(BF16),
            b_glu[l].reshape(1, -1), w_out[l].astype(BF16), g_ffn[l].reshape(1, d), wr_split,
            b_router.reshape(1, -1), tri, nt, n_xt)
        tok = _moe(h2, code, counts, gate, xn, mod_l, g_final.reshape(1, d), w_gate, w_up, w_down,
                   l, nt, n_xt, last)
        xa, ca, ctx_block = tok, tok, n_xt
    return tok.reshape(bsz, seq, d)
```

```python
import functools

import numpy as np
import jax
import jax.numpy as jnp
from jax import lax
from jax.experimental import pallas as pl
from jax.experimental.pallas import tpu as pltpu

F32 = jnp.float32
BF16 = jnp.bfloat16

D_MODEL = 1024
POOL_W = 512
SSM_W = 512
POOL_WINDOWS = (2, 4, 8, 16)
POOL_GW = 128
SSM_GROUP = 16
SSM_GROUPS = 32
SSM_STATE = 64
N_EXPERTS = 32
N_EXPERT_GROUPS = 4
EXPERTS_PER_GROUP = 8
D_EXPERT = 512
GRID_W = 64
EPS = 1e-6
N_MOD = 6

LANES = 128
SUBLANES = 8
TM = 256
CH = 8
GB = 8
NGB = SSM_GROUPS // GB
CW = CH * LANES
SW = 2 * GB * SSM_STATE
SR = 256
MOE_BLK = 256
ROW_TILE = D_MODEL // LANES
VMEM_LIMIT = 48 * 1024 * 1024


def _cparams(sem):
    return pltpu.CompilerParams(dimension_semantics=sem, vmem_limit_bytes=VMEM_LIMIT)


def _rmsnorm_mod(x, g, shift, scale):
    ms = jnp.mean(x * x, axis=-1, keepdims=True)
    y = x * lax.rsqrt(ms + EPS) * g
    return y * (1.0 + scale) + shift


def _mod_kernel(c_ref, w_ref, b_ref, o_ref):
    c = c_ref[...]
    a = c * jax.nn.sigmoid(c)
    o_ref[...] = jnp.dot(a, w_ref[...], preferred_element_type=F32,
                         precision=lax.Precision.HIGHEST) + b_ref[...]


def _modulation(cvec, w_mod, b_mod):
    depth, d, n = w_mod.shape
    tn = 1536
    return pl.pallas_call(
        _mod_kernel,
        grid=(depth, n // tn),
        in_specs=[
            pl.BlockSpec((SUBLANES, d), lambda l, j: (0, 0)),
            pl.BlockSpec((None, d, tn), lambda l, j: (l, 0, j)),
            pl.BlockSpec((None, 1, tn), lambda l, j: (l, 0, j)),
        ],
        out_specs=pl.BlockSpec((None, SUBLANES, tn), lambda l, j: (l, 0, j)),
        out_shape=jax.ShapeDtypeStruct((depth, SUBLANES, n), F32),
        compiler_params=_cparams(("arbitrary", "arbitrary")),
        name="modulation",
    )(cvec, w_mod, b_mod.reshape(depth, 1, n))


def _token_tile(x_ref, c_ref, n_x_tiles):
    return jnp.where(pl.program_id(0) == n_x_tiles, c_ref[...], x_ref[...])


def _inproj_kernel(x_ref, c_ref, mod_ref, g_ref, w_ref, ux_ref, uv_ref, scr, *, n_x_tiles):
    i = pl.program_id(0)

    @pl.when(i <= n_x_tiles)
    def _():
        x = _token_tile(x_ref, c_ref, n_x_tiles)
        h = _rmsnorm_mod(x, g_ref[...], mod_ref[0:1, :], mod_ref[1:2, :])
        ux = jnp.dot(h.astype(BF16), w_ref[...], preferred_element_type=F32)
        ux_ref[...] = ux
        rows = TM // CH
        for b in range(NGB):
            scr[b] = ux[:, POOL_W + LANES * b:POOL_W + LANES * (b + 1)]
            for t in range(CH):
                piece = scr[b, pl.ds(t, rows, stride=CH), :]
                uv_ref[:, pl.ds((b * CH + t) * LANES, LANES)] = piece.astype(BF16)

    @pl.when(i > n_x_tiles)
    def _():
        uv_ref[...] = jnp.zeros_like(uv_ref)


def _token_specs(n_x_tiles, ctx_block):
    return [pl.BlockSpec((TM, D_MODEL), lambda i: (jnp.minimum(i, n_x_tiles - 1), 0)),
            pl.BlockSpec((TM, D_MODEL), lambda i: (ctx_block, 0))]


def _inproj(xa, ca, ctx_block, mod_l, g, w_bf, n_x_tiles):
    nt = n_x_tiles + 1
    n_steps = n_x_tiles + SR // (TM // CH)
    sel = lambda i: (jnp.where(i >= n_x_tiles, 1, 0), 0, 0)
    last = lambda i: (jnp.minimum(i, n_x_tiles), 0)
    return pl.pallas_call(
        functools.partial(_inproj_kernel, n_x_tiles=n_x_tiles),
        grid=(n_steps,),
        in_specs=_token_specs(n_x_tiles, ctx_block) + [
            pl.BlockSpec((None, N_MOD, D_MODEL), sel),
            pl.BlockSpec((1, D_MODEL), lambda i: (0, 0)),
            pl.BlockSpec((D_MODEL, D_MODEL), lambda i: (0, 0)),
        ],
        out_specs=[
            pl.BlockSpec((TM, D_MODEL), last),
            pl.BlockSpec((TM // CH, NGB * CW), lambda i: (i, 0)),
        ],
        out_shape=[
            jax.ShapeDtypeStruct((nt * TM, D_MODEL), F32),
            jax.ShapeDtypeStruct((n_steps * TM // CH, NGB * CW), BF16),
        ],
        scratch_shapes=[pltpu.VMEM((NGB, TM, LANES), F32)],
        compiler_params=_cparams(("arbitrary",)),
        name="mixer_inproj",
    )(xa, ca, mod_l, g, w_bf)


def _chunk_scan(s_ref, tab_ref, carry_ref, out_ref, nblk, reverse):
    half = SW // 2
    ncol = half // LANES
    row = lax.broadcasted_iota(jnp.int32, (SUBLANES, LANES), 0)
    edge = (row == SUBLANES - 1) if reverse else (row == 0)
    last = 0 if reverse else SUBLANES - 1

    def sub_block(r0, carry):
        outs_r, outs_i, new_carry = [], [], []
        for j in range(ncol):
            cr, ci = carry[2 * j], carry[2 * j + 1]
            lre = pl.ds(LANES * j, LANES)
            lim = pl.ds(half + LANES * j, LANES)
            zr = s_ref[pl.ds(r0, SUBLANES), lre]
            zi = s_ref[pl.ds(r0, SUBLANES), lim]
            for q, k in enumerate((1, 2, 4)):
                ar = tab_ref[pl.ds(16 * q, SUBLANES), lre]
                ai = tab_ref[pl.ds(16 * q + 8, SUBLANES), lre]
                sh = SUBLANES - k if reverse else k
                sr = pltpu.roll(zr, sh, axis=0)
                si = pltpu.roll(zi, sh, axis=0)
                zr, zi = zr + ar * sr - ai * si, zi + ar * si + ai * sr
            pr = tab_ref[pl.ds(48, SUBLANES), lre]
            pi = tab_ref[pl.ds(56, SUBLANES), lre]
            zr, zi = zr + pr * cr - pi * ci, zi + pr * ci + pi * cr
            sh1 = SUBLANES - 1 if reverse else 1
            outs_r.append(jnp.where(edge, cr, pltpu.roll(zr, sh1, axis=0)))
            outs_i.append(jnp.where(edge, ci, pltpu.roll(zi, sh1, axis=0)))
            new_carry.append(jnp.broadcast_to(zr[last:last + 1, :], (SUBLANES, LANES)))
            new_carry.append(jnp.broadcast_to(zi[last:last + 1, :], (SUBLANES, LANES)))
        return outs_r, outs_i, tuple(new_carry)

    def body(it, carry):
        bi = (nblk - 1 - it) if reverse else it
        r0 = pl.multiple_of(bi * 2 * SUBLANES, 2 * SUBLANES)
        if reverse:
            hi_r, hi_i, carry = sub_block(r0 + SUBLANES, carry)
            lo_r, lo_i, carry = sub_block(r0, carry)
        else:
            lo_r, lo_i, carry = sub_block(r0, carry)
            hi_r, hi_i, carry = sub_block(r0 + SUBLANES, carry)
        for j in range(ncol):
            out_ref[pl.ds(r0, 2 * SUBLANES), pl.ds(LANES * j, LANES)] = (
                jnp.concatenate([lo_r[j], hi_r[j]], axis=0).astype(BF16))
            out_ref[pl.ds(r0, 2 * SUBLANES), pl.ds(half + LANES * j, LANES)] = (
                jnp.concatenate([lo_i[j], hi_i[j]], axis=0).astype(BF16))
        return carry

    init = tuple(carry_ref[:, pl.ds(LANES * c, LANES)] for c in range(2 * ncol))
    final = lax.fori_loop(0, nblk, body, init)
    for c in range(2 * ncol):
        carry_ref[:, pl.ds(LANES * c, LANES)] = final[c]


def _ssm_state_kernel(uf_ref, ub_ref, msf_ref, msb_ref, tf_ref, tb_ref, hf_ref, gb_ref,
                      sf, sb, cf, cb, *, ctx_rows):
    step = pl.program_id(1)

    @pl.when(step == 0)
    def _():
        cf[...] = jnp.zeros_like(cf)
        cb[...] = jnp.zeros_like(cb)
        hf_ref[...] = jnp.zeros_like(hf_ref)
        gb_ref[...] = jnp.zeros_like(gb_ref)

    sf[...] = jnp.dot(uf_ref[...], msf_ref[...], preferred_element_type=F32)
    sb[...] = jnp.dot(ub_ref[...], msb_ref[...], preferred_element_type=F32)
    nblk = jnp.where(step == 0, ctx_rows // (2 * SUBLANES), SR // (2 * SUBLANES))
    _chunk_scan(sf, tf_ref, cf, hf_ref, nblk, reverse=False)
    _chunk_scan(sb, tb_ref, cb, gb_ref, nblk, reverse=True)


def _ssm_states(uv_ext, ms, tab, layer, n_xt, ctx_rows):
    rows = uv_ext.shape[0]
    fwd = lambda b, s: (jnp.where(s == 0, n_xt, s - 1), b)
    bwd = lambda b, s: (jnp.where(s == 0, n_xt, n_xt - s), b)
    return pl.pallas_call(
        functools.partial(_ssm_state_kernel, ctx_rows=ctx_rows),
        grid=(NGB, n_xt + 1),
        in_specs=[
            pl.BlockSpec((SR, CW), fwd),
            pl.BlockSpec((SR, CW), bwd),
            pl.BlockSpec((None, None, None, CW, SW), lambda b, s: (layer, b, 0, 0, 0)),
            pl.BlockSpec((None, None, None, CW, SW), lambda b, s: (layer, b, 1, 0, 0)),
            pl.BlockSpec((None, None, None, 8 * SUBLANES, SW // 2),
                         lambda b, s: (layer, 0, b, 0, 0)),
            pl.BlockSpec((None, None, None, 8 * SUBLANES, SW // 2),
                         lambda b, s: (layer, 1, b, 0, 0)),
        ],
        out_specs=[pl.BlockSpec((SR, SW), fwd), pl.BlockSpec((SR, SW), bwd)],
        out_shape=[jax.ShapeDtypeStruct((rows, NGB * SW), BF16)] * 2,
        scratch_shapes=[pltpu.VMEM((SR, SW), F32), pltpu.VMEM((SR, SW), F32),
                        pltpu.VMEM((SUBLANES, SW), F32), pltpu.VMEM((SUBLANES, SW), F32)],
        compiler_params=_cparams(("arbitrary", "arbitrary")),
        name="ssm_states",
    )(uv_ext, uv_ext, ms, ms, tab, tab)


def _ssm_out_kernel(u_ref, hf_ref, gb_ref, m_ref, y_ref):
    res = jnp.dot(u_ref[...], m_ref[0:CW, :], preferred_element_type=F32)
    res += jnp.dot(hf_ref[...], m_ref[CW:CW + SW, :], preferred_element_type=F32)
    res += jnp.dot(gb_ref[...], m_ref[CW + SW:CW + 2 * SW, :], preferred_element_type=F32)
    for t in range(CH):
        y_ref[pl.ds(t, SR, stride=CH), :] = res[:, t * LANES:(t + 1) * LANES]


def _ssm_readout(uv_ext, hf, gb, mcat, layer):
    rows = uv_ext.shape[0]
    nt = rows // SR
    blk = lambda b, i: (i, b)
    return pl.pallas_call(
        _ssm_out_kernel,
        grid=(NGB, nt),
        in_specs=[
            pl.BlockSpec((SR, CW), blk),
            pl.BlockSpec((SR, SW), blk),
            pl.BlockSpec((SR, SW), blk),
            pl.BlockSpec((None, None, CW + 2 * SW, CW), lambda b, i: (layer, b, 0, 0)),
        ],
        out_specs=pl.BlockSpec((SR * CH, LANES), blk),
        out_shape=jax.ShapeDtypeStruct((rows * CH, SSM_W), F32),
        compiler_params=_cparams(("arbitrary", "arbitrary")),
        name="ssm_readout",
    )(uv_ext, hf, gb, mcat)


def _gelu_tanh(x):
    return 0.5 * x * (1.0 + jnp.tanh(0.7978845608028654 * (x + 0.044715 * x * x * x)))


def _route(s, b_router):
    sel = s + b_router
    lane_i = lax.broadcasted_iota(jnp.int32, s.shape, 1)
    grp = lane_i // EXPERTS_PER_GROUP
    lane = lane_i.astype(F32)
    neg = jnp.float32(-jnp.inf)
    big = jnp.float32(N_EXPERTS)

    def top2(vals):
        m1 = jnp.max(vals, axis=-1, keepdims=True)
        i1 = jnp.min(jnp.where(vals == m1, lane, big), axis=-1, keepdims=True)
        rest = jnp.where(lane == i1, neg, vals)
        m2 = jnp.max(rest, axis=-1, keepdims=True)
        i2 = jnp.min(jnp.where(rest == m2, lane, big), axis=-1, keepdims=True)
        return m1, i1, m2, i2

    best = None
    gidx = None
    for g in range(N_EXPERT_GROUPS):
        m1, _, m2, _ = top2(jnp.where(grp == g, sel, neg))
        score = m1 + m2
        if best is None:
            best, gidx = score, jnp.zeros_like(lane_i[:, 0:1])
        else:
            upd = score > best
            best = jnp.where(upd, score, best)
            gidx = jnp.where(upd, g, gidx)
    _, e1, _, e2 = top2(jnp.where(grp == gidx, sel, neg))
    w1 = jnp.sum(jnp.where(lane == e1, s, 0.0), axis=-1, keepdims=True)
    w2 = jnp.sum(jnp.where(lane == e2, s, 0.0), axis=-1, keepdims=True)
    tot = w1 + w2
    return e1.astype(jnp.int32), e2.astype(jnp.int32), w1 / tot, w2 / tot


def _mixout_kernel(x_ref, c_ref, ux_ref, ys_ref, mod_ref, band_ref, icnt_ref, wp_ref, sp_ref, d_ref,
                   wglu_ref, bglu_ref, wout_ref, gffn_ref, wr_ref, br_ref, tri_ref,
                   xo_ref, hs_ref, pos_ref, gate_ref, cnt_ref, *, n_x_tiles):
    ux = ux_ref[...]
    parts = []
    for g in range(len(POOL_WINDOWS)):
        ug = ux[:, g * POOL_GW:(g + 1) * POOL_GW]
        hi = ug.astype(BF16)
        lo = (ug - hi.astype(F32)).astype(BF16)
        band = band_ref[g]
        wsum = (jnp.dot(band, hi, preferred_element_type=F32)
                + jnp.dot(band, lo, preferred_element_type=F32))
        p = wsum * icnt_ref[:, g * POOL_GW:(g + 1) * POOL_GW] - ug
        parts.append(jnp.dot(p.astype(BF16), wp_ref[g], preferred_element_type=F32))
    pool = jnp.concatenate(parts, axis=-1) * sp_ref[...]

    y = ys_ref[...] + d_ref[...] * ux[:, POOL_W:]
    y = _gelu_tanh(y)
    z = jnp.dot(y.astype(BF16), wglu_ref[...], preferred_element_type=F32) + bglu_ref[...]
    glu = y * jax.nn.sigmoid(z)

    cat = jnp.concatenate([pool, glu], axis=-1).astype(BF16)
    o = jnp.dot(cat, wout_ref[...], preferred_element_type=F32)
    xn = _token_tile(x_ref, c_ref, n_x_tiles) + mod_ref[2:3, :] * o
    xo_ref[...] = xn

    h2 = _rmsnorm_mod(xn, gffn_ref[...], mod_ref[3:4, :], mod_ref[4:5, :])
    h_hi = h2.astype(BF16)
    h_lo = (h2 - h_hi.astype(F32)).astype(BF16)
    wr = wr_ref[...]
    p_hi = jnp.dot(h_hi, wr, preferred_element_type=F32)
    p_lo = jnp.dot(h_lo, wr[:, :N_EXPERTS], preferred_element_type=F32)
    logits = p_hi[:, :N_EXPERTS] + (p_hi[:, N_EXPERTS:] + p_lo)
    e1, e2, g1, g2 = _route(jax.nn.sigmoid(logits), br_ref[...])
    gate_ref[...] = jnp.concatenate([g1, g2], axis=-1)

    lane = lax.broadcasted_iota(jnp.int32, (TM, N_EXPERTS), 1)
    oh0 = jnp.where(lane == e1, 1.0, 0.0)
    oh1 = jnp.where(lane == e2, 1.0, 0.0)
    tri = tri_ref[...]
    before0 = jnp.dot(tri, oh0.astype(BF16), preferred_element_type=F32)
    before1 = jnp.dot(tri, oh1.astype(BF16), preferred_element_type=F32)
    tot0 = jnp.sum(oh0, axis=0, keepdims=True)
    tot1 = jnp.sum(oh1, axis=0, keepdims=True)
    smaller = jnp.where(lane > e1, 1.0, 0.0) + jnp.where(lane > e2, 1.0, 0.0)
    off = jnp.sum(smaller, axis=0, keepdims=True)
    p0 = jnp.sum(oh0 * (off + before0), axis=-1, keepdims=True)
    p1 = jnp.sum(oh1 * (off + tot0 + before1), axis=-1, keepdims=True)
    pos_ref[...] = jnp.concatenate([p0, p1], axis=-1).astype(jnp.int32)
    cnt_ref[...] = (tot0 + tot1).astype(jnp.int32)

    slot = lax.broadcasted_iota(jnp.int32, (TM, 2 * TM), 1).astype(F32)
    perm_t = jnp.where(jnp.logical_or(slot == p0, slot == p1), 1.0, 0.0).astype(BF16)
    hs = lax.dot_general(perm_t, h2.astype(BF16), (((0,), (0,)), ((), ())),
                         preferred_element_type=F32)
    for q in range(ROW_TILE):
        hs_ref[pl.ds(q, 2 * TM, stride=ROW_TILE), :] = hs[:, q * LANES:(q + 1) * LANES]


def _mixout(xa, ca, ctx_block, ux, y_ssm, mod_l, band, icnt, wp_bf, sp, dvec, wglu_bf, bglu, wout_bf,
            gffn, w_router, b_router, tri, nt, n_x_tiles):
    t = nt * TM
    sel = lambda i: (jnp.where(i == n_x_tiles, 1, 0), 0, 0)
    sel4 = lambda i: (jnp.where(i == n_x_tiles, 1, 0), 0, 0, 0)
    row = lambda i: (i, 0)
    fix2 = lambda i: (0, 0)
    return pl.pallas_call(
        functools.partial(_mixout_kernel, n_x_tiles=n_x_tiles),
        grid=(nt,),
        in_specs=_token_specs(n_x_tiles, ctx_block) + [
            pl.BlockSpec((TM, D_MODEL), row),
            pl.BlockSpec((TM, SSM_W), row),
            pl.BlockSpec((None, N_MOD, D_MODEL), sel),
            pl.BlockSpec((None, len(POOL_WINDOWS), TM, TM), sel4),
            pl.BlockSpec((None, TM, POOL_W), sel),
            pl.BlockSpec((len(POOL_WINDOWS), POOL_GW, POOL_GW), lambda i: (0, 0, 0)),
            pl.BlockSpec((1, POOL_W), fix2),
            pl.BlockSpec((1, SSM_W), fix2),
            pl.BlockSpec((SSM_W, SSM_W), fix2),
            pl.BlockSpec((1, SSM_W), fix2),
            pl.BlockSpec((D_MODEL, D_MODEL), fix2),
            pl.BlockSpec((1, D_MODEL), fix2),
            pl.BlockSpec((D_MODEL, 2 * N_EXPERTS), fix2),
            pl.BlockSpec((1, N_EXPERTS), fix2),
            pl.BlockSpec((TM, TM), fix2),
        ],
        out_specs=[
            pl.BlockSpec((TM, D_MODEL), row),
            pl.BlockSpec((2 * TM * ROW_TILE, LANES), row),
            pl.BlockSpec((TM, 2), row),
            pl.BlockSpec((TM, 2), row),
            pl.BlockSpec((None, 1, N_EXPERTS), lambda i: (i, 0, 0)),
        ],
        out_shape=[
            jax.ShapeDtypeStruct((t, D_MODEL), F32),
            jax.ShapeDtypeStruct((2 * t * ROW_TILE, LANES), F32),
            jax.ShapeDtypeStruct((t, 2), jnp.int32),
            jax.ShapeDtypeStruct((t, 2), F32),
            jax.ShapeDtypeStruct((nt, 1, N_EXPERTS), jnp.int32),
        ],
        compiler_params=_cparams(("arbitrary",)),
        name="mixer_out_router",
    )(xa, ca, ux, y_ssm, mod_l, band, icnt, wp_bf, sp, dvec, wglu_bf, bglu, wout_bf, gffn,
      w_router, b_router, tri)


RUN_BITS = tuple(1 << b for b in range(MOE_BLK.bit_length() - 1, -1, -1))


def _copy_rows(src, src_row, dst, dst_row, n, sem):
    done = 0
    for bit in RUN_BITS:
        take = n & bit

        @pl.when(take != 0)
        def _(done=done, bit=bit):
            s0 = pl.multiple_of((src_row + done) * ROW_TILE, ROW_TILE)
            d0 = pl.multiple_of((dst_row + done) * ROW_TILE, ROW_TILE)
            pltpu.make_async_copy(src.at[pl.ds(s0, bit * ROW_TILE)],
                                  dst.at[pl.ds(d0, bit * ROW_TILE)], sem).start()

        done = done + take


def _wait_rows(src, dst, n, sem):
    size = pl.multiple_of(n * ROW_TILE, ROW_TILE)
    pltpu.make_async_copy(src.at[pl.ds(0, size)], dst.at[pl.ds(0, size)], sem).wait()


def _expert_kernel(be_ref, nbu_ref, nv_ref, rs_ref, re_ref, rsrc_ref, rlen_ref, rdst_ref,
                   hs_hbm, wg_ref, wu_ref, wd_ref, y_ref, xg0, xg1, wg_s, wu_s, wd_s, gsem):
    s = pl.program_id(0)
    nbu = nbu_ref[0]
    xg = (xg0, xg1)

    def gather_start(blk, slot):
        lo_blk = blk * MOE_BLK

        def body(r, c):
            g0 = rdst_ref[r]
            lo = jnp.maximum(g0, lo_blk)
            hi = jnp.minimum(g0 + rlen_ref[r], lo_blk + MOE_BLK)
            _copy_rows(hs_hbm, rsrc_ref[r] + (lo - g0), xg[slot], lo - lo_blk,
                       jnp.maximum(hi - lo, 0), gsem.at[slot])
            return c

        lax.fori_loop(rs_ref[blk], re_ref[blk], body, 0)

    changed = jnp.logical_or(s == 0, be_ref[s] != be_ref[jnp.maximum(s - 1, 0)])

    @pl.when(jnp.logical_and(changed, s < nbu))
    def _():
        wg_s[...] = wg_ref[...].astype(BF16)
        wu_s[...] = wu_ref[...].astype(BF16)
        wd_s[...] = wd_ref[...].astype(BF16)

    def step(slot):
        @pl.when(s == 0)
        def _():
            xg0[...] = jnp.zeros_like(xg0)
            xg1[...] = jnp.zeros_like(xg1)
            gather_start(0, slot)

        @pl.when(s + 1 < nbu)
        def _():
            gather_start(s + 1, 1 - slot)

        _wait_rows(hs_hbm, xg[slot], nv_ref[s], gsem.at[slot])
        xb = jnp.concatenate([xg[slot][pl.ds(q, MOE_BLK, stride=ROW_TILE), :]
                              for q in range(ROW_TILE)], axis=-1).astype(BF16)
        g = jnp.dot(xb, wg_s[...], preferred_element_type=F32)
        u = jnp.dot(xb, wu_s[...], preferred_element_type=F32)
        hid = (g * jax.nn.sigmoid(g)) * u
        out = jnp.dot(hid.astype(BF16), wd_s[...], preferred_element_type=F32)
        for q in range(ROW_TILE):
            y_ref[pl.ds(q, MOE_BLK, stride=ROW_TILE), :] = out[:, q * LANES:(q + 1) * LANES]

    for slot in range(2):
        @pl.when(jnp.logical_and(s < nbu, s % 2 == slot))
        def _(slot=slot):
            step(slot)

    @pl.when(s >= nbu)
    def _():
        y_ref[...] = jnp.zeros_like(y_ref)


def _experts(blk_e, nb_used, blk_valid, rs, re, rsrc, rlen, rdst, hs, w_gate, w_up, w_down, layer):
    nb = blk_e.shape[0]
    wmap = lambda i, be, *_: (layer, be[i], 0, 0)
    buf = pltpu.VMEM((MOE_BLK * ROW_TILE, LANES), F32)
    return pl.pallas_call(
        _expert_kernel,
        grid_spec=pltpu.PrefetchScalarGridSpec(
            num_scalar_prefetch=8,
            grid=(nb,),
            in_specs=[
                pl.BlockSpec(memory_space=pl.ANY),
                pl.BlockSpec((None, None, D_MODEL, D_EXPERT), wmap),
                pl.BlockSpec((None, None, D_MODEL, D_EXPERT), wmap),
                pl.BlockSpec((None, None, D_EXPERT, D_MODEL), wmap),
            ],
            out_specs=pl.BlockSpec((MOE_BLK * ROW_TILE, LANES), lambda i, *_: (i, 0)),
            scratch_shapes=[buf, buf,
                            pltpu.VMEM((D_MODEL, D_EXPERT), BF16),
                            pltpu.VMEM((D_MODEL, D_EXPERT), BF16),
                            pltpu.VMEM((D_EXPERT, D_MODEL), BF16),
                            pltpu.SemaphoreType.DMA((2,))],
        ),
        out_shape=jax.ShapeDtypeStruct((nb * MOE_BLK * ROW_TILE, LANES), F32),
        compiler_params=_cparams(("arbitrary",)),
        name="moe_experts",
    )(blk_e, nb_used, blk_valid, rs, re, rsrc, rlen, rdst, hs, w_gate, w_up, w_down)


def _combine_kernel(meta_ref, x_ref, pos_ref, gate_ref, mod_ref, gfin_ref, ys_hbm, out_ref,
                    st0, st1, sem, *, final, nt):
    i = pl.program_id(0)
    stage = (st0, st1)

    def fetch(tile, slot):
        base = tile * LANES
        for e in range(N_EXPERTS):
            _copy_rows(ys_hbm, meta_ref[base + e], stage[slot], meta_ref[base + 2 * N_EXPERTS + e],
                       meta_ref[base + N_EXPERTS + e], sem.at[slot])

    def step(slot):
        @pl.when(i == 0)
        def _():
            fetch(0, slot)

        @pl.when(i + 1 < nt)
        def _():
            fetch(i + 1, 1 - slot)

        _wait_rows(ys_hbm, stage[slot], 2 * TM, sem.at[slot])
        rows = jnp.concatenate([stage[slot][pl.ds(q, 2 * TM, stride=ROW_TILE), :]
                                for q in range(ROW_TILE)], axis=-1)
        pos = pos_ref[...].astype(F32)
        gate = gate_ref[...]
        slot_id = lax.broadcasted_iota(jnp.int32, (TM, 2 * TM), 1).astype(F32)
        gmat = (jnp.where(slot_id == pos[:, 0:1], gate[:, 0:1], 0.0)
                + jnp.where(slot_id == pos[:, 1:2], gate[:, 1:2], 0.0))
        g_hi = gmat.astype(BF16)
        g_lo = (gmat - g_hi.astype(F32)).astype(BF16)
        r_hi = rows.astype(BF16)
        r_lo = (rows - r_hi.astype(F32)).astype(BF16)
        y = (jnp.dot(g_hi, r_hi, preferred_element_type=F32)
             + (jnp.dot(g_lo, r_hi, preferred_element_type=F32)
                + jnp.dot(g_hi, r_lo, preferred_element_type=F32)))
        xn = x_ref[...] + mod_ref[5:6, :] * y
        if final:
            ms = jnp.mean(xn * xn, axis=-1, keepdims=True)
            xn = xn * lax.rsqrt(ms + EPS) * gfin_ref[...]
        out_ref[...] = xn

    for slot in range(2):
        @pl.when(i % 2 == slot)
        def _(slot=slot):
            step(slot)


def _combine(meta, xn, pos, gate, mod_l, g_final, ys, nt, n_x_tiles, final):
    sel = lambda i, m: (jnp.where(i == n_x_tiles, 1, 0), 0, 0)
    row = lambda i, m: (i, 0)
    stage = pltpu.VMEM((2 * TM * ROW_TILE, LANES), F32)
    return pl.pallas_call(
        functools.partial(_combine_kernel, final=final, nt=nt),
        grid_spec=pltpu.PrefetchScalarGridSpec(
            num_scalar_prefetch=1,
            grid=(nt,),
            in_specs=[
                pl.BlockSpec((TM, D_MODEL), row),
                pl.BlockSpec((TM, 2), row),
                pl.BlockSpec((TM, 2), row),
                pl.BlockSpec((None, N_MOD, D_MODEL), sel),
                pl.BlockSpec((1, D_MODEL), lambda i, m: (0, 0)),
                pl.BlockSpec(memory_space=pl.ANY),
            ],
            out_specs=pl.BlockSpec((TM, D_MODEL), row),
            scratch_shapes=[stage, stage, pltpu.SemaphoreType.DMA((2,))],
        ),
        out_shape=jax.ShapeDtypeStruct((nt * TM, D_MODEL), F32),
        compiler_params=_cparams(("arbitrary",)),
        name="moe_combine",
    )(meta, xn, pos, gate, mod_l, g_final, ys)


def _moe(hs, pos, tcnt, gate, xn, mod_l, g_final, w_gate, w_up, w_down, layer, nt, n_x_tiles,
         final):
    t = nt * TM
    i32 = jnp.int32
    n = tcnt.reshape(nt, N_EXPERTS)
    off = jnp.cumsum(n, axis=1) - n
    base = jnp.cumsum(n, axis=0) - n
    counts = jnp.sum(n, axis=0)
    padded = (counts + MOE_BLK - 1) // MOE_BLK * MOE_BLK
    pend = jnp.cumsum(padded)
    pstart = pend - padded
    nb = (2 * t + N_EXPERTS * (MOE_BLK - 1)) // MOE_BLK + 1
    nb_used = (pend[-1] // MOE_BLK).astype(i32)
    blk_lo = jnp.arange(nb, dtype=i32) * MOE_BLK
    blk_e = jnp.minimum(jnp.sum(pend[None, :] <= jnp.minimum(blk_lo, pend[-1] - MOE_BLK)[:, None],
                                axis=1), N_EXPERTS - 1).astype(i32)
    blk_valid = jnp.clip((pstart + counts)[blk_e] - blk_lo, 0, MOE_BLK).astype(i32)
    gsrc = (pstart[None, :] + base).astype(i32)
    rdst = gsrc.T.reshape(-1)
    rlen = n.T.reshape(-1).astype(i32)
    rsrc = (jnp.arange(nt, dtype=i32)[:, None] * (2 * TM) + off).T.reshape(-1).astype(i32)
    rs = jnp.searchsorted(rdst + rlen, blk_lo, side='right').astype(i32)
    re = jnp.searchsorted(rdst, blk_lo + MOE_BLK, side='left').astype(i32)
    ys = _experts(blk_e, nb_used.reshape(1), blk_valid, rs, re, rsrc, rlen, rdst, hs,
                  w_gate, w_up, w_down, layer)
    meta = jnp.concatenate([gsrc, n.astype(i32), off.astype(i32),
                            jnp.zeros((nt, LANES - 3 * N_EXPERTS), i32)], axis=1).reshape(-1)
    return _combine(meta, xn, pos, gate, mod_l, g_final, ys, nt, n_x_tiles, final)


def _cmul(ar, ai, br, bi):
    return ar * br - ai * bi, ar * bi + ai * br


def _expand_rows(v, rep_ref, mask_ref):
    return jnp.dot(v.astype(BF16), rep_ref[...], preferred_element_type=F32) * mask_ref[...]


def _expand_cols(e, rept_ref, maskt_ref):
    out = lax.dot_general(rept_ref[...], e.astype(BF16), (((1,), (1,)), ((), ())),
                          preferred_element_type=F32)
    return out * maskt_ref[...]


def _ssm_prep_kernel(are_ref, aim_ref, ldt_ref, btr_ref, bti_ref, cr_ref, ci_ref,
                     rep_ref, mask_ref, rept_ref, maskt_ref, ms_ref, mcat_ref, ecat):
    lag_blocks = []
    for d in range(2):
        a_re = jnp.minimum(are_ref[d], -1e-4)
        a_im = aim_ref[d]
        dt = jnp.exp(ldt_ref[d])
        mag = jnp.exp(a_re * dt)
        lr = mag * jnp.cos(a_im * dt)
        li = mag * jnp.sin(a_im * dt)
        den = a_re * a_re + a_im * a_im
        k_re = ((lr - 1.0) * a_re + li * a_im) / den
        k_im = (li * a_re - (lr - 1.0) * a_im) / den
        bbr, bbi = _cmul(k_re, k_im, btr_ref[d], bti_ref[d])
        cr, ci = cr_ref[d], ci_ref[d]
        pr, pi = jnp.ones_like(lr), jnp.zeros_like(lr)
        v0 = None
        for n in range(CH + 1):
            er, ei = _cmul(pr, pi, cr, ci)
            et = jnp.concatenate([_expand_cols(er, rept_ref, maskt_ref),
                                  _expand_cols(-ei, rept_ref, maskt_ref)], axis=0).astype(BF16)
            if n < CH:
                vr, vi = _cmul(pr, pi, bbr, bbi)
                v = jnp.concatenate([_expand_rows(vr, rep_ref, mask_ref),
                                     _expand_rows(vi, rep_ref, mask_ref)], axis=1).astype(BF16)
                if n == 0:
                    v0 = v
                sigma = CH - 1 - n if d == 0 else n
                ms_ref[d, sigma * LANES:(sigma + 1) * LANES, :] = v
                ecat[:, n * LANES:(n + 1) * LANES] = et
            if n >= 1:
                tau = n - 1 if d == 0 else CH - n
                mcat_ref[CW + d * SW:CW + (d + 1) * SW, tau * LANES:(tau + 1) * LANES] = et
            pr, pi = _cmul(pr, pi, lr, li)
        lag_blocks.append(jnp.dot(v0, ecat[...], preferred_element_type=F32))
    kf, kb = lag_blocks
    for s in range(CH):
        for t in range(CH):
            if t > s:
                blk = kf[:, (t - s) * LANES:(t - s + 1) * LANES]
            elif t < s:
                blk = kb[:, (s - t) * LANES:(s - t + 1) * LANES]
            else:
                blk = kf[:, :LANES] + kb[:, :LANES]
            mcat_ref[s * LANES:(s + 1) * LANES, t * LANES:(t + 1) * LANES] = blk.astype(BF16)


def _ssm_operators(a_re, a_im, log_dt, b_re, b_im, c_re, c_im):
    depth = a_re.shape[0]
    rows = SSM_GROUPS * SSM_GROUP
    p = SSM_STATE
    rep_rows = lambda v: jnp.repeat(v, SSM_GROUP, axis=2)
    are = rep_rows(a_re)
    aim = rep_rows(a_im)
    ldt = jnp.broadcast_to(rep_rows(log_dt[..., None]), are.shape)
    btr = jnp.swapaxes(b_re, -1, -2).reshape(depth, 2, rows, p)
    bti = jnp.swapaxes(b_im, -1, -2).reshape(depth, 2, rows, p)
    cr = c_re.reshape(depth, 2, rows, p)
    ci = c_im.reshape(depth, 2, rows, p)
    rep = np.tile(np.eye(p, dtype=np.float32), (1, GB))
    mask = np.kron(np.eye(GB, dtype=np.float32), np.ones((SSM_GROUP, p), np.float32))
    par = pl.BlockSpec((None, 2, LANES, p), lambda l, b: (l, 0, b, 0))
    fix = lambda shape: pl.BlockSpec(shape, lambda l, b: (0, 0))
    return pl.pallas_call(
        _ssm_prep_kernel,
        grid=(depth, NGB),
        in_specs=[par] * 7 + [fix((p, GB * p)), fix((LANES, GB * p)),
                              fix((GB * p, p)), fix((GB * p, LANES))],
        out_specs=[
            pl.BlockSpec((None, None, 2, CW, SW), lambda l, b: (l, b, 0, 0, 0)),
            pl.BlockSpec((None, None, CW + 2 * SW, CW), lambda l, b: (l, b, 0, 0)),
        ],
        out_shape=[
            jax.ShapeDtypeStruct((depth, NGB, 2, CW, SW), BF16),
            jax.ShapeDtypeStruct((depth, NGB, CW + 2 * SW, CW), BF16),
        ],
        scratch_shapes=[pltpu.VMEM((SW, CW), BF16)],
        compiler_params=_cparams(("arbitrary", "arbitrary")),
        name="ssm_operators",
    )(are, aim, ldt, btr, bti, cr, ci, jnp.asarray(rep, BF16), jnp.asarray(mask),
      jnp.asarray(rep.T, BF16), jnp.asarray(mask.T))


def _scan_tables(a_re, a_im, log_dt):
    depth = a_re.shape[0]
    a_re = jnp.minimum(a_re, -1e-4)
    dt = jnp.exp(log_dt)[..., None]
    mag = jnp.exp(a_re * dt)
    lr = mag * jnp.cos(a_im * dt)
    li = mag * jnp.sin(a_im * dt)
    ar, ai = lr, li
    for _ in range(CH - 1):
        ar, ai = _cmul(ar, ai, lr, li)
    apw = [(ar, ai)]
    for _ in range(SUBLANES - 1):
        apw.append(_cmul(apw[-1][0], apw[-1][1], ar, ai))
    apr = jnp.stack([q[0] for q in apw], axis=1)
    api = jnp.stack([q[1] for q in apw], axis=1)
    rows = np.arange(SUBLANES)
    tabs = []
    for d in range(2):
        consts = []
        for k in (1, 2, 4):
            keep = (rows >= k) if d == 0 else (rows < SUBLANES - k)
            keep = jnp.asarray(keep.astype(np.float32))[None, :, None, None]
            consts.append(keep * apr[:, k - 1:k, d])
            consts.append(keep * api[:, k - 1:k, d])
        order = rows if d == 0 else SUBLANES - 1 - rows
        consts.append(apr[:, order, d])
        consts.append(api[:, order, d])
        tabs.append(jnp.stack(consts, axis=1))
    tab = jnp.stack(tabs, axis=1).reshape(depth, 2, 8 * SUBLANES, NGB, GB * SSM_STATE)
    return tab.transpose(0, 1, 3, 2, 4)


def _pool_constants(seg):
    t = np.arange(TM)
    s0 = t // seg * seg
    band = np.zeros((len(POOL_WINDOWS), TM, TM), np.float32)
    icnt = np.zeros((TM, POOL_W), np.float32)
    for g, w in enumerate(POOL_WINDOWS):
        lo = np.maximum(t - w // 2, s0)
        hi = np.minimum(t + w // 2, s0 + seg)
        band[g] = (t[None, :] >= lo[:, None]) & (t[None, :] < hi[:, None])
        icnt[:, g * POOL_GW:(g + 1) * POOL_GW] = (1.0 / (hi - lo).astype(np.float32))[:, None]
    return band, icnt


def kernel(x, c, ctx, c_ctx, w_mod, b_mod, g_mix, g_ffn, w_in, w_out, w_pool, s_pool,
           ssm_a_re, ssm_a_im, ssm_log_dt, ssm_b_re, ssm_b_im, ssm_c_re, ssm_c_im, ssm_d,
           w_glu, b_glu, w_router, b_router, w_gate, w_up, w_down, g_final):
    bsz, seq, d = x.shape
    ctx_len = ctx.shape[1]
    depth = w_mod.shape[0]
    assert bsz == 1 and d == D_MODEL and ctx_len == TM
    assert seq % (SR * CH) == 0 and seq % GRID_W == 0
    n_xt = seq // TM
    n_st = seq // (SR * CH)
    ctx_rows = ctx_len // CH

    cvec = jnp.concatenate([c.reshape(1, d), c_ctx.reshape(1, d),
                            jnp.zeros((SUBLANES - 2, d), F32)], axis=0)
    mod = _modulation(cvec, w_mod, b_mod).reshape(depth, SUBLANES, N_MOD, d)

    band_x, icnt_x = _pool_constants(GRID_W)
    band_c, icnt_c = _pool_constants(ctx_len)
    band = jnp.asarray(np.stack([band_x, band_c]), BF16)
    icnt = jnp.asarray(np.stack([icnt_x, icnt_c]), F32)
    tri = jnp.asarray(np.tril(np.ones((TM, TM), np.float32), -1), BF16)
    wr_hi = w_router.astype(BF16)
    wr_split = jnp.concatenate([wr_hi, (w_router - wr_hi.astype(F32)).astype(BF16)], axis=1)

    ms, mcat = _ssm_operators(ssm_a_re, ssm_a_im, ssm_log_dt, ssm_b_re, ssm_b_im,
                              ssm_c_re, ssm_c_im)
    tab = _scan_tables(ssm_a_re, ssm_a_im, ssm_log_dt)

    xa, ca, ctx_block = x[0], ctx[0], 0
    for l in range(depth):
        last = l == depth - 1
        mod_l = mod[l, :2]
        ux, uv_ext = _inproj(xa, ca, ctx_block, mod_l, g_mix[l].reshape(1, d),
                             w_in[l].astype(BF16), n_xt)
        hf, gb = _ssm_states(uv_ext, ms, tab, l, n_st, ctx_rows)
        y_ssm = _ssm_readout(uv_ext, hf, gb, mcat, l)
        nt = n_xt if last else n_xt + 1
        xn, hs, pos, gate, tcnt = _mixout(
            xa, ca, ctx_block, ux, y_ssm, mod_l, band, icnt, w_pool[l].astype(BF16),
            s_pool[l].reshape(1, -1), ssm_d[l].reshape(1, -1), w_glu[l].astype(BF16),
            b_glu[l].reshape(1, -1), w_out[l].astype(BF16), g_ffn[l].reshape(1, d), wr_split,
            b_router.reshape(1, -1), tri, nt, n_xt)
        tok = _moe(hs, pos, tcnt, gate, xn, mod_l, g_final.reshape(1, d), w_gate, w_up, w_down,
                   l, nt, n_xt, last)
        xa, ca, ctx_block = tok, tok, n_xt
    return tok.reshape(bsz, seq, d)
```

```python
import functools

import numpy as np
import jax
import jax.numpy as jnp
from jax import lax
from jax.experimental import pallas as pl
from jax.experimental.pallas import tpu as pltpu

F32 = jnp.float32
BF16 = jnp.bfloat16

D_MODEL = 1024
POOL_W = 512
SSM_W = 512
POOL_WINDOWS = (2, 4, 8, 16)
POOL_GW = 128
SSM_GROUP = 16
SSM_GROUPS = 32
SSM_STATE = 64
N_EXPERTS = 32
N_EXPERT_GROUPS = 4
EXPERTS_PER_GROUP = 8
D_EXPERT = 512
GRID_W = 64
EPS = 1e-6
N_MOD = 6

LANES = 128
SUBLANES = 8
TM = 256
CH = 8
GB = 8
NGB = SSM_GROUPS // GB
CW = CH * LANES
SW = 2 * GB * SSM_STATE
SR = 256
MOE_BLK = 256
ROW_TILE = D_MODEL // LANES
VMEM_LIMIT = 48 * 1024 * 1024


def _cparams(sem):
    return pltpu.CompilerParams(dimension_semantics=sem, vmem_limit_bytes=VMEM_LIMIT)


def _rmsnorm_mod(x, g, shift, scale):
    ms = jnp.mean(x * x, axis=-1, keepdims=True)
    y = x * lax.rsqrt(ms + EPS) * g
    return y * (1.0 + scale) + shift


def _mod_kernel(c_ref, w_ref, b_ref, o_ref):
    c = c_ref[...]
    a = c * jax.nn.sigmoid(c)
    o_ref[...] = jnp.dot(a, w_ref[...], preferred_element_type=F32,
                         precision=lax.Precision.HIGHEST) + b_ref[...]


def _modulation(cvec, w_mod, b_mod):
    depth, d, n = w_mod.shape
    tn = 1536
    return pl.pallas_call(
        _mod_kernel,
        grid=(depth, n // tn),
        in_specs=[
            pl.BlockSpec((SUBLANES, d), lambda l, j: (0, 0)),
            pl.BlockSpec((None, d, tn), lambda l, j: (l, 0, j)),
            pl.BlockSpec((None, 1, tn), lambda l, j: (l, 0, j)),
        ],
        out_specs=pl.BlockSpec((None, SUBLANES, tn), lambda l, j: (l, 0, j)),
        out_shape=jax.ShapeDtypeStruct((depth, SUBLANES, n), F32),
        compiler_params=_cparams(("arbitrary", "arbitrary")),
        name="modulation",
    )(cvec, w_mod, b_mod.reshape(depth, 1, n))


def _token_tile(x_ref, c_ref, n_x_tiles):
    return jnp.where(pl.program_id(0) == n_x_tiles, c_ref[...], x_ref[...])


def _inproj_kernel(x_ref, c_ref, mod_ref, g_ref, w_ref, ux_ref, uv_ref, scr, *, n_x_tiles):
    i = pl.program_id(0)

    @pl.when(i <= n_x_tiles)
    def _():
        x = _token_tile(x_ref, c_ref, n_x_tiles)
        h = _rmsnorm_mod(x, g_ref[...], mod_ref[0:1, :], mod_ref[1:2, :])
        ux = jnp.dot(h.astype(BF16), w_ref[...], preferred_element_type=F32)
        ux_ref[...] = ux
        rows = TM // CH
        for b in range(NGB):
            scr[b] = ux[:, POOL_W + LANES * b:POOL_W + LANES * (b + 1)]
            for t in range(CH):
                piece = scr[b, pl.ds(t, rows, stride=CH), :]
                uv_ref[:, pl.ds((b * CH + t) * LANES, LANES)] = piece.astype(BF16)

    @pl.when(i > n_x_tiles)
    def _():
        uv_ref[...] = jnp.zeros_like(uv_ref)


def _token_specs(n_x_tiles, ctx_block):
    return [pl.BlockSpec((TM, D_MODEL), lambda i: (jnp.minimum(i, n_x_tiles - 1), 0)),
            pl.BlockSpec((TM, D_MODEL), lambda i: (ctx_block, 0))]


def _inproj(xa, ca, ctx_block, mod_l, g, w_bf, n_x_tiles):
    nt = n_x_tiles + 1
    n_steps = n_x_tiles + SR // (TM // CH)
    sel = lambda i: (jnp.where(i >= n_x_tiles, 1, 0), 0, 0)
    last = lambda i: (jnp.minimum(i, n_x_tiles), 0)
    return pl.pallas_call(
        functools.partial(_inproj_kernel, n_x_tiles=n_x_tiles),
        grid=(n_steps,),
        in_specs=_token_specs(n_x_tiles, ctx_block) + [
            pl.BlockSpec((None, N_MOD, D_MODEL), sel),
            pl.BlockSpec((1, D_MODEL), lambda i: (0, 0)),
            pl.BlockSpec((D_MODEL, D_MODEL), lambda i: (0, 0)),
        ],
        out_specs=[
            pl.BlockSpec((TM, D_MODEL), last),
            pl.BlockSpec((TM // CH, NGB * CW), lambda i: (i, 0)),
        ],
        out_shape=[
            jax.ShapeDtypeStruct((nt * TM, D_MODEL), F32),
            jax.ShapeDtypeStruct((n_steps * TM // CH, NGB * CW), BF16),
        ],
        scratch_shapes=[pltpu.VMEM((NGB, TM, LANES), F32)],
        compiler_params=_cparams(("arbitrary",)),
        name="mixer_inproj",
    )(xa, ca, mod_l, g, w_bf)


def _chunk_scan(s_ref, tab_ref, carry_ref, out_ref, nblk, reverse):
    half = SW // 2
    ncol = half // LANES
    row = lax.broadcasted_iota(jnp.int32, (SUBLANES, LANES), 0)
    edge = (row == SUBLANES - 1) if reverse else (row == 0)
    last = 0 if reverse else SUBLANES - 1

    def sub_block(r0, carry):
        outs_r, outs_i, new_carry = [], [], []
        for j in range(ncol):
            cr, ci = carry[2 * j], carry[2 * j + 1]
            lre = pl.ds(LANES * j, LANES)
            lim = pl.ds(half + LANES * j, LANES)
            zr = s_ref[pl.ds(r0, SUBLANES), lre]
            zi = s_ref[pl.ds(r0, SUBLANES), lim]
            for q, k in enumerate((1, 2, 4)):
                ar = tab_ref[pl.ds(16 * q, SUBLANES), lre]
                ai = tab_ref[pl.ds(16 * q + 8, SUBLANES), lre]
                sh = SUBLANES - k if reverse else k
                sr = pltpu.roll(zr, sh, axis=0)
                si = pltpu.roll(zi, sh, axis=0)
                zr, zi = zr + ar * sr - ai * si, zi + ar * si + ai * sr
            pr = tab_ref[pl.ds(48, SUBLANES), lre]
            pi = tab_ref[pl.ds(56, SUBLANES), lre]
            zr, zi = zr + pr * cr - pi * ci, zi + pr * ci + pi * cr
            sh1 = SUBLANES - 1 if reverse else 1
            outs_r.append(jnp.where(edge, cr, pltpu.roll(zr, sh1, axis=0)))
            outs_i.append(jnp.where(edge, ci, pltpu.roll(zi, sh1, axis=0)))
            new_carry.append(jnp.broadcast_to(zr[last:last + 1, :], (SUBLANES, LANES)))
            new_carry.append(jnp.broadcast_to(zi[last:last + 1, :], (SUBLANES, LANES)))
        return outs_r, outs_i, tuple(new_carry)

    def body(it, carry):
        bi = (nblk - 1 - it) if reverse else it
        r0 = pl.multiple_of(bi * 2 * SUBLANES, 2 * SUBLANES)
        if reverse:
            hi_r, hi_i, carry = sub_block(r0 + SUBLANES, carry)
            lo_r, lo_i, carry = sub_block(r0, carry)
        else:
            lo_r, lo_i, carry = sub_block(r0, carry)
            hi_r, hi_i, carry = sub_block(r0 + SUBLANES, carry)
        for j in range(ncol):
            out_ref[pl.ds(r0, 2 * SUBLANES), pl.ds(LANES * j, LANES)] = (
                jnp.concatenate([lo_r[j], hi_r[j]], axis=0).astype(BF16))
            out_ref[pl.ds(r0, 2 * SUBLANES), pl.ds(half + LANES * j, LANES)] = (
                jnp.concatenate([lo_i[j], hi_i[j]], axis=0).astype(BF16))
        return carry

    init = tuple(carry_ref[:, pl.ds(LANES * c, LANES)] for c in range(2 * ncol))
    final = lax.fori_loop(0, nblk, body, init)
    for c in range(2 * ncol):
        carry_ref[:, pl.ds(LANES * c, LANES)] = final[c]


def _ssm_state_kernel(uf_ref, ub_ref, msf_ref, msb_ref, tf_ref, tb_ref, hf_ref, gb_ref,
                      sf, sb, cf, cb, *, ctx_rows):
    step = pl.program_id(1)

    @pl.when(step == 0)
    def _():
        cf[...] = jnp.zeros_like(cf)
        cb[...] = jnp.zeros_like(cb)
        hf_ref[...] = jnp.zeros_like(hf_ref)
        gb_ref[...] = jnp.zeros_like(gb_ref)

    sf[...] = jnp.dot(uf_ref[...], msf_ref[...], preferred_element_type=F32)
    sb[...] = jnp.dot(ub_ref[...], msb_ref[...], preferred_element_type=F32)
    nblk = jnp.where(step == 0, ctx_rows // (2 * SUBLANES), SR // (2 * SUBLANES))
    _chunk_scan(sf, tf_ref, cf, hf_ref, nblk, reverse=False)
    _chunk_scan(sb, tb_ref, cb, gb_ref, nblk, reverse=True)


def _ssm_states(uv_ext, ms, tab, layer, n_xt, ctx_rows):
    rows = uv_ext.shape[0]
    fwd = lambda b, s: (jnp.where(s == 0, n_xt, s - 1), b)
    bwd = lambda b, s: (jnp.where(s == 0, n_xt, n_xt - s), b)
    return pl.pallas_call(
        functools.partial(_ssm_state_kernel, ctx_rows=ctx_rows),
        grid=(NGB, n_xt + 1),
        in_specs=[
            pl.BlockSpec((SR, CW), fwd),
            pl.BlockSpec((SR, CW), bwd),
            pl.BlockSpec((None, None, None, CW, SW), lambda b, s: (layer, b, 0, 0, 0)),
            pl.BlockSpec((None, None, None, CW, SW), lambda b, s: (layer, b, 1, 0, 0)),
            pl.BlockSpec((None, None, None, 8 * SUBLANES, SW // 2),
                         lambda b, s: (layer, 0, b, 0, 0)),
            pl.BlockSpec((None, None, None, 8 * SUBLANES, SW // 2),
                         lambda b, s: (layer, 1, b, 0, 0)),
        ],
        out_specs=[pl.BlockSpec((SR, SW), fwd), pl.BlockSpec((SR, SW), bwd)],
        out_shape=[jax.ShapeDtypeStruct((rows, NGB * SW), BF16)] * 2,
        scratch_shapes=[pltpu.VMEM((SR, SW), F32), pltpu.VMEM((SR, SW), F32),
                        pltpu.VMEM((SUBLANES, SW), F32), pltpu.VMEM((SUBLANES, SW), F32)],
        compiler_params=_cparams(("arbitrary", "arbitrary")),
        name="ssm_states",
    )(uv_ext, uv_ext, ms, ms, tab, tab)


def _ssm_out_kernel(u_ref, hf_ref, gb_ref, m_ref, y_ref):
    res = jnp.dot(u_ref[...], m_ref[0:CW, :], preferred_element_type=F32)
    res += jnp.dot(hf_ref[...], m_ref[CW:CW + SW, :], preferred_element_type=F32)
    res += jnp.dot(gb_ref[...], m_ref[CW + SW:CW + 2 * SW, :], preferred_element_type=F32)
    for t in range(CH):
        y_ref[pl.ds(t, SR, stride=CH), :] = res[:, t * LANES:(t + 1) * LANES]


def _ssm_readout(uv_ext, hf, gb, mcat, layer):
    rows = uv_ext.shape[0]
    nt = rows // SR
    blk = lambda b, i: (i, b)
    return pl.pallas_call(
        _ssm_out_kernel,
        grid=(NGB, nt),
        in_specs=[
            pl.BlockSpec((SR, CW), blk),
            pl.BlockSpec((SR, SW), blk),
            pl.BlockSpec((SR, SW), blk),
            pl.BlockSpec((None, None, CW + 2 * SW, CW), lambda b, i: (layer, b, 0, 0)),
        ],
        out_specs=pl.BlockSpec((SR * CH, LANES), blk),
        out_shape=jax.ShapeDtypeStruct((rows * CH, SSM_W), F32),
        compiler_params=_cparams(("arbitrary", "arbitrary")),
        name="ssm_readout",
    )(uv_ext, hf, gb, mcat)


def _gelu_tanh(x):
    return 0.5 * x * (1.0 + jnp.tanh(0.7978845608028654 * (x + 0.044715 * x * x * x)))


def _route(s, b_router):
    sel = s + b_router
    lane_i = lax.broadcasted_iota(jnp.int32, s.shape, 1)
    grp = lane_i // EXPERTS_PER_GROUP
    lane = lane_i.astype(F32)
    neg = jnp.float32(-jnp.inf)
    big = jnp.float32(N_EXPERTS)

    def top2(vals):
        m1 = jnp.max(vals, axis=-1, keepdims=True)
        i1 = jnp.min(jnp.where(vals == m1, lane, big), axis=-1, keepdims=True)
        rest = jnp.where(lane == i1, neg, vals)
        m2 = jnp.max(rest, axis=-1, keepdims=True)
        i2 = jnp.min(jnp.where(rest == m2, lane, big), axis=-1, keepdims=True)
        return m1, i1, m2, i2

    best = None
    gidx = None
    for g in range(N_EXPERT_GROUPS):
        m1, _, m2, _ = top2(jnp.where(grp == g, sel, neg))
        score = m1 + m2
        if best is None:
            best, gidx = score, jnp.zeros_like(lane_i[:, 0:1])
        else:
            upd = score > best
            best = jnp.where(upd, score, best)
            gidx = jnp.where(upd, g, gidx)
    _, e1, _, e2 = top2(jnp.where(grp == gidx, sel, neg))
    w1 = jnp.sum(jnp.where(lane == e1, s, 0.0), axis=-1, keepdims=True)
    w2 = jnp.sum(jnp.where(lane == e2, s, 0.0), axis=-1, keepdims=True)
    tot = w1 + w2
    return e1.astype(jnp.int32), e2.astype(jnp.int32), w1 / tot, w2 / tot


def _mixout_kernel(x_ref, c_ref, ux_ref, ys_ref, mod_ref, band_ref, icnt_ref, wp_ref, sp_ref, d_ref,
                   wglu_ref, bglu_ref, wout_ref, gffn_ref, wr_ref, br_ref, tri_ref,
                   xo_ref, hs_ref, pos_ref, gate_ref, cnt_ref, *, n_x_tiles):
    ux = ux_ref[...]
    parts = []
    for g in range(len(POOL_WINDOWS)):
        ug = ux[:, g * POOL_GW:(g + 1) * POOL_GW]
        hi = ug.astype(BF16)
        lo = (ug - hi.astype(F32)).astype(BF16)
        band = band_ref[g]
        wsum = (jnp.dot(band, hi, preferred_element_type=F32)
                + jnp.dot(band, lo, preferred_element_type=F32))
        p = wsum * icnt_ref[:, g * POOL_GW:(g + 1) * POOL_GW] - ug
        parts.append(jnp.dot(p.astype(BF16), wp_ref[g], preferred_element_type=F32))
    pool = jnp.concatenate(parts, axis=-1) * sp_ref[...]

    y = ys_ref[...] + d_ref[...] * ux[:, POOL_W:]
    y = _gelu_tanh(y)
    z = jnp.dot(y.astype(BF16), wglu_ref[...], preferred_element_type=F32) + bglu_ref[...]
    glu = y * jax.nn.sigmoid(z)

    cat = jnp.concatenate([pool, glu], axis=-1).astype(BF16)
    o = jnp.dot(cat, wout_ref[...], preferred_element_type=F32)
    xn = _token_tile(x_ref, c_ref, n_x_tiles) + mod_ref[2:3, :] * o
    xo_ref[...] = xn

    h2 = _rmsnorm_mod(xn, gffn_ref[...], mod_ref[3:4, :], mod_ref[4:5, :])
    h_hi = h2.astype(BF16)
    h_lo = (h2 - h_hi.astype(F32)).astype(BF16)
    wr = wr_ref[...]
    p_hi = jnp.dot(h_hi, wr, preferred_element_type=F32)
    p_lo = jnp.dot(h_lo, wr[:, :N_EXPERTS], preferred_element_type=F32)
    logits = p_hi[:, :N_EXPERTS] + (p_hi[:, N_EXPERTS:] + p_lo)
    e1, e2, g1, g2 = _route(jax.nn.sigmoid(logits), br_ref[...])
    gate_ref[...] = jnp.concatenate([g1, g2], axis=-1)

    lane = lax.broadcasted_iota(jnp.int32, (TM, N_EXPERTS), 1)
    oh0 = jnp.where(lane == e1, 1.0, 0.0)
    oh1 = jnp.where(lane == e2, 1.0, 0.0)
    tri = tri_ref[...]
    before0 = jnp.dot(tri, oh0.astype(BF16), preferred_element_type=F32)
    before1 = jnp.dot(tri, oh1.astype(BF16), preferred_element_type=F32)
    tot0 = jnp.sum(oh0, axis=0, keepdims=True)
    tot1 = jnp.sum(oh1, axis=0, keepdims=True)
    smaller = jnp.where(lane > e1, 1.0, 0.0) + jnp.where(lane > e2, 1.0, 0.0)
    off = jnp.sum(smaller, axis=0, keepdims=True)
    p0 = jnp.sum(oh0 * (off + before0), axis=-1, keepdims=True)
    p1 = jnp.sum(oh1 * (off + tot0 + before1), axis=-1, keepdims=True)
    pos_ref[...] = jnp.concatenate([p0, p1], axis=-1).astype(jnp.int32)
    cnt_ref[...] = (tot0 + tot1).astype(jnp.int32)

    slot = lax.broadcasted_iota(jnp.int32, (TM, 2 * TM), 1).astype(F32)
    perm_t = jnp.where(jnp.logical_or(slot == p0, slot == p1), 1.0, 0.0).astype(BF16)
    hs = lax.dot_general(perm_t, h2.astype(BF16), (((0,), (0,)), ((), ())),
                         preferred_element_type=F32)
    for q in range(ROW_TILE):
        hs_ref[pl.ds(q, 2 * TM, stride=ROW_TILE), :] = hs[:, q * LANES:(q + 1) * LANES]


def _mixout(xa, ca, ctx_block, ux, y_ssm, mod_l, band, icnt, wp_bf, sp, dvec, wglu_bf, bglu, wout_bf,
            gffn, w_router, b_router, tri, nt, n_x_tiles):
    t = nt * TM
    sel = lambda i: (jnp.where(i == n_x_tiles, 1, 0), 0, 0)
    sel4 = lambda i: (jnp.where(i == n_x_tiles, 1, 0), 0, 0, 0)
    row = lambda i: (i, 0)
    fix2 = lambda i: (0, 0)
    return pl.pallas_call(
        functools.partial(_mixout_kernel, n_x_tiles=n_x_tiles),
        grid=(nt,),
        in_specs=_token_specs(n_x_tiles, ctx_block) + [
            pl.BlockSpec((TM, D_MODEL), row),
            pl.BlockSpec((TM, SSM_W), row),
            pl.BlockSpec((None, N_MOD, D_MODEL), sel),
            pl.BlockSpec((None, len(POOL_WINDOWS), TM, TM), sel4),
            pl.BlockSpec((None, TM, POOL_W), sel),
            pl.BlockSpec((len(POOL_WINDOWS), POOL_GW, POOL_GW), lambda i: (0, 0, 0)),
            pl.BlockSpec((1, POOL_W), fix2),
            pl.BlockSpec((1, SSM_W), fix2),
            pl.BlockSpec((SSM_W, SSM_W), fix2),
            pl.BlockSpec((1, SSM_W), fix2),
            pl.BlockSpec((D_MODEL, D_MODEL), fix2),
            pl.BlockSpec((1, D_MODEL), fix2),
            pl.BlockSpec((D_MODEL, 2 * N_EXPERTS), fix2),
            pl.BlockSpec((1, N_EXPERTS), fix2),
            pl.BlockSpec((TM, TM), fix2),
        ],
        out_specs=[
            pl.BlockSpec((TM, D_MODEL), row),
            pl.BlockSpec((2 * TM * ROW_TILE, LANES), row),
            pl.BlockSpec((TM, 2), row),
            pl.BlockSpec((TM, 2), row),
            pl.BlockSpec((None, 1, N_EXPERTS), lambda i: (i, 0, 0)),
        ],
        out_shape=[
            jax.ShapeDtypeStruct((t, D_MODEL), F32),
            jax.ShapeDtypeStruct((2 * t * ROW_TILE, LANES), F32),
            jax.ShapeDtypeStruct((t, 2), jnp.int32),
            jax.ShapeDtypeStruct((t, 2), F32),
            jax.ShapeDtypeStruct((nt, 1, N_EXPERTS), jnp.int32),
        ],
        compiler_params=_cparams(("arbitrary",)),
        name="mixer_out_router",
    )(xa, ca, ux, y_ssm, mod_l, band, icnt, wp_bf, sp, dvec, wglu_bf, bglu, wout_bf, gffn,
      w_router, b_router, tri)


RUN_BITS = tuple(1 << b for b in range(MOE_BLK.bit_length() - 1, -1, -1))


def _copy_rows(src, src_row, dst, dst_row, n, sem):
    done = 0
    for bit in RUN_BITS:
        take = n & bit

        @pl.when(take != 0)
        def _(done=done, bit=bit):
            s0 = pl.multiple_of((src_row + done) * ROW_TILE, ROW_TILE)
            d0 = pl.multiple_of((dst_row + done) * ROW_TILE, ROW_TILE)
            pltpu.make_async_copy(src.at[pl.ds(s0, bit * ROW_TILE)],
                                  dst.at[pl.ds(d0, bit * ROW_TILE)], sem).start()

        done = done + take


def _wait_rows(src, dst, n, sem):
    size = pl.multiple_of(n * ROW_TILE, ROW_TILE)
    pltpu.make_async_copy(src.at[pl.ds(0, size)], dst.at[pl.ds(0, size)], sem).wait()


def _expert_kernel(be_ref, nbu_ref, nv_ref, rs_ref, re_ref, rsrc_ref, rlen_ref, rdst_ref,
                   hs_hbm, wg_ref, wu_ref, wd_ref, y_ref, xg0, xg1, wg_s, wu_s, wd_s, gsem):
    s = pl.program_id(0)
    nbu = nbu_ref[0]
    xg = (xg0, xg1)

    def gather_start(blk, slot):
        lo_blk = blk * MOE_BLK

        def body(r, c):
            g0 = rdst_ref[r]
            lo = jnp.maximum(g0, lo_blk)
            hi = jnp.minimum(g0 + rlen_ref[r], lo_blk + MOE_BLK)
            _copy_rows(hs_hbm, rsrc_ref[r] + (lo - g0), xg[slot], lo - lo_blk,
                       jnp.maximum(hi - lo, 0), gsem.at[slot])
            return c

        lax.fori_loop(rs_ref[blk], re_ref[blk], body, 0)

    changed = jnp.logical_or(s == 0, be_ref[s] != be_ref[jnp.maximum(s - 1, 0)])

    @pl.when(jnp.logical_and(changed, s < nbu))
    def _():
        wg_s[...] = wg_ref[...].astype(BF16)
        wu_s[...] = wu_ref[...].astype(BF16)
        wd_s[...] = wd_ref[...].astype(BF16)

    def step(slot):
        @pl.when(s == 0)
        def _():
            xg0[...] = jnp.zeros_like(xg0)
            xg1[...] = jnp.zeros_like(xg1)
            gather_start(0, slot)

        @pl.when(s + 1 < nbu)
        def _():
            gather_start(s + 1, 1 - slot)

        _wait_rows(hs_hbm, xg[slot], nv_ref[s], gsem.at[slot])
        xb = jnp.concatenate([xg[slot][pl.ds(q, MOE_BLK, stride=ROW_TILE), :]
                              for q in range(ROW_TILE)], axis=-1).astype(BF16)
        g = jnp.dot(xb, wg_s[...], preferred_element_type=F32)
        u = jnp.dot(xb, wu_s[...], preferred_element_type=F32)
        hid = (g * jax.nn.sigmoid(g)) * u
        out = jnp.dot(hid.astype(BF16), wd_s[...], preferred_element_type=F32)
        for q in range(ROW_TILE):
            y_ref[pl.ds(q, MOE_BLK, stride=ROW_TILE), :] = out[:, q * LANES:(q + 1) * LANES]

    for slot in range(2):
        @pl.when(jnp.logical_and(s < nbu, s % 2 == slot))
        def _(slot=slot):
            step(slot)

    @pl.when(s >= nbu)
    def _():
        y_ref[...] = jnp.zeros_like(y_ref)


def _experts(blk_e, nb_used, blk_valid, rs, re, rsrc, rlen, rdst, hs, w_gate, w_up, w_down, layer):
    nb = blk_e.shape[0]
    wmap = lambda i, be, *_: (layer, be[i], 0, 0)
    buf = pltpu.VMEM((MOE_BLK * ROW_TILE, LANES), F32)
    return pl.pallas_call(
        _expert_kernel,
        grid_spec=pltpu.PrefetchScalarGridSpec(
            num_scalar_prefetch=8,
            grid=(nb,),
            in_specs=[
                pl.BlockSpec(memory_space=pl.ANY),
                pl.BlockSpec((None, None, D_MODEL, D_EXPERT), wmap),
                pl.BlockSpec((None, None, D_MODEL, D_EXPERT), wmap),
                pl.BlockSpec((None, None, D_EXPERT, D_MODEL), wmap),
            ],
            out_specs=pl.BlockSpec((MOE_BLK * ROW_TILE, LANES), lambda i, *_: (i, 0)),
            scratch_shapes=[buf, buf,
                            pltpu.VMEM((D_MODEL, D_EXPERT), BF16),
                            pltpu.VMEM((D_MODEL, D_EXPERT), BF16),
                            pltpu.VMEM((D_EXPERT, D_MODEL), BF16),
                            pltpu.SemaphoreType.DMA((2,))],
        ),
        out_shape=jax.ShapeDtypeStruct((nb * MOE_BLK * ROW_TILE, LANES), F32),
        compiler_params=_cparams(("arbitrary",)),
        name="moe_experts",
    )(blk_e, nb_used, blk_valid, rs, re, rsrc, rlen, rdst, hs, w_gate, w_up, w_down)


def _combine_kernel(meta_ref, x_ref, pos_ref, gate_ref, mod_ref, gfin_ref, ys_hbm, out_ref,
                    st0, st1, sem, *, final, nt):
    i = pl.program_id(0)
    stage = (st0, st1)

    def fetch(tile, slot):
        base = tile * LANES
        for e in range(N_EXPERTS):
            _copy_rows(ys_hbm, meta_ref[base + e], stage[slot], meta_ref[base + 2 * N_EXPERTS + e],
                       meta_ref[base + N_EXPERTS + e], sem.at[slot])

    def step(slot):
        @pl.when(i == 0)
        def _():
            fetch(0, slot)

        @pl.when(i + 1 < nt)
        def _():
            fetch(i + 1, 1 - slot)

        _wait_rows(ys_hbm, stage[slot], 2 * TM, sem.at[slot])
        rows = jnp.concatenate([stage[slot][pl.ds(q, 2 * TM, stride=ROW_TILE), :]
                                for q in range(ROW_TILE)], axis=-1)
        pos = pos_ref[...].astype(F32)
        gate = gate_ref[...]
        slot_id = lax.broadcasted_iota(jnp.int32, (TM, 2 * TM), 1).astype(F32)
        gmat = (jnp.where(slot_id == pos[:, 0:1], gate[:, 0:1], 0.0)
                + jnp.where(slot_id == pos[:, 1:2], gate[:, 1:2], 0.0))
        g_hi = gmat.astype(BF16)
        g_lo = (gmat - g_hi.astype(F32)).astype(BF16)
        r_hi = rows.astype(BF16)
        r_lo = (rows - r_hi.astype(F32)).astype(BF16)
        y = (jnp.dot(g_hi, r_hi, preferred_element_type=F32)
             + (jnp.dot(g_lo, r_hi, preferred_element_type=F32)
                + jnp.dot(g_hi, r_lo, preferred_element_type=F32)))
        xn = x_ref[...] + mod_ref[5:6, :] * y
        if final:
            ms = jnp.mean(xn * xn, axis=-1, keepdims=True)
            xn = xn * lax.rsqrt(ms + EPS) * gfin_ref[...]
        out_ref[...] = xn

    for slot in range(2):
        @pl.when(i % 2 == slot)
        def _(slot=slot):
            step(slot)


def _combine(meta, xn, pos, gate, mod_l, g_final, ys, nt, n_x_tiles, final):
    sel = lambda i, m: (jnp.where(i == n_x_tiles, 1, 0), 0, 0)
    row = lambda i, m: (i, 0)
    stage = pltpu.VMEM((2 * TM * ROW_TILE, LANES), F32)
    return pl.pallas_call(
        functools.partial(_combine_kernel, final=final, nt=nt),
        grid_spec=pltpu.PrefetchScalarGridSpec(
            num_scalar_prefetch=1,
            grid=(nt,),
            in_specs=[
                pl.BlockSpec((TM, D_MODEL), row),
                pl.BlockSpec((TM, 2), row),
                pl.BlockSpec((TM, 2), row),
                pl.BlockSpec((None, N_MOD, D_MODEL), sel),
                pl.BlockSpec((1, D_MODEL), lambda i, m: (0, 0)),
                pl.BlockSpec(memory_space=pl.ANY),
            ],
            out_specs=pl.BlockSpec((TM, D_MODEL), row),
            scratch_shapes=[stage, stage, pltpu.SemaphoreType.DMA((2,))],
        ),
        out_shape=jax.ShapeDtypeStruct((nt * TM, D_MODEL), F32),
        compiler_params=_cparams(("arbitrary",)),
        name="moe_combine",
    )(meta, xn, pos, gate, mod_l, g_final, ys)


def _moe(hs, pos, tcnt, gate, xn, mod_l, g_final, w_gate, w_up, w_down, layer, nt, n_x_tiles,
         final):
    t = nt * TM
    i32 = jnp.int32
    n = tcnt.reshape(nt, N_EXPERTS)
    off = jnp.cumsum(n, axis=1) - n
    base = jnp.cumsum(n, axis=0) - n
    counts = jnp.sum(n, axis=0)
    padded = (counts + MOE_BLK - 1) // MOE_BLK * MOE_BLK
    pend = jnp.cumsum(padded)
    pstart = pend - padded
    nb = (2 * t + N_EXPERTS * (MOE_BLK - 1)) // MOE_BLK + 1
    nb_used = (pend[-1] // MOE_BLK).astype(i32)
    blk_lo = jnp.arange(nb, dtype=i32) * MOE_BLK
    blk_e = jnp.minimum(jnp.sum(pend[None, :] <= jnp.minimum(blk_lo, pend[-1] - MOE_BLK)[:, None],
                                axis=1), N_EXPERTS - 1).astype(i32)
    blk_valid = jnp.clip((pstart + counts)[blk_e] - blk_lo, 0, MOE_BLK).astype(i32)
    gsrc = (pstart[None, :] + base).astype(i32)
    rdst = gsrc.T.reshape(-1)
    rlen = n.T.reshape(-1).astype(i32)
    rsrc = (jnp.arange(nt, dtype=i32)[:, None] * (2 * TM) + off).T.reshape(-1).astype(i32)
    rs = jnp.sum((rdst + rlen)[None, :] <= blk_lo[:, None], axis=1).astype(i32)
    re = jnp.sum(rdst[None, :] < (blk_lo + MOE_BLK)[:, None], axis=1).astype(i32)
    ys = _experts(blk_e, nb_used.reshape(1), blk_valid, rs, re, rsrc, rlen, rdst, hs,
                  w_gate, w_up, w_down, layer)
    meta = jnp.concatenate([gsrc, n.astype(i32), off.astype(i32),
                            jnp.zeros((nt, LANES - 3 * N_EXPERTS), i32)], axis=1).reshape(-1)
    return _combine(meta, xn, pos, gate, mod_l, g_final, ys, nt, n_x_tiles, final)


def _cmul(ar, ai, br, bi):
    return ar * br - ai * bi, ar * bi + ai * br


def _expand_rows(v, rep_ref, mask_ref):
    return jnp.dot(v.astype(BF16), rep_ref[...], preferred_element_type=F32) * mask_ref[...]


def _expand_cols(e, rept_ref, maskt_ref):
    out = lax.dot_general(rept_ref[...], e.astype(BF16), (((1,), (1,)), ((), ())),
                          preferred_element_type=F32)
    return out * maskt_ref[...]


def _ssm_prep_kernel(are_ref, aim_ref, ldt_ref, btr_ref, bti_ref, cr_ref, ci_ref,
                     rep_ref, mask_ref, rept_ref, maskt_ref, ms_ref, mcat_ref, ecat):
    lag_blocks = []
    for d in range(2):
        a_re = jnp.minimum(are_ref[d], -1e-4)
        a_im = aim_ref[d]
        dt = jnp.exp(ldt_ref[d])
        mag = jnp.exp(a_re * dt)
        lr = mag * jnp.cos(a_im * dt)
        li = mag * jnp.sin(a_im * dt)
        den = a_re * a_re + a_im * a_im
        k_re = ((lr - 1.0) * a_re + li * a_im) / den
        k_im = (li * a_re - (lr - 1.0) * a_im) / den
        bbr, bbi = _cmul(k_re, k_im, btr_ref[d], bti_ref[d])
        cr, ci = cr_ref[d], ci_ref[d]
        pr, pi = jnp.ones_like(lr), jnp.zeros_like(lr)
        v0 = None
        for n in range(CH + 1):
            er, ei = _cmul(pr, pi, cr, ci)
            et = jnp.concatenate([_expand_cols(er, rept_ref, maskt_ref),
                                  _expand_cols(-ei, rept_ref, maskt_ref)], axis=0).astype(BF16)
            if n < CH:
                vr, vi = _cmul(pr, pi, bbr, bbi)
                v = jnp.concatenate([_expand_rows(vr, rep_ref, mask_ref),
                                     _expand_rows(vi, rep_ref, mask_ref)], axis=1).astype(BF16)
                if n == 0:
                    v0 = v
                sigma = CH - 1 - n if d == 0 else n
                ms_ref[d, sigma * LANES:(sigma + 1) * LANES, :] = v
                ecat[:, n * LANES:(n + 1) * LANES] = et
            if n >= 1:
                tau = n - 1 if d == 0 else CH - n
                mcat_ref[CW + d * SW:CW + (d + 1) * SW, tau * LANES:(tau + 1) * LANES] = et
            pr, pi = _cmul(pr, pi, lr, li)
        lag_blocks.append(jnp.dot(v0, ecat[...], preferred_element_type=F32))
    kf, kb = lag_blocks
    for s in range(CH):
        for t in range(CH):
            if t > s:
                blk = kf[:, (t - s) * LANES:(t - s + 1) * LANES]
            elif t < s:
                blk = kb[:, (s - t) * LANES:(s - t + 1) * LANES]
            else:
                blk = kf[:, :LANES] + kb[:, :LANES]
            mcat_ref[s * LANES:(s + 1) * LANES, t * LANES:(t + 1) * LANES] = blk.astype(BF16)


def _ssm_operators(a_re, a_im, log_dt, b_re, b_im, c_re, c_im):
    depth = a_re.shape[0]
    rows = SSM_GROUPS * SSM_GROUP
    p = SSM_STATE
    rep_rows = lambda v: jnp.repeat(v, SSM_GROUP, axis=2)
    are = rep_rows(a_re)
    aim = rep_rows(a_im)
    ldt = jnp.broadcast_to(rep_rows(log_dt[..., None]), are.shape)
    btr = jnp.swapaxes(b_re, -1, -2).reshape(depth, 2, rows, p)
    bti = jnp.swapaxes(b_im, -1, -2).reshape(depth, 2, rows, p)
    cr = c_re.reshape(depth, 2, rows, p)
    ci = c_im.reshape(depth, 2, rows, p)
    rep = np.tile(np.eye(p, dtype=np.float32), (1, GB))
    mask = np.kron(np.eye(GB, dtype=np.float32), np.ones((SSM_GROUP, p), np.float32))
    par = pl.BlockSpec((None, 2, LANES, p), lambda l, b: (l, 0, b, 0))
    fix = lambda shape: pl.BlockSpec(shape, lambda l, b: (0, 0))
    return pl.pallas_call(
        _ssm_prep_kernel,
        grid=(depth, NGB),
        in_specs=[par] * 7 + [fix((p, GB * p)), fix((LANES, GB * p)),
                              fix((GB * p, p)), fix((GB * p, LANES))],
        out_specs=[
            pl.BlockSpec((None, None, 2, CW, SW), lambda l, b: (l, b, 0, 0, 0)),
            pl.BlockSpec((None, None, CW + 2 * SW, CW), lambda l, b: (l, b, 0, 0)),
        ],
        out_shape=[
            jax.ShapeDtypeStruct((depth, NGB, 2, CW, SW), BF16),
            jax.ShapeDtypeStruct((depth, NGB, CW + 2 * SW, CW), BF16),
        ],
        scratch_shapes=[pltpu.VMEM((SW, CW), BF16)],
        compiler_params=_cparams(("arbitrary", "arbitrary")),
        name="ssm_operators",
    )(are, aim, ldt, btr, bti, cr, ci, jnp.asarray(rep, BF16), jnp.asarray(mask),
      jnp.asarray(rep.T, BF16), jnp.asarray(mask.T))


def _scan_tables(a_re, a_im, log_dt):
    depth = a_re.shape[0]
    a_re = jnp.minimum(a_re, -1e-4)
    dt = jnp.exp(log_dt)[..., None]
    mag = jnp.exp(a_re * dt)
    lr = mag * jnp.cos(a_im * dt)
    li = mag * jnp.sin(a_im * dt)
    ar, ai = lr, li
    for _ in range(CH - 1):
        ar, ai = _cmul(ar, ai, lr, li)
    apw = [(ar, ai)]
    for _ in range(SUBLANES - 1):
        apw.append(_cmul(apw[-1][0], apw[-1][1], ar, ai))
    apr = jnp.stack([q[0] for q in apw], axis=1)
    api = jnp.stack([q[1] for q in apw], axis=1)
    rows = np.arange(SUBLANES)
    tabs = []
    for d in range(2):
        consts = []
        for k in (1, 2, 4):
            keep = (rows >= k) if d == 0 else (rows < SUBLANES - k)
            keep = jnp.asarray(keep.astype(np.float32))[None, :, None, None]
            consts.append(keep * apr[:, k - 1:k, d])
            consts.append(keep * api[:, k - 1:k, d])
        order = rows if d == 0 else SUBLANES - 1 - rows
        consts.append(apr[:, order, d])
        consts.append(api[:, order, d])
        tabs.append(jnp.stack(consts, axis=1))
    tab = jnp.stack(tabs, axis=1).reshape(depth, 2, 8 * SUBLANES, NGB, GB * SSM_STATE)
    return tab.transpose(0, 1, 3, 2, 4)


def _pool_constants(seg):
    t = np.arange(TM)
    s0 = t // seg * seg
    band = np.zeros((len(POOL_WINDOWS), TM, TM), np.float32)
    icnt = np.zeros((TM, POOL_W), np.float32)
    for g, w in enumerate(POOL_WINDOWS):
        lo = np.maximum(t - w // 2, s0)
        hi = np.minimum(t + w // 2, s0 + seg)
        band[g] = (t[None, :] >= lo[:, None]) & (t[None, :] < hi[:, None])
        icnt[:, g * POOL_GW:(g + 1) * POOL_GW] = (1.0 / (hi - lo).astype(np.float32))[:, None]
    return band, icnt


def kernel(x, c, ctx, c_ctx, w_mod, b_mod, g_mix, g_ffn, w_in, w_out, w_pool, s_pool,
           ssm_a_re, ssm_a_im, ssm_log_dt, ssm_b_re, ssm_b_im, ssm_c_re, ssm_c_im, ssm_d,
           w_glu, b_glu, w_router, b_router, w_gate, w_up, w_down, g_final):
    bsz, seq, d = x.shape
    ctx_len = ctx.shape[1]
    depth = w_mod.shape[0]
    assert bsz == 1 and d == D_MODEL and ctx_len == TM
    assert seq % (SR * CH) == 0 and seq % GRID_W == 0
    n_xt = seq // TM
    n_st = seq // (SR * CH)
    ctx_rows = ctx_len // CH

    cvec = jnp.concatenate([c.reshape(1, d), c_ctx.reshape(1, d),
                            jnp.zeros((SUBLANES - 2, d), F32)], axis=0)
    mod = _modulation(cvec, w_mod, b_mod).reshape(depth, SUBLANES, N_MOD, d)

    band_x, icnt_x = _pool_constants(GRID_W)
    band_c, icnt_c = _pool_constants(ctx_len)
    band = jnp.asarray(np.stack([band_x, band_c]), BF16)
    icnt = jnp.asarray(np.stack([icnt_x, icnt_c]), F32)
    tri = jnp.asarray(np.tril(np.ones((TM, TM), np.float32), -1), BF16)
    wr_hi = w_router.astype(BF16)
    wr_split = jnp.concatenate([wr_hi, (w_router - wr_hi.astype(F32)).astype(BF16)], axis=1)

    ms, mcat = _ssm_operators(ssm_a_re, ssm_a_im, ssm_log_dt, ssm_b_re, ssm_b_im,
                              ssm_c_re, ssm_c_im)
    tab = _scan_tables(ssm_a_re, ssm_a_im, ssm_log_dt)

    xa, ca, ctx_block = x[0], ctx[0], 0
    for l in range(depth):
        last = l == depth - 1
        mod_l = mod[l, :2]
        ux, uv_ext = _inproj(xa, ca, ctx_block, mod_l, g_mix[l].reshape(1, d),
                             w_in[l].astype(BF16), n_xt)
        hf, gb = _ssm_states(uv_ext, ms, tab, l, n_st, ctx_rows)
        y_ssm = _ssm_readout(uv_ext, hf, gb, mcat, l)
        nt = n_xt if last else n_xt + 1
        xn, hs, pos, gate, tcnt = _mixout(
            xa, ca, ctx_block, ux, y_ssm, mod_l, band, icnt, w_pool[l].astype(BF16),
            s_pool[l].reshape(1, -1), ssm_d[l].reshape(1, -1), w_glu[l].astype(BF16),
            b_glu[l].reshape(1, -1), w_out[l].astype(BF16), g_ffn[l].reshape(1, d), wr_split,
            b_router.reshape(1, -1), tri, nt, n_xt)
        tok = _moe(hs, pos, tcnt, gate, xn, mod_l, g_final.reshape(1, d), w_gate, w_up, w_down,
                   l, nt, n_xt, last)
        xa, ca, ctx_block = tok, tok, n_xt
    return tok.reshape(bsz, seq, d)
```

```python
import functools

import numpy as np
import jax
import jax.numpy as jnp
from jax import lax
from jax.experimental import pallas as pl
from jax.experimental.pallas import tpu as pltpu

F32 = jnp.float32
BF16 = jnp.bfloat16

D_MODEL = 1024
POOL_W = 512
SSM_W = 512
POOL_WINDOWS = (2, 4, 8, 16)
POOL_GW = 128
SSM_GROUP = 16
SSM_GROUPS = 32
SSM_STATE = 64
N_EXPERTS = 32
N_EXPERT_GROUPS = 4
EXPERTS_PER_GROUP = 8
D_EXPERT = 512
GRID_W = 64
EPS = 1e-6
N_MOD = 6

LANES = 128
SUBLANES = 8
TM = 256
CH = 8
GB = 8
NGB = SSM_GROUPS // GB
CW = CH * LANES
SW = 2 * GB * SSM_STATE
SR = 256
MOE_BLK = 512
ROW_TILE = D_MODEL // LANES
VMEM_LIMIT = 48 * 1024 * 1024


def _cparams(sem):
    return pltpu.CompilerParams(dimension_semantics=sem, vmem_limit_bytes=VMEM_LIMIT)


def _rmsnorm_mod(x, g, shift, scale):
    ms = jnp.mean(x * x, axis=-1, keepdims=True)
    y = x * lax.rsqrt(ms + EPS) * g
    return y * (1.0 + scale) + shift


def _mod_kernel(c_ref, w_ref, b_ref, o_ref):
    c = c_ref[...]
    a = c * jax.nn.sigmoid(c)
    o_ref[...] = jnp.dot(a, w_ref[...], preferred_element_type=F32,
                         precision=lax.Precision.HIGHEST) + b_ref[...]


def _modulation(cvec, w_mod, b_mod):
    depth, d, n = w_mod.shape
    tn = 1536
    return pl.pallas_call(
        _mod_kernel,
        grid=(depth, n // tn),
        in_specs=[
            pl.BlockSpec((SUBLANES, d), lambda l, j: (0, 0)),
            pl.BlockSpec((None, d, tn), lambda l, j: (l, 0, j)),
            pl.BlockSpec((None, 1, tn), lambda l, j: (l, 0, j)),
        ],
        out_specs=pl.BlockSpec((None, SUBLANES, tn), lambda l, j: (l, 0, j)),
        out_shape=jax.ShapeDtypeStruct((depth, SUBLANES, n), F32),
        compiler_params=_cparams(("arbitrary", "arbitrary")),
        name="modulation",
    )(cvec, w_mod, b_mod.reshape(depth, 1, n))


def _token_tile(x_ref, c_ref, n_x_tiles):
    return jnp.where(pl.program_id(0) == n_x_tiles, c_ref[...], x_ref[...])


def _inproj_kernel(x_ref, c_ref, mod_ref, g_ref, w_ref, ux_ref, uv_ref, scr, *, n_x_tiles):
    i = pl.program_id(0)

    @pl.when(i <= n_x_tiles)
    def _():
        x = _token_tile(x_ref, c_ref, n_x_tiles)
        h = _rmsnorm_mod(x, g_ref[...], mod_ref[0:1, :], mod_ref[1:2, :])
        ux = jnp.dot(h.astype(BF16), w_ref[...], preferred_element_type=F32)
        ux_ref[...] = ux
        rows = TM // CH
        for b in range(NGB):
            scr[b] = ux[:, POOL_W + LANES * b:POOL_W + LANES * (b + 1)]
            for t in range(CH):
                piece = scr[b, pl.ds(t, rows, stride=CH), :]
                uv_ref[:, pl.ds((b * CH + t) * LANES, LANES)] = piece.astype(BF16)

    @pl.when(i > n_x_tiles)
    def _():
        uv_ref[...] = jnp.zeros_like(uv_ref)


def _token_specs(n_x_tiles, ctx_block):
    return [pl.BlockSpec((TM, D_MODEL), lambda i: (jnp.minimum(i, n_x_tiles - 1), 0)),
            pl.BlockSpec((TM, D_MODEL), lambda i: (ctx_block, 0))]


def _inproj(xa, ca, ctx_block, mod_l, g, w_bf, n_x_tiles):
    nt = n_x_tiles + 1
    n_steps = n_x_tiles + SR // (TM // CH)
    sel = lambda i: (jnp.where(i >= n_x_tiles, 1, 0), 0, 0)
    last = lambda i: (jnp.minimum(i, n_x_tiles), 0)
    return pl.pallas_call(
        functools.partial(_inproj_kernel, n_x_tiles=n_x_tiles),
        grid=(n_steps,),
        in_specs=_token_specs(n_x_tiles, ctx_block) + [
            pl.BlockSpec((None, N_MOD, D_MODEL), sel),
            pl.BlockSpec((1, D_MODEL), lambda i: (0, 0)),
            pl.BlockSpec((D_MODEL, D_MODEL), lambda i: (0, 0)),
        ],
        out_specs=[
            pl.BlockSpec((TM, D_MODEL), last),
            pl.BlockSpec((TM // CH, NGB * CW), lambda i: (i, 0)),
        ],
        out_shape=[
            jax.ShapeDtypeStruct((nt * TM, D_MODEL), F32),
            jax.ShapeDtypeStruct((n_steps * TM // CH, NGB * CW), BF16),
        ],
        scratch_shapes=[pltpu.VMEM((NGB, TM, LANES), F32)],
        compiler_params=_cparams(("arbitrary",)),
        name="mixer_inproj",
    )(xa, ca, mod_l, g, w_bf)


def _chunk_scan(s_ref, tab_ref, carry_ref, out_ref, nblk, reverse):
    half = SW // 2
    ncol = half // LANES
    row = lax.broadcasted_iota(jnp.int32, (SUBLANES, LANES), 0)
    edge = (row == SUBLANES - 1) if reverse else (row == 0)
    last = 0 if reverse else SUBLANES - 1

    def sub_block(r0, carry):
        outs_r, outs_i, new_carry = [], [], []
        for j in range(ncol):
            cr, ci = carry[2 * j], carry[2 * j + 1]
            lre = pl.ds(LANES * j, LANES)
            lim = pl.ds(half + LANES * j, LANES)
            zr = s_ref[pl.ds(r0, SUBLANES), lre]
            zi = s_ref[pl.ds(r0, SUBLANES), lim]
            for q, k in enumerate((1, 2, 4)):
                ar = tab_ref[pl.ds(16 * q, SUBLANES), lre]
                ai = tab_ref[pl.ds(16 * q + 8, SUBLANES), lre]
                sh = SUBLANES - k if reverse else k
                sr = pltpu.roll(zr, sh, axis=0)
                si = pltpu.roll(zi, sh, axis=0)
                zr, zi = zr + ar * sr - ai * si, zi + ar * si + ai * sr
            pr = tab_ref[pl.ds(48, SUBLANES), lre]
            pi = tab_ref[pl.ds(56, SUBLANES), lre]
            zr, zi = zr + pr * cr - pi * ci, zi + pr * ci + pi * cr
            sh1 = SUBLANES - 1 if reverse else 1
            outs_r.append(jnp.where(edge, cr, pltpu.roll(zr, sh1, axis=0)))
            outs_i.append(jnp.where(edge, ci, pltpu.roll(zi, sh1, axis=0)))
            new_carry.append(jnp.broadcast_to(zr[last:last + 1, :], (SUBLANES, LANES)))
            new_carry.append(jnp.broadcast_to(zi[last:last + 1, :], (SUBLANES, LANES)))
        return outs_r, outs_i, tuple(new_carry)

    def body(it, carry):
        bi = (nblk - 1 - it) if reverse else it
        r0 = pl.multiple_of(bi * 2 * SUBLANES, 2 * SUBLANES)
        if reverse:
            hi_r, hi_i, carry = sub_block(r0 + SUBLANES, carry)
            lo_r, lo_i, carry = sub_block(r0, carry)
        else:
            lo_r, lo_i, carry = sub_block(r0, carry)
            hi_r, hi_i, carry = sub_block(r0 + SUBLANES, carry)
        for j in range(ncol):
            out_ref[pl.ds(r0, 2 * SUBLANES), pl.ds(LANES * j, LANES)] = (
                jnp.concatenate([lo_r[j], hi_r[j]], axis=0).astype(BF16))
            out_ref[pl.ds(r0, 2 * SUBLANES), pl.ds(half + LANES * j, LANES)] = (
                jnp.concatenate([lo_i[j], hi_i[j]], axis=0).astype(BF16))
        return carry

    init = tuple(carry_ref[:, pl.ds(LANES * c, LANES)] for c in range(2 * ncol))
    final = lax.fori_loop(0, nblk, body, init)
    for c in range(2 * ncol):
        carry_ref[:, pl.ds(LANES * c, LANES)] = final[c]


def _ssm_state_kernel(uf_ref, ub_ref, msf_ref, msb_ref, tf_ref, tb_ref, hf_ref, gb_ref,
                      sf, sb, cf, cb, *, ctx_rows):
    step = pl.program_id(1)

    @pl.when(step == 0)
    def _():
        cf[...] = jnp.zeros_like(cf)
        cb[...] = jnp.zeros_like(cb)
        hf_ref[...] = jnp.zeros_like(hf_ref)
        gb_ref[...] = jnp.zeros_like(gb_ref)

    sf[...] = jnp.dot(uf_ref[...], msf_ref[...], preferred_element_type=F32)
    sb[...] = jnp.dot(ub_ref[...], msb_ref[...], preferred_element_type=F32)
    nblk = jnp.where(step == 0, ctx_rows // (2 * SUBLANES), SR // (2 * SUBLANES))
    _chunk_scan(sf, tf_ref, cf, hf_ref, nblk, reverse=False)
    _chunk_scan(sb, tb_ref, cb, gb_ref, nblk, reverse=True)


def _ssm_states(uv_ext, ms, tab, layer, n_xt, ctx_rows):
    rows = uv_ext.shape[0]
    fwd = lambda b, s: (jnp.where(s == 0, n_xt, s - 1), b)
    bwd = lambda b, s: (jnp.where(s == 0, n_xt, n_xt - s), b)
    return pl.pallas_call(
        functools.partial(_ssm_state_kernel, ctx_rows=ctx_rows),
        grid=(NGB, n_xt + 1),
        in_specs=[
            pl.BlockSpec((SR, CW), fwd),
            pl.BlockSpec((SR, CW), bwd),
            pl.BlockSpec((None, None, None, CW, SW), lambda b, s: (layer, b, 0, 0, 0)),
            pl.BlockSpec((None, None, None, CW, SW), lambda b, s: (layer, b, 1, 0, 0)),
            pl.BlockSpec((None, None, None, 8 * SUBLANES, SW // 2),
                         lambda b, s: (layer, 0, b, 0, 0)),
            pl.BlockSpec((None, None, None, 8 * SUBLANES, SW // 2),
                         lambda b, s: (layer, 1, b, 0, 0)),
        ],
        out_specs=[pl.BlockSpec((SR, SW), fwd), pl.BlockSpec((SR, SW), bwd)],
        out_shape=[jax.ShapeDtypeStruct((rows, NGB * SW), BF16)] * 2,
        scratch_shapes=[pltpu.VMEM((SR, SW), F32), pltpu.VMEM((SR, SW), F32),
                        pltpu.VMEM((SUBLANES, SW), F32), pltpu.VMEM((SUBLANES, SW), F32)],
        compiler_params=_cparams(("arbitrary", "arbitrary")),
        name="ssm_states",
    )(uv_ext, uv_ext, ms, ms, tab, tab)


def _ssm_out_kernel(u_ref, hf_ref, gb_ref, m_ref, y_ref):
    res = jnp.dot(u_ref[...], m_ref[0:CW, :], preferred_element_type=F32)
    res += jnp.dot(hf_ref[...], m_ref[CW:CW + SW, :], preferred_element_type=F32)
    res += jnp.dot(gb_ref[...], m_ref[CW + SW:CW + 2 * SW, :], preferred_element_type=F32)
    for t in range(CH):
        y_ref[pl.ds(t, SR, stride=CH), :] = res[:, t * LANES:(t + 1) * LANES]


def _ssm_readout(uv_ext, hf, gb, mcat, layer):
    rows = uv_ext.shape[0]
    nt = rows // SR
    blk = lambda b, i: (i, b)
    return pl.pallas_call(
        _ssm_out_kernel,
        grid=(NGB, nt),
        in_specs=[
            pl.BlockSpec((SR, CW), blk),
            pl.BlockSpec((SR, SW), blk),
            pl.BlockSpec((SR, SW), blk),
            pl.BlockSpec((None, None, CW + 2 * SW, CW), lambda b, i: (layer, b, 0, 0)),
        ],
        out_specs=pl.BlockSpec((SR * CH, LANES), blk),
        out_shape=jax.ShapeDtypeStruct((rows * CH, SSM_W), F32),
        compiler_params=_cparams(("arbitrary", "arbitrary")),
        name="ssm_readout",
    )(uv_ext, hf, gb, mcat)


def _gelu_tanh(x):
    return 0.5 * x * (1.0 + jnp.tanh(0.7978845608028654 * (x + 0.044715 * x * x * x)))


def _route(s, b_router):
    sel = s + b_router
    lane_i = lax.broadcasted_iota(jnp.int32, s.shape, 1)
    grp = lane_i // EXPERTS_PER_GROUP
    lane = lane_i.astype(F32)
    neg = jnp.float32(-jnp.inf)
    big = jnp.float32(N_EXPERTS)

    def top2(vals):
        m1 = jnp.max(vals, axis=-1, keepdims=True)
        i1 = jnp.min(jnp.where(vals == m1, lane, big), axis=-1, keepdims=True)
        rest = jnp.where(lane == i1, neg, vals)
        m2 = jnp.max(rest, axis=-1, keepdims=True)
        i2 = jnp.min(jnp.where(rest == m2, lane, big), axis=-1, keepdims=True)
        return m1, i1, m2, i2

    best = None
    gidx = None
    for g in range(N_EXPERT_GROUPS):
        m1, _, m2, _ = top2(jnp.where(grp == g, sel, neg))
        score = m1 + m2
        if best is None:
            best, gidx = score, jnp.zeros_like(lane_i[:, 0:1])
        else:
            upd = score > best
            best = jnp.where(upd, score, best)
            gidx = jnp.where(upd, g, gidx)
    _, e1, _, e2 = top2(jnp.where(grp == gidx, sel, neg))
    w1 = jnp.sum(jnp.where(lane == e1, s, 0.0), axis=-1, keepdims=True)
    w2 = jnp.sum(jnp.where(lane == e2, s, 0.0), axis=-1, keepdims=True)
    tot = w1 + w2
    return e1.astype(jnp.int32), e2.astype(jnp.int32), w1 / tot, w2 / tot


def _mixout_kernel(x_ref, c_ref, ux_ref, ys_ref, mod_ref, band_ref, icnt_ref, wp_ref, sp_ref, d_ref,
                   wglu_ref, bglu_ref, wout_ref, gffn_ref, wr_ref, br_ref, tri_ref,
                   xo_ref, hs_ref, pos_ref, gate_ref, cnt_ref, *, n_x_tiles):
    ux = ux_ref[...]
    parts = []
    for g in range(len(POOL_WINDOWS)):
        ug = ux[:, g * POOL_GW:(g + 1) * POOL_GW]
        hi = ug.astype(BF16)
        lo = (ug - hi.astype(F32)).astype(BF16)
        band = band_ref[g]
        wsum = (jnp.dot(band, hi, preferred_element_type=F32)
                + jnp.dot(band, lo, preferred_element_type=F32))
        p = wsum * icnt_ref[:, g * POOL_GW:(g + 1) * POOL_GW] - ug
        parts.append(jnp.dot(p.astype(BF16), wp_ref[g], preferred_element_type=F32))
    pool = jnp.concatenate(parts, axis=-1) * sp_ref[...]

    y = ys_ref[...] + d_ref[...] * ux[:, POOL_W:]
    y = _gelu_tanh(y)
    z = jnp.dot(y.astype(BF16), wglu_ref[...], preferred_element_type=F32) + bglu_ref[...]
    glu = y * jax.nn.sigmoid(z)

    cat = jnp.concatenate([pool, glu], axis=-1).astype(BF16)
    o = jnp.dot(cat, wout_ref[...], preferred_element_type=F32)
    xn = _token_tile(x_ref, c_ref, n_x_tiles) + mod_ref[2:3, :] * o
    xo_ref[...] = xn

    h2 = _rmsnorm_mod(xn, gffn_ref[...], mod_ref[3:4, :], mod_ref[4:5, :])
    h_hi = h2.astype(BF16)
    h_lo = (h2 - h_hi.astype(F32)).astype(BF16)
    wr = wr_ref[...]
    p_hi = jnp.dot(h_hi, wr, preferred_element_type=F32)
    p_lo = jnp.dot(h_lo, wr[:, :N_EXPERTS], preferred_element_type=F32)
    logits = p_hi[:, :N_EXPERTS] + (p_hi[:, N_EXPERTS:] + p_lo)
    e1, e2, g1, g2 = _route(jax.nn.sigmoid(logits), br_ref[...])
    gate_ref[...] = jnp.concatenate([g1, g2], axis=-1)

    lane = lax.broadcasted_iota(jnp.int32, (TM, N_EXPERTS), 1)
    oh0 = jnp.where(lane == e1, 1.0, 0.0)
    oh1 = jnp.where(lane == e2, 1.0, 0.0)
    tri = tri_ref[...]
    before0 = jnp.dot(tri, oh0.astype(BF16), preferred_element_type=F32)
    before1 = jnp.dot(tri, oh1.astype(BF16), preferred_element_type=F32)
    tot0 = jnp.sum(oh0, axis=0, keepdims=True)
    tot1 = jnp.sum(oh1, axis=0, keepdims=True)
    smaller = jnp.where(lane > e1, 1.0, 0.0) + jnp.where(lane > e2, 1.0, 0.0)
    off = jnp.sum(smaller, axis=0, keepdims=True)
    p0 = jnp.sum(oh0 * (off + before0), axis=-1, keepdims=True)
    p1 = jnp.sum(oh1 * (off + tot0 + before1), axis=-1, keepdims=True)
    pos_ref[...] = jnp.concatenate([p0, p1], axis=-1).astype(jnp.int32)
    cnt_ref[...] = (tot0 + tot1).astype(jnp.int32)

    slot = lax.broadcasted_iota(jnp.int32, (TM, 2 * TM), 1).astype(F32)
    perm_t = jnp.where(jnp.logical_or(slot == p0, slot == p1), 1.0, 0.0).astype(BF16)
    hs = lax.dot_general(perm_t, h2.astype(BF16), (((0,), (0,)), ((), ())),
                         preferred_element_type=F32)
    for q in range(ROW_TILE):
        hs_ref[pl.ds(q, 2 * TM, stride=ROW_TILE), :] = hs[:, q * LANES:(q + 1) * LANES]


def _mixout(xa, ca, ctx_block, ux, y_ssm, mod_l, band, icnt, wp_bf, sp, dvec, wglu_bf, bglu, wout_bf,
            gffn, w_router, b_router, tri, nt, n_x_tiles):
    t = nt * TM
    sel = lambda i: (jnp.where(i == n_x_tiles, 1, 0), 0, 0)
    sel4 = lambda i: (jnp.where(i == n_x_tiles, 1, 0), 0, 0, 0)
    row = lambda i: (i, 0)
    fix2 = lambda i: (0, 0)
    return pl.pallas_call(
        functools.partial(_mixout_kernel, n_x_tiles=n_x_tiles),
        grid=(nt,),
        in_specs=_token_specs(n_x_tiles, ctx_block) + [
            pl.BlockSpec((TM, D_MODEL), row),
            pl.BlockSpec((TM, SSM_W), row),
            pl.BlockSpec((None, N_MOD, D_MODEL), sel),
            pl.BlockSpec((None, len(POOL_WINDOWS), TM, TM), sel4),
            pl.BlockSpec((None, TM, POOL_W), sel),
            pl.BlockSpec((len(POOL_WINDOWS), POOL_GW, POOL_GW), lambda i: (0, 0, 0)),
            pl.BlockSpec((1, POOL_W), fix2),
            pl.BlockSpec((1, SSM_W), fix2),
            pl.BlockSpec((SSM_W, SSM_W), fix2),
            pl.BlockSpec((1, SSM_W), fix2),
            pl.BlockSpec((D_MODEL, D_MODEL), fix2),
            pl.BlockSpec((1, D_MODEL), fix2),
            pl.BlockSpec((D_MODEL, 2 * N_EXPERTS), fix2),
            pl.BlockSpec((1, N_EXPERTS), fix2),
            pl.BlockSpec((TM, TM), fix2),
        ],
        out_specs=[
            pl.BlockSpec((TM, D_MODEL), row),
            pl.BlockSpec((2 * TM * ROW_TILE, LANES), row),
            pl.BlockSpec((TM, 2), row),
            pl.BlockSpec((TM, 2), row),
            pl.BlockSpec((None, 1, N_EXPERTS), lambda i: (i, 0, 0)),
        ],
        out_shape=[
            jax.ShapeDtypeStruct((t, D_MODEL), F32),
            jax.ShapeDtypeStruct((2 * t * ROW_TILE, LANES), F32),
            jax.ShapeDtypeStruct((t, 2), jnp.int32),
            jax.ShapeDtypeStruct((t, 2), F32),
            jax.ShapeDtypeStruct((nt, 1, N_EXPERTS), jnp.int32),
        ],
        compiler_params=_cparams(("arbitrary",)),
        name="mixer_out_router",
    )(xa, ca, ux, y_ssm, mod_l, band, icnt, wp_bf, sp, dvec, wglu_bf, bglu, wout_bf, gffn,
      w_router, b_router, tri)


def _copy_rows(src, src_row, dst, dst_row, n, n_max, sem):
    done = 0
    for bit in (1 << b for b in range(n_max.bit_length() - 1, -1, -1)):
        take = n & bit

        @pl.when(take != 0)
        def _(done=done, bit=bit):
            s0 = pl.multiple_of((src_row + done) * ROW_TILE, ROW_TILE)
            d0 = pl.multiple_of((dst_row + done) * ROW_TILE, ROW_TILE)
            pltpu.make_async_copy(src.at[pl.ds(s0, bit * ROW_TILE)],
                                  dst.at[pl.ds(d0, bit * ROW_TILE)], sem).start()

        done = done + take


def _wait_rows(src, dst, n, sem):
    size = pl.multiple_of(n * ROW_TILE, ROW_TILE)
    pltpu.make_async_copy(src.at[pl.ds(0, size)], dst.at[pl.ds(0, size)], sem).wait()


def _expert_kernel(be_ref, nbu_ref, nv_ref, rs_ref, re_ref, rsrc_ref, rlen_ref, rdst_ref,
                   hs_hbm, wg_ref, wu_ref, wd_ref, y_ref, xg0, xg1, wg_s, wu_s, wd_s, gsem):
    s = pl.program_id(0)
    nbu = nbu_ref[0]
    xg = (xg0, xg1)

    def gather_start(blk, slot):
        lo_blk = blk * MOE_BLK

        def body(r, c):
            g0 = rdst_ref[r]
            lo = jnp.maximum(g0, lo_blk)
            hi = jnp.minimum(g0 + rlen_ref[r], lo_blk + MOE_BLK)
            _copy_rows(hs_hbm, rsrc_ref[r] + (lo - g0), xg[slot], lo - lo_blk,
                       jnp.maximum(hi - lo, 0), TM, gsem.at[slot])
            return c

        lax.fori_loop(rs_ref[blk], re_ref[blk], body, 0)

    changed = jnp.logical_or(s == 0, be_ref[s] != be_ref[jnp.maximum(s - 1, 0)])

    @pl.when(jnp.logical_and(changed, s < nbu))
    def _():
        wg_s[...] = wg_ref[...].astype(BF16)
        wu_s[...] = wu_ref[...].astype(BF16)
        wd_s[...] = wd_ref[...].astype(BF16)

    def step(slot):
        @pl.when(s == 0)
        def _():
            xg0[...] = jnp.zeros_like(xg0)
            xg1[...] = jnp.zeros_like(xg1)
            gather_start(0, slot)

        @pl.when(s + 1 < nbu)
        def _():
            gather_start(s + 1, 1 - slot)

        _wait_rows(hs_hbm, xg[slot], nv_ref[s], gsem.at[slot])
        xb = jnp.concatenate([xg[slot][pl.ds(q, MOE_BLK, stride=ROW_TILE), :]
                              for q in range(ROW_TILE)], axis=-1).astype(BF16)
        g = jnp.dot(xb, wg_s[...], preferred_element_type=F32)
        u = jnp.dot(xb, wu_s[...], preferred_element_type=F32)
        hid = (g * jax.nn.sigmoid(g)) * u
        out = jnp.dot(hid.astype(BF16), wd_s[...], preferred_element_type=F32)
        for q in range(ROW_TILE):
            y_ref[pl.ds(q, MOE_BLK, stride=ROW_TILE), :] = out[:, q * LANES:(q + 1) * LANES]

    for slot in range(2):
        @pl.when(jnp.logical_and(s < nbu, s % 2 == slot))
        def _(slot=slot):
            step(slot)

    @pl.when(s >= nbu)
    def _():
        y_ref[...] = jnp.zeros_like(y_ref)


def _experts(blk_e, nb_used, blk_valid, rs, re, rsrc, rlen, rdst, hs, w_gate, w_up, w_down, layer):
    nb = blk_e.shape[0]
    wmap = lambda i, be, *_: (layer, be[i], 0, 0)
    buf = pltpu.VMEM((MOE_BLK * ROW_TILE, LANES), F32)
    return pl.pallas_call(
        _expert_kernel,
        grid_spec=pltpu.PrefetchScalarGridSpec(
            num_scalar_prefetch=8,
            grid=(nb,),
            in_specs=[
                pl.BlockSpec(memory_space=pl.ANY),
                pl.BlockSpec((None, None, D_MODEL, D_EXPERT), wmap),
                pl.BlockSpec((None, None, D_MODEL, D_EXPERT), wmap),
                pl.BlockSpec((None, None, D_EXPERT, D_MODEL), wmap),
            ],
            out_specs=pl.BlockSpec((MOE_BLK * ROW_TILE, LANES), lambda i, *_: (i, 0)),
            scratch_shapes=[buf, buf,
                            pltpu.VMEM((D_MODEL, D_EXPERT), BF16),
                            pltpu.VMEM((D_MODEL, D_EXPERT), BF16),
                            pltpu.VMEM((D_EXPERT, D_MODEL), BF16),
                            pltpu.SemaphoreType.DMA((2,))],
        ),
        out_shape=jax.ShapeDtypeStruct((nb * MOE_BLK * ROW_TILE, LANES), F32),
        compiler_params=_cparams(("arbitrary",)),
        name="moe_experts",
    )(blk_e, nb_used, blk_valid, rs, re, rsrc, rlen, rdst, hs, w_gate, w_up, w_down)


def _combine_kernel(meta_ref, x_ref, pos_ref, gate_ref, mod_ref, gfin_ref, ys_hbm, out_ref,
                    st0, st1, sem, *, final, nt):
    i = pl.program_id(0)
    stage = (st0, st1)

    def fetch(tile, slot):
        base = tile * LANES
        for e in range(N_EXPERTS):
            _copy_rows(ys_hbm, meta_ref[base + e], stage[slot], meta_ref[base + 2 * N_EXPERTS + e],
                       meta_ref[base + N_EXPERTS + e], TM, sem.at[slot])

    def step(slot):
        @pl.when(i == 0)
        def _():
            fetch(0, slot)

        @pl.when(i + 1 < nt)
        def _():
            fetch(i + 1, 1 - slot)

        _wait_rows(ys_hbm, stage[slot], 2 * TM, sem.at[slot])
        rows = jnp.concatenate([stage[slot][pl.ds(q, 2 * TM, stride=ROW_TILE), :]
                                for q in range(ROW_TILE)], axis=-1)
        pos = pos_ref[...].astype(F32)
        gate = gate_ref[...]
        slot_id = lax.broadcasted_iota(jnp.int32, (TM, 2 * TM), 1).astype(F32)
        gmat = (jnp.where(slot_id == pos[:, 0:1], gate[:, 0:1], 0.0)
                + jnp.where(slot_id == pos[:, 1:2], gate[:, 1:2], 0.0))
        g_hi = gmat.astype(BF16)
        g_lo = (gmat - g_hi.astype(F32)).astype(BF16)
        r_hi = rows.astype(BF16)
        r_lo = (rows - r_hi.astype(F32)).astype(BF16)
        y = (jnp.dot(g_hi, r_hi, preferred_element_type=F32)
             + (jnp.dot(g_lo, r_hi, preferred_element_type=F32)
                + jnp.dot(g_hi, r_lo, preferred_element_type=F32)))
        xn = x_ref[...] + mod_ref[5:6, :] * y
        if final:
            ms = jnp.mean(xn * xn, axis=-1, keepdims=True)
            xn = xn * lax.rsqrt(ms + EPS) * gfin_ref[...]
        out_ref[...] = xn

    for slot in range(2):
        @pl.when(i % 2 == slot)
        def _(slot=slot):
            step(slot)


def _combine(meta, xn, pos, gate, mod_l, g_final, ys, nt, n_x_tiles, final):
    sel = lambda i, m: (jnp.where(i == n_x_tiles, 1, 0), 0, 0)
    row = lambda i, m: (i, 0)
    stage = pltpu.VMEM((2 * TM * ROW_TILE, LANES), F32)
    return pl.pallas_call(
        functools.partial(_combine_kernel, final=final, nt=nt),
        grid_spec=pltpu.PrefetchScalarGridSpec(
            num_scalar_prefetch=1,
            grid=(nt,),
            in_specs=[
                pl.BlockSpec((TM, D_MODEL), row),
                pl.BlockSpec((TM, 2), row),
                pl.BlockSpec((TM, 2), row),
                pl.BlockSpec((None, N_MOD, D_MODEL), sel),
                pl.BlockSpec((1, D_MODEL), lambda i, m: (0, 0)),
                pl.BlockSpec(memory_space=pl.ANY),
            ],
            out_specs=pl.BlockSpec((TM, D_MODEL), row),
            scratch_shapes=[stage, stage, pltpu.SemaphoreType.DMA((2,))],
        ),
        out_shape=jax.ShapeDtypeStruct((nt * TM, D_MODEL), F32),
        compiler_params=_cparams(("arbitrary",)),
        name="moe_combine",
    )(meta, xn, pos, gate, mod_l, g_final, ys)


def _moe(hs, pos, tcnt, gate, xn, mod_l, g_final, w_gate, w_up, w_down, layer, nt, n_x_tiles,
         final):
    t = nt * TM
    i32 = jnp.int32
    n = tcnt.reshape(nt, N_EXPERTS)
    off = jnp.cumsum(n, axis=1) - n
    base = jnp.cumsum(n, axis=0) - n
    counts = jnp.sum(n, axis=0)
    padded = (counts + MOE_BLK - 1) // MOE_BLK * MOE_BLK
    pend = jnp.cumsum(padded)
    pstart = pend - padded
    nb = (2 * t + N_EXPERTS * (MOE_BLK - 1)) // MOE_BLK + 1
    nb_used = (pend[-1] // MOE_BLK).astype(i32)
    blk_lo = jnp.arange(nb, dtype=i32) * MOE_BLK
    blk_e = jnp.minimum(jnp.sum(pend[None, :] <= jnp.minimum(blk_lo, pend[-1] - MOE_BLK)[:, None],
                                axis=1), N_EXPERTS - 1).astype(i32)
    blk_valid = jnp.clip((pstart + counts)[blk_e] - blk_lo, 0, MOE_BLK).astype(i32)
    gsrc = (pstart[None, :] + base).astype(i32)
    rdst = gsrc.T.reshape(-1)
    rlen = n.T.reshape(-1).astype(i32)
    rsrc = (jnp.arange(nt, dtype=i32)[:, None] * (2 * TM) + off).T.reshape(-1).astype(i32)
    e_dst = gsrc.T[blk_e]
    e_end = e_dst + n.T[blk_e]
    rs = (blk_e * nt + jnp.sum(e_end <= blk_lo[:, None], axis=1)).astype(i32)
    re = (blk_e * nt + jnp.sum(e_dst < (blk_lo + MOE_BLK)[:, None], axis=1)).astype(i32)
    ys = _experts(blk_e, nb_used.reshape(1), blk_valid, rs, re, rsrc, rlen, rdst, hs,
                  w_gate, w_up, w_down, layer)
    meta = jnp.concatenate([gsrc, n.astype(i32), off.astype(i32),
                            jnp.zeros((nt, LANES - 3 * N_EXPERTS), i32)], axis=1).reshape(-1)
    return _combine(meta, xn, pos, gate, mod_l, g_final, ys, nt, n_x_tiles, final)


def _cmul(ar, ai, br, bi):
    return ar * br - ai * bi, ar * bi + ai * br


def _expand_rows(v, rep_ref, mask_ref):
    return jnp.dot(v.astype(BF16), rep_ref[...], preferred_element_type=F32) * mask_ref[...]


def _expand_cols(e, rept_ref, maskt_ref):
    out = lax.dot_general(rept_ref[...], e.astype(BF16), (((1,), (1,)), ((), ())),
                          preferred_element_type=F32)
    return out * maskt_ref[...]


def _ssm_prep_kernel(are_ref, aim_ref, ldt_ref, btr_ref, bti_ref, cr_ref, ci_ref,
                     rep_ref, mask_ref, rept_ref, maskt_ref, ms_ref, mcat_ref, ecat):
    lag_blocks = []
    for d in range(2):
        a_re = jnp.minimum(are_ref[d], -1e-4)
        a_im = aim_ref[d]
        dt = jnp.exp(ldt_ref[d])
        mag = jnp.exp(a_re * dt)
        lr = mag * jnp.cos(a_im * dt)
        li = mag * jnp.sin(a_im * dt)
        den = a_re * a_re + a_im * a_im
        k_re = ((lr - 1.0) * a_re + li * a_im) / den
        k_im = (li * a_re - (lr - 1.0) * a_im) / den
        bbr, bbi = _cmul(k_re, k_im, btr_ref[d], bti_ref[d])
        cr, ci = cr_ref[d], ci_ref[d]
        pr, pi = jnp.ones_like(lr), jnp.zeros_like(lr)
        v0 = None
        for n in range(CH + 1):
            er, ei = _cmul(pr, pi, cr, ci)
            et = jnp.concatenate([_expand_cols(er, rept_ref, maskt_ref),
                                  _expand_cols(-ei, rept_ref, maskt_ref)], axis=0).astype(BF16)
            if n < CH:
                vr, vi = _cmul(pr, pi, bbr, bbi)
                v = jnp.concatenate([_expand_rows(vr, rep_ref, mask_ref),
                                     _expand_rows(vi, rep_ref, mask_ref)], axis=1).astype(BF16)
                if n == 0:
                    v0 = v
                sigma = CH - 1 - n if d == 0 else n
                ms_ref[d, sigma * LANES:(sigma + 1) * LANES, :] = v
                ecat[:, n * LANES:(n + 1) * LANES] = et
            if n >= 1:
                tau = n - 1 if d == 0 else CH - n
                mcat_ref[CW + d * SW:CW + (d + 1) * SW, tau * LANES:(tau + 1) * LANES] = et
            pr, pi = _cmul(pr, pi, lr, li)
        lag_blocks.append(jnp.dot(v0, ecat[...], preferred_element_type=F32))
    kf, kb = lag_blocks
    for s in range(CH):
        for t in range(CH):
            if t > s:
                blk = kf[:, (t - s) * LANES:(t - s + 1) * LANES]
            elif t < s:
                blk = kb[:, (s - t) * LANES:(s - t + 1) * LANES]
            else:
                blk = kf[:, :LANES] + kb[:, :LANES]
            mcat_ref[s * LANES:(s + 1) * LANES, t * LANES:(t + 1) * LANES] = blk.astype(BF16)


def _ssm_operators(a_re, a_im, log_dt, b_re, b_im, c_re, c_im):
    depth = a_re.shape[0]
    rows = SSM_GROUPS * SSM_GROUP
    p = SSM_STATE
    rep_rows = lambda v: jnp.repeat(v, SSM_GROUP, axis=2)
    are = rep_rows(a_re)
    aim = rep_rows(a_im)
    ldt = jnp.broadcast_to(rep_rows(log_dt[..., None]), are.shape)
    btr = jnp.swapaxes(b_re, -1, -2).reshape(depth, 2, rows, p)
    bti = jnp.swapaxes(b_im, -1, -2).reshape(depth, 2, rows, p)
    cr = c_re.reshape(depth, 2, rows, p)
    ci = c_im.reshape(depth, 2, rows, p)
    rep = np.tile(np.eye(p, dtype=np.float32), (1, GB))
    mask = np.kron(np.eye(GB, dtype=np.float32), np.ones((SSM_GROUP, p), np.float32))
    par = pl.BlockSpec((None, 2, LANES, p), lambda l, b: (l, 0, b, 0))
    fix = lambda shape: pl.BlockSpec(shape, lambda l, b: (0, 0))
    return pl.pallas_call(
        _ssm_prep_kernel,
        grid=(depth, NGB),
        in_specs=[par] * 7 + [fix((p, GB * p)), fix((LANES, GB * p)),
                              fix((GB * p, p)), fix((GB * p, LANES))],
        out_specs=[
            pl.BlockSpec((None, None, 2, CW, SW), lambda l, b: (l, b, 0, 0, 0)),
            pl.BlockSpec((None, None, CW + 2 * SW, CW), lambda l, b: (l, b, 0, 0)),
        ],
        out_shape=[
            jax.ShapeDtypeStruct((depth, NGB, 2, CW, SW), BF16),
            jax.ShapeDtypeStruct((depth, NGB, CW + 2 * SW, CW), BF16),
        ],
        scratch_shapes=[pltpu.VMEM((SW, CW), BF16)],
        compiler_params=_cparams(("arbitrary", "arbitrary")),
        name="ssm_operators",
    )(are, aim, ldt, btr, bti, cr, ci, jnp.asarray(rep, BF16), jnp.asarray(mask),
      jnp.asarray(rep.T, BF16), jnp.asarray(mask.T))


def _scan_tables(a_re, a_im, log_dt):
    depth = a_re.shape[0]
    a_re = jnp.minimum(a_re, -1e-4)
    dt = jnp.exp(log_dt)[..., None]
    mag = jnp.exp(a_re * dt)
    lr = mag * jnp.cos(a_im * dt)
    li = mag * jnp.sin(a_im * dt)
    ar, ai = lr, li
    for _ in range(CH - 1):
        ar, ai = _cmul(ar, ai, lr, li)
    apw = [(ar, ai)]
    for _ in range(SUBLANES - 1):
        apw.append(_cmul(apw[-1][0], apw[-1][1], ar, ai))
    apr = jnp.stack([q[0] for q in apw], axis=1)
    api = jnp.stack([q[1] for q in apw], axis=1)
    rows = np.arange(SUBLANES)
    tabs = []
    for d in range(2):
        consts = []
        for k in (1, 2, 4):
            keep = (rows >= k) if d == 0 else (rows < SUBLANES - k)
            keep = jnp.asarray(keep.astype(np.float32))[None, :, None, None]
            consts.append(keep * apr[:, k - 1:k, d])
            consts.append(keep * api[:, k - 1:k, d])
        order = rows if d == 0 else SUBLANES - 1 - rows
        consts.append(apr[:, order, d])
        consts.append(api[:, order, d])
        tabs.append(jnp.stack(consts, axis=1))
    tab = jnp.stack(tabs, axis=1).reshape(depth, 2, 8 * SUBLANES, NGB, GB * SSM_STATE)
    return tab.transpose(0, 1, 3, 2, 4)


def _pool_constants(seg):
    t = np.arange(TM)
    s0 = t // seg * seg
    band = np.zeros((len(POOL_WINDOWS), TM, TM), np.float32)
    icnt = np.zeros((TM, POOL_W), np.float32)
    for g, w in enumerate(POOL_WINDOWS):
        lo = np.maximum(t - w // 2, s0)
        hi = np.minimum(t + w // 2, s0 + seg)
        band[g] = (t[None, :] >= lo[:, None]) & (t[None, :] < hi[:, None])
        icnt[:, g * POOL_GW:(g + 1) * POOL_GW] = (1.0 / (hi - lo).astype(np.float32))[:, None]
    return band, icnt


def kernel(x, c, ctx, c_ctx, w_mod, b_mod, g_mix, g_ffn, w_in, w_out, w_pool, s_pool,
           ssm_a_re, ssm_a_im, ssm_log_dt, ssm_b_re, ssm_b_im, ssm_c_re, ssm_c_im, ssm_d,
           w_glu, b_glu, w_router, b_router, w_gate, w_up, w_down, g_final):
    bsz, seq, d = x.shape
    ctx_len = ctx.shape[1]
    depth = w_mod.shape[0]
    assert bsz == 1 and d == D_MODEL and ctx_len == TM
    assert seq % (SR * CH) == 0 and seq % GRID_W == 0
    n_xt = seq // TM
    n_st = seq // (SR * CH)
    ctx_rows = ctx_len // CH

    cvec = jnp.concatenate([c.reshape(1, d), c_ctx.reshape(1, d),
                            jnp.zeros((SUBLANES - 2, d), F32)], axis=0)
    mod = _modulation(cvec, w_mod, b_mod).reshape(depth, SUBLANES, N_MOD, d)

    band_x, icnt_x = _pool_constants(GRID_W)
    band_c, icnt_c = _pool_constants(ctx_len)
    band = jnp.asarray(np.stack([band_x, band_c]), BF16)
    icnt = jnp.asarray(np.stack([icnt_x, icnt_c]), F32)
    tri = jnp.asarray(np.tril(np.ones((TM, TM), np.float32), -1), BF16)
    wr_hi = w_router.astype(BF16)
    wr_split = jnp.concatenate([wr_hi, (w_router - wr_hi.astype(F32)).astype(BF16)], axis=1)

    ms, mcat = _ssm_operators(ssm_a_re, ssm_a_im, ssm_log_dt, ssm_b_re, ssm_b_im,
                              ssm_c_re, ssm_c_im)
    tab = _scan_tables(ssm_a_re, ssm_a_im, ssm_log_dt)

    xa, ca, ctx_block = x[0], ctx[0], 0
    for l in range(depth):
        last = l == depth - 1
        mod_l = mod[l, :2]
        ux, uv_ext = _inproj(xa, ca, ctx_block, mod_l, g_mix[l].reshape(1, d),
                             w_in[l].astype(BF16), n_xt)
        hf, gb = _ssm_states(uv_ext, ms, tab, l, n_st, ctx_rows)
        y_ssm = _ssm_readout(uv_ext, hf, gb, mcat, l)
        nt = n_xt if last else n_xt + 1
        xn, hs, pos, gate, tcnt = _mixout(
            xa, ca, ctx_block, ux, y_ssm, mod_l, band, icnt, w_pool[l].astype(BF16),
            s_pool[l].reshape(1, -1), ssm_d[l].reshape(1, -1), w_glu[l].astype(BF16),
            b_glu[l].reshape(1, -1), w_out[l].astype(BF16), g_ffn[l].reshape(1, d), wr_split,
            b_router.reshape(1, -1), tri, nt, n_xt)
        tok = _moe(hs, pos, tcnt, gate, xn, mod_l, g_final.reshape(1, d), w_gate, w_up, w_down,
                   l, nt, n_xt, last)
        xa, ca, ctx_block = tok, tok, n_xt
    return tok.reshape(bsz, seq, d)
```

```python
import functools

import numpy as np
import jax
import jax.numpy as jnp
from jax import lax
from jax.experimental import pallas as pl
from jax.experimental.pallas import tpu as pltpu

F32 = jnp.float32
BF16 = jnp.bfloat16

D_MODEL = 1024
POOL_W = 512
SSM_W = 512
POOL_WINDOWS = (2, 4, 8, 16)
POOL_GW = 128
SSM_GROUP = 16
SSM_GROUPS = 32
SSM_STATE = 64
N_EXPERTS = 32
N_EXPERT_GROUPS = 4
EXPERTS_PER_GROUP = 8
D_EXPERT = 512
GRID_W = 64
EPS = 1e-6
N_MOD = 6

LANES = 128
SUBLANES = 8
TM = 256
TI = 512
CH = 8
GB = 8
NGB = SSM_GROUPS // GB
CW = CH * LANES
SW = 2 * GB * SSM_STATE
SR = 256
MOE_BLK = 512
ROW_TILE = D_MODEL // LANES
SHORT_RUN = 31
VMEM_LIMIT = 48 * 1024 * 1024


def _cparams(sem):
    return pltpu.CompilerParams(dimension_semantics=sem, vmem_limit_bytes=VMEM_LIMIT)


def _rmsnorm_mod(x, g, shift, scale):
    ms = jnp.mean(x * x, axis=-1, keepdims=True)
    y = x * lax.rsqrt(ms + EPS) * g
    return y * (1.0 + scale) + shift


def _mod_kernel(c_ref, w_ref, b_ref, o_ref):
    c = c_ref[...]
    a = c * jax.nn.sigmoid(c)
    o_ref[...] = jnp.dot(a, w_ref[...], preferred_element_type=F32,
                         precision=lax.Precision.HIGHEST) + b_ref[...]


def _modulation(cvec, w_mod, b_mod):
    depth, d, n = w_mod.shape
    tn = 1536
    return pl.pallas_call(
        _mod_kernel,
        grid=(depth, n // tn),
        in_specs=[
            pl.BlockSpec((SUBLANES, d), lambda l, j: (0, 0)),
            pl.BlockSpec((None, d, tn), lambda l, j: (l, 0, j)),
            pl.BlockSpec((None, 1, tn), lambda l, j: (l, 0, j)),
        ],
        out_specs=pl.BlockSpec((None, SUBLANES, tn), lambda l, j: (l, 0, j)),
        out_shape=jax.ShapeDtypeStruct((depth, SUBLANES, n), F32),
        compiler_params=_cparams(("arbitrary", "arbitrary")),
        name="modulation",
    )(cvec, w_mod, b_mod.reshape(depth, 1, n))


def _token_tile(x_ref, c_ref, n_x_tiles):
    return jnp.where(pl.program_id(0) == n_x_tiles, c_ref[...], x_ref[...])


def _inproj_kernel(x_ref, c_ref, mod_ref, g_ref, w_ref, ux_ref, uv_ref, scr, *, n_big):
    i = pl.program_id(0)

    def project(x, rows):
        h = _rmsnorm_mod(x, g_ref[...], mod_ref[0:1, :], mod_ref[1:2, :])
        ux = jnp.dot(h.astype(BF16), w_ref[...], preferred_element_type=F32)
        ux_ref[0:rows, :] = ux
        crow = rows // CH
        for b in range(NGB):
            scr[b, 0:rows, :] = ux[:, POOL_W + LANES * b:POOL_W + LANES * (b + 1)]
            for t in range(CH):
                piece = scr[b, pl.ds(t, crow, stride=CH), :]
                uv_ref[0:crow, pl.ds((b * CH + t) * LANES, LANES)] = piece.astype(BF16)

    @pl.when(i < n_big)
    def _():
        project(x_ref[...], TI)

    @pl.when(i == n_big)
    def _():
        project(c_ref[...], TM)
        ux_ref[TM:TI, :] = jnp.zeros((TI - TM, D_MODEL), F32)
        uv_ref[TM // CH:TI // CH, :] = jnp.zeros(((TI - TM) // CH, NGB * CW), BF16)

    @pl.when(i > n_big)
    def _():
        uv_ref[...] = jnp.zeros_like(uv_ref)


def _token_specs(n_x_tiles, ctx_block):
    return [pl.BlockSpec((TM, D_MODEL), lambda i: (jnp.minimum(i, n_x_tiles - 1), 0)),
            pl.BlockSpec((TM, D_MODEL), lambda i: (ctx_block, 0))]


def _inproj(xa, ca, ctx_block, mod_l, g, w_bf, n_x_tiles):
    n_big = n_x_tiles * TM // TI
    n_steps = n_big + SR // (TI // CH)
    sel = lambda i: (jnp.where(i >= n_big, 1, 0), 0, 0)
    return pl.pallas_call(
        functools.partial(_inproj_kernel, n_big=n_big),
        grid=(n_steps,),
        in_specs=[
            pl.BlockSpec((TI, D_MODEL), lambda i: (jnp.minimum(i, n_big - 1), 0)),
            pl.BlockSpec((TM, D_MODEL), lambda i: (ctx_block, 0)),
            pl.BlockSpec((None, N_MOD, D_MODEL), sel),
            pl.BlockSpec((1, D_MODEL), lambda i: (0, 0)),
            pl.BlockSpec((D_MODEL, D_MODEL), lambda i: (0, 0)),
        ],
        out_specs=[
            pl.BlockSpec((TI, D_MODEL), lambda i: (jnp.minimum(i, n_big), 0)),
            pl.BlockSpec((TI // CH, NGB * CW), lambda i: (i, 0)),
        ],
        out_shape=[
            jax.ShapeDtypeStruct(((n_x_tiles + 1) * TM, D_MODEL), F32),
            jax.ShapeDtypeStruct((n_steps * TI // CH, NGB * CW), BF16),
        ],
        scratch_shapes=[pltpu.VMEM((NGB, TI, LANES), F32)],
        compiler_params=_cparams(("arbitrary",)),
        name="mixer_inproj",
    )(xa, ca, mod_l, g, w_bf)


def _chunk_scan(s_ref, tab_ref, carry_ref, out_ref, nblk, reverse):
    half = SW // 2
    ncol = half // LANES
    row = lax.broadcasted_iota(jnp.int32, (SUBLANES, LANES), 0)
    edge = (row == SUBLANES - 1) if reverse else (row == 0)
    last = 0 if reverse else SUBLANES - 1

    def sub_block(r0, carry):
        outs_r, outs_i, new_carry = [], [], []
        for j in range(ncol):
            cr, ci = carry[2 * j], carry[2 * j + 1]
            lre = pl.ds(LANES * j, LANES)
            lim = pl.ds(half + LANES * j, LANES)
            zr = s_ref[pl.ds(r0, SUBLANES), lre]
            zi = s_ref[pl.ds(r0, SUBLANES), lim]
            for q, k in enumerate((1, 2, 4)):
                ar = tab_ref[pl.ds(16 * q, SUBLANES), lre]
                ai = tab_ref[pl.ds(16 * q + 8, SUBLANES), lre]
                sh = SUBLANES - k if reverse else k
                sr = pltpu.roll(zr, sh, axis=0)
                si = pltpu.roll(zi, sh, axis=0)
                zr, zi = zr + ar * sr - ai * si, zi + ar * si + ai * sr
            pr = tab_ref[pl.ds(48, SUBLANES), lre]
            pi = tab_ref[pl.ds(56, SUBLANES), lre]
            zr, zi = zr + pr * cr - pi * ci, zi + pr * ci + pi * cr
            sh1 = SUBLANES - 1 if reverse else 1
            outs_r.append(jnp.where(edge, cr, pltpu.roll(zr, sh1, axis=0)))
            outs_i.append(jnp.where(edge, ci, pltpu.roll(zi, sh1, axis=0)))
            new_carry.append(jnp.broadcast_to(zr[last:last + 1, :], (SUBLANES, LANES)))
            new_carry.append(jnp.broadcast_to(zi[last:last + 1, :], (SUBLANES, LANES)))
        return outs_r, outs_i, tuple(new_carry)

    def body(it, carry):
        bi = (nblk - 1 - it) if reverse else it
        r0 = pl.multiple_of(bi * 2 * SUBLANES, 2 * SUBLANES)
        if reverse:
            hi_r, hi_i, carry = sub_block(r0 + SUBLANES, carry)
            lo_r, lo_i, carry = sub_block(r0, carry)
        else:
            lo_r, lo_i, carry = sub_block(r0, carry)
            hi_r, hi_i, carry = sub_block(r0 + SUBLANES, carry)
        for j in range(ncol):
            out_ref[pl.ds(r0, 2 * SUBLANES), pl.ds(LANES * j, LANES)] = (
                jnp.concatenate([lo_r[j], hi_r[j]], axis=0).astype(BF16))
            out_ref[pl.ds(r0, 2 * SUBLANES), pl.ds(half + LANES * j, LANES)] = (
                jnp.concatenate([lo_i[j], hi_i[j]], axis=0).astype(BF16))
        return carry

    init = tuple(carry_ref[:, pl.ds(LANES * c, LANES)] for c in range(2 * ncol))
    final = lax.fori_loop(0, nblk, body, init)
    for c in range(2 * ncol):
        carry_ref[:, pl.ds(LANES * c, LANES)] = final[c]


def _ssm_state_kernel(uf_ref, ub_ref, msf_ref, msb_ref, tf_ref, tb_ref, hf_ref, gb_ref,
                      sf, sb, cf, cb, *, ctx_rows):
    step = pl.program_id(1)

    @pl.when(step == 0)
    def _():
        cf[...] = jnp.zeros_like(cf)
        cb[...] = jnp.zeros_like(cb)
        hf_ref[...] = jnp.zeros_like(hf_ref)
        gb_ref[...] = jnp.zeros_like(gb_ref)

    sf[...] = jnp.dot(uf_ref[...], msf_ref[...], preferred_element_type=F32)
    sb[...] = jnp.dot(ub_ref[...], msb_ref[...], preferred_element_type=F32)
    nblk = jnp.where(step == 0, ctx_rows // (2 * SUBLANES), SR // (2 * SUBLANES))
    _chunk_scan(sf, tf_ref, cf, hf_ref, nblk, reverse=False)
    _chunk_scan(sb, tb_ref, cb, gb_ref, nblk, reverse=True)


def _ssm_states(uv_ext, ms, tab, layer, n_xt, ctx_rows):
    rows = uv_ext.shape[0]
    fwd = lambda b, s: (jnp.where(s == 0, n_xt, s - 1), b)
    bwd = lambda b, s: (jnp.where(s == 0, n_xt, n_xt - s), b)
    return pl.pallas_call(
        functools.partial(_ssm_state_kernel, ctx_rows=ctx_rows),
        grid=(NGB, n_xt + 1),
        in_specs=[
            pl.BlockSpec((SR, CW), fwd),
            pl.BlockSpec((SR, CW), bwd),
            pl.BlockSpec((None, None, None, CW, SW), lambda b, s: (layer, b, 0, 0, 0)),
            pl.BlockSpec((None, None, None, CW, SW), lambda b, s: (layer, b, 1, 0, 0)),
            pl.BlockSpec((None, None, None, 8 * SUBLANES, SW // 2),
                         lambda b, s: (layer, 0, b, 0, 0)),
            pl.BlockSpec((None, None, None, 8 * SUBLANES, SW // 2),
                         lambda b, s: (layer, 1, b, 0, 0)),
        ],
        out_specs=[pl.BlockSpec((SR, SW), fwd), pl.BlockSpec((SR, SW), bwd)],
        out_shape=[jax.ShapeDtypeStruct((rows, NGB * SW), BF16)] * 2,
        scratch_shapes=[pltpu.VMEM((SR, SW), F32), pltpu.VMEM((SR, SW), F32),
                        pltpu.VMEM((SUBLANES, SW), F32), pltpu.VMEM((SUBLANES, SW), F32)],
        compiler_params=_cparams(("arbitrary", "arbitrary")),
        name="ssm_states",
    )(uv_ext, uv_ext, ms, ms, tab, tab)


def _ssm_out_kernel(u_ref, hf_ref, gb_ref, m_ref, y_ref):
    res = jnp.dot(u_ref[...], m_ref[0:CW, :], preferred_element_type=F32)
    res += jnp.dot(hf_ref[...], m_ref[CW:CW + SW, :], preferred_element_type=F32)
    res += jnp.dot(gb_ref[...], m_ref[CW + SW:CW + 2 * SW, :], preferred_element_type=F32)
    for t in range(CH):
        y_ref[pl.ds(t, SR, stride=CH), :] = res[:, t * LANES:(t + 1) * LANES]


def _ssm_readout(uv_ext, hf, gb, mcat, layer):
    rows = uv_ext.shape[0]
    nt = rows // SR
    blk = lambda b, i: (i, b)
    return pl.pallas_call(
        _ssm_out_kernel,
        grid=(NGB, nt),
        in_specs=[
            pl.BlockSpec((SR, CW), blk),
            pl.BlockSpec((SR, SW), blk),
            pl.BlockSpec((SR, SW), blk),
            pl.BlockSpec((None, None, CW + 2 * SW, CW), lambda b, i: (layer, b, 0, 0)),
        ],
        out_specs=pl.BlockSpec((SR * CH, LANES), blk),
        out_shape=jax.ShapeDtypeStruct((rows * CH, SSM_W), F32),
        compiler_params=_cparams(("arbitrary", "arbitrary")),
        name="ssm_readout",
    )(uv_ext, hf, gb, mcat)


def _gelu_tanh(x):
    return 0.5 * x * (1.0 + jnp.tanh(0.7978845608028654 * (x + 0.044715 * x * x * x)))


def _route(s, b_router):
    sel = s + b_router
    lane_i = lax.broadcasted_iota(jnp.int32, s.shape, 1)
    grp = lane_i // EXPERTS_PER_GROUP
    lane = lane_i.astype(F32)
    neg = jnp.float32(-jnp.inf)
    big = jnp.float32(N_EXPERTS)

    def top2(vals):
        m1 = jnp.max(vals, axis=-1, keepdims=True)
        i1 = jnp.min(jnp.where(vals == m1, lane, big), axis=-1, keepdims=True)
        rest = jnp.where(lane == i1, neg, vals)
        m2 = jnp.max(rest, axis=-1, keepdims=True)
        i2 = jnp.min(jnp.where(rest == m2, lane, big), axis=-1, keepdims=True)
        return m1, i1, m2, i2

    best = None
    gidx = None
    for g in range(N_EXPERT_GROUPS):
        m1, _, m2, _ = top2(jnp.where(grp == g, sel, neg))
        score = m1 + m2
        if best is None:
            best, gidx = score, jnp.zeros_like(lane_i[:, 0:1])
        else:
            upd = score > best
            best = jnp.where(upd, score, best)
            gidx = jnp.where(upd, g, gidx)
    _, e1, _, e2 = top2(jnp.where(grp == gidx, sel, neg))
    w1 = jnp.sum(jnp.where(lane == e1, s, 0.0), axis=-1, keepdims=True)
    w2 = jnp.sum(jnp.where(lane == e2, s, 0.0), axis=-1, keepdims=True)
    tot = w1 + w2
    return e1.astype(jnp.int32), e2.astype(jnp.int32), w1 / tot, w2 / tot


def _mixout_kernel(x_ref, c_ref, ux_ref, ys_ref, mod_ref, band_ref, icnt_ref, wp_ref, sp_ref, d_ref,
                   wglu_ref, bglu_ref, wout_ref, gffn_ref, wr_ref, br_ref, tri_ref,
                   xo_ref, hs_ref, pos_ref, gate_ref, cnt_ref, *, n_x_tiles):
    ux = ux_ref[...]
    parts = []
    for g in range(len(POOL_WINDOWS)):
        ug = ux[:, g * POOL_GW:(g + 1) * POOL_GW]
        hi = ug.astype(BF16)
        lo = (ug - hi.astype(F32)).astype(BF16)
        band = band_ref[g]
        wsum = (jnp.dot(band, hi, preferred_element_type=F32)
                + jnp.dot(band, lo, preferred_element_type=F32))
        p = wsum * icnt_ref[:, g * POOL_GW:(g + 1) * POOL_GW] - ug
        parts.append(jnp.dot(p.astype(BF16), wp_ref[g], preferred_element_type=F32))
    pool = jnp.concatenate(parts, axis=-1) * sp_ref[...]

    y = ys_ref[...] + d_ref[...] * ux[:, POOL_W:]
    y = _gelu_tanh(y)
    z = jnp.dot(y.astype(BF16), wglu_ref[...], preferred_element_type=F32) + bglu_ref[...]
    glu = y * jax.nn.sigmoid(z)

    cat = jnp.concatenate([pool, glu], axis=-1).astype(BF16)
    o = jnp.dot(cat, wout_ref[...], preferred_element_type=F32)
    xn = _token_tile(x_ref, c_ref, n_x_tiles) + mod_ref[2:3, :] * o
    xo_ref[...] = xn

    h2 = _rmsnorm_mod(xn, gffn_ref[...], mod_ref[3:4, :], mod_ref[4:5, :])
    h_hi = h2.astype(BF16)
    h_lo = (h2 - h_hi.astype(F32)).astype(BF16)
    wr = wr_ref[...]
    p_hi = jnp.dot(h_hi, wr, preferred_element_type=F32)
    p_lo = jnp.dot(h_lo, wr[:, :N_EXPERTS], preferred_element_type=F32)
    logits = p_hi[:, :N_EXPERTS] + (p_hi[:, N_EXPERTS:] + p_lo)
    e1, e2, g1, g2 = _route(jax.nn.sigmoid(logits), br_ref[...])
    gate_ref[...] = jnp.concatenate([g1, g2], axis=-1)

    lane = lax.broadcasted_iota(jnp.int32, (TM, N_EXPERTS), 1)
    oh0 = jnp.where(lane == e1, 1.0, 0.0)
    oh1 = jnp.where(lane == e2, 1.0, 0.0)
    tri = tri_ref[...]
    before0 = jnp.dot(tri, oh0.astype(BF16), preferred_element_type=F32)
    before1 = jnp.dot(tri, oh1.astype(BF16), preferred_element_type=F32)
    tot0 = jnp.sum(oh0, axis=0, keepdims=True)
    tot1 = jnp.sum(oh1, axis=0, keepdims=True)
    smaller = jnp.where(lane > e1, 1.0, 0.0) + jnp.where(lane > e2, 1.0, 0.0)
    off = jnp.sum(smaller, axis=0, keepdims=True)
    p0 = jnp.sum(oh0 * (off + before0), axis=-1, keepdims=True)
    p1 = jnp.sum(oh1 * (off + tot0 + before1), axis=-1, keepdims=True)
    pos_ref[...] = jnp.concatenate([p0, p1], axis=-1).astype(jnp.int32)
    cnt_ref[...] = (tot0 + tot1).astype(jnp.int32)

    slot = lax.broadcasted_iota(jnp.int32, (TM, 2 * TM), 1).astype(F32)
    perm_t = jnp.where(jnp.logical_or(slot == p0, slot == p1), 1.0, 0.0).astype(BF16)
    hs = lax.dot_general(perm_t, h2.astype(BF16), (((0,), (0,)), ((), ())),
                         preferred_element_type=F32)
    for q in range(ROW_TILE):
        hs_ref[pl.ds(q, 2 * TM, stride=ROW_TILE), :] = hs[:, q * LANES:(q + 1) * LANES]


def _mixout(xa, ca, ctx_block, ux, y_ssm, mod_l, band, icnt, wp_bf, sp, dvec, wglu_bf, bglu, wout_bf,
            gffn, w_router, b_router, tri, nt, n_x_tiles):
    t = nt * TM
    sel = lambda i: (jnp.where(i == n_x_tiles, 1, 0), 0, 0)
    sel4 = lambda i: (jnp.where(i == n_x_tiles, 1, 0), 0, 0, 0)
    row = lambda i: (i, 0)
    fix2 = lambda i: (0, 0)
    return pl.pallas_call(
        functools.partial(_mixout_kernel, n_x_tiles=n_x_tiles),
        grid=(nt,),
        in_specs=_token_specs(n_x_tiles, ctx_block) + [
            pl.BlockSpec((TM, D_MODEL), row),
            pl.BlockSpec((TM, SSM_W), row),
            pl.BlockSpec((None, N_MOD, D_MODEL), sel),
            pl.BlockSpec((None, len(POOL_WINDOWS), TM, TM), sel4),
            pl.BlockSpec((None, TM, POOL_W), sel),
            pl.BlockSpec((len(POOL_WINDOWS), POOL_GW, POOL_GW), lambda i: (0, 0, 0)),
            pl.BlockSpec((1, POOL_W), fix2),
            pl.BlockSpec((1, SSM_W), fix2),
            pl.BlockSpec((SSM_W, SSM_W), fix2),
            pl.BlockSpec((1, SSM_W), fix2),
            pl.BlockSpec((D_MODEL, D_MODEL), fix2),
            pl.BlockSpec((1, D_MODEL), fix2),
            pl.BlockSpec((D_MODEL, 2 * N_EXPERTS), fix2),
            pl.BlockSpec((1, N_EXPERTS), fix2),
            pl.BlockSpec((TM, TM), fix2),
        ],
        out_specs=[
            pl.BlockSpec((TM, D_MODEL), row),
            pl.BlockSpec((2 * TM * ROW_TILE, LANES), row),
            pl.BlockSpec((TM, 2), row),
            pl.BlockSpec((TM, 2), row),
            pl.BlockSpec((None, 1, N_EXPERTS), lambda i: (i, 0, 0)),
        ],
        out_shape=[
            jax.ShapeDtypeStruct((t, D_MODEL), F32),
            jax.ShapeDtypeStruct((2 * t * ROW_TILE, LANES), F32),
            jax.ShapeDtypeStruct((t, 2), jnp.int32),
            jax.ShapeDtypeStruct((t, 2), F32),
            jax.ShapeDtypeStruct((nt, 1, N_EXPERTS), jnp.int32),
        ],
        compiler_params=_cparams(("arbitrary",)),
        name="mixer_out_router",
    )(xa, ca, ux, y_ssm, mod_l, band, icnt, wp_bf, sp, dvec, wglu_bf, bglu, wout_bf, gffn,
      w_router, b_router, tri)


def _copy_rows(src, src_row, dst, dst_row, n, n_max, sem):
    done = 0
    for bit in (1 << b for b in range(n_max.bit_length() - 1, -1, -1)):
        take = n & bit

        @pl.when(take != 0)
        def _(done=done, bit=bit):
            s0 = pl.multiple_of((src_row + done) * ROW_TILE, ROW_TILE)
            d0 = pl.multiple_of((dst_row + done) * ROW_TILE, ROW_TILE)
            pltpu.make_async_copy(src.at[pl.ds(s0, bit * ROW_TILE)],
                                  dst.at[pl.ds(d0, bit * ROW_TILE)], sem).start()

        done = done + take


def _wait_rows(src, dst, n, sem):
    size = pl.multiple_of(n * ROW_TILE, ROW_TILE)
    pltpu.make_async_copy(src.at[pl.ds(0, size)], dst.at[pl.ds(0, size)], sem).wait()


def _expert_kernel(be_ref, nbu_ref, nv_ref, rs_ref, re_ref, rsrc_ref, rlen_ref, rdst_ref,
                   hs_hbm, wg_ref, wu_ref, wd_ref, y_ref, xg0, xg1, wg_s, wu_s, wd_s, gsem):
    s = pl.program_id(0)
    nbu = nbu_ref[0]
    xg = (xg0, xg1)

    def gather_start(blk, slot):
        lo_blk = blk * MOE_BLK

        def body(r, c):
            g0 = rdst_ref[r]
            lo = jnp.maximum(g0, lo_blk)
            hi = jnp.minimum(g0 + rlen_ref[r], lo_blk + MOE_BLK)
            _copy_rows(hs_hbm, rsrc_ref[r] + (lo - g0), xg[slot], lo - lo_blk,
                       jnp.maximum(hi - lo, 0), TM, gsem.at[slot])
            return c

        lax.fori_loop(rs_ref[blk], re_ref[blk], body, 0)

    changed = jnp.logical_or(s == 0, be_ref[s] != be_ref[jnp.maximum(s - 1, 0)])

    @pl.when(jnp.logical_and(changed, s < nbu))
    def _():
        wg_s[...] = wg_ref[...].astype(BF16)
        wu_s[...] = wu_ref[...].astype(BF16)
        wd_s[...] = wd_ref[...].astype(BF16)

    def step(slot):
        @pl.when(s == 0)
        def _():
            xg0[...] = jnp.zeros_like(xg0)
            xg1[...] = jnp.zeros_like(xg1)
            gather_start(0, slot)

        @pl.when(s + 1 < nbu)
        def _():
            gather_start(s + 1, 1 - slot)

        _wait_rows(hs_hbm, xg[slot], nv_ref[s], gsem.at[slot])
        xb = jnp.concatenate([xg[slot][pl.ds(q, MOE_BLK, stride=ROW_TILE), :]
                              for q in range(ROW_TILE)], axis=-1).astype(BF16)
        g = jnp.dot(xb, wg_s[...], preferred_element_type=F32)
        u = jnp.dot(xb, wu_s[...], preferred_element_type=F32)
        hid = (g * jax.nn.sigmoid(g)) * u
        out = jnp.dot(hid.astype(BF16), wd_s[...], preferred_element_type=F32)
        for q in range(ROW_TILE):
            y_ref[pl.ds(q, MOE_BLK, stride=ROW_TILE), :] = out[:, q * LANES:(q + 1) * LANES]

    for slot in range(2):
        @pl.when(jnp.logical_and(s < nbu, s % 2 == slot))
        def _(slot=slot):
            step(slot)

    @pl.when(s >= nbu)
    def _():
        y_ref[...] = jnp.zeros_like(y_ref)


def _experts(blk_e, nb_used, blk_valid, rs, re, rsrc, rlen, rdst, hs, w_gate, w_up, w_down, layer):
    nb = blk_e.shape[0]
    wmap = lambda i, be, *_: (layer, be[i], 0, 0)
    buf = pltpu.VMEM((MOE_BLK * ROW_TILE, LANES), F32)
    return pl.pallas_call(
        _expert_kernel,
        grid_spec=pltpu.PrefetchScalarGridSpec(
            num_scalar_prefetch=8,
            grid=(nb,),
            in_specs=[
                pl.BlockSpec(memory_space=pl.ANY),
                pl.BlockSpec((None, None, D_MODEL, D_EXPERT), wmap),
                pl.BlockSpec((None, None, D_MODEL, D_EXPERT), wmap),
                pl.BlockSpec((None, None, D_EXPERT, D_MODEL), wmap),
            ],
            out_specs=pl.BlockSpec((MOE_BLK * ROW_TILE, LANES), lambda i, *_: (i, 0)),
            scratch_shapes=[buf, buf,
                            pltpu.VMEM((D_MODEL, D_EXPERT), BF16),
                            pltpu.VMEM((D_MODEL, D_EXPERT), BF16),
                            pltpu.VMEM((D_EXPERT, D_MODEL), BF16),
                            pltpu.SemaphoreType.DMA((2,))],
        ),
        out_shape=jax.ShapeDtypeStruct((nb * MOE_BLK * ROW_TILE, LANES), F32),
        compiler_params=_cparams(("arbitrary",)),
        name="moe_experts",
    )(blk_e, nb_used, blk_valid, rs, re, rsrc, rlen, rdst, hs, w_gate, w_up, w_down)


def _combine_kernel(meta_ref, x_ref, pos_ref, gate_ref, mod_ref, gfin_ref, ys_hbm, out_ref,
                    st0, st1, sem, *, final, nt):
    i = pl.program_id(0)
    stage = (st0, st1)

    def fetch(tile, slot):
        base = tile * LANES

        def runs(n_max):
            for e in range(N_EXPERTS):
                _copy_rows(ys_hbm, meta_ref[base + e], stage[slot],
                           meta_ref[base + 2 * N_EXPERTS + e], meta_ref[base + N_EXPERTS + e],
                           n_max, sem.at[slot])

        longest = meta_ref[base + 3 * N_EXPERTS]

        @pl.when(longest <= SHORT_RUN)
        def _():
            runs(SHORT_RUN)

        @pl.when(longest > SHORT_RUN)
        def _():
            runs(TM)

    def step(slot):
        @pl.when(i == 0)
        def _():
            fetch(0, slot)

        @pl.when(i + 1 < nt)
        def _():
            fetch(i + 1, 1 - slot)

        _wait_rows(ys_hbm, stage[slot], 2 * TM, sem.at[slot])
        rows = jnp.concatenate([stage[slot][pl.ds(q, 2 * TM, stride=ROW_TILE), :]
                                for q in range(ROW_TILE)], axis=-1)
        pos = pos_ref[...].astype(F32)
        gate = gate_ref[...]
        slot_id = lax.broadcasted_iota(jnp.int32, (TM, 2 * TM), 1).astype(F32)
        gmat = (jnp.where(slot_id == pos[:, 0:1], gate[:, 0:1], 0.0)
                + jnp.where(slot_id == pos[:, 1:2], gate[:, 1:2], 0.0))
        g_hi = gmat.astype(BF16)
        g_lo = (gmat - g_hi.astype(F32)).astype(BF16)
        r_hi = rows.astype(BF16)
        r_lo = (rows - r_hi.astype(F32)).astype(BF16)
        y = (jnp.dot(g_hi, r_hi, preferred_element_type=F32)
             + (jnp.dot(g_lo, r_hi, preferred_element_type=F32)
                + jnp.dot(g_hi, r_lo, preferred_element_type=F32)))
        xn = x_ref[...] + mod_ref[5:6, :] * y
        if final:
            ms = jnp.mean(xn * xn, axis=-1, keepdims=True)
            xn = xn * lax.rsqrt(ms + EPS) * gfin_ref[...]
        out_ref[...] = xn

    for slot in range(2):
        @pl.when(i % 2 == slot)
        def _(slot=slot):
            step(slot)


def _combine(meta, xn, pos, gate, mod_l, g_final, ys, nt, n_x_tiles, final):
    sel = lambda i, m: (jnp.where(i == n_x_tiles, 1, 0), 0, 0)
    row = lambda i, m: (i, 0)
    stage = pltpu.VMEM((2 * TM * ROW_TILE, LANES), F32)
    return pl.pallas_call(
        functools.partial(_combine_kernel, final=final, nt=nt),
        grid_spec=pltpu.PrefetchScalarGridSpec(
            num_scalar_prefetch=1,
            grid=(nt,),
            in_specs=[
                pl.BlockSpec((TM, D_MODEL), row),
                pl.BlockSpec((TM, 2), row),
                pl.BlockSpec((TM, 2), row),
                pl.BlockSpec((None, N_MOD, D_MODEL), sel),
                pl.BlockSpec((1, D_MODEL), lambda i, m: (0, 0)),
                pl.BlockSpec(memory_space=pl.ANY),
            ],
            out_specs=pl.BlockSpec((TM, D_MODEL), row),
            scratch_shapes=[stage, stage, pltpu.SemaphoreType.DMA((2,))],
        ),
        out_shape=jax.ShapeDtypeStruct((nt * TM, D_MODEL), F32),
        compiler_params=_cparams(("arbitrary",)),
        name="moe_combine",
    )(meta, xn, pos, gate, mod_l, g_final, ys)


def _moe(hs, pos, tcnt, gate, xn, mod_l, g_final, w_gate, w_up, w_down, layer, nt, n_x_tiles,
         final):
    t = nt * TM
    i32 = jnp.int32
    n = tcnt.reshape(nt, N_EXPERTS)
    off = jnp.cumsum(n, axis=1) - n
    base = jnp.cumsum(n, axis=0) - n
    counts = jnp.sum(n, axis=0)
    padded = (counts + MOE_BLK - 1) // MOE_BLK * MOE_BLK
    pend = jnp.cumsum(padded)
    pstart = pend - padded
    nb = (2 * t + N_EXPERTS * (MOE_BLK - 1)) // MOE_BLK + 1
    nb_used = (pend[-1] // MOE_BLK).astype(i32)
    blk_lo = jnp.arange(nb, dtype=i32) * MOE_BLK
    blk_e = jnp.minimum(jnp.sum(pend[None, :] <= jnp.minimum(blk_lo, pend[-1] - MOE_BLK)[:, None],
                                axis=1), N_EXPERTS - 1).astype(i32)
    blk_valid = jnp.clip((pstart + counts)[blk_e] - blk_lo, 0, MOE_BLK).astype(i32)
    gsrc = (pstart[None, :] + base).astype(i32)
    rdst = gsrc.T.reshape(-1)
    rlen = n.T.reshape(-1).astype(i32)
    rsrc = (jnp.arange(nt, dtype=i32)[:, None] * (2 * TM) + off).T.reshape(-1).astype(i32)
    e_dst = gsrc.T[blk_e]
    e_end = e_dst + n.T[blk_e]
    rs = (blk_e * nt + jnp.sum(e_end <= blk_lo[:, None], axis=1)).astype(i32)
    re = (blk_e * nt + jnp.sum(e_dst < (blk_lo + MOE_BLK)[:, None], axis=1)).astype(i32)
    ys = _experts(blk_e, nb_used.reshape(1), blk_valid, rs, re, rsrc, rlen, rdst, hs,
                  w_gate, w_up, w_down, layer)
    longest = jnp.max(n, axis=1, keepdims=True).astype(i32)
    meta = jnp.concatenate([gsrc, n.astype(i32), off.astype(i32), longest,
                            jnp.zeros((nt, LANES - 3 * N_EXPERTS - 1), i32)], axis=1).reshape(-1)
    return _combine(meta, xn, pos, gate, mod_l, g_final, ys, nt, n_x_tiles, final)


def _cmul(ar, ai, br, bi):
    return ar * br - ai * bi, ar * bi + ai * br


def _expand_rows(v, rep_ref, mask_ref):
    return jnp.dot(v.astype(BF16), rep_ref[...], preferred_element_type=F32) * mask_ref[...]


def _expand_cols(e, rept_ref, maskt_ref):
    out = lax.dot_general(rept_ref[...], e.astype(BF16), (((1,), (1,)), ((), ())),
                          preferred_element_type=F32)
    return out * maskt_ref[...]


def _ssm_prep_kernel(are_ref, aim_ref, ldt_ref, btr_ref, bti_ref, cr_ref, ci_ref,
                     rep_ref, mask_ref, rept_ref, maskt_ref, ms_ref, mcat_ref, ecat):
    lag_blocks = []
    for d in range(2):
        a_re = jnp.minimum(are_ref[d], -1e-4)
        a_im = aim_ref[d]
        dt = jnp.exp(ldt_ref[d])
        mag = jnp.exp(a_re * dt)
        lr = mag * jnp.cos(a_im * dt)
        li = mag * jnp.sin(a_im * dt)
        den = a_re * a_re + a_im * a_im
        k_re = ((lr - 1.0) * a_re + li * a_im) / den
        k_im = (li * a_re - (lr - 1.0) * a_im) / den
        bbr, bbi = _cmul(k_re, k_im, btr_ref[d], bti_ref[d])
        cr, ci = cr_ref[d], ci_ref[d]
        pr, pi = jnp.ones_like(lr), jnp.zeros_like(lr)
        v0 = None
        for n in range(CH + 1):
            er, ei = _cmul(pr, pi, cr, ci)
            et = jnp.concatenate([_expand_cols(er, rept_ref, maskt_ref),
                                  _expand_cols(-ei, rept_ref, maskt_ref)], axis=0).astype(BF16)
            if n < CH:
                vr, vi = _cmul(pr, pi, bbr, bbi)
                v = jnp.concatenate([_expand_rows(vr, rep_ref, mask_ref),
                                     _expand_rows(vi, rep_ref, mask_ref)], axis=1).astype(BF16)
                if n == 0:
                    v0 = v
                sigma = CH - 1 - n if d == 0 else n
                ms_ref[d, sigma * LANES:(sigma + 1) * LANES, :] = v
                ecat[:, n * LANES:(n + 1) * LANES] = et
            if n >= 1:
                tau = n - 1 if d == 0 else CH - n
                mcat_ref[CW + d * SW:CW + (d + 1) * SW, tau * LANES:(tau + 1) * LANES] = et
            pr, pi = _cmul(pr, pi, lr, li)
        lag_blocks.append(jnp.dot(v0, ecat[...], preferred_element_type=F32))
    kf, kb = lag_blocks
    for s in range(CH):
        for t in range(CH):
            if t > s:
                blk = kf[:, (t - s) * LANES:(t - s + 1) * LANES]
            elif t < s:
                blk = kb[:, (s - t) * LANES:(s - t + 1) * LANES]
            else:
                blk = kf[:, :LANES] + kb[:, :LANES]
            mcat_ref[s * LANES:(s + 1) * LANES, t * LANES:(t + 1) * LANES] = blk.astype(BF16)


def _ssm_operators(a_re, a_im, log_dt, b_re, b_im, c_re, c_im):
    depth = a_re.shape[0]
    rows = SSM_GROUPS * SSM_GROUP
    p = SSM_STATE
    rep_rows = lambda v: jnp.repeat(v, SSM_GROUP, axis=2)
    are = rep_rows(a_re)
    aim = rep_rows(a_im)
    ldt = jnp.broadcast_to(rep_rows(log_dt[..., None]), are.shape)
    btr = jnp.swapaxes(b_re, -1, -2).reshape(depth, 2, rows, p)
    bti = jnp.swapaxes(b_im, -1, -2).reshape(depth, 2, rows, p)
    cr = c_re.reshape(depth, 2, rows, p)
    ci = c_im.reshape(depth, 2, rows, p)
    rep = np.tile(np.eye(p, dtype=np.float32), (1, GB))
    mask = np.kron(np.eye(GB, dtype=np.float32), np.ones((SSM_GROUP, p), np.float32))
    par = pl.BlockSpec((None, 2, LANES, p), lambda l, b: (l, 0, b, 0))
    fix = lambda shape: pl.BlockSpec(shape, lambda l, b: (0, 0))
    return pl.pallas_call(
        _ssm_prep_kernel,
        grid=(depth, NGB),
        in_specs=[par] * 7 + [fix((p, GB * p)), fix((LANES, GB * p)),
                              fix((GB * p, p)), fix((GB * p, LANES))],
        out_specs=[
            pl.BlockSpec((None, None, 2, CW, SW), lambda l, b: (l, b, 0, 0, 0)),
            pl.BlockSpec((None, None, CW + 2 * SW, CW), lambda l, b: (l, b, 0, 0)),
        ],
        out_shape=[
            jax.ShapeDtypeStruct((depth, NGB, 2, CW, SW), BF16),
            jax.ShapeDtypeStruct((depth, NGB, CW + 2 * SW, CW), BF16),
        ],
        scratch_shapes=[pltpu.VMEM((SW, CW), BF16)],
        compiler_params=_cparams(("arbitrary", "arbitrary")),
        name="ssm_operators",
    )(are, aim, ldt, btr, bti, cr, ci, jnp.asarray(rep, BF16), jnp.asarray(mask),
      jnp.asarray(rep.T, BF16), jnp.asarray(mask.T))


def _scan_tables(a_re, a_im, log_dt):
    depth = a_re.shape[0]
    a_re = jnp.minimum(a_re, -1e-4)
    dt = jnp.exp(log_dt)[..., None]
    mag = jnp.exp(a_re * dt)
    lr = mag * jnp.cos(a_im * dt)
    li = mag * jnp.sin(a_im * dt)
    ar, ai = lr, li
    for _ in range(CH - 1):
        ar, ai = _cmul(ar, ai, lr, li)
    apw = [(ar, ai)]
    for _ in range(SUBLANES - 1):
        apw.append(_cmul(apw[-1][0], apw[-1][1], ar, ai))
    apr = jnp.stack([q[0] for q in apw], axis=1)
    api = jnp.stack([q[1] for q in apw], axis=1)
    rows = np.arange(SUBLANES)
    tabs = []
    for d in range(2):
        consts = []
        for k in (1, 2, 4):
            keep = (rows >= k) if d == 0 else (rows < SUBLANES - k)
            keep = jnp.asarray(keep.astype(np.float32))[None, :, None, None]
            consts.append(keep * apr[:, k - 1:k, d])
            consts.append(keep * api[:, k - 1:k, d])
        order = rows if d == 0 else SUBLANES - 1 - rows
        consts.append(apr[:, order, d])
        consts.append(api[:, order, d])
        tabs.append(jnp.stack(consts, axis=1))
    tab = jnp.stack(tabs, axis=1).reshape(depth, 2, 8 * SUBLANES, NGB, GB * SSM_STATE)
    return tab.transpose(0, 1, 3, 2, 4)


def _pool_constants(seg):
    t = np.arange(TM)
    s0 = t // seg * seg
    band = np.zeros((len(POOL_WINDOWS), TM, TM), np.float32)
    icnt = np.zeros((TM, POOL_W), np.float32)
    for g, w in enumerate(POOL_WINDOWS):
        lo = np.maximum(t - w // 2, s0)
        hi = np.minimum(t + w // 2, s0 + seg)
        band[g] = (t[None, :] >= lo[:, None]) & (t[None, :] < hi[:, None])
        icnt[:, g * POOL_GW:(g + 1) * POOL_GW] = (1.0 / (hi - lo).astype(np.float32))[:, None]
    return band, icnt


def kernel(x, c, ctx, c_ctx, w_mod, b_mod, g_mix, g_ffn, w_in, w_out, w_pool, s_pool,
           ssm_a_re, ssm_a_im, ssm_log_dt, ssm_b_re, ssm_b_im, ssm_c_re, ssm_c_im, ssm_d,
           w_glu, b_glu, w_router, b_router, w_gate, w_up, w_down, g_final):
    bsz, seq, d = x.shape
    ctx_len = ctx.shape[1]
    depth = w_mod.shape[0]
    assert bsz == 1 and d == D_MODEL and ctx_len == TM
    assert seq % (SR * CH) == 0 and seq % GRID_W == 0 and seq % TI == 0
    n_xt = seq // TM
    n_st = seq // (SR * CH)
    ctx_rows = ctx_len // CH

    cvec = jnp.concatenate([c.reshape(1, d), c_ctx.reshape(1, d),
                            jnp.zeros((SUBLANES - 2, d), F32)], axis=0)
    mod = _modulation(cvec, w_mod, b_mod).reshape(depth, SUBLANES, N_MOD, d)

    band_x, icnt_x = _pool_constants(GRID_W)
    band_c, icnt_c = _pool_constants(ctx_len)
    band = jnp.asarray(np.stack([band_x, band_c]), BF16)
    icnt = jnp.asarray(np.stack([icnt_x, icnt_c]), F32)
    tri = jnp.asarray(np.tril(np.ones((TM, TM), np.float32), -1), BF16)
    wr_hi = w_router.astype(BF16)
    wr_split = jnp.concatenate([wr_hi, (w_router - wr_hi.astype(F32)).astype(BF16)], axis=1)

    ms, mcat = _ssm_operators(ssm_a_re, ssm_a_im, ssm_log_dt, ssm_b_re, ssm_b_im,
                              ssm_c_re, ssm_c_im)
    tab = _scan_tables(ssm_a_re, ssm_a_im, ssm_log_dt)

    xa, ca, ctx_block = x[0], ctx[0], 0
    for l in range(depth):
        last = l == depth - 1
        mod_l = mod[l, :2]
        ux, uv_ext = _inproj(xa, ca, ctx_block, mod_l, g_mix[l].reshape(1, d),
                             w_in[l].astype(BF16), n_xt)
        hf, gb = _ssm_states(uv_ext, ms, tab, l, n_st, ctx_rows)
        y_ssm = _ssm_readout(uv_ext, hf, gb, mcat, l)
        nt = n_xt if last else n_xt + 1
        xn, hs, pos, gate, tcnt = _mixout(
            xa, ca, ctx_block, ux, y_ssm, mod_l, band, icnt, w_pool[l].astype(BF16),
            s_pool[l].reshape(1, -1), ssm_d[l].reshape(1, -1), w_glu[l].astype(BF16),
            b_glu[l].reshape(1, -1), w_out[l].astype(BF16), g_ffn[l].reshape(1, d), wr_split,
            b_router.reshape(1, -1), tri, nt, n_xt)
        tok = _moe(hs, pos, tcnt, gate, xn, mod_l, g_final.reshape(1, d), w_gate, w_up, w_down,
                   l, nt, n_xt, last)
        xa, ca, ctx_block = tok, tok, n_xt
    return tok.reshape(bsz, seq, d)
```

```python
import functools

import numpy as np
import jax
import jax.numpy as jnp
from jax import lax
from jax.experimental import pallas as pl
from jax.experimental.pallas import tpu as pltpu

F32 = jnp.float32
BF16 = jnp.bfloat16

D_MODEL = 1024
POOL_W = 512
SSM_W = 512
POOL_WINDOWS = (2, 4, 8, 16)
POOL_GW = 128
SSM_GROUP = 16
SSM_GROUPS = 32
SSM_STATE = 64
N_EXPERTS = 32
N_EXPERT_GROUPS = 4
EXPERTS_PER_GROUP = 8
D_EXPERT = 512
GRID_W = 64
EPS = 1e-6
N_MOD = 6

LANES = 128
SUBLANES = 8
TM = 256
TI = 512
CH = 8
GB = 8
NGB = SSM_GROUPS // GB
CW = CH * LANES
SW = 2 * GB * SSM_STATE
SR = 256
MOE_BLK = 512
ROW_TILE = D_MODEL // LANES
SHORT_RUN = 31
VMEM_LIMIT = 48 * 1024 * 1024


def _cparams(sem):
    return pltpu.CompilerParams(dimension_semantics=sem, vmem_limit_bytes=VMEM_LIMIT)


def _rmsnorm_mod(x, g, shift, scale):
    ms = jnp.mean(x * x, axis=-1, keepdims=True)
    y = x * lax.rsqrt(ms + EPS) * g
    return y * (1.0 + scale) + shift


def _mod_kernel(c_ref, w_ref, b_ref, o_ref):
    c = c_ref[...]
    a = c * jax.nn.sigmoid(c)
    o_ref[...] = jnp.dot(a, w_ref[...], preferred_element_type=F32,
                         precision=lax.Precision.HIGHEST) + b_ref[...]


def _modulation(cvec, w_mod, b_mod):
    depth, d, n = w_mod.shape
    tn = 1536
    return pl.pallas_call(
        _mod_kernel,
        grid=(depth, n // tn),
        in_specs=[
            pl.BlockSpec((SUBLANES, d), lambda l, j: (0, 0)),
            pl.BlockSpec((None, d, tn), lambda l, j: (l, 0, j)),
            pl.BlockSpec((None, 1, tn), lambda l, j: (l, 0, j)),
        ],
        out_specs=pl.BlockSpec((None, SUBLANES, tn), lambda l, j: (l, 0, j)),
        out_shape=jax.ShapeDtypeStruct((depth, SUBLANES, n), F32),
        compiler_params=_cparams(("arbitrary", "arbitrary")),
        name="modulation",
    )(cvec, w_mod, b_mod.reshape(depth, 1, n))


def _token_tile(x_ref, c_ref, n_x_tiles):
    return jnp.where(pl.program_id(0) == n_x_tiles, c_ref[...], x_ref[...])


def _inproj_kernel(x_ref, c_ref, mod_ref, g_ref, w_ref, ux_ref, uv_ref, scr, *, n_big):
    i = pl.program_id(0)

    def project(x, rows):
        h = _rmsnorm_mod(x, g_ref[...], mod_ref[0:1, :], mod_ref[1:2, :])
        ux = jnp.dot(h.astype(BF16), w_ref[...], preferred_element_type=F32)
        ux_ref[0:rows, :] = ux
        crow = rows // CH
        for b in range(NGB):
            scr[b, 0:rows, :] = ux[:, POOL_W + LANES * b:POOL_W + LANES * (b + 1)]
            for t in range(CH):
                piece = scr[b, pl.ds(t, crow, stride=CH), :]
                uv_ref[0:crow, pl.ds((b * CH + t) * LANES, LANES)] = piece.astype(BF16)

    @pl.when(i < n_big)
    def _():
        project(x_ref[...], TI)

    @pl.when(i == n_big)
    def _():
        project(c_ref[...], TM)
        ux_ref[TM:TI, :] = jnp.zeros((TI - TM, D_MODEL), F32)
        uv_ref[TM // CH:TI // CH, :] = jnp.zeros(((TI - TM) // CH, NGB * CW), BF16)

    @pl.when(i > n_big)
    def _():
        uv_ref[...] = jnp.zeros_like(uv_ref)


def _token_specs(n_x_tiles, ctx_block):
    return [pl.BlockSpec((TM, D_MODEL), lambda i: (jnp.minimum(i, n_x_tiles - 1), 0)),
            pl.BlockSpec((TM, D_MODEL), lambda i: (ctx_block, 0))]


def _inproj(xa, ca, ctx_block, mod_l, g, w_bf, n_x_tiles):
    n_big = n_x_tiles * TM // TI
    n_steps = n_big + SR // (TI // CH)
    sel = lambda i: (jnp.where(i >= n_big, 1, 0), 0, 0)
    return pl.pallas_call(
        functools.partial(_inproj_kernel, n_big=n_big),
        grid=(n_steps,),
        in_specs=[
            pl.BlockSpec((TI, D_MODEL), lambda i: (jnp.minimum(i, n_big - 1), 0)),
            pl.BlockSpec((TM, D_MODEL), lambda i: (ctx_block, 0)),
            pl.BlockSpec((None, N_MOD, D_MODEL), sel),
            pl.BlockSpec((1, D_MODEL), lambda i: (0, 0)),
            pl.BlockSpec((D_MODEL, D_MODEL), lambda i: (0, 0)),
        ],
        out_specs=[
            pl.BlockSpec((TI, D_MODEL), lambda i: (jnp.minimum(i, n_big), 0)),
            pl.BlockSpec((TI // CH, NGB * CW), lambda i: (i, 0)),
        ],
        out_shape=[
            jax.ShapeDtypeStruct(((n_x_tiles + 1) * TM, D_MODEL), F32),
            jax.ShapeDtypeStruct((n_steps * TI // CH, NGB * CW), BF16),
        ],
        scratch_shapes=[pltpu.VMEM((NGB, TI, LANES), F32)],
        compiler_params=_cparams(("arbitrary",)),
        name="mixer_inproj",
    )(xa, ca, mod_l, g, w_bf)


def _chunk_scan(s_ref, tab_ref, carry_ref, out_ref, nblk, reverse):
    half = SW // 2
    ncol = half // LANES
    row = lax.broadcasted_iota(jnp.int32, (SUBLANES, LANES), 0)
    edge = (row == SUBLANES - 1) if reverse else (row == 0)
    last = 0 if reverse else SUBLANES - 1

    def sub_block(r0, carry):
        outs_r, outs_i, new_carry = [], [], []
        for j in range(ncol):
            cr, ci = carry[2 * j], carry[2 * j + 1]
            lre = pl.ds(LANES * j, LANES)
            lim = pl.ds(half + LANES * j, LANES)
            zr = s_ref[pl.ds(r0, SUBLANES), lre]
            zi = s_ref[pl.ds(r0, SUBLANES), lim]
            for q, k in enumerate((1, 2, 4)):
                ar = tab_ref[pl.ds(16 * q, SUBLANES), lre]
                ai = tab_ref[pl.ds(16 * q + 8, SUBLANES), lre]
                sh = SUBLANES - k if reverse else k
                sr = pltpu.roll(zr, sh, axis=0)
                si = pltpu.roll(zi, sh, axis=0)
                zr, zi = zr + ar * sr - ai * si, zi + ar * si + ai * sr
            pr = tab_ref[pl.ds(48, SUBLANES), lre]
            pi = tab_ref[pl.ds(56, SUBLANES), lre]
            zr, zi = zr + pr * cr - pi * ci, zi + pr * ci + pi * cr
            sh1 = SUBLANES - 1 if reverse else 1
            outs_r.append(jnp.where(edge, cr, pltpu.roll(zr, sh1, axis=0)))
            outs_i.append(jnp.where(edge, ci, pltpu.roll(zi, sh1, axis=0)))
            new_carry.append(jnp.broadcast_to(zr[last:last + 1, :], (SUBLANES, LANES)))
            new_carry.append(jnp.broadcast_to(zi[last:last + 1, :], (SUBLANES, LANES)))
        return outs_r, outs_i, tuple(new_carry)

    def body(it, carry):
        bi = (nblk - 1 - it) if reverse else it
        r0 = pl.multiple_of(bi * 2 * SUBLANES, 2 * SUBLANES)
        if reverse:
            hi_r, hi_i, carry = sub_block(r0 + SUBLANES, carry)
            lo_r, lo_i, carry = sub_block(r0, carry)
        else:
            lo_r, lo_i, carry = sub_block(r0, carry)
            hi_r, hi_i, carry = sub_block(r0 + SUBLANES, carry)
        for j in range(ncol):
            out_ref[pl.ds(r0, 2 * SUBLANES), pl.ds(LANES * j, LANES)] = (
                jnp.concatenate([lo_r[j], hi_r[j]], axis=0).astype(BF16))
            out_ref[pl.ds(r0, 2 * SUBLANES), pl.ds(half + LANES * j, LANES)] = (
                jnp.concatenate([lo_i[j], hi_i[j]], axis=0).astype(BF16))
        return carry

    init = tuple(carry_ref[:, pl.ds(LANES * c, LANES)] for c in range(2 * ncol))
    final = lax.fori_loop(0, nblk, body, init)
    for c in range(2 * ncol):
        carry_ref[:, pl.ds(LANES * c, LANES)] = final[c]


def _ssm_state_kernel(uf_ref, ub_ref, msf_ref, msb_ref, tf_ref, tb_ref, hf_ref, gb_ref,
                      sf, sb, cf, cb, *, ctx_rows):
    step = pl.program_id(1)

    @pl.when(step == 0)
    def _():
        cf[...] = jnp.zeros_like(cf)
        cb[...] = jnp.zeros_like(cb)
        hf_ref[...] = jnp.zeros_like(hf_ref)
        gb_ref[...] = jnp.zeros_like(gb_ref)

    sf[...] = jnp.dot(uf_ref[...], msf_ref[...], preferred_element_type=F32)
    sb[...] = jnp.dot(ub_ref[...], msb_ref[...], preferred_element_type=F32)
    nblk = jnp.where(step == 0, ctx_rows // (2 * SUBLANES), SR // (2 * SUBLANES))
    _chunk_scan(sf, tf_ref, cf, hf_ref, nblk, reverse=False)
    _chunk_scan(sb, tb_ref, cb, gb_ref, nblk, reverse=True)


def _ssm_states(uv_ext, ms, tab, layer, n_xt, ctx_rows):
    rows = uv_ext.shape[0]
    fwd = lambda b, s: (jnp.where(s == 0, n_xt, s - 1), b)
    bwd = lambda b, s: (jnp.where(s == 0, n_xt, n_xt - s), b)
    return pl.pallas_call(
        functools.partial(_ssm_state_kernel, ctx_rows=ctx_rows),
        grid=(NGB, n_xt + 1),
        in_specs=[
            pl.BlockSpec((SR, CW), fwd),
            pl.BlockSpec((SR, CW), bwd),
            pl.BlockSpec((None, None, None, CW, SW), lambda b, s: (layer, b, 0, 0, 0)),
            pl.BlockSpec((None, None, None, CW, SW), lambda b, s: (layer, b, 1, 0, 0)),
            pl.BlockSpec((None, None, None, 8 * SUBLANES, SW // 2),
                         lambda b, s: (layer, 0, b, 0, 0)),
            pl.BlockSpec((None, None, None, 8 * SUBLANES, SW // 2),
                         lambda b, s: (layer, 1, b, 0, 0)),
        ],
        out_specs=[pl.BlockSpec((SR, SW), fwd), pl.BlockSpec((SR, SW), bwd)],
        out_shape=[jax.ShapeDtypeStruct((rows, NGB * SW), BF16)] * 2,
        scratch_shapes=[pltpu.VMEM((SR, SW), F32), pltpu.VMEM((SR, SW), F32),
                        pltpu.VMEM((SUBLANES, SW), F32), pltpu.VMEM((SUBLANES, SW), F32)],
        compiler_params=_cparams(("arbitrary", "arbitrary")),
        name="ssm_states",
    )(uv_ext, uv_ext, ms, ms, tab, tab)


def _ssm_out_kernel(u_ref, hf_ref, gb_ref, m_ref, y_ref):
    res = jnp.dot(u_ref[...], m_ref[0:CW, :], preferred_element_type=F32)
    res += jnp.dot(hf_ref[...], m_ref[CW:CW + SW, :], preferred_element_type=F32)
    res += jnp.dot(gb_ref[...], m_ref[CW + SW:CW + 2 * SW, :], preferred_element_type=F32)
    for t in range(CH):
        y_ref[pl.ds(t, SR, stride=CH), :] = res[:, t * LANES:(t + 1) * LANES]


def _ssm_readout(uv_ext, hf, gb, mcat, layer):
    rows = uv_ext.shape[0]
    nt = rows // SR
    blk = lambda b, i: (i, b)
    return pl.pallas_call(
        _ssm_out_kernel,
        grid=(NGB, nt),
        in_specs=[
            pl.BlockSpec((SR, CW), blk),
            pl.BlockSpec((SR, SW), blk),
            pl.BlockSpec((SR, SW), blk),
            pl.BlockSpec((None, None, CW + 2 * SW, CW), lambda b, i: (layer, b, 0, 0)),
        ],
        out_specs=pl.BlockSpec((SR * CH, LANES), blk),
        out_shape=jax.ShapeDtypeStruct((rows * CH, SSM_W), F32),
        compiler_params=_cparams(("arbitrary", "arbitrary")),
        name="ssm_readout",
    )(uv_ext, hf, gb, mcat)


def _gelu_tanh(x):
    return 0.5 * x * (1.0 + jnp.tanh(0.7978845608028654 * (x + 0.044715 * x * x * x)))


def _route(s, b_col):
    sel = s + b_col
    neg = jnp.float32(-jnp.inf)

    def top2(vals):
        row = lax.broadcasted_iota(jnp.int32, vals.shape, 0).astype(F32)
        big = jnp.float32(vals.shape[0])
        m1 = jnp.max(vals, axis=0, keepdims=True)
        i1 = jnp.min(jnp.where(vals == m1, row, big), axis=0, keepdims=True)
        rest = jnp.where(row == i1, neg, vals)
        m2 = jnp.max(rest, axis=0, keepdims=True)
        i2 = jnp.min(jnp.where(rest == m2, row, big), axis=0, keepdims=True)
        return m1, i1, m2, i2

    best = None
    gidx = None
    for g in range(N_EXPERT_GROUPS):
        m1, _, m2, _ = top2(sel[g * EXPERTS_PER_GROUP:(g + 1) * EXPERTS_PER_GROUP])
        score = m1 + m2
        if best is None:
            best, gidx = score, jnp.zeros(score.shape, jnp.int32)
        else:
            upd = score > best
            best = jnp.where(upd, score, best)
            gidx = jnp.where(upd, g, gidx)
    row_i = lax.broadcasted_iota(jnp.int32, s.shape, 0)
    _, e1, _, e2 = top2(jnp.where(row_i // EXPERTS_PER_GROUP == gidx, sel, neg))
    row = row_i.astype(F32)
    w1 = jnp.sum(jnp.where(row == e1, s, 0.0), axis=0, keepdims=True)
    w2 = jnp.sum(jnp.where(row == e2, s, 0.0), axis=0, keepdims=True)
    tot = w1 + w2
    return e1.astype(jnp.int32), e2.astype(jnp.int32), w1 / tot, w2 / tot


def _mixout_kernel(x_ref, c_ref, ux_ref, ys_ref, mod_ref, band_ref, icnt_ref, wp_ref, sp_ref, d_ref,
                   wglu_ref, bglu_ref, wout_ref, gffn_ref, wrt_ref, br_ref, tri_ref,
                   xo_ref, hs_ref, pos_ref, gate_ref, cnt_ref, *, n_x_tiles):
    ux = ux_ref[...]
    parts = []
    for g in range(len(POOL_WINDOWS)):
        ug = ux[:, g * POOL_GW:(g + 1) * POOL_GW]
        hi = ug.astype(BF16)
        lo = (ug - hi.astype(F32)).astype(BF16)
        band = band_ref[g]
        wsum = (jnp.dot(band, hi, preferred_element_type=F32)
                + jnp.dot(band, lo, preferred_element_type=F32))
        p = wsum * icnt_ref[:, g * POOL_GW:(g + 1) * POOL_GW] - ug
        parts.append(jnp.dot(p.astype(BF16), wp_ref[g], preferred_element_type=F32))
    pool = jnp.concatenate(parts, axis=-1) * sp_ref[...]

    y = ys_ref[...] + d_ref[...] * ux[:, POOL_W:]
    y = _gelu_tanh(y)
    z = jnp.dot(y.astype(BF16), wglu_ref[...], preferred_element_type=F32) + bglu_ref[...]
    glu = y * jax.nn.sigmoid(z)

    cat = jnp.concatenate([pool, glu], axis=-1).astype(BF16)
    o = jnp.dot(cat, wout_ref[...], preferred_element_type=F32)
    xn = _token_tile(x_ref, c_ref, n_x_tiles) + mod_ref[2:3, :] * o
    xo_ref[...] = xn

    h2 = _rmsnorm_mod(xn, gffn_ref[...], mod_ref[3:4, :], mod_ref[4:5, :])
    h_hi = h2.astype(BF16)
    h_lo = (h2 - h_hi.astype(F32)).astype(BF16)
    wrt = wrt_ref[...]
    nt_dims = (((1,), (1,)), ((), ()))
    q_hi = lax.dot_general(wrt, h_hi, nt_dims, preferred_element_type=F32)
    q_lo = lax.dot_general(wrt[:N_EXPERTS], h_lo, nt_dims, preferred_element_type=F32)
    logits = q_hi[:N_EXPERTS] + (q_hi[N_EXPERTS:] + q_lo)
    e1, e2, g1, g2 = _route(jax.nn.sigmoid(logits), br_ref[...])
    gate_ref[...] = jnp.concatenate([g1, g2], axis=0)

    row_i = lax.broadcasted_iota(jnp.int32, (N_EXPERTS, TM), 0)
    oh0 = jnp.where(row_i == e1, 1.0, 0.0)
    oh1 = jnp.where(row_i == e2, 1.0, 0.0)
    tri = tri_ref[...]
    before0 = jnp.dot(oh0.astype(BF16), tri, preferred_element_type=F32)
    before1 = jnp.dot(oh1.astype(BF16), tri, preferred_element_type=F32)
    tot0 = jnp.sum(oh0, axis=1, keepdims=True)
    tot1 = jnp.sum(oh1, axis=1, keepdims=True)
    smaller = jnp.where(row_i > e1, 1.0, 0.0) + jnp.where(row_i > e2, 1.0, 0.0)
    off = jnp.sum(smaller, axis=1, keepdims=True)
    p0 = jnp.sum(oh0 * (off + before0), axis=0, keepdims=True)
    p1 = jnp.sum(oh1 * (off + tot0 + before1), axis=0, keepdims=True)
    pos_ref[...] = jnp.concatenate([p0, p1], axis=0).astype(jnp.int32)
    cnt_ref[...] = (tot0 + tot1).astype(jnp.int32)

    slot = lax.broadcasted_iota(jnp.int32, (2 * TM, TM), 0).astype(F32)
    perm = jnp.where(jnp.logical_or(slot == p0, slot == p1), 1.0, 0.0).astype(BF16)
    hs = jnp.dot(perm, h2.astype(BF16), preferred_element_type=F32)
    for q in range(ROW_TILE):
        hs_ref[pl.ds(q, 2 * TM, stride=ROW_TILE), :] = hs[:, q * LANES:(q + 1) * LANES]


def _mixout(xa, ca, ctx_block, ux, y_ssm, mod_l, band, icnt, wp_bf, sp, dvec, wglu_bf, bglu, wout_bf,
            gffn, w_router, b_router, tri, nt, n_x_tiles):
    t = nt * TM
    sel = lambda i: (jnp.where(i == n_x_tiles, 1, 0), 0, 0)
    sel4 = lambda i: (jnp.where(i == n_x_tiles, 1, 0), 0, 0, 0)
    row = lambda i: (i, 0)
    fix2 = lambda i: (0, 0)
    return pl.pallas_call(
        functools.partial(_mixout_kernel, n_x_tiles=n_x_tiles),
        grid=(nt,),
        in_specs=_token_specs(n_x_tiles, ctx_block) + [
            pl.BlockSpec((TM, D_MODEL), row),
            pl.BlockSpec((TM, SSM_W), row),
            pl.BlockSpec((None, N_MOD, D_MODEL), sel),
            pl.BlockSpec((None, len(POOL_WINDOWS), TM, TM), sel4),
            pl.BlockSpec((None, TM, POOL_W), sel),
            pl.BlockSpec((len(POOL_WINDOWS), POOL_GW, POOL_GW), lambda i: (0, 0, 0)),
            pl.BlockSpec((1, POOL_W), fix2),
            pl.BlockSpec((1, SSM_W), fix2),
            pl.BlockSpec((SSM_W, SSM_W), fix2),
            pl.BlockSpec((1, SSM_W), fix2),
            pl.BlockSpec((D_MODEL, D_MODEL), fix2),
            pl.BlockSpec((1, D_MODEL), fix2),
            pl.BlockSpec((2 * N_EXPERTS, D_MODEL), fix2),
            pl.BlockSpec((N_EXPERTS, 1), fix2),
            pl.BlockSpec((TM, TM), fix2),
        ],
        out_specs=[
            pl.BlockSpec((TM, D_MODEL), row),
            pl.BlockSpec((2 * TM * ROW_TILE, LANES), row),
            pl.BlockSpec((2, TM), lambda i: (0, i)),
            pl.BlockSpec((2, TM), lambda i: (0, i)),
            pl.BlockSpec((None, N_EXPERTS, 1), lambda i: (i, 0, 0)),
        ],
        out_shape=[
            jax.ShapeDtypeStruct((t, D_MODEL), F32),
            jax.ShapeDtypeStruct((2 * t * ROW_TILE, LANES), F32),
            jax.ShapeDtypeStruct((2, t), jnp.int32),
            jax.ShapeDtypeStruct((2, t), F32),
            jax.ShapeDtypeStruct((nt, N_EXPERTS, 1), jnp.int32),
        ],
        compiler_params=_cparams(("arbitrary",)),
        name="mixer_out_router",
    )(xa, ca, ux, y_ssm, mod_l, band, icnt, wp_bf, sp, dvec, wglu_bf, bglu, wout_bf, gffn,
      w_router, b_router, tri)


def _copy_rows(src, src_row, dst, dst_row, n, n_max, sem):
    done = 0
    for bit in (1 << b for b in range(n_max.bit_length() - 1, -1, -1)):
        take = n & bit

        @pl.when(take != 0)
        def _(done=done, bit=bit):
            s0 = pl.multiple_of((src_row + done) * ROW_TILE, ROW_TILE)
            d0 = pl.multiple_of((dst_row + done) * ROW_TILE, ROW_TILE)
            pltpu.make_async_copy(src.at[pl.ds(s0, bit * ROW_TILE)],
                                  dst.at[pl.ds(d0, bit * ROW_TILE)], sem).start()

        done = done + take


def _wait_rows(src, dst, n, sem):
    size = pl.multiple_of(n * ROW_TILE, ROW_TILE)
    pltpu.make_async_copy(src.at[pl.ds(0, size)], dst.at[pl.ds(0, size)], sem).wait()


def _expert_kernel(be_ref, nbu_ref, nv_ref, rs_ref, re_ref, rsrc_ref, rlen_ref, rdst_ref,
                   hs_hbm, wg_ref, wu_ref, wd_ref, y_ref, xg0, xg1, wg_s, wu_s, wd_s, gsem):
    s = pl.program_id(0)
    nbu = nbu_ref[0]
    xg = (xg0, xg1)

    def gather_start(blk, slot):
        lo_blk = blk * MOE_BLK

        def body(r, c):
            g0 = rdst_ref[r]
            lo = jnp.maximum(g0, lo_blk)
            hi = jnp.minimum(g0 + rlen_ref[r], lo_blk + MOE_BLK)
            _copy_rows(hs_hbm, rsrc_ref[r] + (lo - g0), xg[slot], lo - lo_blk,
                       jnp.maximum(hi - lo, 0), TM, gsem.at[slot])
            return c

        lax.fori_loop(rs_ref[blk], re_ref[blk], body, 0)

    changed = jnp.logical_or(s == 0, be_ref[s] != be_ref[jnp.maximum(s - 1, 0)])

    @pl.when(jnp.logical_and(changed, s < nbu))
    def _():
        wg_s[...] = wg_ref[...].astype(BF16)
        wu_s[...] = wu_ref[...].astype(BF16)
        wd_s[...] = wd_ref[...].astype(BF16)

    def step(slot):
        @pl.when(s == 0)
        def _():
            xg0[...] = jnp.zeros_like(xg0)
            xg1[...] = jnp.zeros_like(xg1)
            gather_start(0, slot)

        @pl.when(s + 1 < nbu)
        def _():
            gather_start(s + 1, 1 - slot)

        _wait_rows(hs_hbm, xg[slot], nv_ref[s], gsem.at[slot])
        xb = jnp.concatenate([xg[slot][pl.ds(q, MOE_BLK, stride=ROW_TILE), :]
                              for q in range(ROW_TILE)], axis=-1).astype(BF16)
        g = jnp.dot(xb, wg_s[...], preferred_element_type=F32)
        u = jnp.dot(xb, wu_s[...], preferred_element_type=F32)
        hid = (g * jax.nn.sigmoid(g)) * u
        out = jnp.dot(hid.astype(BF16), wd_s[...], preferred_element_type=F32)
        for q in range(ROW_TILE):
            y_ref[pl.ds(q, MOE_BLK, stride=ROW_TILE), :] = out[:, q * LANES:(q + 1) * LANES]

    for slot in range(2):
        @pl.when(jnp.logical_and(s < nbu, s % 2 == slot))
        def _(slot=slot):
            step(slot)

    @pl.when(s >= nbu)
    def _():
        y_ref[...] = jnp.zeros_like(y_ref)


def _experts(blk_e, nb_used, blk_valid, rs, re, rsrc, rlen, rdst, hs, w_gate, w_up, w_down, layer):
    nb = blk_e.shape[0]
    wmap = lambda i, be, *_: (layer, be[i], 0, 0)
    buf = pltpu.VMEM((MOE_BLK * ROW_TILE, LANES), F32)
    return pl.pallas_call(
        _expert_kernel,
        grid_spec=pltpu.PrefetchScalarGridSpec(
            num_scalar_prefetch=8,
            grid=(nb,),
            in_specs=[
                pl.BlockSpec(memory_space=pl.ANY),
                pl.BlockSpec((None, None, D_MODEL, D_EXPERT), wmap),
                pl.BlockSpec((None, None, D_MODEL, D_EXPERT), wmap),
                pl.BlockSpec((None, None, D_EXPERT, D_MODEL), wmap),
            ],
            out_specs=pl.BlockSpec((MOE_BLK * ROW_TILE, LANES), lambda i, *_: (i, 0)),
            scratch_shapes=[buf, buf,
                            pltpu.VMEM((D_MODEL, D_EXPERT), BF16),
                            pltpu.VMEM((D_MODEL, D_EXPERT), BF16),
                            pltpu.VMEM((D_EXPERT, D_MODEL), BF16),
                            pltpu.SemaphoreType.DMA((2,))],
        ),
        out_shape=jax.ShapeDtypeStruct((nb * MOE_BLK * ROW_TILE, LANES), F32),
        compiler_params=_cparams(("arbitrary",)),
        name="moe_experts",
    )(blk_e, nb_used, blk_valid, rs, re, rsrc, rlen, rdst, hs, w_gate, w_up, w_down)


def _combine_kernel(meta_ref, x_ref, pos_ref, gate_ref, mod_ref, gfin_ref, ys_hbm, out_ref,
                    st0, st1, sem, *, final, nt):
    i = pl.program_id(0)
    stage = (st0, st1)

    def fetch(tile, slot):
        base = tile * LANES

        def runs(n_max):
            for e in range(N_EXPERTS):
                _copy_rows(ys_hbm, meta_ref[base + e], stage[slot],
                           meta_ref[base + 2 * N_EXPERTS + e], meta_ref[base + N_EXPERTS + e],
                           n_max, sem.at[slot])

        longest = meta_ref[base + 3 * N_EXPERTS]

        @pl.when(longest <= SHORT_RUN)
        def _():
            runs(SHORT_RUN)

        @pl.when(longest > SHORT_RUN)
        def _():
            runs(TM)

    def step(slot):
        @pl.when(i == 0)
        def _():
            fetch(0, slot)

        @pl.when(i + 1 < nt)
        def _():
            fetch(i + 1, 1 - slot)

        _wait_rows(ys_hbm, stage[slot], 2 * TM, sem.at[slot])
        rows = jnp.concatenate([stage[slot][pl.ds(q, 2 * TM, stride=ROW_TILE), :]
                                for q in range(ROW_TILE)], axis=-1)
        pos = pos_ref[...].astype(F32)
        gate = gate_ref[...]
        slot_id = lax.broadcasted_iota(jnp.int32, (2 * TM, TM), 0).astype(F32)
        gmat_t = (jnp.where(slot_id == pos[0:1, :], gate[0:1, :], 0.0)
                  + jnp.where(slot_id == pos[1:2, :], gate[1:2, :], 0.0))
        g_hi = gmat_t.astype(BF16)
        g_lo = (gmat_t - g_hi.astype(F32)).astype(BF16)
        r_hi = rows.astype(BF16)
        r_lo = (rows - r_hi.astype(F32)).astype(BF16)
        tn = (((0,), (0,)), ((), ()))
        y = (lax.dot_general(g_hi, r_hi, tn, preferred_element_type=F32)
             + (lax.dot_general(g_lo, r_hi, tn, preferred_element_type=F32)
                + lax.dot_general(g_hi, r_lo, tn, preferred_element_type=F32)))
        xn = x_ref[...] + mod_ref[5:6, :] * y
        if final:
            ms = jnp.mean(xn * xn, axis=-1, keepdims=True)
            xn = xn * lax.rsqrt(ms + EPS) * gfin_ref[...]
        out_ref[...] = xn

    for slot in range(2):
        @pl.when(i % 2 == slot)
        def _(slot=slot):
            step(slot)


def _combine(meta, xn, pos, gate, mod_l, g_final, ys, nt, n_x_tiles, final):
    sel = lambda i, m: (jnp.where(i == n_x_tiles, 1, 0), 0, 0)
    row = lambda i, m: (i, 0)
    stage = pltpu.VMEM((2 * TM * ROW_TILE, LANES), F32)
    return pl.pallas_call(
        functools.partial(_combine_kernel, final=final, nt=nt),
        grid_spec=pltpu.PrefetchScalarGridSpec(
            num_scalar_prefetch=1,
            grid=(nt,),
            in_specs=[
                pl.BlockSpec((TM, D_MODEL), row),
                pl.BlockSpec((2, TM), lambda i, m: (0, i)),
                pl.BlockSpec((2, TM), lambda i, m: (0, i)),
                pl.BlockSpec((None, N_MOD, D_MODEL), sel),
                pl.BlockSpec((1, D_MODEL), lambda i, m: (0, 0)),
                pl.BlockSpec(memory_space=pl.ANY),
            ],
            out_specs=pl.BlockSpec((TM, D_MODEL), row),
            scratch_shapes=[stage, stage, pltpu.SemaphoreType.DMA((2,))],
        ),
        out_shape=jax.ShapeDtypeStruct((nt * TM, D_MODEL), F32),
        compiler_params=_cparams(("arbitrary",)),
        name="moe_combine",
    )(meta, xn, pos, gate, mod_l, g_final, ys)


def _moe(hs, pos, tcnt, gate, xn, mod_l, g_final, w_gate, w_up, w_down, layer, nt, n_x_tiles,
         final):
    t = nt * TM
    i32 = jnp.int32
    n = tcnt.reshape(nt, N_EXPERTS)
    off = jnp.cumsum(n, axis=1) - n
    base = jnp.cumsum(n, axis=0) - n
    counts = jnp.sum(n, axis=0)
    padded = (counts + MOE_BLK - 1) // MOE_BLK * MOE_BLK
    pend = jnp.cumsum(padded)
    pstart = pend - padded
    nb = (2 * t + N_EXPERTS * (MOE_BLK - 1)) // MOE_BLK + 1
    nb_used = (pend[-1] // MOE_BLK).astype(i32)
    blk_lo = jnp.arange(nb, dtype=i32) * MOE_BLK
    blk_e = jnp.minimum(jnp.sum(pend[None, :] <= jnp.minimum(blk_lo, pend[-1] - MOE_BLK)[:, None],
                                axis=1), N_EXPERTS - 1).astype(i32)
    blk_valid = jnp.clip((pstart + counts)[blk_e] - blk_lo, 0, MOE_BLK).astype(i32)
    gsrc = (pstart[None, :] + base).astype(i32)
    rdst = gsrc.T.reshape(-1)
    rlen = n.T.reshape(-1).astype(i32)
    rsrc = (jnp.arange(nt, dtype=i32)[:, None] * (2 * TM) + off).T.reshape(-1).astype(i32)
    e_dst = gsrc.T[blk_e]
    e_end = e_dst + n.T[blk_e]
    rs = (blk_e * nt + jnp.sum(e_end <= blk_lo[:, None], axis=1)).astype(i32)
    re = (blk_e * nt + jnp.sum(e_dst < (blk_lo + MOE_BLK)[:, None], axis=1)).astype(i32)
    ys = _experts(blk_e, nb_used.reshape(1), blk_valid, rs, re, rsrc, rlen, rdst, hs,
                  w_gate, w_up, w_down, layer)
    longest = jnp.max(n, axis=1, keepdims=True).astype(i32)
    meta = jnp.concatenate([gsrc, n.astype(i32), off.astype(i32), longest,
                            jnp.zeros((nt, LANES - 3 * N_EXPERTS - 1), i32)], axis=1).reshape(-1)
    return _combine(meta, xn, pos, gate, mod_l, g_final, ys, nt, n_x_tiles, final)


def _cmul(ar, ai, br, bi):
    return ar * br - ai * bi, ar * bi + ai * br


def _expand_rows(v, rep_ref, mask_ref):
    return jnp.dot(v.astype(BF16), rep_ref[...], preferred_element_type=F32) * mask_ref[...]


def _expand_cols(e, rept_ref, maskt_ref):
    out = lax.dot_general(rept_ref[...], e.astype(BF16), (((1,), (1,)), ((), ())),
                          preferred_element_type=F32)
    return out * maskt_ref[...]


def _ssm_prep_kernel(are_ref, aim_ref, ldt_ref, btr_ref, bti_ref, cr_ref, ci_ref,
                     rep_ref, mask_ref, rept_ref, maskt_ref, ms_ref, mcat_ref, ecat):
    lag_blocks = []
    for d in range(2):
        a_re = jnp.minimum(are_ref[d], -1e-4)
        a_im = aim_ref[d]
        dt = jnp.exp(ldt_ref[d])
        mag = jnp.exp(a_re * dt)
        lr = mag * jnp.cos(a_im * dt)
        li = mag * jnp.sin(a_im * dt)
        den = a_re * a_re + a_im * a_im
        k_re = ((lr - 1.0) * a_re + li * a_im) / den
        k_im = (li * a_re - (lr - 1.0) * a_im) / den
        bbr, bbi = _cmul(k_re, k_im, btr_ref[d], bti_ref[d])
        cr, ci = cr_ref[d], ci_ref[d]
        pr, pi = jnp.ones_like(lr), jnp.zeros_like(lr)
        v0 = None
        for n in range(CH + 1):
            er, ei = _cmul(pr, pi, cr, ci)
            et = jnp.concatenate([_expand_cols(er, rept_ref, maskt_ref),
                                  _expand_cols(-ei, rept_ref, maskt_ref)], axis=0).astype(BF16)
            if n < CH:
                vr, vi = _cmul(pr, pi, bbr, bbi)
                v = jnp.concatenate([_expand_rows(vr, rep_ref, mask_ref),
                                     _expand_rows(vi, rep_ref, mask_ref)], axis=1).astype(BF16)
                if n == 0:
                    v0 = v
                sigma = CH - 1 - n if d == 0 else n
                ms_ref[d, sigma * LANES:(sigma + 1) * LANES, :] = v
                ecat[:, n * LANES:(n + 1) * LANES] = et
            if n >= 1:
                tau = n - 1 if d == 0 else CH - n
                mcat_ref[CW + d * SW:CW + (d + 1) * SW, tau * LANES:(tau + 1) * LANES] = et
            pr, pi = _cmul(pr, pi, lr, li)
        lag_blocks.append(jnp.dot(v0, ecat[...], preferred_element_type=F32))
    kf, kb = lag_blocks
    for s in range(CH):
        for t in range(CH):
            if t > s:
                blk = kf[:, (t - s) * LANES:(t - s + 1) * LANES]
            elif t < s:
                blk = kb[:, (s - t) * LANES:(s - t + 1) * LANES]
            else:
                blk = kf[:, :LANES] + kb[:, :LANES]
            mcat_ref[s * LANES:(s + 1) * LANES, t * LANES:(t + 1) * LANES] = blk.astype(BF16)


def _ssm_operators(a_re, a_im, log_dt, b_re, b_im, c_re, c_im):
    depth = a_re.shape[0]
    rows = SSM_GROUPS * SSM_GROUP
    p = SSM_STATE
    rep_rows = lambda v: jnp.repeat(v, SSM_GROUP, axis=2)
    are = rep_rows(a_re)
    aim = rep_rows(a_im)
    ldt = jnp.broadcast_to(rep_rows(log_dt[..., None]), are.shape)
    btr = jnp.swapaxes(b_re, -1, -2).reshape(depth, 2, rows, p)
    bti = jnp.swapaxes(b_im, -1, -2).reshape(depth, 2, rows, p)
    cr = c_re.reshape(depth, 2, rows, p)
    ci = c_im.reshape(depth, 2, rows, p)
    rep = np.tile(np.eye(p, dtype=np.float32), (1, GB))
    mask = np.kron(np.eye(GB, dtype=np.float32), np.ones((SSM_GROUP, p), np.float32))
    par = pl.BlockSpec((None, 2, LANES, p), lambda l, b: (l, 0, b, 0))
    fix = lambda shape: pl.BlockSpec(shape, lambda l, b: (0, 0))
    return pl.pallas_call(
        _ssm_prep_kernel,
        grid=(depth, NGB),
        in_specs=[par] * 7 + [fix((p, GB * p)), fix((LANES, GB * p)),
                              fix((GB * p, p)), fix((GB * p, LANES))],
        out_specs=[
            pl.BlockSpec((None, None, 2, CW, SW), lambda l, b: (l, b, 0, 0, 0)),
            pl.BlockSpec((None, None, CW + 2 * SW, CW), lambda l, b: (l, b, 0, 0)),
        ],
        out_shape=[
            jax.ShapeDtypeStruct((depth, NGB, 2, CW, SW), BF16),
            jax.ShapeDtypeStruct((depth, NGB, CW + 2 * SW, CW), BF16),
        ],
        scratch_shapes=[pltpu.VMEM((SW, CW), BF16)],
        compiler_params=_cparams(("arbitrary", "arbitrary")),
        name="ssm_operators",
    )(are, aim, ldt, btr, bti, cr, ci, jnp.asarray(rep, BF16), jnp.asarray(mask),
      jnp.asarray(rep.T, BF16), jnp.asarray(mask.T))


def _scan_tables(a_re, a_im, log_dt):
    depth = a_re.shape[0]
    a_re = jnp.minimum(a_re, -1e-4)
    dt = jnp.exp(log_dt)[..., None]
    mag = jnp.exp(a_re * dt)
    lr = mag * jnp.cos(a_im * dt)
    li = mag * jnp.sin(a_im * dt)
    ar, ai = lr, li
    for _ in range(CH - 1):
        ar, ai = _cmul(ar, ai, lr, li)
    apw = [(ar, ai)]
    for _ in range(SUBLANES - 1):
        apw.append(_cmul(apw[-1][0], apw[-1][1], ar, ai))
    apr = jnp.stack([q[0] for q in apw], axis=1)
    api = jnp.stack([q[1] for q in apw], axis=1)
    rows = np.arange(SUBLANES)
    tabs = []
    for d in range(2):
        consts = []
        for k in (1, 2, 4):
            keep = (rows >= k) if d == 0 else (rows < SUBLANES - k)
            keep = jnp.asarray(keep.astype(np.float32))[None, :, None, None]
            consts.append(keep * apr[:, k - 1:k, d])
            consts.append(keep * api[:, k - 1:k, d])
        order = rows if d == 0 else SUBLANES - 1 - rows
        consts.append(apr[:, order, d])
        consts.append(api[:, order, d])
        tabs.append(jnp.stack(consts, axis=1))
    tab = jnp.stack(tabs, axis=1).reshape(depth, 2, 8 * SUBLANES, NGB, GB * SSM_STATE)
    return tab.transpose(0, 1, 3, 2, 4)


def _pool_constants(seg):
    t = np.arange(TM)
    s0 = t // seg * seg
    band = np.zeros((len(POOL_WINDOWS), TM, TM), np.float32)
    icnt = np.zeros((TM, POOL_W), np.float32)
    for g, w in enumerate(POOL_WINDOWS):
        lo = np.maximum(t - w // 2, s0)
        hi = np.minimum(t + w // 2, s0 + seg)
        band[g] = (t[None, :] >= lo[:, None]) & (t[None, :] < hi[:, None])
        icnt[:, g * POOL_GW:(g + 1) * POOL_GW] = (1.0 / (hi - lo).astype(np.float32))[:, None]
    return band, icnt


def kernel(x, c, ctx, c_ctx, w_mod, b_mod, g_mix, g_ffn, w_in, w_out, w_pool, s_pool,
           ssm_a_re, ssm_a_im, ssm_log_dt, ssm_b_re, ssm_b_im, ssm_c_re, ssm_c_im, ssm_d,
           w_glu, b_glu, w_router, b_router, w_gate, w_up, w_down, g_final):
    bsz, seq, d = x.shape
    ctx_len = ctx.shape[1]
    depth = w_mod.shape[0]
    assert bsz == 1 and d == D_MODEL and ctx_len == TM
    assert seq % (SR * CH) == 0 and seq % GRID_W == 0 and seq % TI == 0
    n_xt = seq // TM
    n_st = seq // (SR * CH)
    ctx_rows = ctx_len // CH

    cvec = jnp.concatenate([c.reshape(1, d), c_ctx.reshape(1, d),
                            jnp.zeros((SUBLANES - 2, d), F32)], axis=0)
    mod = _modulation(cvec, w_mod, b_mod).reshape(depth, SUBLANES, N_MOD, d)

    band_x, icnt_x = _pool_constants(GRID_W)
    band_c, icnt_c = _pool_constants(ctx_len)
    band = jnp.asarray(np.stack([band_x, band_c]), BF16)
    icnt = jnp.asarray(np.stack([icnt_x, icnt_c]), F32)
    tri = jnp.asarray(np.triu(np.ones((TM, TM), np.float32), 1), BF16)
    wr_hi = w_router.astype(BF16)
    wr_split = jnp.concatenate([wr_hi, (w_router - wr_hi.astype(F32)).astype(BF16)], axis=1).T

    ms, mcat = _ssm_operators(ssm_a_re, ssm_a_im, ssm_log_dt, ssm_b_re, ssm_b_im,
                              ssm_c_re, ssm_c_im)
    tab = _scan_tables(ssm_a_re, ssm_a_im, ssm_log_dt)

    xa, ca, ctx_block = x[0], ctx[0], 0
    for l in range(depth):
        last = l == depth - 1
        mod_l = mod[l, :2]
        ux, uv_ext = _inproj(xa, ca, ctx_block, mod_l, g_mix[l].reshape(1, d),
                             w_in[l].astype(BF16), n_xt)
        hf, gb = _ssm_states(uv_ext, ms, tab, l, n_st, ctx_rows)
        y_ssm = _ssm_readout(uv_ext, hf, gb, mcat, l)
        nt = n_xt if last else n_xt + 1
        xn, hs, pos, gate, tcnt = _mixout(
            xa, ca, ctx_block, ux, y_ssm, mod_l, band, icnt, w_pool[l].astype(BF16),
            s_pool[l].reshape(1, -1), ssm_d[l].reshape(1, -1), w_glu[l].astype(BF16),
            b_glu[l].reshape(1, -1), w_out[l].astype(BF16), g_ffn[l].reshape(1, d), wr_split,
            b_router.reshape(-1, 1), tri, nt, n_xt)
        tok = _moe(hs, pos, tcnt, gate, xn, mod_l, g_final.reshape(1, d), w_gate, w_up, w_down,
                   l, nt, n_xt, last)
        xa, ca, ctx_block = tok, tok, n_xt
    return tok.reshape(bsz, seq, d)
```

```python
import functools

import numpy as np
import jax
import jax.numpy as jnp
from jax import lax
from jax.experimental import pallas as pl
from jax.experimental.pallas import tpu as pltpu

F32 = jnp.float32
BF16 = jnp.bfloat16

D_MODEL = 1024
POOL_W = 512
SSM_W = 512
POOL_WINDOWS = (2, 4, 8, 16)
POOL_GW = 128
SSM_GROUP = 16
SSM_GROUPS = 32
SSM_STATE = 64
N_EXPERTS = 32
N_EXPERT_GROUPS = 4
EXPERTS_PER_GROUP = 8
D_EXPERT = 512
GRID_W = 64
EPS = 1e-6
N_MOD = 6

LANES = 128
SUBLANES = 8
TM = 256
TI = 512
CH = 8
GB = 8
NGB = SSM_GROUPS // GB
CW = CH * LANES
SW = 2 * GB * SSM_STATE
SR = 256
MOE_BLK = 512
ROW_TILE = D_MODEL // LANES
SHORT_RUN = 31
VMEM_LIMIT = 48 * 1024 * 1024


def _cparams(sem):
    return pltpu.CompilerParams(dimension_semantics=sem, vmem_limit_bytes=VMEM_LIMIT)


def _rmsnorm_mod(x, g, shift, scale):
    ms = jnp.mean(x * x, axis=-1, keepdims=True)
    y = x * lax.rsqrt(ms + EPS) * g
    return y * (1.0 + scale) + shift


def _mod_kernel(c_ref, w_ref, b_ref, o_ref):
    c = c_ref[...]
    a = c * jax.nn.sigmoid(c)
    o_ref[...] = jnp.dot(a, w_ref[...], preferred_element_type=F32,
                         precision=lax.Precision.HIGHEST) + b_ref[...]


def _modulation(cvec, w_mod, b_mod):
    depth, d, n = w_mod.shape
    tn = 1536
    return pl.pallas_call(
        _mod_kernel,
        grid=(depth, n // tn),
        in_specs=[
            pl.BlockSpec((SUBLANES, d), lambda l, j: (0, 0)),
            pl.BlockSpec((None, d, tn), lambda l, j: (l, 0, j)),
            pl.BlockSpec((None, 1, tn), lambda l, j: (l, 0, j)),
        ],
        out_specs=pl.BlockSpec((None, SUBLANES, tn), lambda l, j: (l, 0, j)),
        out_shape=jax.ShapeDtypeStruct((depth, SUBLANES, n), F32),
        compiler_params=_cparams(("arbitrary", "arbitrary")),
        name="modulation",
    )(cvec, w_mod, b_mod.reshape(depth, 1, n))


def _token_tile(x_ref, c_ref, n_x_tiles):
    return jnp.where(pl.program_id(0) == n_x_tiles, c_ref[...], x_ref[...])


def _inproj_kernel(x_ref, c_ref, mod_ref, g_ref, w_ref, ux_ref, uv_ref, scr, *, n_big):
    i = pl.program_id(0)

    def project(x, rows):
        h = _rmsnorm_mod(x, g_ref[...], mod_ref[0:1, :], mod_ref[1:2, :])
        ux = jnp.dot(h.astype(BF16), w_ref[...], preferred_element_type=F32)
        ux_ref[0:rows, :] = ux
        crow = rows // CH
        for b in range(NGB):
            scr[b, 0:rows, :] = ux[:, POOL_W + LANES * b:POOL_W + LANES * (b + 1)]
            for t in range(CH):
                piece = scr[b, pl.ds(t, crow, stride=CH), :]
                uv_ref[0:crow, pl.ds((b * CH + t) * LANES, LANES)] = piece.astype(BF16)

    @pl.when(i < n_big)
    def _():
        project(x_ref[...], TI)

    @pl.when(i == n_big)
    def _():
        project(c_ref[...], TM)
        ux_ref[TM:TI, :] = jnp.zeros((TI - TM, D_MODEL), F32)
        uv_ref[TM // CH:TI // CH, :] = jnp.zeros(((TI - TM) // CH, NGB * CW), BF16)

    @pl.when(i > n_big)
    def _():
        uv_ref[...] = jnp.zeros_like(uv_ref)


def _token_specs(n_x_tiles, ctx_block):
    return [pl.BlockSpec((TM, D_MODEL), lambda i: (jnp.minimum(i, n_x_tiles - 1), 0)),
            pl.BlockSpec((TM, D_MODEL), lambda i: (ctx_block, 0))]


def _inproj(xa, ca, ctx_block, mod_l, g, w_bf, n_x_tiles):
    n_big = n_x_tiles * TM // TI
    n_steps = n_big + SR // (TI // CH)
    sel = lambda i: (jnp.where(i >= n_big, 1, 0), 0, 0)
    return pl.pallas_call(
        functools.partial(_inproj_kernel, n_big=n_big),
        grid=(n_steps,),
        in_specs=[
            pl.BlockSpec((TI, D_MODEL), lambda i: (jnp.minimum(i, n_big - 1), 0)),
            pl.BlockSpec((TM, D_MODEL), lambda i: (ctx_block, 0)),
            pl.BlockSpec((None, N_MOD, D_MODEL), sel),
            pl.BlockSpec((1, D_MODEL), lambda i: (0, 0)),
            pl.BlockSpec((D_MODEL, D_MODEL), lambda i: (0, 0)),
        ],
        out_specs=[
            pl.BlockSpec((TI, D_MODEL), lambda i: (jnp.minimum(i, n_big), 0)),
            pl.BlockSpec((TI // CH, NGB * CW), lambda i: (i, 0)),
        ],
        out_shape=[
            jax.ShapeDtypeStruct(((n_x_tiles + 1) * TM, D_MODEL), F32),
            jax.ShapeDtypeStruct((n_steps * TI // CH, NGB * CW), BF16),
        ],
        scratch_shapes=[pltpu.VMEM((NGB, TI, LANES), F32)],
        compiler_params=_cparams(("arbitrary",)),
        name="mixer_inproj",
    )(xa, ca, mod_l, g, w_bf)


def _chunk_scan(s_ref, tab_ref, carry_ref, out_ref, nblk, reverse):
    half = SW // 2
    ncol = half // LANES
    row = lax.broadcasted_iota(jnp.int32, (SUBLANES, LANES), 0)
    edge = (row == SUBLANES - 1) if reverse else (row == 0)
    last = 0 if reverse else SUBLANES - 1

    def sub_block(r0, carry):
        outs_r, outs_i, new_carry = [], [], []
        for j in range(ncol):
            cr, ci = carry[2 * j], carry[2 * j + 1]
            lre = pl.ds(LANES * j, LANES)
            lim = pl.ds(half + LANES * j, LANES)
            zr = s_ref[pl.ds(r0, SUBLANES), lre]
            zi = s_ref[pl.ds(r0, SUBLANES), lim]
            for q, k in enumerate((1, 2, 4)):
                ar = tab_ref[pl.ds(16 * q, SUBLANES), lre]
                ai = tab_ref[pl.ds(16 * q + 8, SUBLANES), lre]
                sh = SUBLANES - k if reverse else k
                sr = pltpu.roll(zr, sh, axis=0)
                si = pltpu.roll(zi, sh, axis=0)
                zr, zi = zr + ar * sr - ai * si, zi + ar * si + ai * sr
            pr = tab_ref[pl.ds(48, SUBLANES), lre]
            pi = tab_ref[pl.ds(56, SUBLANES), lre]
            zr, zi = zr + pr * cr - pi * ci, zi + pr * ci + pi * cr
            sh1 = SUBLANES - 1 if reverse else 1
            outs_r.append(jnp.where(edge, cr, pltpu.roll(zr, sh1, axis=0)))
            outs_i.append(jnp.where(edge, ci, pltpu.roll(zi, sh1, axis=0)))
            new_carry.append(jnp.broadcast_to(zr[last:last + 1, :], (SUBLANES, LANES)))
            new_carry.append(jnp.broadcast_to(zi[last:last + 1, :], (SUBLANES, LANES)))
        return outs_r, outs_i, tuple(new_carry)

    def body(it, carry):
        bi = (nblk - 1 - it) if reverse else it
        r0 = pl.multiple_of(bi * 2 * SUBLANES, 2 * SUBLANES)
        if reverse:
            hi_r, hi_i, carry = sub_block(r0 + SUBLANES, carry)
            lo_r, lo_i, carry = sub_block(r0, carry)
        else:
            lo_r, lo_i, carry = sub_block(r0, carry)
            hi_r, hi_i, carry = sub_block(r0 + SUBLANES, carry)
        for j in range(ncol):
            out_ref[pl.ds(r0, 2 * SUBLANES), pl.ds(LANES * j, LANES)] = (
                jnp.concatenate([lo_r[j], hi_r[j]], axis=0).astype(BF16))
            out_ref[pl.ds(r0, 2 * SUBLANES), pl.ds(half + LANES * j, LANES)] = (
                jnp.concatenate([lo_i[j], hi_i[j]], axis=0).astype(BF16))
        return carry

    init = tuple(carry_ref[:, pl.ds(LANES * c, LANES)] for c in range(2 * ncol))
    final = lax.fori_loop(0, nblk, body, init)
    for c in range(2 * ncol):
        carry_ref[:, pl.ds(LANES * c, LANES)] = final[c]


def _ssm_state_kernel(uf_ref, ub_ref, msf_ref, msb_ref, tf_ref, tb_ref, hf_ref, gb_ref,
                      sf, sb, cf, cb, *, ctx_rows):
    step = pl.program_id(1)

    @pl.when(step == 0)
    def _():
        cf[...] = jnp.zeros_like(cf)
        cb[...] = jnp.zeros_like(cb)
        hf_ref[...] = jnp.zeros_like(hf_ref)
        gb_ref[...] = jnp.zeros_like(gb_ref)

    sf[...] = jnp.dot(uf_ref[...], msf_ref[...], preferred_element_type=F32)
    sb[...] = jnp.dot(ub_ref[...], msb_ref[...], preferred_element_type=F32)
    nblk = jnp.where(step == 0, ctx_rows // (2 * SUBLANES), SR // (2 * SUBLANES))
    _chunk_scan(sf, tf_ref, cf, hf_ref, nblk, reverse=False)
    _chunk_scan(sb, tb_ref, cb, gb_ref, nblk, reverse=True)


def _ssm_states(uv_ext, ms, tab, layer, n_xt, ctx_rows):
    rows = uv_ext.shape[0]
    fwd = lambda b, s: (jnp.where(s == 0, n_xt, s - 1), b)
    bwd = lambda b, s: (jnp.where(s == 0, n_xt, n_xt - s), b)
    return pl.pallas_call(
        functools.partial(_ssm_state_kernel, ctx_rows=ctx_rows),
        grid=(NGB, n_xt + 1),
        in_specs=[
            pl.BlockSpec((SR, CW), fwd),
            pl.BlockSpec((SR, CW), bwd),
            pl.BlockSpec((None, None, None, CW, SW), lambda b, s: (layer, b, 0, 0, 0)),
            pl.BlockSpec((None, None, None, CW, SW), lambda b, s: (layer, b, 1, 0, 0)),
            pl.BlockSpec((None, None, None, 8 * SUBLANES, SW // 2),
                         lambda b, s: (layer, 0, b, 0, 0)),
            pl.BlockSpec((None, None, None, 8 * SUBLANES, SW // 2),
                         lambda b, s: (layer, 1, b, 0, 0)),
        ],
        out_specs=[pl.BlockSpec((SR, SW), fwd), pl.BlockSpec((SR, SW), bwd)],
        out_shape=[jax.ShapeDtypeStruct((rows, NGB * SW), BF16)] * 2,
        scratch_shapes=[pltpu.VMEM((SR, SW), F32), pltpu.VMEM((SR, SW), F32),
                        pltpu.VMEM((SUBLANES, SW), F32), pltpu.VMEM((SUBLANES, SW), F32)],
        compiler_params=_cparams(("arbitrary", "arbitrary")),
        name="ssm_states",
    )(uv_ext, uv_ext, ms, ms, tab, tab)


def _ssm_out_kernel(u_ref, hf_ref, gb_ref, m_ref, y_ref):
    res = jnp.dot(u_ref[...], m_ref[0:CW, :], preferred_element_type=F32)
    res += jnp.dot(hf_ref[...], m_ref[CW:CW + SW, :], preferred_element_type=F32)
    res += jnp.dot(gb_ref[...], m_ref[CW + SW:CW + 2 * SW, :], preferred_element_type=F32)
    for t in range(CH):
        y_ref[pl.ds(t, SR, stride=CH), :] = res[:, t * LANES:(t + 1) * LANES]


def _ssm_readout(uv_ext, hf, gb, mcat, layer):
    rows = uv_ext.shape[0]
    nt = rows // SR
    blk = lambda b, i: (i, b)
    return pl.pallas_call(
        _ssm_out_kernel,
        grid=(NGB, nt),
        in_specs=[
            pl.BlockSpec((SR, CW), blk),
            pl.BlockSpec((SR, SW), blk),
            pl.BlockSpec((SR, SW), blk),
            pl.BlockSpec((None, None, CW + 2 * SW, CW), lambda b, i: (layer, b, 0, 0)),
        ],
        out_specs=pl.BlockSpec((SR * CH, LANES), blk),
        out_shape=jax.ShapeDtypeStruct((rows * CH, SSM_W), F32),
        compiler_params=_cparams(("arbitrary", "arbitrary")),
        name="ssm_readout",
    )(uv_ext, hf, gb, mcat)


def _gelu_tanh(x):
    return 0.5 * x * (1.0 + jnp.tanh(0.7978845608028654 * (x + 0.044715 * x * x * x)))


def _route(s, b_col):
    sel = s + b_col
    neg = jnp.float32(-jnp.inf)

    def top2(vals):
        row = lax.broadcasted_iota(jnp.int32, vals.shape, 0).astype(F32)
        big = jnp.float32(vals.shape[0])
        m1 = jnp.max(vals, axis=0, keepdims=True)
        i1 = jnp.min(jnp.where(vals == m1, row, big), axis=0, keepdims=True)
        rest = jnp.where(row == i1, neg, vals)
        m2 = jnp.max(rest, axis=0, keepdims=True)
        i2 = jnp.min(jnp.where(rest == m2, row, big), axis=0, keepdims=True)
        return m1, i1, m2, i2

    best = None
    gidx = None
    for g in range(N_EXPERT_GROUPS):
        m1, _, m2, _ = top2(sel[g * EXPERTS_PER_GROUP:(g + 1) * EXPERTS_PER_GROUP])
        score = m1 + m2
        if best is None:
            best, gidx = score, jnp.zeros(score.shape, jnp.int32)
        else:
            upd = score > best
            best = jnp.where(upd, score, best)
            gidx = jnp.where(upd, g, gidx)
    row_i = lax.broadcasted_iota(jnp.int32, s.shape, 0)
    _, e1, _, e2 = top2(jnp.where(row_i // EXPERTS_PER_GROUP == gidx, sel, neg))
    row = row_i.astype(F32)
    w1 = jnp.sum(jnp.where(row == e1, s, 0.0), axis=0, keepdims=True)
    w2 = jnp.sum(jnp.where(row == e2, s, 0.0), axis=0, keepdims=True)
    tot = w1 + w2
    return e1.astype(jnp.int32), e2.astype(jnp.int32), w1 / tot, w2 / tot


def _mixout_kernel(x_ref, c_ref, ux_ref, ys_ref, mod_ref, band_ref, icnt_ref, wp_ref, sp_ref, d_ref,
                   wglu_ref, bglu_ref, wout_ref, gffn_ref, wrt_ref, br_ref, tri_ref,
                   xo_ref, hs_ref, pos_ref, gate_ref, cnt_ref, *, n_x_tiles):
    ux = ux_ref[...]
    parts = []
    for g in range(len(POOL_WINDOWS)):
        ug = ux[:, g * POOL_GW:(g + 1) * POOL_GW]
        hi = ug.astype(BF16)
        lo = (ug - hi.astype(F32)).astype(BF16)
        band = band_ref[g]
        both = jnp.dot(band, jnp.concatenate([hi, lo], axis=-1), preferred_element_type=F32)
        wsum = both[:, :POOL_GW] + both[:, POOL_GW:]
        p = wsum * icnt_ref[:, g * POOL_GW:(g + 1) * POOL_GW] - ug
        parts.append(jnp.dot(p.astype(BF16), wp_ref[g], preferred_element_type=F32))
    pool = jnp.concatenate(parts, axis=-1) * sp_ref[...]

    y = ys_ref[...] + d_ref[...] * ux[:, POOL_W:]
    y = _gelu_tanh(y)
    z = jnp.dot(y.astype(BF16), wglu_ref[...], preferred_element_type=F32) + bglu_ref[...]
    glu = y * jax.nn.sigmoid(z)

    cat = jnp.concatenate([pool, glu], axis=-1).astype(BF16)
    o = jnp.dot(cat, wout_ref[...], preferred_element_type=F32)
    xn = _token_tile(x_ref, c_ref, n_x_tiles) + mod_ref[2:3, :] * o
    xo_ref[...] = xn

    h2 = _rmsnorm_mod(xn, gffn_ref[...], mod_ref[3:4, :], mod_ref[4:5, :])
    h_hi = h2.astype(BF16)
    h_lo = (h2 - h_hi.astype(F32)).astype(BF16)
    wrt = wrt_ref[...]
    nt_dims = (((1,), (1,)), ((), ()))
    q_hi = lax.dot_general(wrt, h_hi, nt_dims, preferred_element_type=F32)
    q_lo = lax.dot_general(wrt[:N_EXPERTS], h_lo, nt_dims, preferred_element_type=F32)
    logits = q_hi[:N_EXPERTS] + (q_hi[N_EXPERTS:] + q_lo)
    e1, e2, g1, g2 = _route(jax.nn.sigmoid(logits), br_ref[...])
    gate_ref[...] = jnp.concatenate([g1, g2], axis=0)

    row_i = lax.broadcasted_iota(jnp.int32, (N_EXPERTS, TM), 0)
    oh0 = jnp.where(row_i == e1, 1.0, 0.0)
    oh1 = jnp.where(row_i == e2, 1.0, 0.0)
    tri = tri_ref[...]
    before0 = jnp.dot(oh0.astype(BF16), tri, preferred_element_type=F32)
    before1 = jnp.dot(oh1.astype(BF16), tri, preferred_element_type=F32)
    tot0 = jnp.sum(oh0, axis=1, keepdims=True)
    tot1 = jnp.sum(oh1, axis=1, keepdims=True)
    smaller = jnp.where(row_i > e1, 1.0, 0.0) + jnp.where(row_i > e2, 1.0, 0.0)
    off = jnp.sum(smaller, axis=1, keepdims=True)
    p0 = jnp.sum(oh0 * (off + before0), axis=0, keepdims=True)
    p1 = jnp.sum(oh1 * (off + tot0 + before1), axis=0, keepdims=True)
    pos_ref[...] = jnp.concatenate([p0, p1], axis=0).astype(jnp.int32)
    cnt_ref[...] = (tot0 + tot1).astype(jnp.int32)

    slot = lax.broadcasted_iota(jnp.int32, (2 * TM, TM), 0).astype(F32)
    perm = jnp.where(jnp.logical_or(slot == p0, slot == p1), 1.0, 0.0).astype(BF16)
    hs = jnp.dot(perm, h2.astype(BF16), preferred_element_type=F32)
    for q in range(ROW_TILE):
        hs_ref[pl.ds(q, 2 * TM, stride=ROW_TILE), :] = hs[:, q * LANES:(q + 1) * LANES]


def _mixout(xa, ca, ctx_block, ux, y_ssm, mod_l, band, icnt, wp_bf, sp, dvec, wglu_bf, bglu, wout_bf,
            gffn, w_router, b_router, tri, nt, n_x_tiles):
    t = nt * TM
    sel = lambda i: (jnp.where(i == n_x_tiles, 1, 0), 0, 0)
    sel4 = lambda i: (jnp.where(i == n_x_tiles, 1, 0), 0, 0, 0)
    row = lambda i: (i, 0)
    fix2 = lambda i: (0, 0)
    return pl.pallas_call(
        functools.partial(_mixout_kernel, n_x_tiles=n_x_tiles),
        grid=(nt,),
        in_specs=_token_specs(n_x_tiles, ctx_block) + [
            pl.BlockSpec((TM, D_MODEL), row),
            pl.BlockSpec((TM, SSM_W), row),
            pl.BlockSpec((None, N_MOD, D_MODEL), sel),
            pl.BlockSpec((None, len(POOL_WINDOWS), TM, TM), sel4),
            pl.BlockSpec((None, TM, POOL_W), sel),
            pl.BlockSpec((len(POOL_WINDOWS), POOL_GW, POOL_GW), lambda i: (0, 0, 0)),
            pl.BlockSpec((1, POOL_W), fix2),
            pl.BlockSpec((1, SSM_W), fix2),
            pl.BlockSpec((SSM_W, SSM_W), fix2),
            pl.BlockSpec((1, SSM_W), fix2),
            pl.BlockSpec((D_MODEL, D_MODEL), fix2),
            pl.BlockSpec((1, D_MODEL), fix2),
            pl.BlockSpec((2 * N_EXPERTS, D_MODEL), fix2),
            pl.BlockSpec((N_EXPERTS, 1), fix2),
            pl.BlockSpec((TM, TM), fix2),
        ],
        out_specs=[
            pl.BlockSpec((TM, D_MODEL), row),
            pl.BlockSpec((2 * TM * ROW_TILE, LANES), row),
            pl.BlockSpec((2, TM), lambda i: (0, i)),
            pl.BlockSpec((2, TM), lambda i: (0, i)),
            pl.BlockSpec((None, N_EXPERTS, 1), lambda i: (i, 0, 0)),
        ],
        out_shape=[
            jax.ShapeDtypeStruct((t, D_MODEL), F32),
            jax.ShapeDtypeStruct((2 * t * ROW_TILE, LANES), F32),
            jax.ShapeDtypeStruct((2, t), jnp.int32),
            jax.ShapeDtypeStruct((2, t), F32),
            jax.ShapeDtypeStruct((nt, N_EXPERTS, 1), jnp.int32),
        ],
        compiler_params=_cparams(("arbitrary",)),
        name="mixer_out_router",
    )(xa, ca, ux, y_ssm, mod_l, band, icnt, wp_bf, sp, dvec, wglu_bf, bglu, wout_bf, gffn,
      w_router, b_router, tri)


def _copy_rows(src, src_row, dst, dst_row, n, n_max, sem):
    done = 0
    for bit in (1 << b for b in range(n_max.bit_length() - 1, -1, -1)):
        take = n & bit

        @pl.when(take != 0)
        def _(done=done, bit=bit):
            s0 = pl.multiple_of((src_row + done) * ROW_TILE, ROW_TILE)
            d0 = pl.multiple_of((dst_row + done) * ROW_TILE, ROW_TILE)
            pltpu.make_async_copy(src.at[pl.ds(s0, bit * ROW_TILE)],
                                  dst.at[pl.ds(d0, bit * ROW_TILE)], sem).start()

        done = done + take


def _wait_rows(src, dst, n, sem):
    size = pl.multiple_of(n * ROW_TILE, ROW_TILE)
    pltpu.make_async_copy(src.at[pl.ds(0, size)], dst.at[pl.ds(0, size)], sem).wait()


def _expert_kernel(be_ref, nbu_ref, nv_ref, rs_ref, re_ref, rsrc_ref, rlen_ref, rdst_ref,
                   hs_hbm, wg_ref, wu_ref, wd_ref, y_ref, xg0, xg1, wg_s, wu_s, wd_s, gsem):
    s = pl.program_id(0)
    nbu = nbu_ref[0]
    xg = (xg0, xg1)

    def gather_start(blk, slot):
        lo_blk = blk * MOE_BLK

        def body(r, c):
            g0 = rdst_ref[r]
            lo = jnp.maximum(g0, lo_blk)
            hi = jnp.minimum(g0 + rlen_ref[r], lo_blk + MOE_BLK)
            _copy_rows(hs_hbm, rsrc_ref[r] + (lo - g0), xg[slot], lo - lo_blk,
                       jnp.maximum(hi - lo, 0), TM, gsem.at[slot])
            return c

        lax.fori_loop(rs_ref[blk], re_ref[blk], body, 0)

    changed = jnp.logical_or(s == 0, be_ref[s] != be_ref[jnp.maximum(s - 1, 0)])

    @pl.when(jnp.logical_and(changed, s < nbu))
    def _():
        wg_s[...] = wg_ref[...].astype(BF16)
        wu_s[...] = wu_ref[...].astype(BF16)
        wd_s[...] = wd_ref[...].astype(BF16)

    def step(slot):
        @pl.when(s == 0)
        def _():
            xg0[...] = jnp.zeros_like(xg0)
            xg1[...] = jnp.zeros_like(xg1)
            gather_start(0, slot)

        @pl.when(s + 1 < nbu)
        def _():
            gather_start(s + 1, 1 - slot)

        _wait_rows(hs_hbm, xg[slot], nv_ref[s], gsem.at[slot])
        xb = jnp.concatenate([xg[slot][pl.ds(q, MOE_BLK, stride=ROW_TILE), :]
                              for q in range(ROW_TILE)], axis=-1).astype(BF16)
        g = jnp.dot(xb, wg_s[...], preferred_element_type=F32)
        u = jnp.dot(xb, wu_s[...], preferred_element_type=F32)
        hid = (g * jax.nn.sigmoid(g)) * u
        out = jnp.dot(hid.astype(BF16), wd_s[...], preferred_element_type=F32)
        for q in range(ROW_TILE):
            y_ref[pl.ds(q, MOE_BLK, stride=ROW_TILE), :] = out[:, q * LANES:(q + 1) * LANES]

    for slot in range(2):
        @pl.when(jnp.logical_and(s < nbu, s % 2 == slot))
        def _(slot=slot):
            step(slot)

    @pl.when(s >= nbu)
    def _():
        y_ref[...] = jnp.zeros_like(y_ref)


def _experts(blk_e, nb_used, blk_valid, rs, re, rsrc, rlen, rdst, hs, w_gate, w_up, w_down, layer):
    nb = blk_e.shape[0]
    wmap = lambda i, be, *_: (layer, be[i], 0, 0)
    buf = pltpu.VMEM((MOE_BLK * ROW_TILE, LANES), F32)
    return pl.pallas_call(
        _expert_kernel,
        grid_spec=pltpu.PrefetchScalarGridSpec(
            num_scalar_prefetch=8,
            grid=(nb,),
            in_specs=[
                pl.BlockSpec(memory_space=pl.ANY),
                pl.BlockSpec((None, None, D_MODEL, D_EXPERT), wmap),
                pl.BlockSpec((None, None, D_MODEL, D_EXPERT), wmap),
                pl.BlockSpec((None, None, D_EXPERT, D_MODEL), wmap),
            ],
            out_specs=pl.BlockSpec((MOE_BLK * ROW_TILE, LANES), lambda i, *_: (i, 0)),
            scratch_shapes=[buf, buf,
                            pltpu.VMEM((D_MODEL, D_EXPERT), BF16),
                            pltpu.VMEM((D_MODEL, D_EXPERT), BF16),
                            pltpu.VMEM((D_EXPERT, D_MODEL), BF16),
                            pltpu.SemaphoreType.DMA((2,))],
        ),
        out_shape=jax.ShapeDtypeStruct((nb * MOE_BLK * ROW_TILE, LANES), F32),
        compiler_params=_cparams(("arbitrary",)),
        name="moe_experts",
    )(blk_e, nb_used, blk_valid, rs, re, rsrc, rlen, rdst, hs, w_gate, w_up, w_down)


def _combine_kernel(meta_ref, x_ref, pos_ref, gate_ref, mod_ref, gfin_ref, ys_hbm, out_ref,
                    st0, st1, sem, *, final, nt):
    i = pl.program_id(0)
    stage = (st0, st1)

    def fetch(tile, slot):
        base = tile * LANES

        def runs(n_max):
            for e in range(N_EXPERTS):
                _copy_rows(ys_hbm, meta_ref[base + e], stage[slot],
                           meta_ref[base + 2 * N_EXPERTS + e], meta_ref[base + N_EXPERTS + e],
                           n_max, sem.at[slot])

        longest = meta_ref[base + 3 * N_EXPERTS]

        @pl.when(longest <= SHORT_RUN)
        def _():
            runs(SHORT_RUN)

        @pl.when(longest > SHORT_RUN)
        def _():
            runs(TM)

    def step(slot):
        @pl.when(i == 0)
        def _():
            fetch(0, slot)

        @pl.when(i + 1 < nt)
        def _():
            fetch(i + 1, 1 - slot)

        _wait_rows(ys_hbm, stage[slot], 2 * TM, sem.at[slot])
        rows = jnp.concatenate([stage[slot][pl.ds(q, 2 * TM, stride=ROW_TILE), :]
                                for q in range(ROW_TILE)], axis=-1)
        pos = pos_ref[...].astype(F32)
        gate = gate_ref[...]
        slot_id = lax.broadcasted_iota(jnp.int32, (2 * TM, TM), 0).astype(F32)
        gmat_t = (jnp.where(slot_id == pos[0:1, :], gate[0:1, :], 0.0)
                  + jnp.where(slot_id == pos[1:2, :], gate[1:2, :], 0.0))
        g_hi = gmat_t.astype(BF16)
        g_lo = (gmat_t - g_hi.astype(F32)).astype(BF16)
        r_hi = rows.astype(BF16)
        r_lo = (rows - r_hi.astype(F32)).astype(BF16)
        tn = (((0,), (0,)), ((), ()))
        y = (lax.dot_general(g_hi, r_hi, tn, preferred_element_type=F32)
             + (lax.dot_general(g_lo, r_hi, tn, preferred_element_type=F32)
                + lax.dot_general(g_hi, r_lo, tn, preferred_element_type=F32)))
        xn = x_ref[...] + mod_ref[5:6, :] * y
        if final:
            ms = jnp.mean(xn * xn, axis=-1, keepdims=True)
            xn = xn * lax.rsqrt(ms + EPS) * gfin_ref[...]
        out_ref[...] = xn

    for slot in range(2):
        @pl.when(i % 2 == slot)
        def _(slot=slot):
            step(slot)


def _combine(meta, xn, pos, gate, mod_l, g_final, ys, nt, n_x_tiles, final):
    sel = lambda i, m: (jnp.where(i == n_x_tiles, 1, 0), 0, 0)
    row = lambda i, m: (i, 0)
    stage = pltpu.VMEM((2 * TM * ROW_TILE, LANES), F32)
    return pl.pallas_call(
        functools.partial(_combine_kernel, final=final, nt=nt),
        grid_spec=pltpu.PrefetchScalarGridSpec(
            num_scalar_prefetch=1,
            grid=(nt,),
            in_specs=[
                pl.BlockSpec((TM, D_MODEL), row),
                pl.BlockSpec((2, TM), lambda i, m: (0, i)),
                pl.BlockSpec((2, TM), lambda i, m: (0, i)),
                pl.BlockSpec((None, N_MOD, D_MODEL), sel),
                pl.BlockSpec((1, D_MODEL), lambda i, m: (0, 0)),
                pl.BlockSpec(memory_space=pl.ANY),
            ],
            out_specs=pl.BlockSpec((TM, D_MODEL), row),
            scratch_shapes=[stage, stage, pltpu.SemaphoreType.DMA((2,))],
        ),
        out_shape=jax.ShapeDtypeStruct((nt * TM, D_MODEL), F32),
        compiler_params=_cparams(("arbitrary",)),
        name="moe_combine",
    )(meta, xn, pos, gate, mod_l, g_final, ys)


def _moe(hs, pos, tcnt, gate, xn, mod_l, g_final, w_gate, w_up, w_down, layer, nt, n_x_tiles,
         final):
    t = nt * TM
    i32 = jnp.int32
    n = tcnt.reshape(nt, N_EXPERTS)
    off = jnp.cumsum(n, axis=1) - n
    base = jnp.cumsum(n, axis=0) - n
    counts = jnp.sum(n, axis=0)
    padded = (counts + MOE_BLK - 1) // MOE_BLK * MOE_BLK
    pend = jnp.cumsum(padded)
    pstart = pend - padded
    nb = (2 * t + N_EXPERTS * (MOE_BLK - 1)) // MOE_BLK + 1
    nb_used = (pend[-1] // MOE_BLK).astype(i32)
    blk_lo = jnp.arange(nb, dtype=i32) * MOE_BLK
    blk_e = jnp.minimum(jnp.sum(pend[None, :] <= jnp.minimum(blk_lo, pend[-1] - MOE_BLK)[:, None],
                                axis=1), N_EXPERTS - 1).astype(i32)
    blk_valid = jnp.clip((pstart + counts)[blk_e] - blk_lo, 0, MOE_BLK).astype(i32)
    gsrc = (pstart[None, :] + base).astype(i32)
    rdst = gsrc.T.reshape(-1)
    rlen = n.T.reshape(-1).astype(i32)
    rsrc = (jnp.arange(nt, dtype=i32)[:, None] * (2 * TM) + off).T.reshape(-1).astype(i32)
    e_dst = gsrc.T[blk_e]
    e_end = e_dst + n.T[blk_e]
    rs = (blk_e * nt + jnp.sum(e_end <= blk_lo[:, None], axis=1)).astype(i32)
    re = (blk_e * nt + jnp.sum(e_dst < (blk_lo + MOE_BLK)[:, None], axis=1)).astype(i32)
    ys = _experts(blk_e, nb_used.reshape(1), blk_valid, rs, re, rsrc, rlen, rdst, hs,
                  w_gate, w_up, w_down, layer)
    longest = jnp.max(n, axis=1, keepdims=True).astype(i32)
    meta = jnp.concatenate([gsrc, n.astype(i32), off.astype(i32), longest,
                            jnp.zeros((nt, LANES - 3 * N_EXPERTS - 1), i32)], axis=1).reshape(-1)
    return _combine(meta, xn, pos, gate, mod_l, g_final, ys, nt, n_x_tiles, final)


def _cmul(ar, ai, br, bi):
    return ar * br - ai * bi, ar * bi + ai * br


def _expand_rows(v, rep_ref, mask_ref):
    return jnp.dot(v.astype(BF16), rep_ref[...], preferred_element_type=F32) * mask_ref[...]


def _expand_cols(e, rept_ref, maskt_ref):
    out = lax.dot_general(rept_ref[...], e.astype(BF16), (((1,), (1,)), ((), ())),
                          preferred_element_type=F32)
    return out * maskt_ref[...]


def _ssm_prep_kernel(are_ref, aim_ref, ldt_ref, btr_ref, bti_ref, cr_ref, ci_ref,
                     rep_ref, mask_ref, rept_ref, maskt_ref, ms_ref, mcat_ref, ecat):
    lag_blocks = []
    for d in range(2):
        a_re = jnp.minimum(are_ref[d], -1e-4)
        a_im = aim_ref[d]
        dt = jnp.exp(ldt_ref[d])
        mag = jnp.exp(a_re * dt)
        lr = mag * jnp.cos(a_im * dt)
        li = mag * jnp.sin(a_im * dt)
        den = a_re * a_re + a_im * a_im
        k_re = ((lr - 1.0) * a_re + li * a_im) / den
        k_im = (li * a_re - (lr - 1.0) * a_im) / den
        bbr, bbi = _cmul(k_re, k_im, btr_ref[d], bti_ref[d])
        cr, ci = cr_ref[d], ci_ref[d]
        pr, pi = jnp.ones_like(lr), jnp.zeros_like(lr)
        v0 = None
        for n in range(CH + 1):
            er, ei = _cmul(pr, pi, cr, ci)
            et = jnp.concatenate([_expand_cols(er, rept_ref, maskt_ref),
                                  _expand_cols(-ei, rept_ref, maskt_ref)], axis=0).astype(BF16)
            if n < CH:
                vr, vi = _cmul(pr, pi, bbr, bbi)
                v = jnp.concatenate([_expand_rows(vr, rep_ref, mask_ref),
                                     _expand_rows(vi, rep_ref, mask_ref)], axis=1).astype(BF16)
                if n == 0:
                    v0 = v
                sigma = CH - 1 - n if d == 0 else n
                ms_ref[d, sigma * LANES:(sigma + 1) * LANES, :] = v
                ecat[:, n * LANES:(n + 1) * LANES] = et
            if n >= 1:
                tau = n - 1 if d == 0 else CH - n
                mcat_ref[CW + d * SW:CW + (d + 1) * SW, tau * LANES:(tau + 1) * LANES] = et
            pr, pi = _cmul(pr, pi, lr, li)
        lag_blocks.append(jnp.dot(v0, ecat[...], preferred_element_type=F32))
    kf, kb = lag_blocks
    for s in range(CH):
        for t in range(CH):
            if t > s:
                blk = kf[:, (t - s) * LANES:(t - s + 1) * LANES]
            elif t < s:
                blk = kb[:, (s - t) * LANES:(s - t + 1) * LANES]
            else:
                blk = kf[:, :LANES] + kb[:, :LANES]
            mcat_ref[s * LANES:(s + 1) * LANES, t * LANES:(t + 1) * LANES] = blk.astype(BF16)


def _ssm_operators(a_re, a_im, log_dt, b_re, b_im, c_re, c_im):
    depth = a_re.shape[0]
    rows = SSM_GROUPS * SSM_GROUP
    p = SSM_STATE
    rep_rows = lambda v: jnp.repeat(v, SSM_GROUP, axis=2)
    are = rep_rows(a_re)
    aim = rep_rows(a_im)
    ldt = jnp.broadcast_to(rep_rows(log_dt[..., None]), are.shape)
    btr = jnp.swapaxes(b_re, -1, -2).reshape(depth, 2, rows, p)
    bti = jnp.swapaxes(b_im, -1, -2).reshape(depth, 2, rows, p)
    cr = c_re.reshape(depth, 2, rows, p)
    ci = c_im.reshape(depth, 2, rows, p)
    rep = np.tile(np.eye(p, dtype=np.float32), (1, GB))
    mask = np.kron(np.eye(GB, dtype=np.float32), np.ones((SSM_GROUP, p), np.float32))
    par = pl.BlockSpec((None, 2, LANES, p), lambda l, b: (l, 0, b, 0))
    fix = lambda shape: pl.BlockSpec(shape, lambda l, b: (0, 0))
    return pl.pallas_call(
        _ssm_prep_kernel,
        grid=(depth, NGB),
        in_specs=[par] * 7 + [fix((p, GB * p)), fix((LANES, GB * p)),
                              fix((GB * p, p)), fix((GB * p, LANES))],
        out_specs=[
            pl.BlockSpec((None, None, 2, CW, SW), lambda l, b: (l, b, 0, 0, 0)),
            pl.BlockSpec((None, None, CW + 2 * SW, CW), lambda l, b: (l, b, 0, 0)),
        ],
        out_shape=[
            jax.ShapeDtypeStruct((depth, NGB, 2, CW, SW), BF16),
            jax.ShapeDtypeStruct((depth, NGB, CW + 2 * SW, CW), BF16),
        ],
        scratch_shapes=[pltpu.VMEM((SW, CW), BF16)],
        compiler_params=_cparams(("arbitrary", "arbitrary")),
        name="ssm_operators",
    )(are, aim, ldt, btr, bti, cr, ci, jnp.asarray(rep, BF16), jnp.asarray(mask),
      jnp.asarray(rep.T, BF16), jnp.asarray(mask.T))


def _scan_tables(a_re, a_im, log_dt):
    depth = a_re.shape[0]
    a_re = jnp.minimum(a_re, -1e-4)
    dt = jnp.exp(log_dt)[..., None]
    mag = jnp.exp(a_re * dt)
    lr = mag * jnp.cos(a_im * dt)
    li = mag * jnp.sin(a_im * dt)
    ar, ai = lr, li
    for _ in range(CH - 1):
        ar, ai = _cmul(ar, ai, lr, li)
    apw = [(ar, ai)]
    for _ in range(SUBLANES - 1):
        apw.append(_cmul(apw[-1][0], apw[-1][1], ar, ai))
    apr = jnp.stack([q[0] for q in apw], axis=1)
    api = jnp.stack([q[1] for q in apw], axis=1)
    rows = np.arange(SUBLANES)
    tabs = []
    for d in range(2):
        consts = []
        for k in (1, 2, 4):
            keep = (rows >= k) if d == 0 else (rows < SUBLANES - k)
            keep = jnp.asarray(keep.astype(np.float32))[None, :, None, None]
            consts.append(keep * apr[:, k - 1:k, d])
            consts.append(keep * api[:, k - 1:k, d])
        order = rows if d == 0 else SUBLANES - 1 - rows
        consts.append(apr[:, order, d])
        consts.append(api[:, order, d])
        tabs.append(jnp.stack(consts, axis=1))
    tab = jnp.stack(tabs, axis=1).reshape(depth, 2, 8 * SUBLANES, NGB, GB * SSM_STATE)
    return tab.transpose(0, 1, 3, 2, 4)


def _pool_constants(seg):
    t = np.arange(TM)
    s0 = t // seg * seg
    band = np.zeros((len(POOL_WINDOWS), TM, TM), np.float32)
    icnt = np.zeros((TM, POOL_W), np.float32)
    for g, w in enumerate(POOL_WINDOWS):
        lo = np.maximum(t - w // 2, s0)
        hi = np.minimum(t + w // 2, s0 + seg)
        band[g] = (t[None, :] >= lo[:, None]) & (t[None, :] < hi[:, None])
        icnt[:, g * POOL_GW:(g + 1) * POOL_GW] = (1.0 / (hi - lo).astype(np.float32))[:, None]
    return band, icnt


def kernel(x, c, ctx, c_ctx, w_mod, b_mod, g_mix, g_ffn, w_in, w_out, w_pool, s_pool,
           ssm_a_re, ssm_a_im, ssm_log_dt, ssm_b_re, ssm_b_im, ssm_c_re, ssm_c_im, ssm_d,
           w_glu, b_glu, w_router, b_router, w_gate, w_up, w_down, g_final):
    bsz, seq, d = x.shape
    ctx_len = ctx.shape[1]
    depth = w_mod.shape[0]
    assert bsz == 1 and d == D_MODEL and ctx_len == TM
    assert seq % (SR * CH) == 0 and seq % GRID_W == 0 and seq % TI == 0
    n_xt = seq // TM
    n_st = seq // (SR * CH)
    ctx_rows = ctx_len // CH

    cvec = jnp.concatenate([c.reshape(1, d), c_ctx.reshape(1, d),
                            jnp.zeros((SUBLANES - 2, d), F32)], axis=0)
    mod = _modulation(cvec, w_mod, b_mod).reshape(depth, SUBLANES, N_MOD, d)

    band_x, icnt_x = _pool_constants(GRID_W)
    band_c, icnt_c = _pool_constants(ctx_len)
    band = jnp.asarray(np.stack([band_x, band_c]), BF16)
    icnt = jnp.asarray(np.stack([icnt_x, icnt_c]), F32)
    tri = jnp.asarray(np.triu(np.ones((TM, TM), np.float32), 1), BF16)
    wr_hi = w_router.astype(BF16)
    wr_split = jnp.concatenate([wr_hi, (w_router - wr_hi.astype(F32)).astype(BF16)], axis=1).T

    ms, mcat = _ssm_operators(ssm_a_re, ssm_a_im, ssm_log_dt, ssm_b_re, ssm_b_im,
                              ssm_c_re, ssm_c_im)
    tab = _scan_tables(ssm_a_re, ssm_a_im, ssm_log_dt)

    xa, ca, ctx_block = x[0], ctx[0], 0
    for l in range(depth):
        last = l == depth - 1
        mod_l = mod[l, :2]
        ux, uv_ext = _inproj(xa, ca, ctx_block, mod_l, g_mix[l].reshape(1, d),
                             w_in[l].astype(BF16), n_xt)
        hf, gb = _ssm_states(uv_ext, ms, tab, l, n_st, ctx_rows)
        y_ssm = _ssm_readout(uv_ext, hf, gb, mcat, l)
        nt = n_xt if last else n_xt + 1
        xn, hs, pos, gate, tcnt = _mixout(
            xa, ca, ctx_block, ux, y_ssm, mod_l, band, icnt, w_pool[l].astype(BF16),
            s_pool[l].reshape(1, -1), ssm_d[l].reshape(1, -1), w_glu[l].astype(BF16),
            b_glu[l].reshape(1, -1), w_out[l].astype(BF16), g_ffn[l].reshape(1, d), wr_split,
            b_router.reshape(-1, 1), tri, nt, n_xt)
        tok = _moe(hs, pos, tcnt, gate, xn, mod_l, g_final.reshape(1, d), w_gate, w_up, w_down,
                   l, nt, n_xt, last)
        xa, ca, ctx_block = tok, tok, n_xt
    return tok.reshape(bsz, seq, d)
```

```python
import functools

import numpy as np
import jax
import jax.numpy as jnp
from jax import lax
from jax.experimental import pallas as pl
from jax.experimental.pallas import tpu as pltpu

F32 = jnp.float32
BF16 = jnp.bfloat16

D_MODEL = 1024
POOL_W = 512
SSM_W = 512
POOL_WINDOWS = (2, 4, 8, 16)
POOL_GW = 128
SSM_GROUP = 16
SSM_GROUPS = 32
SSM_STATE = 64
N_EXPERTS = 32
N_EXPERT_GROUPS = 4
EXPERTS_PER_GROUP = 8
D_EXPERT = 512
GRID_W = 64
EPS = 1e-6
N_MOD = 6

LANES = 128
SUBLANES = 8
TM = 256
TI = 512
CH = 8
GB = 8
NGB = SSM_GROUPS // GB
CW = CH * LANES
SW = 2 * GB * SSM_STATE
SR = 256
MOE_BLK = 512
ROW_TILE = D_MODEL // LANES
SHORT_RUN = 31
VMEM_LIMIT = 48 * 1024 * 1024


def _cparams(sem):
    return pltpu.CompilerParams(dimension_semantics=sem, vmem_limit_bytes=VMEM_LIMIT)


def _rmsnorm_mod(x, g, shift, scale):
    ms = jnp.mean(x * x, axis=-1, keepdims=True)
    y = x * lax.rsqrt(ms + EPS) * g
    return y * (1.0 + scale) + shift


def _mod_kernel(c_ref, w_ref, b_ref, o_ref):
    c = c_ref[...]
    a = c * jax.nn.sigmoid(c)
    o_ref[...] = jnp.dot(a, w_ref[...], preferred_element_type=F32,
                         precision=lax.Precision.HIGHEST) + b_ref[...]


def _modulation(cvec, w_mod, b_mod):
    depth, d, n = w_mod.shape
    tn = 1536
    return pl.pallas_call(
        _mod_kernel,
        grid=(depth, n // tn),
        in_specs=[
            pl.BlockSpec((SUBLANES, d), lambda l, j: (0, 0)),
            pl.BlockSpec((None, d, tn), lambda l, j: (l, 0, j)),
            pl.BlockSpec((None, 1, tn), lambda l, j: (l, 0, j)),
        ],
        out_specs=pl.BlockSpec((None, SUBLANES, tn), lambda l, j: (l, 0, j)),
        out_shape=jax.ShapeDtypeStruct((depth, SUBLANES, n), F32),
        compiler_params=_cparams(("arbitrary", "arbitrary")),
        name="modulation",
    )(cvec, w_mod, b_mod.reshape(depth, 1, n))


def _token_tile(x_ref, c_ref, n_x_tiles):
    return jnp.where(pl.program_id(0) == n_x_tiles, c_ref[...], x_ref[...])


def _inproj_kernel(x_ref, c_ref, mod_ref, g_ref, w_ref, ux_ref, uv_ref, scr, *, n_big):
    i = pl.program_id(0)

    def project(x, rows):
        h = _rmsnorm_mod(x, g_ref[...], mod_ref[0:1, :], mod_ref[1:2, :])
        ux = jnp.dot(h.astype(BF16), w_ref[...], preferred_element_type=F32)
        ux_ref[0:rows, :] = ux
        crow = rows // CH
        for b in range(NGB):
            scr[b, 0:rows, :] = ux[:, POOL_W + LANES * b:POOL_W + LANES * (b + 1)]
            for t in range(CH):
                piece = scr[b, pl.ds(t, crow, stride=CH), :]
                uv_ref[0:crow, pl.ds((b * CH + t) * LANES, LANES)] = piece.astype(BF16)

    @pl.when(i < n_big)
    def _():
        project(x_ref[...], TI)

    @pl.when(i == n_big)
    def _():
        project(c_ref[...], TM)
        ux_ref[TM:TI, :] = jnp.zeros((TI - TM, D_MODEL), F32)
        uv_ref[TM // CH:TI // CH, :] = jnp.zeros(((TI - TM) // CH, NGB * CW), BF16)

    @pl.when(i > n_big)
    def _():
        uv_ref[...] = jnp.zeros_like(uv_ref)


def _token_specs(n_x_tiles, ctx_block):
    return [pl.BlockSpec((TM, D_MODEL), lambda i: (jnp.minimum(i, n_x_tiles - 1), 0)),
            pl.BlockSpec((TM, D_MODEL), lambda i: (ctx_block, 0))]


def _inproj(xa, ca, ctx_block, mod_l, g, w_bf, n_x_tiles):
    n_big = n_x_tiles * TM // TI
    n_steps = n_big + SR // (TI // CH)
    sel = lambda i: (jnp.where(i >= n_big, 1, 0), 0, 0)
    return pl.pallas_call(
        functools.partial(_inproj_kernel, n_big=n_big),
        grid=(n_steps,),
        in_specs=[
            pl.BlockSpec((TI, D_MODEL), lambda i: (jnp.minimum(i, n_big - 1), 0)),
            pl.BlockSpec((TM, D_MODEL), lambda i: (ctx_block, 0)),
            pl.BlockSpec((None, N_MOD, D_MODEL), sel),
            pl.BlockSpec((1, D_MODEL), lambda i: (0, 0)),
            pl.BlockSpec((D_MODEL, D_MODEL), lambda i: (0, 0)),
        ],
        out_specs=[
            pl.BlockSpec((TI, D_MODEL), lambda i: (jnp.minimum(i, n_big), 0)),
            pl.BlockSpec((TI // CH, NGB * CW), lambda i: (i, 0)),
        ],
        out_shape=[
            jax.ShapeDtypeStruct(((n_x_tiles + 1) * TM, D_MODEL), F32),
            jax.ShapeDtypeStruct((n_steps * TI // CH, NGB * CW), BF16),
        ],
        scratch_shapes=[pltpu.VMEM((NGB, TI, LANES), F32)],
        compiler_params=_cparams(("arbitrary",)),
        name="mixer_inproj",
    )(xa, ca, mod_l, g, w_bf)


def _chunk_scan(s_ref, tab_ref, carry_ref, out_ref, nblk, reverse):
    half = SW // 2
    ncol = half // LANES
    row = lax.broadcasted_iota(jnp.int32, (SUBLANES, LANES), 0)
    edge = (row == SUBLANES - 1) if reverse else (row == 0)
    last = 0 if reverse else SUBLANES - 1

    def sub_block(r0, carry):
        outs_r, outs_i, new_carry = [], [], []
        for j in range(ncol):
            cr, ci = carry[2 * j], carry[2 * j + 1]
            lre = pl.ds(LANES * j, LANES)
            lim = pl.ds(half + LANES * j, LANES)
            zr = s_ref[pl.ds(r0, SUBLANES), lre]
            zi = s_ref[pl.ds(r0, SUBLANES), lim]
            for q, k in enumerate((1, 2, 4)):
                ar = tab_ref[pl.ds(16 * q, SUBLANES), lre]
                ai = tab_ref[pl.ds(16 * q + 8, SUBLANES), lre]
                sh = SUBLANES - k if reverse else k
                sr = pltpu.roll(zr, sh, axis=0)
                si = pltpu.roll(zi, sh, axis=0)
                zr, zi = zr + ar * sr - ai * si, zi + ar * si + ai * sr
            pr = tab_ref[pl.ds(48, SUBLANES), lre]
            pi = tab_ref[pl.ds(56, SUBLANES), lre]
            zr, zi = zr + pr * cr - pi * ci, zi + pr * ci + pi * cr
            sh1 = SUBLANES - 1 if reverse else 1
            outs_r.append(jnp.where(edge, cr, pltpu.roll(zr, sh1, axis=0)))
            outs_i.append(jnp.where(edge, ci, pltpu.roll(zi, sh1, axis=0)))
            new_carry.append(jnp.broadcast_to(zr[last:last + 1, :], (SUBLANES, LANES)))
            new_carry.append(jnp.broadcast_to(zi[last:last + 1, :], (SUBLANES, LANES)))
        return outs_r, outs_i, tuple(new_carry)

    def body(it, carry):
        bi = (nblk - 1 - it) if reverse else it
        r0 = pl.multiple_of(bi * 2 * SUBLANES, 2 * SUBLANES)
        if reverse:
            hi_r, hi_i, carry = sub_block(r0 + SUBLANES, carry)
            lo_r, lo_i, carry = sub_block(r0, carry)
        else:
            lo_r, lo_i, carry = sub_block(r0, carry)
            hi_r, hi_i, carry = sub_block(r0 + SUBLANES, carry)
        for j in range(ncol):
            out_ref[pl.ds(r0, 2 * SUBLANES), pl.ds(LANES * j, LANES)] = (
                jnp.concatenate([lo_r[j], hi_r[j]], axis=0).astype(BF16))
            out_ref[pl.ds(r0, 2 * SUBLANES), pl.ds(half + LANES * j, LANES)] = (
                jnp.concatenate([lo_i[j], hi_i[j]], axis=0).astype(BF16))
        return carry

    init = tuple(carry_ref[:, pl.ds(LANES * c, LANES)] for c in range(2 * ncol))
    final = lax.fori_loop(0, nblk, body, init)
    for c in range(2 * ncol):
        carry_ref[:, pl.ds(LANES * c, LANES)] = final[c]


def _ssm_state_kernel(uf_ref, ub_ref, msf_ref, msb_ref, tf_ref, tb_ref, hf_ref, gb_ref,
                      sf, sb, cf, cb, *, ctx_rows):
    step = pl.program_id(1)

    @pl.when(step == 0)
    def _():
        cf[...] = jnp.zeros_like(cf)
        cb[...] = jnp.zeros_like(cb)
        hf_ref[...] = jnp.zeros_like(hf_ref)
        gb_ref[...] = jnp.zeros_like(gb_ref)

    sf[...] = jnp.dot(uf_ref[...], msf_ref[...], preferred_element_type=F32)
    sb[...] = jnp.dot(ub_ref[...], msb_ref[...], preferred_element_type=F32)
    nblk = jnp.where(step == 0, ctx_rows // (2 * SUBLANES), SR // (2 * SUBLANES))
    _chunk_scan(sf, tf_ref, cf, hf_ref, nblk, reverse=False)
    _chunk_scan(sb, tb_ref, cb, gb_ref, nblk, reverse=True)


def _ssm_states(uv_ext, ms, tab, layer, n_xt, ctx_rows):
    rows = uv_ext.shape[0]
    fwd = lambda b, s: (jnp.where(s == 0, n_xt, s - 1), b)
    bwd = lambda b, s: (jnp.where(s == 0, n_xt, n_xt - s), b)
    return pl.pallas_call(
        functools.partial(_ssm_state_kernel, ctx_rows=ctx_rows),
        grid=(NGB, n_xt + 1),
        in_specs=[
            pl.BlockSpec((SR, CW), fwd),
            pl.BlockSpec((SR, CW), bwd),
            pl.BlockSpec((None, None, None, CW, SW), lambda b, s: (layer, b, 0, 0, 0)),
            pl.BlockSpec((None, None, None, CW, SW), lambda b, s: (layer, b, 1, 0, 0)),
            pl.BlockSpec((None, None, None, 8 * SUBLANES, SW // 2),
                         lambda b, s: (layer, 0, b, 0, 0)),
            pl.BlockSpec((None, None, None, 8 * SUBLANES, SW // 2),
                         lambda b, s: (layer, 1, b, 0, 0)),
        ],
        out_specs=[pl.BlockSpec((SR, SW), fwd), pl.BlockSpec((SR, SW), bwd)],
        out_shape=[jax.ShapeDtypeStruct((rows, NGB * SW), BF16)] * 2,
        scratch_shapes=[pltpu.VMEM((SR, SW), F32), pltpu.VMEM((SR, SW), F32),
                        pltpu.VMEM((SUBLANES, SW), F32), pltpu.VMEM((SUBLANES, SW), F32)],
        compiler_params=_cparams(("arbitrary", "arbitrary")),
        name="ssm_states",
    )(uv_ext, uv_ext, ms, ms, tab, tab)


def _ssm_out_kernel(u_ref, hf_ref, gb_ref, m_ref, y_ref):
    res = jnp.dot(u_ref[...], m_ref[0:CW, :], preferred_element_type=F32)
    res += jnp.dot(hf_ref[...], m_ref[CW:CW + SW, :], preferred_element_type=F32)
    res += jnp.dot(gb_ref[...], m_ref[CW + SW:CW + 2 * SW, :], preferred_element_type=F32)
    for t in range(CH):
        y_ref[pl.ds(t, SR, stride=CH), :] = res[:, t * LANES:(t + 1) * LANES]


def _ssm_readout(uv_ext, hf, gb, mcat, layer):
    rows = uv_ext.shape[0]
    nt = rows // SR
    blk = lambda b, i: (i, b)
    return pl.pallas_call(
        _ssm_out_kernel,
        grid=(NGB, nt),
        in_specs=[
            pl.BlockSpec((SR, CW), blk),
            pl.BlockSpec((SR, SW), blk),
            pl.BlockSpec((SR, SW), blk),
            pl.BlockSpec((None, None, CW + 2 * SW, CW), lambda b, i: (layer, b, 0, 0)),
        ],
        out_specs=pl.BlockSpec((SR * CH, LANES), blk),
        out_shape=jax.ShapeDtypeStruct((rows * CH, SSM_W), F32),
        compiler_params=_cparams(("arbitrary", "arbitrary")),
        name="ssm_readout",
    )(uv_ext, hf, gb, mcat)


def _gelu_tanh(x):
    return 0.5 * x * (1.0 + jnp.tanh(0.7978845608028654 * (x + 0.044715 * x * x * x)))


def _route(s, b_col):
    sel = s + b_col
    neg = jnp.float32(-jnp.inf)

    def top2(vals):
        row = lax.broadcasted_iota(jnp.int32, vals.shape, 0).astype(F32)
        big = jnp.float32(vals.shape[0])
        m1 = jnp.max(vals, axis=0, keepdims=True)
        i1 = jnp.min(jnp.where(vals == m1, row, big), axis=0, keepdims=True)
        rest = jnp.where(row == i1, neg, vals)
        m2 = jnp.max(rest, axis=0, keepdims=True)
        i2 = jnp.min(jnp.where(rest == m2, row, big), axis=0, keepdims=True)
        return m1, i1, m2, i2

    best = None
    gidx = None
    for g in range(N_EXPERT_GROUPS):
        m1, _, m2, _ = top2(sel[g * EXPERTS_PER_GROUP:(g + 1) * EXPERTS_PER_GROUP])
        score = m1 + m2
        if best is None:
            best, gidx = score, jnp.zeros(score.shape, jnp.int32)
        else:
            upd = score > best
            best = jnp.where(upd, score, best)
            gidx = jnp.where(upd, g, gidx)
    row_i = lax.broadcasted_iota(jnp.int32, s.shape, 0)
    _, e1, _, e2 = top2(jnp.where(row_i // EXPERTS_PER_GROUP == gidx, sel, neg))
    row = row_i.astype(F32)
    w1 = jnp.sum(jnp.where(row == e1, s, 0.0), axis=0, keepdims=True)
    w2 = jnp.sum(jnp.where(row == e2, s, 0.0), axis=0, keepdims=True)
    tot = w1 + w2
    return e1.astype(jnp.int32), e2.astype(jnp.int32), w1 / tot, w2 / tot


def _mixout_kernel(x_ref, c_ref, ux_ref, ys_ref, mod_ref, band_ref, icnt_ref, wp_ref, sp_ref, d_ref,
                   wglu_ref, bglu_ref, wout_ref, gffn_ref, wrt_ref, br_ref, tri_ref,
                   xo_ref, hs_ref, pos_ref, gate_ref, cnt_ref, *, n_x_tiles):
    ux = ux_ref[...]
    parts = []
    for g in range(len(POOL_WINDOWS)):
        ug = ux[:, g * POOL_GW:(g + 1) * POOL_GW]
        hi = ug.astype(BF16)
        lo = (ug - hi.astype(F32)).astype(BF16)
        band = band_ref[g]
        both = jnp.dot(band, jnp.concatenate([hi, lo], axis=-1), preferred_element_type=F32)
        wsum = both[:, :POOL_GW] + both[:, POOL_GW:]
        p = wsum * icnt_ref[:, g * POOL_GW:(g + 1) * POOL_GW] - ug
        parts.append(p.astype(BF16))
    pool = jnp.concatenate(
        [jnp.dot(jnp.concatenate(parts[2 * k:2 * k + 2], axis=-1), wp_ref[k],
                 preferred_element_type=F32) for k in range(len(POOL_WINDOWS) // 2)],
        axis=-1) * sp_ref[...]

    y = ys_ref[...] + d_ref[...] * ux[:, POOL_W:]
    y = _gelu_tanh(y)
    z = jnp.dot(y.astype(BF16), wglu_ref[...], preferred_element_type=F32) + bglu_ref[...]
    glu = y * jax.nn.sigmoid(z)

    cat = jnp.concatenate([pool, glu], axis=-1).astype(BF16)
    o = jnp.dot(cat, wout_ref[...], preferred_element_type=F32)
    xn = _token_tile(x_ref, c_ref, n_x_tiles) + mod_ref[2:3, :] * o
    xo_ref[...] = xn

    h2 = _rmsnorm_mod(xn, gffn_ref[...], mod_ref[3:4, :], mod_ref[4:5, :])
    h_hi = h2.astype(BF16)
    h_lo = (h2 - h_hi.astype(F32)).astype(BF16)
    wrt = wrt_ref[...]
    nt_dims = (((1,), (1,)), ((), ()))
    q_hi = lax.dot_general(wrt, h_hi, nt_dims, preferred_element_type=F32)
    q_lo = lax.dot_general(wrt[:N_EXPERTS], h_lo, nt_dims, preferred_element_type=F32)
    logits = q_hi[:N_EXPERTS] + (q_hi[N_EXPERTS:] + q_lo)
    e1, e2, g1, g2 = _route(jax.nn.sigmoid(logits), br_ref[...])
    gate_ref[...] = jnp.concatenate([g1, g2], axis=0)

    row_i = lax.broadcasted_iota(jnp.int32, (N_EXPERTS, TM), 0)
    oh0 = jnp.where(row_i == e1, 1.0, 0.0)
    oh1 = jnp.where(row_i == e2, 1.0, 0.0)
    tri = tri_ref[...]
    before0 = jnp.dot(oh0.astype(BF16), tri, preferred_element_type=F32)
    before1 = jnp.dot(oh1.astype(BF16), tri, preferred_element_type=F32)
    tot0 = jnp.sum(oh0, axis=1, keepdims=True)
    tot1 = jnp.sum(oh1, axis=1, keepdims=True)
    smaller = jnp.where(row_i > e1, 1.0, 0.0) + jnp.where(row_i > e2, 1.0, 0.0)
    off = jnp.sum(smaller, axis=1, keepdims=True)
    p0 = jnp.sum(oh0 * (off + before0), axis=0, keepdims=True)
    p1 = jnp.sum(oh1 * (off + tot0 + before1), axis=0, keepdims=True)
    pos_ref[...] = jnp.concatenate([p0, p1], axis=0).astype(jnp.int32)
    cnt_ref[...] = (tot0 + tot1).astype(jnp.int32)

    slot = lax.broadcasted_iota(jnp.int32, (2 * TM, TM), 0).astype(F32)
    perm = jnp.where(jnp.logical_or(slot == p0, slot == p1), 1.0, 0.0).astype(BF16)
    hs = jnp.dot(perm, h2.astype(BF16), preferred_element_type=F32)
    for q in range(ROW_TILE):
        hs_ref[pl.ds(q, 2 * TM, stride=ROW_TILE), :] = hs[:, q * LANES:(q + 1) * LANES]


def _mixout(xa, ca, ctx_block, ux, y_ssm, mod_l, band, icnt, wp_bf, sp, dvec, wglu_bf, bglu, wout_bf,
            gffn, w_router, b_router, tri, nt, n_x_tiles):
    t = nt * TM
    sel = lambda i: (jnp.where(i == n_x_tiles, 1, 0), 0, 0)
    sel4 = lambda i: (jnp.where(i == n_x_tiles, 1, 0), 0, 0, 0)
    row = lambda i: (i, 0)
    fix2 = lambda i: (0, 0)
    return pl.pallas_call(
        functools.partial(_mixout_kernel, n_x_tiles=n_x_tiles),
        grid=(nt,),
        in_specs=_token_specs(n_x_tiles, ctx_block) + [
            pl.BlockSpec((TM, D_MODEL), row),
            pl.BlockSpec((TM, SSM_W), row),
            pl.BlockSpec((None, N_MOD, D_MODEL), sel),
            pl.BlockSpec((None, len(POOL_WINDOWS), TM, TM), sel4),
            pl.BlockSpec((None, TM, POOL_W), sel),
            pl.BlockSpec((len(POOL_WINDOWS) // 2, 2 * POOL_GW, 2 * POOL_GW), lambda i: (0, 0, 0)),
            pl.BlockSpec((1, POOL_W), fix2),
            pl.BlockSpec((1, SSM_W), fix2),
            pl.BlockSpec((SSM_W, SSM_W), fix2),
            pl.BlockSpec((1, SSM_W), fix2),
            pl.BlockSpec((D_MODEL, D_MODEL), fix2),
            pl.BlockSpec((1, D_MODEL), fix2),
            pl.BlockSpec((2 * N_EXPERTS, D_MODEL), fix2),
            pl.BlockSpec((N_EXPERTS, 1), fix2),
            pl.BlockSpec((TM, TM), fix2),
        ],
        out_specs=[
            pl.BlockSpec((TM, D_MODEL), row),
            pl.BlockSpec((2 * TM * ROW_TILE, LANES), row),
            pl.BlockSpec((2, TM), lambda i: (0, i)),
            pl.BlockSpec((2, TM), lambda i: (0, i)),
            pl.BlockSpec((None, N_EXPERTS, 1), lambda i: (i, 0, 0)),
        ],
        out_shape=[
            jax.ShapeDtypeStruct((t, D_MODEL), F32),
            jax.ShapeDtypeStruct((2 * t * ROW_TILE, LANES), F32),
            jax.ShapeDtypeStruct((2, t), jnp.int32),
            jax.ShapeDtypeStruct((2, t), F32),
            jax.ShapeDtypeStruct((nt, N_EXPERTS, 1), jnp.int32),
        ],
        compiler_params=_cparams(("arbitrary",)),
        name="mixer_out_router",
    )(xa, ca, ux, y_ssm, mod_l, band, icnt, wp_bf, sp, dvec, wglu_bf, bglu, wout_bf, gffn,
      w_router, b_router, tri)


def _copy_rows(src, src_row, dst, dst_row, n, n_max, sem):
    done = 0
    for bit in (1 << b for b in range(n_max.bit_length() - 1, -1, -1)):
        take = n & bit

        @pl.when(take != 0)
        def _(done=done, bit=bit):
            s0 = pl.multiple_of((src_row + done) * ROW_TILE, ROW_TILE)
            d0 = pl.multiple_of((dst_row + done) * ROW_TILE, ROW_TILE)
            pltpu.make_async_copy(src.at[pl.ds(s0, bit * ROW_TILE)],
                                  dst.at[pl.ds(d0, bit * ROW_TILE)], sem).start()

        done = done + take


def _wait_rows(src, dst, n, sem):
    size = pl.multiple_of(n * ROW_TILE, ROW_TILE)
    pltpu.make_async_copy(src.at[pl.ds(0, size)], dst.at[pl.ds(0, size)], sem).wait()


def _expert_kernel(be_ref, nbu_ref, nv_ref, rs_ref, re_ref, rsrc_ref, rlen_ref, rdst_ref,
                   hs_hbm, wg_ref, wu_ref, wd_ref, y_ref, xg0, xg1, wg_s, wu_s, wd_s, gsem):
    s = pl.program_id(0)
    nbu = nbu_ref[0]
    xg = (xg0, xg1)

    def gather_start(blk, slot):
        lo_blk = blk * MOE_BLK

        def body(r, c):
            g0 = rdst_ref[r]
            lo = jnp.maximum(g0, lo_blk)
            hi = jnp.minimum(g0 + rlen_ref[r], lo_blk + MOE_BLK)
            _copy_rows(hs_hbm, rsrc_ref[r] + (lo - g0), xg[slot], lo - lo_blk,
                       jnp.maximum(hi - lo, 0), TM, gsem.at[slot])
            return c

        lax.fori_loop(rs_ref[blk], re_ref[blk], body, 0)

    changed = jnp.logical_or(s == 0, be_ref[s] != be_ref[jnp.maximum(s - 1, 0)])

    @pl.when(jnp.logical_and(changed, s < nbu))
    def _():
        wg_s[...] = wg_ref[...].astype(BF16)
        wu_s[...] = wu_ref[...].astype(BF16)
        wd_s[...] = wd_ref[...].astype(BF16)

    def step(slot):
        @pl.when(s == 0)
        def _():
            xg0[...] = jnp.zeros_like(xg0)
            xg1[...] = jnp.zeros_like(xg1)
            gather_start(0, slot)

        @pl.when(s + 1 < nbu)
        def _():
            gather_start(s + 1, 1 - slot)

        _wait_rows(hs_hbm, xg[slot], nv_ref[s], gsem.at[slot])
        xb = jnp.concatenate([xg[slot][pl.ds(q, MOE_BLK, stride=ROW_TILE), :]
                              for q in range(ROW_TILE)], axis=-1).astype(BF16)
        g = jnp.dot(xb, wg_s[...], preferred_element_type=F32)
        u = jnp.dot(xb, wu_s[...], preferred_element_type=F32)
        hid = (g * jax.nn.sigmoid(g)) * u
        out = jnp.dot(hid.astype(BF16), wd_s[...], preferred_element_type=F32)
        for q in range(ROW_TILE):
            y_ref[pl.ds(q, MOE_BLK, stride=ROW_TILE), :] = out[:, q * LANES:(q + 1) * LANES]

    for slot in range(2):
        @pl.when(jnp.logical_and(s < nbu, s % 2 == slot))
        def _(slot=slot):
            step(slot)

    @pl.when(s >= nbu)
    def _():
        y_ref[...] = jnp.zeros_like(y_ref)


def _experts(blk_e, nb_used, blk_valid, rs, re, rsrc, rlen, rdst, hs, w_gate, w_up, w_down, layer):
    nb = blk_e.shape[0]
    wmap = lambda i, be, *_: (layer, be[i], 0, 0)
    buf = pltpu.VMEM((MOE_BLK * ROW_TILE, LANES), F32)
    return pl.pallas_call(
        _expert_kernel,
        grid_spec=pltpu.PrefetchScalarGridSpec(
            num_scalar_prefetch=8,
            grid=(nb,),
            in_specs=[
                pl.BlockSpec(memory_space=pl.ANY),
                pl.BlockSpec((None, None, D_MODEL, D_EXPERT), wmap),
                pl.BlockSpec((None, None, D_MODEL, D_EXPERT), wmap),
                pl.BlockSpec((None, None, D_EXPERT, D_MODEL), wmap),
            ],
            out_specs=pl.BlockSpec((MOE_BLK * ROW_TILE, LANES), lambda i, *_: (i, 0)),
            scratch_shapes=[buf, buf,
                            pltpu.VMEM((D_MODEL, D_EXPERT), BF16),
                            pltpu.VMEM((D_MODEL, D_EXPERT), BF16),
                            pltpu.VMEM((D_EXPERT, D_MODEL), BF16),
                            pltpu.SemaphoreType.DMA((2,))],
        ),
        out_shape=jax.ShapeDtypeStruct((nb * MOE_BLK * ROW_TILE, LANES), F32),
        compiler_params=_cparams(("arbitrary",)),
        name="moe_experts",
    )(blk_e, nb_used, blk_valid, rs, re, rsrc, rlen, rdst, hs, w_gate, w_up, w_down)


def _combine_kernel(meta_ref, x_ref, pos_ref, gate_ref, mod_ref, gfin_ref, ys_hbm, out_ref,
                    st0, st1, sem, *, final, nt):
    i = pl.program_id(0)
    stage = (st0, st1)

    def fetch(tile, slot):
        base = tile * LANES

        def runs(n_max):
            for e in range(N_EXPERTS):
                _copy_rows(ys_hbm, meta_ref[base + e], stage[slot],
                           meta_ref[base + 2 * N_EXPERTS + e], meta_ref[base + N_EXPERTS + e],
                           n_max, sem.at[slot])

        longest = meta_ref[base + 3 * N_EXPERTS]

        @pl.when(longest <= SHORT_RUN)
        def _():
            runs(SHORT_RUN)

        @pl.when(longest > SHORT_RUN)
        def _():
            runs(TM)

    def step(slot):
        @pl.when(i == 0)
        def _():
            fetch(0, slot)

        @pl.when(i + 1 < nt)
        def _():
            fetch(i + 1, 1 - slot)

        _wait_rows(ys_hbm, stage[slot], 2 * TM, sem.at[slot])
        rows = jnp.concatenate([stage[slot][pl.ds(q, 2 * TM, stride=ROW_TILE), :]
                                for q in range(ROW_TILE)], axis=-1)
        pos = pos_ref[...].astype(F32)
        gate = gate_ref[...]
        slot_id = lax.broadcasted_iota(jnp.int32, (2 * TM, TM), 0).astype(F32)
        gmat_t = (jnp.where(slot_id == pos[0:1, :], gate[0:1, :], 0.0)
                  + jnp.where(slot_id == pos[1:2, :], gate[1:2, :], 0.0))
        g_hi = gmat_t.astype(BF16)
        g_lo = (gmat_t - g_hi.astype(F32)).astype(BF16)
        r_hi = rows.astype(BF16)
        r_lo = (rows - r_hi.astype(F32)).astype(BF16)
        tn = (((0,), (0,)), ((), ()))
        y = (lax.dot_general(g_hi, r_hi, tn, preferred_element_type=F32)
             + (lax.dot_general(g_lo, r_hi, tn, preferred_element_type=F32)
                + lax.dot_general(g_hi, r_lo, tn, preferred_element_type=F32)))
        xn = x_ref[...] + mod_ref[5:6, :] * y
        if final:
            ms = jnp.mean(xn * xn, axis=-1, keepdims=True)
            xn = xn * lax.rsqrt(ms + EPS) * gfin_ref[...]
        out_ref[...] = xn

    for slot in range(2):
        @pl.when(i % 2 == slot)
        def _(slot=slot):
            step(slot)


def _combine(meta, xn, pos, gate, mod_l, g_final, ys, nt, n_x_tiles, final):
    sel = lambda i, m: (jnp.where(i == n_x_tiles, 1, 0), 0, 0)
    row = lambda i, m: (i, 0)
    stage = pltpu.VMEM((2 * TM * ROW_TILE, LANES), F32)
    return pl.pallas_call(
        functools.partial(_combine_kernel, final=final, nt=nt),
        grid_spec=pltpu.PrefetchScalarGridSpec(
            num_scalar_prefetch=1,
            grid=(nt,),
            in_specs=[
                pl.BlockSpec((TM, D_MODEL), row),
                pl.BlockSpec((2, TM), lambda i, m: (0, i)),
                pl.BlockSpec((2, TM), lambda i, m: (0, i)),
                pl.BlockSpec((None, N_MOD, D_MODEL), sel),
                pl.BlockSpec((1, D_MODEL), lambda i, m: (0, 0)),
                pl.BlockSpec(memory_space=pl.ANY),
            ],
            out_specs=pl.BlockSpec((TM, D_MODEL), row),
            scratch_shapes=[stage, stage, pltpu.SemaphoreType.DMA((2,))],
        ),
        out_shape=jax.ShapeDtypeStruct((nt * TM, D_MODEL), F32),
        compiler_params=_cparams(("arbitrary",)),
        name="moe_combine",
    )(meta, xn, pos, gate, mod_l, g_final, ys)


def _moe(hs, pos, tcnt, gate, xn, mod_l, g_final, w_gate, w_up, w_down, layer, nt, n_x_tiles,
         final):
    t = nt * TM
    i32 = jnp.int32
    n = tcnt.reshape(nt, N_EXPERTS)
    off = jnp.cumsum(n, axis=1) - n
    base = jnp.cumsum(n, axis=0) - n
    counts = jnp.sum(n, axis=0)
    padded = (counts + MOE_BLK - 1) // MOE_BLK * MOE_BLK
    pend = jnp.cumsum(padded)
    pstart = pend - padded
    nb = (2 * t + N_EXPERTS * (MOE_BLK - 1)) // MOE_BLK + 1
    nb_used = (pend[-1] // MOE_BLK).astype(i32)
    blk_lo = jnp.arange(nb, dtype=i32) * MOE_BLK
    blk_e = jnp.minimum(jnp.sum(pend[None, :] <= jnp.minimum(blk_lo, pend[-1] - MOE_BLK)[:, None],
                                axis=1), N_EXPERTS - 1).astype(i32)
    blk_valid = jnp.clip((pstart + counts)[blk_e] - blk_lo, 0, MOE_BLK).astype(i32)
    gsrc = (pstart[None, :] + base).astype(i32)
    rdst = gsrc.T.reshape(-1)
    rlen = n.T.reshape(-1).astype(i32)
    rsrc = (jnp.arange(nt, dtype=i32)[:, None] * (2 * TM) + off).T.reshape(-1).astype(i32)
    e_dst = gsrc.T[blk_e]
    e_end = e_dst + n.T[blk_e]
    rs = (blk_e * nt + jnp.sum(e_end <= blk_lo[:, None], axis=1)).astype(i32)
    re = (blk_e * nt + jnp.sum(e_dst < (blk_lo + MOE_BLK)[:, None], axis=1)).astype(i32)
    ys = _experts(blk_e, nb_used.reshape(1), blk_valid, rs, re, rsrc, rlen, rdst, hs,
                  w_gate, w_up, w_down, layer)
    longest = jnp.max(n, axis=1, keepdims=True).astype(i32)
    meta = jnp.concatenate([gsrc, n.astype(i32), off.astype(i32), longest,
                            jnp.zeros((nt, LANES - 3 * N_EXPERTS - 1), i32)], axis=1).reshape(-1)
    return _combine(meta, xn, pos, gate, mod_l, g_final, ys, nt, n_x_tiles, final)


def _cmul(ar, ai, br, bi):
    return ar * br - ai * bi, ar * bi + ai * br


def _expand_rows(v, rep_ref, mask_ref):
    return jnp.dot(v.astype(BF16), rep_ref[...], preferred_element_type=F32) * mask_ref[...]


def _expand_cols(e, rept_ref, maskt_ref):
    out = lax.dot_general(rept_ref[...], e.astype(BF16), (((1,), (1,)), ((), ())),
                          preferred_element_type=F32)
    return out * maskt_ref[...]


def _ssm_prep_kernel(are_ref, aim_ref, ldt_ref, btr_ref, bti_ref, cr_ref, ci_ref,
                     rep_ref, mask_ref, rept_ref, maskt_ref, ms_ref, mcat_ref, ecat):
    lag_blocks = []
    for d in range(2):
        a_re = jnp.minimum(are_ref[d], -1e-4)
        a_im = aim_ref[d]
        dt = jnp.exp(ldt_ref[d])
        mag = jnp.exp(a_re * dt)
        lr = mag * jnp.cos(a_im * dt)
        li = mag * jnp.sin(a_im * dt)
        den = a_re * a_re + a_im * a_im
        k_re = ((lr - 1.0) * a_re + li * a_im) / den
        k_im = (li * a_re - (lr - 1.0) * a_im) / den
        bbr, bbi = _cmul(k_re, k_im, btr_ref[d], bti_ref[d])
        cr, ci = cr_ref[d], ci_ref[d]
        pr, pi = jnp.ones_like(lr), jnp.zeros_like(lr)
        v0 = None
        for n in range(CH + 1):
            er, ei = _cmul(pr, pi, cr, ci)
            et = jnp.concatenate([_expand_cols(er, rept_ref, maskt_ref),
                                  _expand_cols(-ei, rept_ref, maskt_ref)], axis=0).astype(BF16)
            if n < CH:
                vr, vi = _cmul(pr, pi, bbr, bbi)
                v = jnp.concatenate([_expand_rows(vr, rep_ref, mask_ref),
                                     _expand_rows(vi, rep_ref, mask_ref)], axis=1).astype(BF16)
                if n == 0:
                    v0 = v
                sigma = CH - 1 - n if d == 0 else n
                ms_ref[d, sigma * LANES:(sigma + 1) * LANES, :] = v
                ecat[:, n * LANES:(n + 1) * LANES] = et
            if n >= 1:
                tau = n - 1 if d == 0 else CH - n
                mcat_ref[CW + d * SW:CW + (d + 1) * SW, tau * LANES:(tau + 1) * LANES] = et
            pr, pi = _cmul(pr, pi, lr, li)
        lag_blocks.append(jnp.dot(v0, ecat[...], preferred_element_type=F32))
    kf, kb = lag_blocks
    for s in range(CH):
        for t in range(CH):
            if t > s:
                blk = kf[:, (t - s) * LANES:(t - s + 1) * LANES]
            elif t < s:
                blk = kb[:, (s - t) * LANES:(s - t + 1) * LANES]
            else:
                blk = kf[:, :LANES] + kb[:, :LANES]
            mcat_ref[s * LANES:(s + 1) * LANES, t * LANES:(t + 1) * LANES] = blk.astype(BF16)


def _ssm_operators(a_re, a_im, log_dt, b_re, b_im, c_re, c_im):
    depth = a_re.shape[0]
    rows = SSM_GROUPS * SSM_GROUP
    p = SSM_STATE
    rep_rows = lambda v: jnp.repeat(v, SSM_GROUP, axis=2)
    are = rep_rows(a_re)
    aim = rep_rows(a_im)
    ldt = jnp.broadcast_to(rep_rows(log_dt[..., None]), are.shape)
    btr = jnp.swapaxes(b_re, -1, -2).reshape(depth, 2, rows, p)
    bti = jnp.swapaxes(b_im, -1, -2).reshape(depth, 2, rows, p)
    cr = c_re.reshape(depth, 2, rows, p)
    ci = c_im.reshape(depth, 2, rows, p)
    rep = np.tile(np.eye(p, dtype=np.float32), (1, GB))
    mask = np.kron(np.eye(GB, dtype=np.float32), np.ones((SSM_GROUP, p), np.float32))
    par = pl.BlockSpec((None, 2, LANES, p), lambda l, b: (l, 0, b, 0))
    fix = lambda shape: pl.BlockSpec(shape, lambda l, b: (0, 0))
    return pl.pallas_call(
        _ssm_prep_kernel,
        grid=(depth, NGB),
        in_specs=[par] * 7 + [fix((p, GB * p)), fix((LANES, GB * p)),
                              fix((GB * p, p)), fix((GB * p, LANES))],
        out_specs=[
            pl.BlockSpec((None, None, 2, CW, SW), lambda l, b: (l, b, 0, 0, 0)),
            pl.BlockSpec((None, None, CW + 2 * SW, CW), lambda l, b: (l, b, 0, 0)),
        ],
        out_shape=[
            jax.ShapeDtypeStruct((depth, NGB, 2, CW, SW), BF16),
            jax.ShapeDtypeStruct((depth, NGB, CW + 2 * SW, CW), BF16),
        ],
        scratch_shapes=[pltpu.VMEM((SW, CW), BF16)],
        compiler_params=_cparams(("arbitrary", "arbitrary")),
        name="ssm_operators",
    )(are, aim, ldt, btr, bti, cr, ci, jnp.asarray(rep, BF16), jnp.asarray(mask),
      jnp.asarray(rep.T, BF16), jnp.asarray(mask.T))


def _scan_tables(a_re, a_im, log_dt):
    depth = a_re.shape[0]
    a_re = jnp.minimum(a_re, -1e-4)
    dt = jnp.exp(log_dt)[..., None]
    mag = jnp.exp(a_re * dt)
    lr = mag * jnp.cos(a_im * dt)
    li = mag * jnp.sin(a_im * dt)
    ar, ai = lr, li
    for _ in range(CH - 1):
        ar, ai = _cmul(ar, ai, lr, li)
    apw = [(ar, ai)]
    for _ in range(SUBLANES - 1):
        apw.append(_cmul(apw[-1][0], apw[-1][1], ar, ai))
    apr = jnp.stack([q[0] for q in apw], axis=1)
    api = jnp.stack([q[1] for q in apw], axis=1)
    rows = np.arange(SUBLANES)
    tabs = []
    for d in range(2):
        consts = []
        for k in (1, 2, 4):
            keep = (rows >= k) if d == 0 else (rows < SUBLANES - k)
            keep = jnp.asarray(keep.astype(np.float32))[None, :, None, None]
            consts.append(keep * apr[:, k - 1:k, d])
            consts.append(keep * api[:, k - 1:k, d])
        order = rows if d == 0 else SUBLANES - 1 - rows
        consts.append(apr[:, order, d])
        consts.append(api[:, order, d])
        tabs.append(jnp.stack(consts, axis=1))
    tab = jnp.stack(tabs, axis=1).reshape(depth, 2, 8 * SUBLANES, NGB, GB * SSM_STATE)
    return tab.transpose(0, 1, 3, 2, 4)


def _pair_blockdiag(w):
    g, c, _ = w.shape
    z = jnp.zeros((g // 2, c, c), w.dtype)
    top = jnp.concatenate([w[0::2], z], axis=2)
    bot = jnp.concatenate([z, w[1::2]], axis=2)
    return jnp.concatenate([top, bot], axis=1)


def _pool_constants(seg):
    t = np.arange(TM)
    s0 = t // seg * seg
    band = np.zeros((len(POOL_WINDOWS), TM, TM), np.float32)
    icnt = np.zeros((TM, POOL_W), np.float32)
    for g, w in enumerate(POOL_WINDOWS):
        lo = np.maximum(t - w // 2, s0)
        hi = np.minimum(t + w // 2, s0 + seg)
        band[g] = (t[None, :] >= lo[:, None]) & (t[None, :] < hi[:, None])
        icnt[:, g * POOL_GW:(g + 1) * POOL_GW] = (1.0 / (hi - lo).astype(np.float32))[:, None]
    return band, icnt


def kernel(x, c, ctx, c_ctx, w_mod, b_mod, g_mix, g_ffn, w_in, w_out, w_pool, s_pool,
           ssm_a_re, ssm_a_im, ssm_log_dt, ssm_b_re, ssm_b_im, ssm_c_re, ssm_c_im, ssm_d,
           w_glu, b_glu, w_router, b_router, w_gate, w_up, w_down, g_final):
    bsz, seq, d = x.shape
    ctx_len = ctx.shape[1]
    depth = w_mod.shape[0]
    assert bsz == 1 and d == D_MODEL and ctx_len == TM
    assert seq % (SR * CH) == 0 and seq % GRID_W == 0 and seq % TI == 0
    n_xt = seq // TM
    n_st = seq // (SR * CH)
    ctx_rows = ctx_len // CH

    cvec = jnp.concatenate([c.reshape(1, d), c_ctx.reshape(1, d),
                            jnp.zeros((SUBLANES - 2, d), F32)], axis=0)
    mod = _modulation(cvec, w_mod, b_mod).reshape(depth, SUBLANES, N_MOD, d)

    band_x, icnt_x = _pool_constants(GRID_W)
    band_c, icnt_c = _pool_constants(ctx_len)
    band = jnp.asarray(np.stack([band_x, band_c]), BF16)
    icnt = jnp.asarray(np.stack([icnt_x, icnt_c]), F32)
    tri = jnp.asarray(np.triu(np.ones((TM, TM), np.float32), 1), BF16)
    wr_hi = w_router.astype(BF16)
    wr_split = jnp.concatenate([wr_hi, (w_router - wr_hi.astype(F32)).astype(BF16)], axis=1).T

    ms, mcat = _ssm_operators(ssm_a_re, ssm_a_im, ssm_log_dt, ssm_b_re, ssm_b_im,
                              ssm_c_re, ssm_c_im)
    tab = _scan_tables(ssm_a_re, ssm_a_im, ssm_log_dt)

    xa, ca, ctx_block = x[0], ctx[0], 0
    for l in range(depth):
        last = l == depth - 1
        mod_l = mod[l, :2]
        ux, uv_ext = _inproj(xa, ca, ctx_block, mod_l, g_mix[l].reshape(1, d),
                             w_in[l].astype(BF16), n_xt)
        hf, gb = _ssm_states(uv_ext, ms, tab, l, n_st, ctx_rows)
        y_ssm = _ssm_readout(uv_ext, hf, gb, mcat, l)
        nt = n_xt if last else n_xt + 1
        xn, hs, pos, gate, tcnt = _mixout(
            xa, ca, ctx_block, ux, y_ssm, mod_l, band, icnt, _pair_blockdiag(w_pool[l]).astype(BF16),
            s_pool[l].reshape(1, -1), ssm_d[l].reshape(1, -1), w_glu[l].astype(BF16),
            b_glu[l].reshape(1, -1), w_out[l].astype(BF16), g_ffn[l].reshape(1, d), wr_split,
            b_router.reshape(-1, 1), tri, nt, n_xt)
        tok = _moe(hs, pos, tcnt, gate, xn, mod_l, g_final.reshape(1, d), w_gate, w_up, w_down,
                   l, nt, n_xt, last)
        xa, ca, ctx_block = tok, tok, n_xt
    return tok.reshape(bsz, seq, d)
```

```python
import functools

import numpy as np
import jax
import jax.numpy as jnp
from jax import lax
from jax.experimental import pallas as pl
from jax.experimental.pallas import tpu as pltpu

F32 = jnp.float32
BF16 = jnp.bfloat16

D_MODEL = 1024
POOL_W = 512
SSM_W = 512
POOL_WINDOWS = (2, 4, 8, 16)
POOL_GW = 128
SSM_GROUP = 16
SSM_GROUPS = 32
SSM_STATE = 64
N_EXPERTS = 32
N_EXPERT_GROUPS = 4
EXPERTS_PER_GROUP = 8
D_EXPERT = 512
GRID_W = 64
EPS = 1e-6
N_MOD = 6

LANES = 128
SUBLANES = 8
TM = 256
TI = 512
CH = 8
GB = 8
NGB = SSM_GROUPS // GB
CW = CH * LANES
SW = 2 * GB * SSM_STATE
SR = 256
MOE_BLK = 512
ROW_TILE = D_MODEL // LANES
SHORT_RUN = 31
EXPERT_SPLIT = 2
VMEM_LIMIT = 48 * 1024 * 1024


def _cparams(sem):
    return pltpu.CompilerParams(dimension_semantics=sem, vmem_limit_bytes=VMEM_LIMIT)


def _rmsnorm_mod(x, g, shift, scale):
    ms = jnp.mean(x * x, axis=-1, keepdims=True)
    y = x * lax.rsqrt(ms + EPS) * g
    return y * (1.0 + scale) + shift


def _mod_kernel(c_ref, w_ref, b_ref, o_ref):
    c = c_ref[...]
    a = c * jax.nn.sigmoid(c)
    o_ref[...] = jnp.dot(a, w_ref[...], preferred_element_type=F32,
                         precision=lax.Precision.HIGHEST) + b_ref[...]


def _modulation(cvec, w_mod, b_mod):
    depth, d, n = w_mod.shape
    tn = 1536
    return pl.pallas_call(
        _mod_kernel,
        grid=(depth, n // tn),
        in_specs=[
            pl.BlockSpec((SUBLANES, d), lambda l, j: (0, 0)),
            pl.BlockSpec((None, d, tn), lambda l, j: (l, 0, j)),
            pl.BlockSpec((None, 1, tn), lambda l, j: (l, 0, j)),
        ],
        out_specs=pl.BlockSpec((None, SUBLANES, tn), lambda l, j: (l, 0, j)),
        out_shape=jax.ShapeDtypeStruct((depth, SUBLANES, n), F32),
        compiler_params=_cparams(("arbitrary", "arbitrary")),
        name="modulation",
    )(cvec, w_mod, b_mod.reshape(depth, 1, n))


def _token_tile(x_ref, c_ref, n_x_tiles):
    return jnp.where(pl.program_id(0) == n_x_tiles, c_ref[...], x_ref[...])


def _inproj_kernel(x_ref, c_ref, mod_ref, g_ref, w_ref, ux_ref, uv_ref, scr, *, n_big):
    i = pl.program_id(0)

    def project(x, rows):
        h = _rmsnorm_mod(x, g_ref[...], mod_ref[0:1, :], mod_ref[1:2, :])
        ux = jnp.dot(h.astype(BF16), w_ref[...], preferred_element_type=F32)
        ux_ref[0:rows, :] = ux
        crow = rows // CH
        for b in range(NGB):
            scr[b, 0:rows, :] = ux[:, POOL_W + LANES * b:POOL_W + LANES * (b + 1)]
            for t in range(CH):
                piece = scr[b, pl.ds(t, crow, stride=CH), :]
                uv_ref[0:crow, pl.ds((b * CH + t) * LANES, LANES)] = piece.astype(BF16)

    @pl.when(i < n_big)
    def _():
        project(x_ref[...], TI)

    @pl.when(i == n_big)
    def _():
        project(c_ref[...], TM)
        ux_ref[TM:TI, :] = jnp.zeros((TI - TM, D_MODEL), F32)
        uv_ref[TM // CH:TI // CH, :] = jnp.zeros(((TI - TM) // CH, NGB * CW), BF16)

    @pl.when(i > n_big)
    def _():
        uv_ref[...] = jnp.zeros_like(uv_ref)


def _token_specs(n_x_tiles, ctx_block):
    return [pl.BlockSpec((TM, D_MODEL), lambda i: (jnp.minimum(i, n_x_tiles - 1), 0)),
            pl.BlockSpec((TM, D_MODEL), lambda i: (ctx_block, 0))]


def _inproj(xa, ca, ctx_block, mod_l, g, w_bf, n_x_tiles):
    n_big = n_x_tiles * TM // TI
    n_steps = n_big + SR // (TI // CH)
    sel = lambda i: (jnp.where(i >= n_big, 1, 0), 0, 0)
    return pl.pallas_call(
        functools.partial(_inproj_kernel, n_big=n_big),
        grid=(n_steps,),
        in_specs=[
            pl.BlockSpec((TI, D_MODEL), lambda i: (jnp.minimum(i, n_big - 1), 0)),
            pl.BlockSpec((TM, D_MODEL), lambda i: (ctx_block, 0)),
            pl.BlockSpec((None, N_MOD, D_MODEL), sel),
            pl.BlockSpec((1, D_MODEL), lambda i: (0, 0)),
            pl.BlockSpec((D_MODEL, D_MODEL), lambda i: (0, 0)),
        ],
        out_specs=[
            pl.BlockSpec((TI, D_MODEL), lambda i: (jnp.minimum(i, n_big), 0)),
            pl.BlockSpec((TI // CH, NGB * CW), lambda i: (i, 0)),
        ],
        out_shape=[
            jax.ShapeDtypeStruct(((n_x_tiles + 1) * TM, D_MODEL), F32),
            jax.ShapeDtypeStruct((n_steps * TI // CH, NGB * CW), BF16),
        ],
        scratch_shapes=[pltpu.VMEM((NGB, TI, LANES), F32)],
        compiler_params=_cparams(("arbitrary",)),
        name="mixer_inproj",
    )(xa, ca, mod_l, g, w_bf)


def _chunk_scan(s_ref, tab_ref, carry_ref, out_ref, nblk, reverse):
    half = SW // 2
    ncol = half // LANES
    row = lax.broadcasted_iota(jnp.int32, (SUBLANES, LANES), 0)
    edge = (row == SUBLANES - 1) if reverse else (row == 0)
    last = 0 if reverse else SUBLANES - 1

    def sub_block(r0, carry):
        outs_r, outs_i, new_carry = [], [], []
        for j in range(ncol):
            cr, ci = carry[2 * j], carry[2 * j + 1]
            lre = pl.ds(LANES * j, LANES)
            lim = pl.ds(half + LANES * j, LANES)
            zr = s_ref[pl.ds(r0, SUBLANES), lre]
            zi = s_ref[pl.ds(r0, SUBLANES), lim]
            for q, k in enumerate((1, 2, 4)):
                ar = tab_ref[pl.ds(16 * q, SUBLANES), lre]
                ai = tab_ref[pl.ds(16 * q + 8, SUBLANES), lre]
                sh = SUBLANES - k if reverse else k
                sr = pltpu.roll(zr, sh, axis=0)
                si = pltpu.roll(zi, sh, axis=0)
                zr, zi = zr + ar * sr - ai * si, zi + ar * si + ai * sr
            pr = tab_ref[pl.ds(48, SUBLANES), lre]
            pi = tab_ref[pl.ds(56, SUBLANES), lre]
            zr, zi = zr + pr * cr - pi * ci, zi + pr * ci + pi * cr
            sh1 = SUBLANES - 1 if reverse else 1
            outs_r.append(jnp.where(edge, cr, pltpu.roll(zr, sh1, axis=0)))
            outs_i.append(jnp.where(edge, ci, pltpu.roll(zi, sh1, axis=0)))
            new_carry.append(jnp.broadcast_to(zr[last:last + 1, :], (SUBLANES, LANES)))
            new_carry.append(jnp.broadcast_to(zi[last:last + 1, :], (SUBLANES, LANES)))
        return outs_r, outs_i, tuple(new_carry)

    def body(it, carry):
        bi = (nblk - 1 - it) if reverse else it
        r0 = pl.multiple_of(bi * 2 * SUBLANES, 2 * SUBLANES)
        if reverse:
            hi_r, hi_i, carry = sub_block(r0 + SUBLANES, carry)
            lo_r, lo_i, carry = sub_block(r0, carry)
        else:
            lo_r, lo_i, carry = sub_block(r0, carry)
            hi_r, hi_i, carry = sub_block(r0 + SUBLANES, carry)
        for j in range(ncol):
            out_ref[pl.ds(r0, 2 * SUBLANES), pl.ds(LANES * j, LANES)] = (
                jnp.concatenate([lo_r[j], hi_r[j]], axis=0).astype(BF16))
            out_ref[pl.ds(r0, 2 * SUBLANES), pl.ds(half + LANES * j, LANES)] = (
                jnp.concatenate([lo_i[j], hi_i[j]], axis=0).astype(BF16))
        return carry

    init = tuple(carry_ref[:, pl.ds(LANES * c, LANES)] for c in range(2 * ncol))
    final = lax.fori_loop(0, nblk, body, init)
    for c in range(2 * ncol):
        carry_ref[:, pl.ds(LANES * c, LANES)] = final[c]


def _ssm_state_kernel(uf_ref, ub_ref, msf_ref, msb_ref, tf_ref, tb_ref, hf_ref, gb_ref,
                      sf, sb, cf, cb, *, ctx_rows):
    step = pl.program_id(1)

    @pl.when(step == 0)
    def _():
        cf[...] = jnp.zeros_like(cf)
        cb[...] = jnp.zeros_like(cb)
        hf_ref[...] = jnp.zeros_like(hf_ref)
        gb_ref[...] = jnp.zeros_like(gb_ref)

    sf[...] = jnp.dot(uf_ref[...], msf_ref[...], preferred_element_type=F32)
    sb[...] = jnp.dot(ub_ref[...], msb_ref[...], preferred_element_type=F32)
    nblk = jnp.where(step == 0, ctx_rows // (2 * SUBLANES), SR // (2 * SUBLANES))
    _chunk_scan(sf, tf_ref, cf, hf_ref, nblk, reverse=False)
    _chunk_scan(sb, tb_ref, cb, gb_ref, nblk, reverse=True)


def _ssm_states(uv_ext, ms, tab, layer, n_xt, ctx_rows):
    rows = uv_ext.shape[0]
    fwd = lambda b, s: (jnp.where(s == 0, n_xt, s - 1), b)
    bwd = lambda b, s: (jnp.where(s == 0, n_xt, n_xt - s), b)
    return pl.pallas_call(
        functools.partial(_ssm_state_kernel, ctx_rows=ctx_rows),
        grid=(NGB, n_xt + 1),
        in_specs=[
            pl.BlockSpec((SR, CW), fwd),
            pl.BlockSpec((SR, CW), bwd),
            pl.BlockSpec((None, None, None, CW, SW), lambda b, s: (layer, b, 0, 0, 0)),
            pl.BlockSpec((None, None, None, CW, SW), lambda b, s: (layer, b, 1, 0, 0)),
            pl.BlockSpec((None, None, None, 8 * SUBLANES, SW // 2),
                         lambda b, s: (layer, 0, b, 0, 0)),
            pl.BlockSpec((None, None, None, 8 * SUBLANES, SW // 2),
                         lambda b, s: (layer, 1, b, 0, 0)),
        ],
        out_specs=[pl.BlockSpec((SR, SW), fwd), pl.BlockSpec((SR, SW), bwd)],
        out_shape=[jax.ShapeDtypeStruct((rows, NGB * SW), BF16)] * 2,
        scratch_shapes=[pltpu.VMEM((SR, SW), F32), pltpu.VMEM((SR, SW), F32),
                        pltpu.VMEM((SUBLANES, SW), F32), pltpu.VMEM((SUBLANES, SW), F32)],
        compiler_params=_cparams(("arbitrary", "arbitrary")),
        name="ssm_states",
    )(uv_ext, uv_ext, ms, ms, tab, tab)


def _ssm_out_kernel(u_ref, hf_ref, gb_ref, m_ref, y_ref):
    res = jnp.dot(u_ref[...], m_ref[0:CW, :], preferred_element_type=F32)
    res += jnp.dot(hf_ref[...], m_ref[CW:CW + SW, :], preferred_element_type=F32)
    res += jnp.dot(gb_ref[...], m_ref[CW + SW:CW + 2 * SW, :], preferred_element_type=F32)
    for t in range(CH):
        y_ref[pl.ds(t, SR, stride=CH), :] = res[:, t * LANES:(t + 1) * LANES]


def _ssm_readout(uv_ext, hf, gb, mcat, layer):
    rows = uv_ext.shape[0]
    nt = rows // SR
    blk = lambda b, i: (i, b)
    return pl.pallas_call(
        _ssm_out_kernel,
        grid=(NGB, nt),
        in_specs=[
            pl.BlockSpec((SR, CW), blk),
            pl.BlockSpec((SR, SW), blk),
            pl.BlockSpec((SR, SW), blk),
            pl.BlockSpec((None, None, CW + 2 * SW, CW), lambda b, i: (layer, b, 0, 0)),
        ],
        out_specs=pl.BlockSpec((SR * CH, LANES), blk),
        out_shape=jax.ShapeDtypeStruct((rows * CH, SSM_W), F32),
        compiler_params=_cparams(("arbitrary", "arbitrary")),
        name="ssm_readout",
    )(uv_ext, hf, gb, mcat)


def _gelu_tanh(x):
    return 0.5 * x * (1.0 + jnp.tanh(0.7978845608028654 * (x + 0.044715 * x * x * x)))


def _route(s, b_col):
    sel = s + b_col
    neg = jnp.float32(-jnp.inf)

    def top2(vals):
        row = lax.broadcasted_iota(jnp.int32, vals.shape, 0).astype(F32)
        big = jnp.float32(vals.shape[0])
        m1 = jnp.max(vals, axis=0, keepdims=True)
        i1 = jnp.min(jnp.where(vals == m1, row, big), axis=0, keepdims=True)
        rest = jnp.where(row == i1, neg, vals)
        m2 = jnp.max(rest, axis=0, keepdims=True)
        i2 = jnp.min(jnp.where(rest == m2, row, big), axis=0, keepdims=True)
        return m1, i1, m2, i2

    best = None
    gidx = None
    for g in range(N_EXPERT_GROUPS):
        m1, _, m2, _ = top2(sel[g * EXPERTS_PER_GROUP:(g + 1) * EXPERTS_PER_GROUP])
        score = m1 + m2
        if best is None:
            best, gidx = score, jnp.zeros(score.shape, jnp.int32)
        else:
            upd = score > best
            best = jnp.where(upd, score, best)
            gidx = jnp.where(upd, g, gidx)
    row_i = lax.broadcasted_iota(jnp.int32, s.shape, 0)
    _, e1, _, e2 = top2(jnp.where(row_i // EXPERTS_PER_GROUP == gidx, sel, neg))
    row = row_i.astype(F32)
    w1 = jnp.sum(jnp.where(row == e1, s, 0.0), axis=0, keepdims=True)
    w2 = jnp.sum(jnp.where(row == e2, s, 0.0), axis=0, keepdims=True)
    tot = w1 + w2
    return e1.astype(jnp.int32), e2.astype(jnp.int32), w1 / tot, w2 / tot


def _mixout_kernel(x_ref, c_ref, ux_ref, ys_ref, mod_ref, band_ref, icnt_ref, wp_ref, sp_ref, d_ref,
                   wglu_ref, bglu_ref, wout_ref, gffn_ref, wrt_ref, br_ref, tri_ref,
                   xo_ref, hs_ref, pos_ref, gate_ref, cnt_ref, *, n_x_tiles):
    ux = ux_ref[...]
    parts = []
    for g in range(len(POOL_WINDOWS)):
        ug = ux[:, g * POOL_GW:(g + 1) * POOL_GW]
        hi = ug.astype(BF16)
        lo = (ug - hi.astype(F32)).astype(BF16)
        band = band_ref[g]
        both = jnp.dot(band, jnp.concatenate([hi, lo], axis=-1), preferred_element_type=F32)
        wsum = both[:, :POOL_GW] + both[:, POOL_GW:]
        p = wsum * icnt_ref[:, g * POOL_GW:(g + 1) * POOL_GW] - ug
        parts.append(p.astype(BF16))
    pool = jnp.concatenate(
        [jnp.dot(jnp.concatenate(parts[2 * k:2 * k + 2], axis=-1), wp_ref[k],
                 preferred_element_type=F32) for k in range(len(POOL_WINDOWS) // 2)],
        axis=-1) * sp_ref[...]

    y = ys_ref[...] + d_ref[...] * ux[:, POOL_W:]
    y = _gelu_tanh(y)
    z = jnp.dot(y.astype(BF16), wglu_ref[...], preferred_element_type=F32) + bglu_ref[...]
    glu = y * jax.nn.sigmoid(z)

    cat = jnp.concatenate([pool, glu], axis=-1).astype(BF16)
    o = jnp.dot(cat, wout_ref[...], preferred_element_type=F32)
    xn = _token_tile(x_ref, c_ref, n_x_tiles) + mod_ref[2:3, :] * o
    xo_ref[...] = xn

    h2 = _rmsnorm_mod(xn, gffn_ref[...], mod_ref[3:4, :], mod_ref[4:5, :])
    h_hi = h2.astype(BF16)
    h_lo = (h2 - h_hi.astype(F32)).astype(BF16)
    wrt = wrt_ref[...]
    nt_dims = (((1,), (1,)), ((), ()))
    q_hi = lax.dot_general(wrt, h_hi, nt_dims, preferred_element_type=F32)
    q_lo = lax.dot_general(wrt[:N_EXPERTS], h_lo, nt_dims, preferred_element_type=F32)
    logits = q_hi[:N_EXPERTS] + (q_hi[N_EXPERTS:] + q_lo)
    e1, e2, g1, g2 = _route(jax.nn.sigmoid(logits), br_ref[...])
    gate_ref[...] = jnp.concatenate([g1, g2], axis=0)

    row_i = lax.broadcasted_iota(jnp.int32, (N_EXPERTS, TM), 0)
    oh0 = jnp.where(row_i == e1, 1.0, 0.0)
    oh1 = jnp.where(row_i == e2, 1.0, 0.0)
    tri = tri_ref[...]
    before0 = jnp.dot(oh0.astype(BF16), tri, preferred_element_type=F32)
    before1 = jnp.dot(oh1.astype(BF16), tri, preferred_element_type=F32)
    tot0 = jnp.sum(oh0, axis=1, keepdims=True)
    tot1 = jnp.sum(oh1, axis=1, keepdims=True)
    smaller = jnp.where(row_i > e1, 1.0, 0.0) + jnp.where(row_i > e2, 1.0, 0.0)
    off = jnp.sum(smaller, axis=1, keepdims=True)
    p0 = jnp.sum(oh0 * (off + before0), axis=0, keepdims=True)
    p1 = jnp.sum(oh1 * (off + tot0 + before1), axis=0, keepdims=True)
    pos_ref[...] = jnp.concatenate([p0, p1], axis=0).astype(jnp.int32)
    cnt_ref[...] = (tot0 + tot1).astype(jnp.int32)

    slot = lax.broadcasted_iota(jnp.int32, (2 * TM, TM), 0).astype(F32)
    perm = jnp.where(jnp.logical_or(slot == p0, slot == p1), 1.0, 0.0).astype(BF16)
    hs = jnp.dot(perm, h2.astype(BF16), preferred_element_type=F32)
    for q in range(ROW_TILE):
        hs_ref[pl.ds(q, 2 * TM, stride=ROW_TILE), :] = hs[:, q * LANES:(q + 1) * LANES]


def _mixout(xa, ca, ctx_block, ux, y_ssm, mod_l, band, icnt, wp_bf, sp, dvec, wglu_bf, bglu, wout_bf,
            gffn, w_router, b_router, tri, nt, n_x_tiles):
    t = nt * TM
    sel = lambda i: (jnp.where(i == n_x_tiles, 1, 0), 0, 0)
    sel4 = lambda i: (jnp.where(i == n_x_tiles, 1, 0), 0, 0, 0)
    row = lambda i: (i, 0)
    fix2 = lambda i: (0, 0)
    return pl.pallas_call(
        functools.partial(_mixout_kernel, n_x_tiles=n_x_tiles),
        grid=(nt,),
        in_specs=_token_specs(n_x_tiles, ctx_block) + [
            pl.BlockSpec((TM, D_MODEL), row),
            pl.BlockSpec((TM, SSM_W), row),
            pl.BlockSpec((None, N_MOD, D_MODEL), sel),
            pl.BlockSpec((None, len(POOL_WINDOWS), TM, TM), sel4),
            pl.BlockSpec((None, TM, POOL_W), sel),
            pl.BlockSpec((len(POOL_WINDOWS) // 2, 2 * POOL_GW, 2 * POOL_GW), lambda i: (0, 0, 0)),
            pl.BlockSpec((1, POOL_W), fix2),
            pl.BlockSpec((1, SSM_W), fix2),
            pl.BlockSpec((SSM_W, SSM_W), fix2),
            pl.BlockSpec((1, SSM_W), fix2),
            pl.BlockSpec((D_MODEL, D_MODEL), fix2),
            pl.BlockSpec((1, D_MODEL), fix2),
            pl.BlockSpec((2 * N_EXPERTS, D_MODEL), fix2),
            pl.BlockSpec((N_EXPERTS, 1), fix2),
            pl.BlockSpec((TM, TM), fix2),
        ],
        out_specs=[
            pl.BlockSpec((TM, D_MODEL), row),
            pl.BlockSpec((2 * TM * ROW_TILE, LANES), row),
            pl.BlockSpec((2, TM), lambda i: (0, i)),
            pl.BlockSpec((2, TM), lambda i: (0, i)),
            pl.BlockSpec((None, N_EXPERTS, 1), lambda i: (i, 0, 0)),
        ],
        out_shape=[
            jax.ShapeDtypeStruct((t, D_MODEL), F32),
            jax.ShapeDtypeStruct((2 * t * ROW_TILE, LANES), F32),
            jax.ShapeDtypeStruct((2, t), jnp.int32),
            jax.ShapeDtypeStruct((2, t), F32),
            jax.ShapeDtypeStruct((nt, N_EXPERTS, 1), jnp.int32),
        ],
        compiler_params=_cparams(("arbitrary",)),
        name="mixer_out_router",
    )(xa, ca, ux, y_ssm, mod_l, band, icnt, wp_bf, sp, dvec, wglu_bf, bglu, wout_bf, gffn,
      w_router, b_router, tri)


def _copy_rows(src, src_row, dst, dst_row, n, n_max, sem):
    done = 0
    for bit in (1 << b for b in range(n_max.bit_length() - 1, -1, -1)):
        take = n & bit

        @pl.when(take != 0)
        def _(done=done, bit=bit):
            s0 = pl.multiple_of((src_row + done) * ROW_TILE, ROW_TILE)
            d0 = pl.multiple_of((dst_row + done) * ROW_TILE, ROW_TILE)
            pltpu.make_async_copy(src.at[pl.ds(s0, bit * ROW_TILE)],
                                  dst.at[pl.ds(d0, bit * ROW_TILE)], sem).start()

        done = done + take


def _wait_rows(src, dst, n, sem):
    size = pl.multiple_of(n * ROW_TILE, ROW_TILE)
    pltpu.make_async_copy(src.at[pl.ds(0, size)], dst.at[pl.ds(0, size)], sem).wait()


def _expert_kernel(be_ref, nbu_ref, nv_ref, rs_ref, re_ref, rsrc_ref, rlen_ref, rdst_ref,
                   hs_hbm, wg_ref, wu_ref, wd_ref, y_ref, xg0, xg1, wg_s, wu_s, wd_s, gsem):
    s = pl.program_id(0)
    nbu = nbu_ref[0]
    xg = (xg0, xg1)

    def gather_start(blk, slot):
        lo_blk = blk * MOE_BLK

        def body(r, c):
            g0 = rdst_ref[r]
            lo = jnp.maximum(g0, lo_blk)
            hi = jnp.minimum(g0 + rlen_ref[r], lo_blk + MOE_BLK)
            _copy_rows(hs_hbm, rsrc_ref[r] + (lo - g0), xg[slot], lo - lo_blk,
                       jnp.maximum(hi - lo, 0), TM, gsem.at[slot])
            return c

        lax.fori_loop(rs_ref[blk], re_ref[blk], body, 0)

    changed = jnp.logical_or(s == 0, be_ref[s] != be_ref[jnp.maximum(s - 1, 0)])

    @pl.when(jnp.logical_and(changed, s < nbu))
    def _():
        wg_s[...] = wg_ref[...].astype(BF16)
        wu_s[...] = wu_ref[...].astype(BF16)
        wd_s[...] = wd_ref[...].astype(BF16)

    def step(slot):
        @pl.when(s == 0)
        def _():
            xg0[...] = jnp.zeros_like(xg0)
            xg1[...] = jnp.zeros_like(xg1)
            gather_start(0, slot)

        @pl.when(s + 1 < nbu)
        def _():
            gather_start(s + 1, 1 - slot)

        _wait_rows(hs_hbm, xg[slot], nv_ref[s], gsem.at[slot])
        xb = jnp.concatenate([xg[slot][pl.ds(q, MOE_BLK, stride=ROW_TILE), :]
                              for q in range(ROW_TILE)], axis=-1).astype(BF16)
        out = None
        for h in range(EXPERT_SPLIT):
            cols = slice(h * D_EXPERT // EXPERT_SPLIT, (h + 1) * D_EXPERT // EXPERT_SPLIT)
            g = jnp.dot(xb, wg_s[:, cols], preferred_element_type=F32)
            u = jnp.dot(xb, wu_s[:, cols], preferred_element_type=F32)
            hid = ((g * jax.nn.sigmoid(g)) * u).astype(BF16)
            part = jnp.dot(hid, wd_s[cols, :], preferred_element_type=F32)
            out = part if out is None else out + part
        for q in range(ROW_TILE):
            y_ref[pl.ds(q, MOE_BLK, stride=ROW_TILE), :] = out[:, q * LANES:(q + 1) * LANES]

    for slot in range(2):
        @pl.when(jnp.logical_and(s < nbu, s % 2 == slot))
        def _(slot=slot):
            step(slot)

    @pl.when(s >= nbu)
    def _():
        y_ref[...] = jnp.zeros_like(y_ref)


def _experts(blk_e, nb_used, blk_valid, rs, re, rsrc, rlen, rdst, hs, w_gate, w_up, w_down, layer):
    nb = blk_e.shape[0]
    wmap = lambda i, be, *_: (layer, be[i], 0, 0)
    buf = pltpu.VMEM((MOE_BLK * ROW_TILE, LANES), F32)
    return pl.pallas_call(
        _expert_kernel,
        grid_spec=pltpu.PrefetchScalarGridSpec(
            num_scalar_prefetch=8,
            grid=(nb,),
            in_specs=[
                pl.BlockSpec(memory_space=pl.ANY),
                pl.BlockSpec((None, None, D_MODEL, D_EXPERT), wmap),
                pl.BlockSpec((None, None, D_MODEL, D_EXPERT), wmap),
                pl.BlockSpec((None, None, D_EXPERT, D_MODEL), wmap),
            ],
            out_specs=pl.BlockSpec((MOE_BLK * ROW_TILE, LANES), lambda i, *_: (i, 0)),
            scratch_shapes=[buf, buf,
                            pltpu.VMEM((D_MODEL, D_EXPERT), BF16),
                            pltpu.VMEM((D_MODEL, D_EXPERT), BF16),
                            pltpu.VMEM((D_EXPERT, D_MODEL), BF16),
                            pltpu.SemaphoreType.DMA((2,))],
        ),
        out_shape=jax.ShapeDtypeStruct((nb * MOE_BLK * ROW_TILE, LANES), F32),
        compiler_params=_cparams(("arbitrary",)),
        name="moe_experts",
    )(blk_e, nb_used, blk_valid, rs, re, rsrc, rlen, rdst, hs, w_gate, w_up, w_down)


def _combine_kernel(meta_ref, x_ref, pos_ref, gate_ref, mod_ref, gfin_ref, ys_hbm, out_ref,
                    st0, st1, sem, *, final, nt):
    i = pl.program_id(0)
    stage = (st0, st1)

    def fetch(tile, slot):
        base = tile * LANES

        def runs(n_max):
            for e in range(N_EXPERTS):
                _copy_rows(ys_hbm, meta_ref[base + e], stage[slot],
                           meta_ref[base + 2 * N_EXPERTS + e], meta_ref[base + N_EXPERTS + e],
                           n_max, sem.at[slot])

        longest = meta_ref[base + 3 * N_EXPERTS]

        @pl.when(longest <= SHORT_RUN)
        def _():
            runs(SHORT_RUN)

        @pl.when(longest > SHORT_RUN)
        def _():
            runs(TM)

    def step(slot):
        @pl.when(i == 0)
        def _():
            fetch(0, slot)

        @pl.when(i + 1 < nt)
        def _():
            fetch(i + 1, 1 - slot)

        _wait_rows(ys_hbm, stage[slot], 2 * TM, sem.at[slot])
        rows = jnp.concatenate([stage[slot][pl.ds(q, 2 * TM, stride=ROW_TILE), :]
                                for q in range(ROW_TILE)], axis=-1)
        pos = pos_ref[...].astype(F32)
        gate = gate_ref[...]
        slot_id = lax.broadcasted_iota(jnp.int32, (2 * TM, TM), 0).astype(F32)
        gmat_t = (jnp.where(slot_id == pos[0:1, :], gate[0:1, :], 0.0)
                  + jnp.where(slot_id == pos[1:2, :], gate[1:2, :], 0.0))
        g_hi = gmat_t.astype(BF16)
        g_lo = (gmat_t - g_hi.astype(F32)).astype(BF16)
        r_hi = rows.astype(BF16)
        r_lo = (rows - r_hi.astype(F32)).astype(BF16)
        tn = (((0,), (0,)), ((), ()))
        y = (lax.dot_general(g_hi, r_hi, tn, preferred_element_type=F32)
             + (lax.dot_general(g_lo, r_hi, tn, preferred_element_type=F32)
                + lax.dot_general(g_hi, r_lo, tn, preferred_element_type=F32)))
        xn = x_ref[...] + mod_ref[5:6, :] * y
        if final:
            ms = jnp.mean(xn * xn, axis=-1, keepdims=True)
            xn = xn * lax.rsqrt(ms + EPS) * gfin_ref[...]
        out_ref[...] = xn

    for slot in range(2):
        @pl.when(i % 2 == slot)
        def _(slot=slot):
            step(slot)


def _combine(meta, xn, pos, gate, mod_l, g_final, ys, nt, n_x_tiles, final):
    sel = lambda i, m: (jnp.where(i == n_x_tiles, 1, 0), 0, 0)
    row = lambda i, m: (i, 0)
    stage = pltpu.VMEM((2 * TM * ROW_TILE, LANES), F32)
    return pl.pallas_call(
        functools.partial(_combine_kernel, final=final, nt=nt),
        grid_spec=pltpu.PrefetchScalarGridSpec(
            num_scalar_prefetch=1,
            grid=(nt,),
            in_specs=[
                pl.BlockSpec((TM, D_MODEL), row),
                pl.BlockSpec((2, TM), lambda i, m: (0, i)),
                pl.BlockSpec((2, TM), lambda i, m: (0, i)),
                pl.BlockSpec((None, N_MOD, D_MODEL), sel),
                pl.BlockSpec((1, D_MODEL), lambda i, m: (0, 0)),
                pl.BlockSpec(memory_space=pl.ANY),
            ],
            out_specs=pl.BlockSpec((TM, D_MODEL), row),
            scratch_shapes=[stage, stage, pltpu.SemaphoreType.DMA((2,))],
        ),
        out_shape=jax.ShapeDtypeStruct((nt * TM, D_MODEL), F32),
        compiler_params=_cparams(("arbitrary",)),
        name="moe_combine",
    )(meta, xn, pos, gate, mod_l, g_final, ys)


def _moe(hs, pos, tcnt, gate, xn, mod_l, g_final, w_gate, w_up, w_down, layer, nt, n_x_tiles,
         final):
    t = nt * TM
    i32 = jnp.int32
    n = tcnt.reshape(nt, N_EXPERTS)
    off = jnp.cumsum(n, axis=1) - n
    base = jnp.cumsum(n, axis=0) - n
    counts = jnp.sum(n, axis=0)
    padded = (counts + MOE_BLK - 1) // MOE_BLK * MOE_BLK
    pend = jnp.cumsum(padded)
    pstart = pend - padded
    nb = (2 * t + N_EXPERTS * (MOE_BLK - 1)) // MOE_BLK + 1
    nb_used = (pend[-1] // MOE_BLK).astype(i32)
    blk_lo = jnp.arange(nb, dtype=i32) * MOE_BLK
    blk_e = jnp.minimum(jnp.sum(pend[None, :] <= jnp.minimum(blk_lo, pend[-1] - MOE_BLK)[:, None],
                                axis=1), N_EXPERTS - 1).astype(i32)
    blk_valid = jnp.clip((pstart + counts)[blk_e] - blk_lo, 0, MOE_BLK).astype(i32)
    gsrc = (pstart[None, :] + base).astype(i32)
    rdst = gsrc.T.reshape(-1)
    rlen = n.T.reshape(-1).astype(i32)
    rsrc = (jnp.arange(nt, dtype=i32)[:, None] * (2 * TM) + off).T.reshape(-1).astype(i32)
    e_dst = gsrc.T[blk_e]
    e_end = e_dst + n.T[blk_e]
    rs = (blk_e * nt + jnp.sum(e_end <= blk_lo[:, None], axis=1)).astype(i32)
    re = (blk_e * nt + jnp.sum(e_dst < (blk_lo + MOE_BLK)[:, None], axis=1)).astype(i32)
    ys = _experts(blk_e, nb_used.reshape(1), blk_valid, rs, re, rsrc, rlen, rdst, hs,
                  w_gate, w_up, w_down, layer)
    longest = jnp.max(n, axis=1, keepdims=True).astype(i32)
    meta = jnp.concatenate([gsrc, n.astype(i32), off.astype(i32), longest,
                            jnp.zeros((nt, LANES - 3 * N_EXPERTS - 1), i32)], axis=1).reshape(-1)
    return _combine(meta, xn, pos, gate, mod_l, g_final, ys, nt, n_x_tiles, final)


def _cmul(ar, ai, br, bi):
    return ar * br - ai * bi, ar * bi + ai * br


def _expand_rows(v, rep_ref, mask_ref):
    return jnp.dot(v.astype(BF16), rep_ref[...], preferred_element_type=F32) * mask_ref[...]


def _expand_cols(e, rept_ref, maskt_ref):
    out = lax.dot_general(rept_ref[...], e.astype(BF16), (((1,), (1,)), ((), ())),
                          preferred_element_type=F32)
    return out * maskt_ref[...]


def _ssm_prep_kernel(are_ref, aim_ref, ldt_ref, btr_ref, bti_ref, cr_ref, ci_ref,
                     rep_ref, mask_ref, rept_ref, maskt_ref, ms_ref, mcat_ref, ecat):
    lag_blocks = []
    for d in range(2):
        a_re = jnp.minimum(are_ref[d], -1e-4)
        a_im = aim_ref[d]
        dt = jnp.exp(ldt_ref[d])
        mag = jnp.exp(a_re * dt)
        lr = mag * jnp.cos(a_im * dt)
        li = mag * jnp.sin(a_im * dt)
        den = a_re * a_re + a_im * a_im
        k_re = ((lr - 1.0) * a_re + li * a_im) / den
        k_im = (li * a_re - (lr - 1.0) * a_im) / den
        bbr, bbi = _cmul(k_re, k_im, btr_ref[d], bti_ref[d])
        cr, ci = cr_ref[d], ci_ref[d]
        pr, pi = jnp.ones_like(lr), jnp.zeros_like(lr)
        v0 = None
        for n in range(CH + 1):
            er, ei = _cmul(pr, pi, cr, ci)
            et = jnp.concatenate([_expand_cols(er, rept_ref, maskt_ref),
                                  _expand_cols(-ei, rept_ref, maskt_ref)], axis=0).astype(BF16)
            if n < CH:
                vr, vi = _cmul(pr, pi, bbr, bbi)
                v = jnp.concatenate([_expand_rows(vr, rep_ref, mask_ref),
                                     _expand_rows(vi, rep_ref, mask_ref)], axis=1).astype(BF16)
                if n == 0:
                    v0 = v
                sigma = CH - 1 - n if d == 0 else n
                ms_ref[d, sigma * LANES:(sigma + 1) * LANES, :] = v
                ecat[:, n * LANES:(n + 1) * LANES] = et
            if n >= 1:
                tau = n - 1 if d == 0 else CH - n
                mcat_ref[CW + d * SW:CW + (d + 1) * SW, tau * LANES:(tau + 1) * LANES] = et
            pr, pi = _cmul(pr, pi, lr, li)
        lag_blocks.append(jnp.dot(v0, ecat[...], preferred_element_type=F32))
    kf, kb = lag_blocks
    for s in range(CH):
        for t in range(CH):
            if t > s:
                blk = kf[:, (t - s) * LANES:(t - s + 1) * LANES]
            elif t < s:
                blk = kb[:, (s - t) * LANES:(s - t + 1) * LANES]
            else:
                blk = kf[:, :LANES] + kb[:, :LANES]
            mcat_ref[s * LANES:(s + 1) * LANES, t * LANES:(t + 1) * LANES] = blk.astype(BF16)


def _ssm_operators(a_re, a_im, log_dt, b_re, b_im, c_re, c_im):
    depth = a_re.shape[0]
    rows = SSM_GROUPS * SSM_GROUP
    p = SSM_STATE
    rep_rows = lambda v: jnp.repeat(v, SSM_GROUP, axis=2)
    are = rep_rows(a_re)
    aim = rep_rows(a_im)
    ldt = jnp.broadcast_to(rep_rows(log_dt[..., None]), are.shape)
    btr = jnp.swapaxes(b_re, -1, -2).reshape(depth, 2, rows, p)
    bti = jnp.swapaxes(b_im, -1, -2).reshape(depth, 2, rows, p)
    cr = c_re.reshape(depth, 2, rows, p)
    ci = c_im.reshape(depth, 2, rows, p)
    rep = np.tile(np.eye(p, dtype=np.float32), (1, GB))
    mask = np.kron(np.eye(GB, dtype=np.float32), np.ones((SSM_GROUP, p), np.float32))
    par = pl.BlockSpec((None, 2, LANES, p), lambda l, b: (l, 0, b, 0))
    fix = lambda shape: pl.BlockSpec(shape, lambda l, b: (0, 0))
    return pl.pallas_call(
        _ssm_prep_kernel,
        grid=(depth, NGB),
        in_specs=[par] * 7 + [fix((p, GB * p)), fix((LANES, GB * p)),
                              fix((GB * p, p)), fix((GB * p, LANES))],
        out_specs=[
            pl.BlockSpec((None, None, 2, CW, SW), lambda l, b: (l, b, 0, 0, 0)),
            pl.BlockSpec((None, None, CW + 2 * SW, CW), lambda l, b: (l, b, 0, 0)),
        ],
        out_shape=[
            jax.ShapeDtypeStruct((depth, NGB, 2, CW, SW), BF16),
            jax.ShapeDtypeStruct((depth, NGB, CW + 2 * SW, CW), BF16),
        ],
        scratch_shapes=[pltpu.VMEM((SW, CW), BF16)],
        compiler_params=_cparams(("arbitrary", "arbitrary")),
        name="ssm_operators",
    )(are, aim, ldt, btr, bti, cr, ci, jnp.asarray(rep, BF16), jnp.asarray(mask),
      jnp.asarray(rep.T, BF16), jnp.asarray(mask.T))


def _scan_tables(a_re, a_im, log_dt):
    depth = a_re.shape[0]
    a_re = jnp.minimum(a_re, -1e-4)
    dt = jnp.exp(log_dt)[..., None]
    mag = jnp.exp(a_re * dt)
    lr = mag * jnp.cos(a_im * dt)
    li = mag * jnp.sin(a_im * dt)
    ar, ai = lr, li
    for _ in range(CH - 1):
        ar, ai = _cmul(ar, ai, lr, li)
    apw = [(ar, ai)]
    for _ in range(SUBLANES - 1):
        apw.append(_cmul(apw[-1][0], apw[-1][1], ar, ai))
    apr = jnp.stack([q[0] for q in apw], axis=1)
    api = jnp.stack([q[1] for q in apw], axis=1)
    rows = np.arange(SUBLANES)
    tabs = []
    for d in range(2):
        consts = []
        for k in (1, 2, 4):
            keep = (rows >= k) if d == 0 else (rows < SUBLANES - k)
            keep = jnp.asarray(keep.astype(np.float32))[None, :, None, None]
            consts.append(keep * apr[:, k - 1:k, d])
            consts.append(keep * api[:, k - 1:k, d])
        order = rows if d == 0 else SUBLANES - 1 - rows
        consts.append(apr[:, order, d])
        consts.append(api[:, order, d])
        tabs.append(jnp.stack(consts, axis=1))
    tab = jnp.stack(tabs, axis=1).reshape(depth, 2, 8 * SUBLANES, NGB, GB * SSM_STATE)
    return tab.transpose(0, 1, 3, 2, 4)


def _pair_blockdiag(w):
    g, c, _ = w.shape
    z = jnp.zeros((g // 2, c, c), w.dtype)
    top = jnp.concatenate([w[0::2], z], axis=2)
    bot = jnp.concatenate([z, w[1::2]], axis=2)
    return jnp.concatenate([top, bot], axis=1)


def _pool_constants(seg):
    t = np.arange(TM)
    s0 = t // seg * seg
    band = np.zeros((len(POOL_WINDOWS), TM, TM), np.float32)
    icnt = np.zeros((TM, POOL_W), np.float32)
    for g, w in enumerate(POOL_WINDOWS):
        lo = np.maximum(t - w // 2, s0)
        hi = np.minimum(t + w // 2, s0 + seg)
        band[g] = (t[None, :] >= lo[:, None]) & (t[None, :] < hi[:, None])
        icnt[:, g * POOL_GW:(g + 1) * POOL_GW] = (1.0 / (hi - lo).astype(np.float32))[:, None]
    return band, icnt


def kernel(x, c, ctx, c_ctx, w_mod, b_mod, g_mix, g_ffn, w_in, w_out, w_pool, s_pool,
           ssm_a_re, ssm_a_im, ssm_log_dt, ssm_b_re, ssm_b_im, ssm_c_re, ssm_c_im, ssm_d,
           w_glu, b_glu, w_router, b_router, w_gate, w_up, w_down, g_final):
    bsz, seq, d = x.shape
    ctx_len = ctx.shape[1]
    depth = w_mod.shape[0]
    assert bsz == 1 and d == D_MODEL and ctx_len == TM
    assert seq % (SR * CH) == 0 and seq % GRID_W == 0 and seq % TI == 0
    n_xt = seq // TM
    n_st = seq // (SR * CH)
    ctx_rows = ctx_len // CH

    cvec = jnp.concatenate([c.reshape(1, d), c_ctx.reshape(1, d),
                            jnp.zeros((SUBLANES - 2, d), F32)], axis=0)
    mod = _modulation(cvec, w_mod, b_mod).reshape(depth, SUBLANES, N_MOD, d)

    band_x, icnt_x = _pool_constants(GRID_W)
    band_c, icnt_c = _pool_constants(ctx_len)
    band = jnp.asarray(np.stack([band_x, band_c]), BF16)
    icnt = jnp.asarray(np.stack([icnt_x, icnt_c]), F32)
    tri = jnp.asarray(np.triu(np.ones((TM, TM), np.float32), 1), BF16)
    wr_hi = w_router.astype(BF16)
    wr_split = jnp.concatenate([wr_hi, (w_router - wr_hi.astype(F32)).astype(BF16)], axis=1).T

    ms, mcat = _ssm_operators(ssm_a_re, ssm_a_im, ssm_log_dt, ssm_b_re, ssm_b_im,
                              ssm_c_re, ssm_c_im)
    tab = _scan_tables(ssm_a_re, ssm_a_im, ssm_log_dt)

    xa, ca, ctx_block = x[0], ctx[0], 0
    for l in range(depth):
        last = l == depth - 1
        mod_l = mod[l, :2]
        ux, uv_ext = _inproj(xa, ca, ctx_block, mod_l, g_mix[l].reshape(1, d),
                             w_in[l].astype(BF16), n_xt)
        hf, gb = _ssm_states(uv_ext, ms, tab, l, n_st, ctx_rows)
        y_ssm = _ssm_readout(uv_ext, hf, gb, mcat, l)
        nt = n_xt if last else n_xt + 1
        xn, hs, pos, gate, tcnt = _mixout(
            xa, ca, ctx_block, ux, y_ssm, mod_l, band, icnt, _pair_blockdiag(w_pool[l]).astype(BF16),
            s_pool[l].reshape(1, -1), ssm_d[l].reshape(1, -1), w_glu[l].astype(BF16),
            b_glu[l].reshape(1, -1), w_out[l].astype(BF16), g_ffn[l].reshape(1, d), wr_split,
            b_router.reshape(-1, 1), tri, nt, n_xt)
        tok = _moe(hs, pos, tcnt, gate, xn, mod_l, g_final.reshape(1, d), w_gate, w_up, w_down,
                   l, nt, n_xt, last)
        xa, ca, ctx_block = tok, tok, n_xt
    return tok.reshape(bsz, seq, d)
```
